```python
import math
import jax, jax.numpy as jnp
from jax import lax
import numpy as np

D_MODEL = 1024
BATCH = 4
SEQ = 4096
DEPTH = 2

GRID_W = 64
CTX_LEN = 256
N_MIXERS = 2
N_ATTN_LAYERS = (DEPTH + 1) // 2
N_SSM_LAYERS = DEPTH // 2
HEAD_DIM = 64
N_HEADS = D_MODEL // HEAD_DIM
N_KV_HEADS = 4
GQA_GROUP = N_HEADS // N_KV_HEADS
ATTN_WIDTH = N_HEADS * HEAD_DIM
KV_WIDTH = N_KV_HEADS * HEAD_DIM
ATTN_IN = 2 * ATTN_WIDTH + 2 * KV_WIDTH
WINDOW = 128
BLOCK = 128
ROPE_BASE = 10000.0
ROPE_FREQS = HEAD_DIM // 4
SSM_WIDTH = D_MODEL
SSM_GROUP = 16
SSM_GROUPS = SSM_WIDTH // SSM_GROUP
SSM_STATE = 64
DT_MIN = 0.001
DT_MAX = 0.1
NORM_EPS = 1e-6
NEG_INF = -1e30

kernel_name = "hybrid_swa_sink_s5_ctx_prefix"


def rmsnorm(x, w):
    xf = x.astype(jnp.float32)
    y = xf * lax.rsqrt(jnp.mean(xf * xf, axis=-1, keepdims=True) + NORM_EPS)
    return (y * w.astype(jnp.float32)).astype(x.dtype)


def rope_angles(rows):
    inv = ROPE_BASE ** (-jnp.arange(ROPE_FREQS, dtype=jnp.float32) / ROPE_FREQS)
    row = jnp.repeat(jnp.arange(rows, dtype=jnp.float32), GRID_W)
    col = jnp.tile(jnp.arange(GRID_W, dtype=jnp.float32), rows)
    return jnp.stack([row[:, None] * inv, col[:, None] * inv], axis=1)


def rope_2d(t, ang):
    ts = t.reshape(t.shape[:-1] + (2, 2, ROPE_FREQS))
    t1, t2 = ts[..., 0, :], ts[..., 1, :]
    cos = jnp.cos(ang)[None, :, None].astype(t.dtype)
    sin = jnp.sin(ang)[None, :, None].astype(t.dtype)
    out = jnp.stack([t1 * cos - t2 * sin, t2 * cos + t1 * sin], axis=-2)
    return out.reshape(t.shape)


def band_mask(n_blocks):
    qi = jnp.arange(BLOCK)[:, None]
    kj = jnp.arange(3 * BLOCK)[None, :]
    in_win = jnp.abs(kj - BLOCK - qi) <= WINDOW
    key_pos = jnp.arange(n_blocks)[:, None] * BLOCK + jnp.arange(3 * BLOCK)[None, :] - BLOCK
    valid = (key_pos >= 0) & (key_pos < n_blocks * BLOCK)
    return in_win[None] & valid[:, None, :]


def band(t):
    b, l = t.shape[0], t.shape[1]
    tp = jnp.pad(t, ((0, 0), (BLOCK, BLOCK), (0, 0), (0, 0)))
    tb = tp.reshape(b, l // BLOCK + 2, BLOCK, t.shape[2], t.shape[3])
    return jnp.concatenate([tb[:, :-2], tb[:, 1:-1], tb[:, 2:]], axis=2)


def attn_mixer(h, hc, w_in, sink, w_out, ang, mask, need_ctx_out):
    b, l, _ = h.shape
    nb = l // BLOCK
    f32 = jnp.float32
    scale = HEAD_DIM ** -0.5
    q, k, v, z = jnp.split(h @ w_in, [ATTN_WIDTH, ATTN_WIDTH + KV_WIDTH, ATTN_WIDTH + 2 * KV_WIDTH], axis=-1)
    q = rope_2d(q.reshape(b, l, N_HEADS, HEAD_DIM), ang)
    k = rope_2d(k.reshape(b, l, N_KV_HEADS, HEAD_DIM), ang)
    v = v.reshape(b, l, N_KV_HEADS, HEAD_DIM)
    kvc = hc @ w_in[:, ATTN_WIDTH:ATTN_WIDTH + 2 * KV_WIDTH]
    kc, vc = jnp.split(kvc, 2, axis=-1)
    n_ctx = hc.shape[1]
    kc = kc.reshape(b, n_ctx, N_KV_HEADS, HEAD_DIM).astype(f32)
    vc = vc.reshape(b, n_ctx, N_KV_HEADS, HEAD_DIM).astype(f32)
    sink_kg = sink.astype(f32).reshape(N_KV_HEADS, GQA_GROUP)

    qb = q.reshape(b, nb, BLOCK, N_KV_HEADS, GQA_GROUP, HEAD_DIM).astype(f32) * scale
    kw = band(k).astype(f32)
    vw = band(v).astype(f32)
    s_win = jnp.einsum('bnqkgd,bnjkd->bnkgqj', qb, kw)
    s_win = jnp.where(mask[None, :, None, None], s_win, NEG_INF)
    s_ctx = jnp.einsum('bnqkgd,bckd->bnkgqc', qb, kc)
    s_sink = jnp.broadcast_to(sink_kg[None, None, :, :, None, None], s_win.shape[:-1] + (1,))
    p = jax.nn.softmax(jnp.concatenate([s_win, s_ctx, s_sink], axis=-1), axis=-1)
    nw = 3 * BLOCK
    o = (jnp.einsum('bnkgqj,bnjkd->bnqkgd', p[..., :nw], vw)
         + jnp.einsum('bnkgqc,bckd->bnqkgd', p[..., nw:nw + n_ctx], vc))
    o = o.reshape(b, l, ATTN_WIDTH).astype(h.dtype)
    y = (o * jax.nn.silu(z)) @ w_out

    yc = None
    if need_ctx_out:
        qc = (hc @ w_in[:, :ATTN_WIDTH]).reshape(b, n_ctx, N_KV_HEADS, GQA_GROUP, HEAD_DIM).astype(f32) * scale
        zc = hc @ w_in[:, ATTN_WIDTH + 2 * KV_WIDTH:]
        sc = jnp.einsum('bqkgd,bckd->bkgqc', qc, kc)
        sc_sink = jnp.broadcast_to(sink_kg[None, :, :, None, None], sc.shape[:-1] + (1,))
        pc = jax.nn.softmax(jnp.concatenate([sc, sc_sink], axis=-1), axis=-1)
        oc = jnp.einsum('bkgqc,bckd->bqkgd', pc[..., :n_ctx], vc)
        oc = oc.reshape(b, n_ctx, ATTN_WIDTH).astype(hc.dtype)
        yc = (oc * jax.nn.silu(zc)) @ w_out
    return y, yc


def _scan_op(e1, e2):
    a1, b1 = e1
    a2, b2 = e2
    return a1 * a2, a2 * b1 + b2


def diag_scan(a_bar, bu):
    a = jnp.broadcast_to(a_bar, bu.shape)
    _, xs = lax.associative_scan(_scan_op, (a, bu), axis=1)
    return xs


def s5_output(y, z, w_glu, w_out, dtype):
    b, l = y.shape[0], y.shape[1]
    y = jax.nn.gelu(y.reshape(b, l, SSM_WIDTH), approximate=False).astype(dtype)
    ya, yg = jnp.split(y @ w_glu, 2, axis=-1)
    y = ya * jax.nn.sigmoid(yg)
    return (y * jax.nn.silu(z)) @ w_out


def s5_mixer(h, hc, w_in, lam_re, lam_im, log_dt, b_re, b_im, c_re, c_im, d_skip, w_glu, w_out, need_ctx_out):
    b, l, _ = h.shape
    n_ctx = hc.shape[1]
    f32 = jnp.float32
    u, z = jnp.split(h @ w_in, 2, axis=-1)
    uc = hc @ w_in[:, :SSM_WIDTH]
    ug = u.reshape(b, l, SSM_GROUPS, SSM_GROUP).astype(f32)
    ucg = uc.reshape(b, n_ctx, SSM_GROUPS, SSM_GROUP).astype(f32)
    d_g = d_skip.astype(f32).reshape(SSM_GROUPS, SSM_GROUP)
    y = d_g * ug
    yc = d_g * ucg if need_ctx_out else None
    for dirn in range(2):
        lam = lax.complex(lam_re[dirn].astype(f32), lam_im[dirn].astype(f32))
        dt = jnp.exp(log_dt[dirn].astype(f32))[:, None]
        a_bar = jnp.exp(lam * dt)
        b_mat = lax.complex(b_re[dirn].astype(f32), b_im[dirn].astype(f32))
        b_bar = ((a_bar - 1.0) / lam)[..., None] * b_mat
        c_mat = lax.complex(c_re[dirn].astype(f32), c_im[dirn].astype(f32))
        bu_c = jnp.einsum('gph,blgh->blgp', b_bar, ucg.astype(jnp.complex64))
        bu = jnp.einsum('gph,blgh->blgp', b_bar, ug.astype(jnp.complex64))
        if dirn == 1:
            bu_c = jnp.flip(bu_c, axis=1)
            bu = jnp.flip(bu, axis=1)
        xs_c = diag_scan(a_bar, bu_c)
        s0 = xs_c[:, -1]
        xs = diag_scan(a_bar, bu.at[:, 0].add(a_bar * s0))
        if dirn == 1:
            xs = jnp.flip(xs, axis=1)
            xs_c = jnp.flip(xs_c, axis=1)
        y = y + jnp.einsum('ghp,blgp->blgh', c_mat, xs).real
        if need_ctx_out:
            yc = yc + jnp.einsum('ghp,blgp->blgh', c_mat, xs_c).real
    out = s5_output(y, z, w_glu, w_out, h.dtype)
    out_c = None
    if need_ctx_out:
        zc = hc @ w_in[:, SSM_WIDTH:]
        out_c = s5_output(yc, zc, w_glu, w_out, hc.dtype)
    return out, out_c


def setup_inputs(seed: int = 0) -> dict:
    key = jax.random.key(seed)
    ks = jax.random.split(key, 24)
    f32 = jnp.float32

    def nrm(k, shape, scale):
        return jax.random.normal(k, shape, f32) * scale

    lam_shape = (N_SSM_LAYERS, 2, SSM_GROUPS, SSM_STATE)
    n_idx = jnp.arange(SSM_STATE, dtype=f32)
    return {
        "x": nrm(ks[0], (BATCH, SEQ, D_MODEL), 1.0),
        "c": nrm(ks[1], (BATCH, D_MODEL), 1.0),
        "ctx": nrm(ks[2], (BATCH, CTX_LEN, D_MODEL), 1.0),
        "c_ctx": nrm(ks[3], (D_MODEL,), 1.0),
        "norm_w": 1.0 + nrm(ks[4], (DEPTH, D_MODEL), 0.02),
        "w_ada": nrm(ks[5], (DEPTH, D_MODEL, 3 * D_MODEL), D_MODEL ** -0.5),
        "b_ada": nrm(ks[6], (DEPTH, 3 * D_MODEL), 0.02),
        "attn_w_in": nrm(ks[7], (N_ATTN_LAYERS, D_MODEL, ATTN_IN), D_MODEL ** -0.5),
        "attn_sink": nrm(ks[8], (N_ATTN_LAYERS, N_HEADS), 0.5),
        "attn_w_out": nrm(ks[9], (N_ATTN_LAYERS, ATTN_WIDTH, D_MODEL), ATTN_WIDTH ** -0.5),
        "ssm_w_in": nrm(ks[10], (N_SSM_LAYERS, D_MODEL, 2 * SSM_WIDTH), D_MODEL ** -0.5),
        "ssm_lam_re": -0.5 + nrm(ks[11], lam_shape, 0.01),
        "ssm_lam_im": math.pi * n_idx + nrm(ks[12], lam_shape, 0.01),
        "ssm_log_dt": jax.random.uniform(ks[13], (N_SSM_LAYERS, 2, SSM_GROUPS), f32,
                                         math.log(DT_MIN), math.log(DT_MAX)),
        "ssm_b_re": nrm(ks[14], (N_SSM_LAYERS, 2, SSM_GROUPS, SSM_STATE, SSM_GROUP), (2 * SSM_GROUP) ** -0.5),
        "ssm_b_im": nrm(ks[15], (N_SSM_LAYERS, 2, SSM_GROUPS, SSM_STATE, SSM_GROUP), (2 * SSM_GROUP) ** -0.5),
        "ssm_c_re": nrm(ks[16], (N_SSM_LAYERS, 2, SSM_GROUPS, SSM_GROUP, SSM_STATE), SSM_STATE ** -0.5),
        "ssm_c_im": nrm(ks[17], (N_SSM_LAYERS, 2, SSM_GROUPS, SSM_GROUP, SSM_STATE), SSM_STATE ** -0.5),
        "ssm_d": nrm(ks[18], (N_SSM_LAYERS, SSM_WIDTH), 1.0),
        "ssm_w_glu": nrm(ks[19], (N_SSM_LAYERS, SSM_WIDTH, 2 * SSM_WIDTH), SSM_WIDTH ** -0.5),
        "ssm_w_out": nrm(ks[20], (N_SSM_LAYERS, SSM_WIDTH, D_MODEL), SSM_WIDTH ** -0.5),
        "final_norm_w": 1.0 + nrm(ks[21], (D_MODEL,), 0.02),
    }


def reference(x, c, ctx, c_ctx, norm_w, w_ada, b_ada, attn_w_in, attn_sink, attn_w_out,
              ssm_w_in, ssm_lam_re, ssm_lam_im, ssm_log_dt, ssm_b_re, ssm_b_im, ssm_c_re, ssm_c_im,
              ssm_d, ssm_w_glu, ssm_w_out, final_norm_w):
    n_lat = x.shape[1]
    rows = n_lat // GRID_W
    ang = rope_angles(rows)
    mask = band_mask(n_lat // BLOCK)
    for i in range(DEPTH):
        need_ctx_out = i < DEPTH - 1
        mod = jax.nn.silu(c) @ w_ada[i] + b_ada[i]
        mod_c = jax.nn.silu(c_ctx) @ w_ada[i] + b_ada[i]
        shift, scale, gate = jnp.split(mod, 3, axis=-1)
        shift_c, scale_c, gate_c = jnp.split(mod_c, 3, axis=-1)
        h = rmsnorm(x, norm_w[i]) * (1.0 + scale[:, None]) + shift[:, None]
        hc = rmsnorm(ctx, norm_w[i]) * (1.0 + scale_c) + shift_c
        j = i // N_MIXERS
        if i % N_MIXERS == 0:
            y, yc = attn_mixer(h, hc, attn_w_in[j], attn_sink[j], attn_w_out[j], ang, mask, need_ctx_out)
        else:
            y, yc = s5_mixer(h, hc, ssm_w_in[j], ssm_lam_re[j], ssm_lam_im[j], ssm_log_dt[j],
                             ssm_b_re[j], ssm_b_im[j], ssm_c_re[j], ssm_c_im[j], ssm_d[j],
                             ssm_w_glu[j], ssm_w_out[j], need_ctx_out)
        x = x + gate[:, None] * y
        if need_ctx_out:
            ctx = ctx + gate_c * yc
    return rmsnorm(x, final_norm_w)
```

```python
import functools
import math

import jax
import jax.numpy as jnp
import numpy as np
from jax import lax
from jax.experimental import pallas as pl
from jax.experimental.pallas import tpu as pltpu

F32 = jnp.float32
BF16 = jnp.bfloat16

D_MODEL = 1024
BATCH = 4
SEQ = 4096
GRID_W = 64
CTX_LEN = 256
TOTAL = CTX_LEN + SEQ
HEAD_DIM = 64
N_HEADS = 16
N_KV_HEADS = 4
ATTN_WIDTH = N_HEADS * HEAD_DIM
KV_WIDTH = N_KV_HEADS * HEAD_DIM
BLOCK = 128
N_BLOCKS = TOTAL // BLOCK
N_CTX_BLOCKS = CTX_LEN // BLOCK
ROPE_BASE = 10000.0
ROPE_FREQS = HEAD_DIM // 4
SSM_GROUP = 16
SSM_GROUPS = D_MODEL // SSM_GROUP
SSM_STATE = 64
NORM_EPS = 1e-6
NEG_INF = -1e30

LANES = 128
SUBLANES = 8
N_SLOTS = ATTN_WIDTH // LANES
ROW_TILE = 256
N_ROW_TILES = TOTAL // ROW_TILE
CHUNK = 16
N_CHUNKS = TOTAL // CHUNK
N_CTX_CHUNKS = CTX_LEN // CHUNK
CHUNK_W = CHUNK * SSM_GROUP
STATE_W = 2 * SSM_STATE
SCAN_ROWS = N_CHUNKS * BATCH
CTX_SCAN_ROWS = N_CTX_CHUNKS * BATCH
N_SCAN_BLOCKS = SCAN_ROWS // SUBLANES
N_CTX_SCAN_BLOCKS = CTX_SCAN_ROWS // SUBLANES
GROUP_BATCH = 4

assert BATCH * 2 == SUBLANES


def _params(semantics, vmem_mb):
    return pltpu.CompilerParams(dimension_semantics=semantics,
                                vmem_limit_bytes=vmem_mb * 1024 * 1024)


def _mod_kernel(c_ref, w_ref, b_ref, o_ref):
    c = c_ref[...]
    a = c * jax.nn.sigmoid(c)
    o_ref[0] = jnp.dot(a, w_ref[0], preferred_element_type=F32,
                       precision=lax.Precision.HIGHEST) + b_ref[0]


def _modulation(c, c_ctx, w_ada, b_ada):
    depth = w_ada.shape[0]
    rows = jnp.zeros((SUBLANES, D_MODEL), F32).at[:BATCH].set(c).at[BATCH].set(c_ctx)
    n_col = 3
    out = pl.pallas_call(
        _mod_kernel,
        grid=(depth, n_col),
        in_specs=[
            pl.BlockSpec((SUBLANES, D_MODEL), lambda l, j: (0, 0)),
            pl.BlockSpec((1, D_MODEL, D_MODEL), lambda l, j: (l, 0, j)),
            pl.BlockSpec((1, 1, D_MODEL), lambda l, j: (l, 0, j)),
        ],
        out_specs=pl.BlockSpec((1, SUBLANES, D_MODEL), lambda l, j: (l, 0, j)),
        out_shape=jax.ShapeDtypeStruct((depth, SUBLANES, 3 * D_MODEL), F32),
        compiler_params=_params(("arbitrary", "arbitrary"), 32),
        name="adaln_modulation",
    )(rows, w_ada, b_ada.reshape(depth, 1, 3 * D_MODEL))
    tabs = []
    for l in range(depth):
        lat = out[l, :BATCH].reshape(BATCH, 3, D_MODEL)
        cx = jnp.broadcast_to(out[l, BATCH].reshape(1, 3, D_MODEL), (BATCH, 3, D_MODEL))
        tab = jnp.stack([cx, lat], axis=1)
        tabs.append(jnp.pad(tab, ((0, 0), (0, 0), (0, SUBLANES - 3), (0, 0))))
    return tabs


def _modulated_norm(xt, nw, mod_ref):
    ms = jnp.mean(xt * xt, axis=-1, keepdims=True)
    y = xt * lax.rsqrt(ms + NORM_EPS) * nw
    return y * (1.0 + mod_ref[0, 0, 1:2, :]) + mod_ref[0, 0, 0:1, :]


def _proj0_kernel(x_ref, c_ref, mod_ref, nw_ref, cos_ref, sin_ref, w_ref,
                  q_ref, z_ref, kbd_ref, vbd_ref):
    i = pl.program_id(1)
    xt = jnp.where(i == 0, c_ref[0], x_ref[0])
    h = _modulated_norm(xt, nw_ref[...], mod_ref).astype(BF16)
    cos = cos_ref[...]
    sin = sin_ref[...]
    lane = lax.broadcasted_iota(jnp.int32, (ROW_TILE, LANES), 1)
    first_half = (lane % HEAD_DIM) < (HEAD_DIM // 2)
    low = lax.broadcasted_iota(jnp.int32, (BLOCK, LANES), 1) < HEAD_DIM

    def rope(t):
        partner = jnp.where(first_half, pltpu.roll(t, LANES - HEAD_DIM // 2, 1),
                            pltpu.roll(t, HEAD_DIM // 2, 1))
        return t * cos + partner * sin

    q = jnp.dot(h, w_ref[:, :ATTN_WIDTH], preferred_element_type=F32)
    for j in range(N_SLOTS):
        sl = slice(LANES * j, LANES * (j + 1))
        q_ref[0, :, sl] = (rope(q[:, sl]) * (HEAD_DIM ** -0.5)).astype(BF16)
    k = jnp.dot(h, w_ref[:, ATTN_WIDTH:ATTN_WIDTH + KV_WIDTH], preferred_element_type=F32)
    v = jnp.dot(h, w_ref[:, ATTN_WIDTH + KV_WIDTH:ATTN_WIDTH + 2 * KV_WIDTH],
                preferred_element_type=F32)
    for m in range(KV_WIDTH // LANES):
        sl = slice(LANES * m, LANES * (m + 1))
        kr = rope(k[:, sl])
        vm = v[:, sl]
        for blk in range(ROW_TILE // BLOCK):
            rows = slice(BLOCK * blk, BLOCK * (blk + 1))
            kbd_ref[0, blk, m, :BLOCK, :] = jnp.where(low, kr[rows], 0.0).astype(BF16)
            kbd_ref[0, blk, m, BLOCK:, :] = jnp.where(low, 0.0, kr[rows]).astype(BF16)
            vbd_ref[0, blk, m, :BLOCK, :] = jnp.where(low, vm[rows], 0.0).astype(BF16)
            vbd_ref[0, blk, m, BLOCK:, :] = jnp.where(low, 0.0, vm[rows]).astype(BF16)
    z = jnp.dot(h, w_ref[:, ATTN_WIDTH + 2 * KV_WIDTH:], preferred_element_type=F32)
    z_ref[0] = z.astype(BF16)


def _proj0(x, ctx, mod, norm_w, cos_tab, sin_tab, w_in):
    n_col = w_in.shape[1]
    blocks_per_tile = ROW_TILE // BLOCK
    kv_shape = jax.ShapeDtypeStruct((BATCH, N_BLOCKS, KV_WIDTH // LANES, 2 * BLOCK, LANES), BF16)
    kv_spec = pl.BlockSpec((1, blocks_per_tile, KV_WIDTH // LANES, 2 * BLOCK, LANES),
                           lambda b, i: (b, i, 0, 0, 0))
    row_spec = pl.BlockSpec((1, ROW_TILE, D_MODEL), lambda b, i: (b, i, 0))
    return pl.pallas_call(
        _proj0_kernel,
        grid=(BATCH, N_ROW_TILES),
        in_specs=[
            pl.BlockSpec((1, ROW_TILE, D_MODEL), lambda b, i: (b, jnp.maximum(i - 1, 0), 0)),
            pl.BlockSpec((1, ROW_TILE, D_MODEL), lambda b, i: (b, 0, 0)),
            pl.BlockSpec((1, 1, SUBLANES, D_MODEL), lambda b, i: (b, jnp.minimum(i, 1), 0, 0)),
            pl.BlockSpec((1, D_MODEL), lambda b, i: (0, 0)),
            pl.BlockSpec((ROW_TILE, LANES), lambda b, i: (i, 0)),
            pl.BlockSpec((ROW_TILE, LANES), lambda b, i: (i, 0)),
            pl.BlockSpec((D_MODEL, n_col), lambda b, i: (0, 0)),
        ],
        out_specs=[row_spec, row_spec, kv_spec, kv_spec],
        out_shape=[
            jax.ShapeDtypeStruct((BATCH, TOTAL, ATTN_WIDTH), BF16),
            jax.ShapeDtypeStruct((BATCH, TOTAL, ATTN_WIDTH), BF16),
            kv_shape, kv_shape,
        ],
        compiler_params=_params(("arbitrary", "arbitrary"), 48),
        name="attn_projection",
    )(x, ctx, mod, norm_w.reshape(1, D_MODEL), cos_tab, sin_tab, w_in)


def _attn_kernel(q_ref, z_ref, kl_ref, kc_ref, kr_ref, kx_ref, vl_ref, vc_ref, vr_ref, vx_ref,
                 sink_ref, x_ref, c_ref, mod_ref, wo_ref, o_ref):
    i = pl.program_id(1)
    is_lat = i >= N_CTX_BLOCKS
    n = i - N_CTX_BLOCKS
    far = 4 * BLOCK
    lim_left = jnp.where(jnp.logical_and(is_lat, n >= 1), 0, far)
    lim_right = jnp.where(jnp.logical_and(is_lat, n <= SEQ // BLOCK - 2), 0, -far)
    lim_center = jnp.where(is_lat, far, -far)
    row = lax.broadcasted_iota(jnp.int32, (BLOCK, 2 * BLOCK), 0)
    lane = lax.broadcasted_iota(jnp.int32, (BLOCK, 2 * BLOCK), 1)
    d = (lane % BLOCK) - row
    masks = [d >= lim_left, d <= lim_center, d <= lim_right, None, None]
    low = lax.broadcasted_iota(jnp.int32, (BLOCK, LANES), 1) < HEAD_DIM

    def kpiece(p, m):
        if p == 0:
            return kl_ref[0, 0, m], vl_ref[0, 0, m]
        if p == 1:
            return kc_ref[0, 0, m], vc_ref[0, 0, m]
        if p == 2:
            return kr_ref[0, 0, m], vr_ref[0, 0, m]
        return kx_ref[0, p - 3, m], vx_ref[0, p - 3, m]

    n_piece = 3 + N_CTX_BLOCKS
    outs = []
    for j in range(N_SLOTS):
        m = j // (N_SLOTS // (KV_WIDTH // LANES))
        sl = slice(LANES * j, LANES * (j + 1))
        qj = q_ref[0, :, sl]
        s_list = []
        for p in range(n_piece):
            kbd, _ = kpiece(p, m)
            s = lax.dot_general(qj, kbd, (((1,), (1,)), ((), ())), preferred_element_type=F32)
            if masks[p] is not None:
                s = jnp.where(masks[p], s, NEG_INF)
            s_list.append(s)
        mx = s_list[0]
        for s in s_list[1:]:
            mx = jnp.maximum(mx, s)
        sink_a = sink_ref[2 * j:2 * j + 1, 0:1]
        sink_b = sink_ref[2 * j + 1:2 * j + 2, 0:1]
        m_a = jnp.maximum(jnp.max(mx[:, :BLOCK], axis=1, keepdims=True), sink_a)
        m_b = jnp.maximum(jnp.max(mx[:, BLOCK:], axis=1, keepdims=True), sink_b)
        acc = jnp.zeros((BLOCK, LANES), F32)
        esum_a = jnp.zeros((BLOCK, BLOCK), F32)
        esum_b = jnp.zeros((BLOCK, BLOCK), F32)
        for p in range(n_piece):
            _, vbd = kpiece(p, m)
            e_a = jnp.exp(s_list[p][:, :BLOCK] - m_a)
            e_b = jnp.exp(s_list[p][:, BLOCK:] - m_b)
            esum_a = esum_a + e_a
            esum_b = esum_b + e_b
            pb = jnp.concatenate([e_a, e_b], axis=1).astype(BF16)
            acc = acc + jnp.dot(pb, vbd, preferred_element_type=F32)
        l_a = jnp.sum(esum_a, axis=1, keepdims=True) + jnp.exp(sink_a - m_a)
        l_b = jnp.sum(esum_b, axis=1, keepdims=True) + jnp.exp(sink_b - m_b)
        inv = jnp.where(low, 1.0 / l_a, 1.0 / l_b)
        zj = z_ref[0, :, sl].astype(F32)
        outs.append((acc * inv * (zj * jax.nn.sigmoid(zj))).astype(BF16))
    g = jnp.concatenate(outs, axis=1)
    y = jnp.dot(g, wo_ref[...], preferred_element_type=F32)
    resid = jnp.where(is_lat, x_ref[0], c_ref[0])
    o_ref[0] = resid + mod_ref[0, 0, 2:3, :] * y


def _attention(q, z, kbd, vbd, sink_tab, x, ctx, mod, w_out):
    last = N_BLOCKS - 1
    n_m = KV_WIDTH // LANES
    row_spec = pl.BlockSpec((1, BLOCK, ATTN_WIDTH), lambda b, i: (b, i, 0))

    def kv_spec(off):
        return pl.BlockSpec((1, 1, n_m, 2 * BLOCK, LANES),
                            lambda b, i: (b, jnp.clip(i + off, 0, last), 0, 0, 0))

    ctx_kv_spec = pl.BlockSpec((1, N_CTX_BLOCKS, n_m, 2 * BLOCK, LANES),
                               lambda b, i: (b, 0, 0, 0, 0))
    return pl.pallas_call(
        _attn_kernel,
        grid=(BATCH, N_BLOCKS),
        in_specs=[
            row_spec, row_spec,
            kv_spec(-1), kv_spec(0), kv_spec(1), ctx_kv_spec,
            kv_spec(-1), kv_spec(0), kv_spec(1), ctx_kv_spec,
            pl.BlockSpec((2 * N_SLOTS, LANES), lambda b, i: (0, 0)),
            pl.BlockSpec((1, BLOCK, D_MODEL),
                         lambda b, i: (b, jnp.maximum(i - N_CTX_BLOCKS, 0), 0)),
            pl.BlockSpec((1, BLOCK, D_MODEL),
                         lambda b, i: (b, jnp.minimum(i, N_CTX_BLOCKS - 1), 0)),
            pl.BlockSpec((1, 1, SUBLANES, D_MODEL),
                         lambda b, i: (b, jnp.minimum(i // N_CTX_BLOCKS, 1), 0, 0)),
            pl.BlockSpec((ATTN_WIDTH, D_MODEL), lambda b, i: (0, 0)),
        ],
        out_specs=pl.BlockSpec((1, BLOCK, D_MODEL), lambda b, i: (b, i, 0)),
        out_shape=jax.ShapeDtypeStruct((BATCH, TOTAL, D_MODEL), F32),
        compiler_params=_params(("arbitrary", "arbitrary"), 48),
        name="window_attention",
    )(q, z, kbd, kbd, kbd, kbd, vbd, vbd, vbd, vbd, sink_tab, x, ctx, mod, w_out)


def _proj1_kernel(x_ref, mod_ref, nw_ref, w_ref, u_ref, z_ref):
    h = _modulated_norm(x_ref[0], nw_ref[...], mod_ref).astype(BF16)
    u_ref[0] = jnp.dot(h, w_ref[:, :D_MODEL], preferred_element_type=F32).astype(BF16)
    z_ref[0] = jnp.dot(h, w_ref[:, D_MODEL:], preferred_element_type=F32).astype(BF16)


def _proj1(xc, mod, norm_w, w_in):
    row_spec = pl.BlockSpec((1, ROW_TILE, D_MODEL), lambda b, i: (b, i, 0))
    return pl.pallas_call(
        _proj1_kernel,
        grid=(BATCH, N_ROW_TILES),
        in_specs=[
            row_spec,
            pl.BlockSpec((1, 1, SUBLANES, D_MODEL), lambda b, i: (b, jnp.minimum(i, 1), 0, 0)),
            pl.BlockSpec((1, D_MODEL), lambda b, i: (0, 0)),
            pl.BlockSpec((D_MODEL, 2 * D_MODEL), lambda b, i: (0, 0)),
        ],
        out_specs=[row_spec, row_spec],
        out_shape=[jax.ShapeDtypeStruct((BATCH, TOTAL, D_MODEL), BF16)] * 2,
        compiler_params=_params(("arbitrary", "arbitrary"), 48),
        name="ssm_projection",
    )(xc, mod, norm_w.reshape(1, D_MODEL), w_in)


def _s5_kernel(u_ref, m_ref, ws_ref, wy_ref, cst_ref, y_ref, s4_ref, xp_ref):
    for g in range(GROUP_BATCH):
        s4_ref[g] = jnp.dot(u_ref[g], ws_ref[g], preferred_element_type=F32)
    upper = lax.broadcasted_iota(jnp.int32, (SUBLANES, STATE_W), 0) < BATCH
    half = SUBLANES // 2
    fwd = slice(0, STATE_W)
    bwd = slice(STATE_W, 2 * STATE_W)
    fwd_sw = slice(2 * STATE_W, 3 * STATE_W)
    bwd_sw = slice(3 * STATE_W, 4 * STATE_W)

    def step(j, carry):
        jb = jnp.where(j < N_CTX_SCAN_BLOCKS, N_CTX_SCAN_BLOCKS - 1 - j,
                       N_SCAN_BLOCKS - 1 + N_CTX_SCAN_BLOCKS - j)
        rf = pl.ds(pl.multiple_of(j * SUBLANES, SUBLANES), SUBLANES)
        rb = pl.ds(pl.multiple_of(jb * SUBLANES, SUBLANES), SUBLANES)
        new = []
        for g in range(GROUP_BATCH):
            cf, cfs, cb, cbs = carry[4 * g:4 * g + 4]
            p1f, p2f, q1f, q2f = cst_ref[g, 0], cst_ref[g, 1], cst_ref[g, 2], cst_ref[g, 3]
            p1b, p2b, q1b, q2b = cst_ref[g, 4], cst_ref[g, 5], cst_ref[g, 6], cst_ref[g, 7]
            zf = s4_ref[g, rf, fwd]
            zfs = s4_ref[g, rf, fwd_sw]
            rzf = pltpu.roll(zf, half, 0)
            rzfs = pltpu.roll(zfs, half, 0)
            xf = p1f * cf + p2f * cfs + (zf + q1f * rzf + q2f * rzfs)
            xfs = p1f * cfs - p2f * cf + (zfs + q1f * rzfs - q2f * rzf)
            rxf = pltpu.roll(xf, half, 0)
            rxfs = pltpu.roll(xfs, half, 0)
            xp_ref[g, rf, fwd] = jnp.where(upper, cf, rxf)
            new += [jnp.where(upper, rxf, xf), jnp.where(upper, rxfs, xfs)]
            zb = s4_ref[g, rb, bwd]
            zbs = s4_ref[g, rb, bwd_sw]
            rzb = pltpu.roll(zb, half, 0)
            rzbs = pltpu.roll(zbs, half, 0)
            xb = p1b * cb + p2b * cbs + (zb + q1b * rzb + q2b * rzbs)
            xbs = p1b * cbs - p2b * cb + (zbs + q1b * rzbs - q2b * rzb)
            rxb = pltpu.roll(xb, half, 0)
            rxbs = pltpu.roll(xbs, half, 0)
            xp_ref[g, rb, bwd] = jnp.where(upper, rxb, cb)
            new += [jnp.where(upper, xb, rxb), jnp.where(upper, xbs, rxbs)]
        return tuple(new)

    zero = jnp.zeros((SUBLANES, STATE_W), F32)
    lax.fori_loop(0, N_SCAN_BLOCKS, step, (zero,) * (4 * GROUP_BATCH))
    for g in range(GROUP_BATCH):
        y_ref[g] = (
            jnp.dot(u_ref[g, CTX_SCAN_ROWS:, :], m_ref[g], preferred_element_type=F32)
            + jnp.dot(xp_ref[g, CTX_SCAN_ROWS:, :].astype(BF16), wy_ref[g],
                      preferred_element_type=F32))


def _s5_core(u_g, m_mat, ws_mat, wy_mat, consts):
    lat_rows = SCAN_ROWS - CTX_SCAN_ROWS

    def gspec(*tail):
        return pl.BlockSpec((GROUP_BATCH,) + tail, lambda i: (i,) + (0,) * len(tail))

    return pl.pallas_call(
        _s5_kernel,
        grid=(SSM_GROUPS // GROUP_BATCH,),
        in_specs=[
            gspec(SCAN_ROWS, CHUNK_W),
            gspec(CHUNK_W, CHUNK_W),
            gspec(CHUNK_W, 4 * STATE_W),
            gspec(2 * STATE_W, CHUNK_W),
            gspec(8, SUBLANES, STATE_W),
        ],
        out_specs=gspec(lat_rows, CHUNK_W),
        out_shape=jax.ShapeDtypeStruct((SSM_GROUPS, lat_rows, CHUNK_W), F32),
        scratch_shapes=[
            pltpu.VMEM((GROUP_BATCH, SCAN_ROWS, 4 * STATE_W), F32),
            pltpu.VMEM((GROUP_BATCH, SCAN_ROWS, 2 * STATE_W), F32),
        ],
        compiler_params=_params(("arbitrary",), 48),
        name="s5_scan",
    )(u_g, m_mat, ws_mat, wy_mat, consts)


def _out1_kernel(y_ref, z_ref, x_ref, mod_ref, wg_ref, wo_ref, fnw_ref, o_ref):
    y = y_ref[0]
    g = (0.5 * y * (1.0 + lax.erf(y * (2.0 ** -0.5)))).astype(BF16)
    t = jnp.dot(g, wg_ref[...], preferred_element_type=F32)
    z = z_ref[0].astype(F32)
    r = t[:, :D_MODEL] * jax.nn.sigmoid(t[:, D_MODEL:]) * (z * jax.nn.sigmoid(z))
    o = jnp.dot(r.astype(BF16), wo_ref[...], preferred_element_type=F32)
    x2 = x_ref[0] + mod_ref[0, 0, 2:3, :] * o
    ms = jnp.mean(x2 * x2, axis=-1, keepdims=True)
    o_ref[0] = x2 * lax.rsqrt(ms + NORM_EPS) * fnw_ref[...]


def _out1(y, z, xc, mod, w_glu, w_out, final_norm_w):
    ctx_tiles = CTX_LEN // ROW_TILE
    lat_spec = pl.BlockSpec((1, ROW_TILE, D_MODEL), lambda b, i: (b, i, 0))
    all_spec = pl.BlockSpec((1, ROW_TILE, D_MODEL), lambda b, i: (b, i + ctx_tiles, 0))
    return pl.pallas_call(
        _out1_kernel,
        grid=(BATCH, SEQ // ROW_TILE),
        in_specs=[
            lat_spec, all_spec, all_spec,
            pl.BlockSpec((1, 1, SUBLANES, D_MODEL), lambda b, i: (b, 1, 0, 0)),
            pl.BlockSpec((D_MODEL, 2 * D_MODEL), lambda b, i: (0, 0)),
            pl.BlockSpec((D_MODEL, D_MODEL), lambda b, i: (0, 0)),
            pl.BlockSpec((1, D_MODEL), lambda b, i: (0, 0)),
        ],
        out_specs=lat_spec,
        out_shape=jax.ShapeDtypeStruct((BATCH, SEQ, D_MODEL), F32),
        compiler_params=_params(("arbitrary", "arbitrary"), 48),
        name="ssm_output",
    )(y, z, xc, mod, w_glu, w_out, final_norm_w.reshape(1, D_MODEL))


def _head_dim_perm():
    w = np.arange(HEAD_DIM)
    half, axis, f = w // 32, (w % 32) // 16, w % 16
    return axis * 32 + half * 16 + f


def _attn_layout():
    perm = _head_dim_perm()
    n_m = KV_WIDTH // LANES
    per_m = N_SLOTS // n_m
    q_cols, o_cols, heads = [], [], []
    for j in range(N_SLOTS):
        m, gi = j // per_m, j % per_m
        for hs in range(2):
            head = (N_HEADS // n_m) * m + (N_HEADS // N_KV_HEADS) * hs + gi
            heads.append(head)
            q_cols.append(head * HEAD_DIM + perm)
            o_cols.append(head * HEAD_DIM + np.arange(HEAD_DIM))
    k_cols = [ATTN_WIDTH + kh * HEAD_DIM + perm for kh in range(N_KV_HEADS)]
    q_cols = np.concatenate(q_cols)
    o_cols = np.concatenate(o_cols)
    k_cols = np.concatenate(k_cols)
    v_cols = ATTN_WIDTH + KV_WIDTH + np.arange(KV_WIDTH)
    z_cols = ATTN_WIDTH + 2 * KV_WIDTH + o_cols
    w_cols = np.concatenate([q_cols, k_cols, v_cols, z_cols])
    return w_cols, o_cols, np.array(heads)


def _rope_tables():
    inv = ROPE_BASE ** (-jnp.arange(ROPE_FREQS, dtype=F32) / ROPE_FREQS)
    pos = jnp.arange(SEQ)
    row = (pos // GRID_W).astype(F32)[:, None] * inv
    col = (pos % GRID_W).astype(F32)[:, None] * inv
    lane = np.arange(LANES)
    w = lane % HEAD_DIM
    half, axis, f = w // 32, (w % 32) // 16, w % 16
    ang = jnp.where(jnp.asarray(axis == 0)[None, :], row[:, f], col[:, f])
    sign = jnp.asarray(np.where(half == 0, -1.0, 1.0), F32)[None, :]
    cos = jnp.concatenate([jnp.ones((CTX_LEN, LANES), F32), jnp.cos(ang)], axis=0)
    sin = jnp.concatenate([jnp.zeros((CTX_LEN, LANES), F32), jnp.sin(ang) * sign], axis=0)
    return cos, sin


def _s5_operators(lam_re, lam_im, log_dt, b_re, b_im, c_re, c_im, d_skip):
    hi = lax.Precision.HIGHEST
    t_len = CHUNK
    lam = lax.complex(lam_re.astype(F32), lam_im.astype(F32))
    dt = jnp.exp(log_dt.astype(F32))[..., None]
    log_a = lam * dt
    a_bar = jnp.exp(log_a)
    b_bar = ((a_bar - 1.0) / lam)[..., None] * lax.complex(b_re.astype(F32), b_im.astype(F32))
    c_mat = lax.complex(c_re.astype(F32), c_im.astype(F32))
    ks = jnp.arange(2 * t_len + 1, dtype=F32)
    a_pow = jnp.exp(log_a[..., None] * ks)

    def split(zc):
        return jnp.real(zc), jnp.imag(zc)

    cr, ci = split(c_mat)
    pbr, pbi = split(a_pow[..., :t_len, None] * b_bar[:, :, :, None, :])
    kern = (jnp.einsum('dghp,dgpkj->dgkhj', cr, pbr, precision=hi)
            - jnp.einsum('dghp,dgpkj->dgkhj', ci, pbi, precision=hi))
    s_idx = np.arange(t_len)[:, None]
    t_idx = np.arange(t_len)[None, :]
    lag_f = t_idx - s_idx
    lag_b = s_idx - t_idx
    kf = kern[0][:, np.clip(lag_f, 0, t_len - 1)] * jnp.asarray(lag_f >= 0, F32)[None, :, :, None, None]
    kb = kern[1][:, np.clip(lag_b, 0, t_len - 1)] * jnp.asarray(lag_b >= 0, F32)[None, :, :, None, None]
    m5 = jnp.transpose(kf + kb, (0, 1, 4, 2, 3))
    eye_t = jnp.eye(t_len, dtype=F32)
    eye_h = jnp.eye(SSM_GROUP, dtype=F32)
    d_g = d_skip.astype(F32).reshape(SSM_GROUPS, SSM_GROUP)
    m5 = m5 + (eye_t[None, :, None, :, None] * eye_h[None, None, :, None, :]
               * d_g[:, None, None, None, :])
    m_mat = m5.reshape(SSM_GROUPS, CHUNK_W, CHUNK_W)

    pow_f = a_pow[0][..., :t_len][..., ::-1]
    pow_b = a_pow[1][..., :t_len]
    wsf = jnp.transpose(pow_f[..., None] * b_bar[0][:, :, None, :], (0, 2, 3, 1))
    wsb = jnp.transpose(pow_b[..., None] * b_bar[1][:, :, None, :], (0, 2, 3, 1))
    fr, fi = split(wsf)
    br, bi = split(wsb)
    ws_mat = jnp.concatenate([fr, fi, br, bi, fi, fr, bi, br], axis=-1)
    ws_mat = ws_mat.reshape(SSM_GROUPS, CHUNK_W, 4 * STATE_W)

    cf = c_mat[0][:, :, :, None] * a_pow[0][:, None, :, 1:t_len + 1]
    cb = c_mat[1][:, :, :, None] * a_pow[1][:, None, :, 1:t_len + 1][..., ::-1]
    cfr, cfi = split(jnp.transpose(cf, (0, 2, 3, 1)))
    cbr, cbi = split(jnp.transpose(cb, (0, 2, 3, 1)))
    wy_mat = jnp.concatenate([cfr, -cfi, cbr, -cbi], axis=1)
    wy_mat = wy_mat.reshape(SSM_GROUPS, 2 * STATE_W, CHUNK_W)

    at1 = a_pow[..., t_len]
    at2 = a_pow[..., 2 * t_len]
    zero = jnp.zeros_like(at1)

    def lanes(zc):
        re, im = split(zc)
        return (jnp.concatenate([re, re], axis=-1)[:, None, :],
                jnp.concatenate([-im, im], axis=-1)[:, None, :])

    def rows(top, bot):
        t1, t2 = lanes(top)
        b1, b2 = lanes(bot)
        rep = (1, SUBLANES // 2, 1)
        return (jnp.concatenate([jnp.tile(t1, rep), jnp.tile(b1, rep)], axis=1),
                jnp.concatenate([jnp.tile(t2, rep), jnp.tile(b2, rep)], axis=1))

    p1f, p2f = rows(at1[0], at2[0])
    q1f, q2f = rows(zero[0], at1[0])
    p1b, p2b = rows(at2[1], at1[1])
    q1b, q2b = rows(at1[1], zero[1])
    consts = jnp.stack([p1f, p2f, q1f, q2f, p1b, p2b, q1b, q2b], axis=1)
    return m_mat.astype(BF16), ws_mat.astype(BF16), wy_mat.astype(BF16), consts


def kernel(x, c, ctx, c_ctx, norm_w, w_ada, b_ada, attn_w_in, attn_sink, attn_w_out,
           ssm_w_in, ssm_lam_re, ssm_lam_im, ssm_log_dt, ssm_b_re, ssm_b_im, ssm_c_re, ssm_c_im,
           ssm_d, ssm_w_glu, ssm_w_out, final_norm_w):
    mod0, mod1 = _modulation(c, c_ctx, w_ada, b_ada)

    w_cols, o_cols, heads = _attn_layout()
    w_in0 = attn_w_in[0][:, w_cols].astype(BF16)
    w_out0 = attn_w_out[0][o_cols, :].astype(BF16)
    sink_tab = jnp.broadcast_to(attn_sink[0].astype(F32)[heads][:, None], (2 * N_SLOTS, LANES))
    cos_tab, sin_tab = _rope_tables()
    q, z0, kbd, vbd = _proj0(x, ctx, mod0, norm_w[0], cos_tab, sin_tab, w_in0)
    xc1 = _attention(q, z0, kbd, vbd, sink_tab, x, ctx, mod0, w_out0)

    u, z1 = _proj1(xc1, mod1, norm_w[1], ssm_w_in[0].astype(BF16))
    m_mat, ws_mat, wy_mat, consts = _s5_operators(
        ssm_lam_re[0], ssm_lam_im[0], ssm_log_dt[0], ssm_b_re[0], ssm_b_im[0],
        ssm_c_re[0], ssm_c_im[0], ssm_d[0])
    u_g = u.reshape(BATCH, N_CHUNKS, CHUNK, SSM_GROUPS, SSM_GROUP)
    u_g = jnp.transpose(u_g, (3, 1, 0, 2, 4)).reshape(SSM_GROUPS, SCAN_ROWS, CHUNK_W)
    y_g = _s5_core(u_g, m_mat, ws_mat, wy_mat, consts)
    y = y_g.reshape(SSM_GROUPS, N_CHUNKS - N_CTX_CHUNKS, BATCH, CHUNK, SSM_GROUP)
    y = jnp.transpose(y, (2, 1, 3, 0, 4)).reshape(BATCH, SEQ, D_MODEL)
    return _out1(y, z1, xc1, mod1, ssm_w_glu[0].astype(BF16), ssm_w_out[0].astype(BF16),
                 final_norm_w)
```

```python
import functools
import math

import jax
import jax.numpy as jnp
import numpy as np
from jax import lax
from jax.experimental import pallas as pl
from jax.experimental.pallas import tpu as pltpu

F32 = jnp.float32
BF16 = jnp.bfloat16

D_MODEL = 1024
BATCH = 4
SEQ = 4096
GRID_W = 64
CTX_LEN = 256
TOTAL = CTX_LEN + SEQ
HEAD_DIM = 64
N_HEADS = 16
N_KV_HEADS = 4
ATTN_WIDTH = N_HEADS * HEAD_DIM
KV_WIDTH = N_KV_HEADS * HEAD_DIM
BLOCK = 128
N_BLOCKS = TOTAL // BLOCK
N_CTX_BLOCKS = CTX_LEN // BLOCK
ROPE_BASE = 10000.0
ROPE_FREQS = HEAD_DIM // 4
SSM_GROUP = 16
SSM_GROUPS = D_MODEL // SSM_GROUP
SSM_STATE = 64
NORM_EPS = 1e-6
NEG_INF = -1e30

LANES = 128
SUBLANES = 8
N_SLOTS = ATTN_WIDTH // LANES
ROW_TILE = 256
N_ROW_TILES = TOTAL // ROW_TILE
CHUNK = 16
N_CHUNKS = TOTAL // CHUNK
N_CTX_CHUNKS = CTX_LEN // CHUNK
CHUNK_W = CHUNK * SSM_GROUP
CHUNKS_PER_TILE = ROW_TILE // CHUNK
PAIRS_PER_TILE = CHUNKS_PER_TILE // 2
STATE_W = 2 * SSM_STATE
SCAN_ROWS = N_CHUNKS * BATCH
CTX_SCAN_ROWS = N_CTX_CHUNKS * BATCH
N_SCAN_BLOCKS = SCAN_ROWS // SUBLANES
N_CTX_SCAN_BLOCKS = CTX_SCAN_ROWS // SUBLANES
GROUP_BATCH = 4

assert BATCH * 2 == SUBLANES


def _params(semantics, vmem_mb):
    return pltpu.CompilerParams(dimension_semantics=semantics,
                                vmem_limit_bytes=vmem_mb * 1024 * 1024)


def _mod_kernel(c_ref, w_ref, b_ref, o_ref):
    c = c_ref[...]
    a = c * jax.nn.sigmoid(c)
    o_ref[0] = jnp.dot(a, w_ref[0], preferred_element_type=F32,
                       precision=lax.Precision.HIGHEST) + b_ref[0]


def _modulation(c, c_ctx, w_ada, b_ada):
    depth = w_ada.shape[0]
    rows = jnp.zeros((SUBLANES, D_MODEL), F32).at[:BATCH].set(c).at[BATCH].set(c_ctx)
    n_col = 3
    out = pl.pallas_call(
        _mod_kernel,
        grid=(depth, n_col),
        in_specs=[
            pl.BlockSpec((SUBLANES, D_MODEL), lambda l, j: (0, 0)),
            pl.BlockSpec((1, D_MODEL, D_MODEL), lambda l, j: (l, 0, j)),
            pl.BlockSpec((1, 1, D_MODEL), lambda l, j: (l, 0, j)),
        ],
        out_specs=pl.BlockSpec((1, SUBLANES, D_MODEL), lambda l, j: (l, 0, j)),
        out_shape=jax.ShapeDtypeStruct((depth, SUBLANES, 3 * D_MODEL), F32),
        compiler_params=_params(("arbitrary", "arbitrary"), 32),
        name="adaln_modulation",
    )(rows, w_ada, b_ada.reshape(depth, 1, 3 * D_MODEL))
    tabs = []
    for l in range(depth):
        lat = out[l, :BATCH].reshape(BATCH, 3, D_MODEL)
        cx = jnp.broadcast_to(out[l, BATCH].reshape(1, 3, D_MODEL), (BATCH, 3, D_MODEL))
        tab = jnp.stack([cx, lat], axis=1)
        tabs.append(jnp.pad(tab, ((0, 0), (0, 0), (0, SUBLANES - 3), (0, 0))))
    return tabs


def _modulated_norm(xt, nw, mod_ref, b=0):
    ms = jnp.mean(xt * xt, axis=-1, keepdims=True)
    y = xt * lax.rsqrt(ms + NORM_EPS) * nw
    return y * (1.0 + mod_ref[b, 0, 1:2, :]) + mod_ref[b, 0, 0:1, :]


def _lane_block_transpose(vs):
    n = len(vs)
    width = LANES // n
    blk = lax.broadcasted_iota(jnp.int32, vs[0].shape, 1) // width
    x = list(vs)
    d = n // 2
    while d >= 1:
        clear = (blk & d) == 0
        y = list(x)
        for i in range(n):
            if i & d == 0:
                a, b = x[i], x[i + d]
                y[i] = jnp.where(clear, a, pltpu.roll(b, width * d, 1))
                y[i + d] = jnp.where(clear, pltpu.roll(a, LANES - width * d, 1), b)
        x = y
        d //= 2
    return x


def _chunk_slot(b, cc):
    return (cc // 2) * (2 * BATCH) + 2 * b + (cc % 2)


def _proj0_kernel(x_ref, c_ref, mod_ref, nw_ref, cos_ref, sin_ref, w_ref,
                  q_ref, z_ref, kbd_ref, vbd_ref):
    i = pl.program_id(1)
    xt = jnp.where(i == 0, c_ref[0], x_ref[0])
    h = _modulated_norm(xt, nw_ref[...], mod_ref).astype(BF16)
    cos = cos_ref[...]
    sin = sin_ref[...]
    lane = lax.broadcasted_iota(jnp.int32, (ROW_TILE, LANES), 1)
    first_half = (lane % HEAD_DIM) < (HEAD_DIM // 2)
    low = lax.broadcasted_iota(jnp.int32, (BLOCK, LANES), 1) < HEAD_DIM

    def rope(t):
        partner = jnp.where(first_half, pltpu.roll(t, LANES - HEAD_DIM // 2, 1),
                            pltpu.roll(t, HEAD_DIM // 2, 1))
        return t * cos + partner * sin

    q = jnp.dot(h, w_ref[:, :ATTN_WIDTH], preferred_element_type=F32)
    for j in range(N_SLOTS):
        sl = slice(LANES * j, LANES * (j + 1))
        q_ref[0, :, sl] = (rope(q[:, sl]) * (HEAD_DIM ** -0.5)).astype(BF16)
    k = jnp.dot(h, w_ref[:, ATTN_WIDTH:ATTN_WIDTH + KV_WIDTH], preferred_element_type=F32)
    v = jnp.dot(h, w_ref[:, ATTN_WIDTH + KV_WIDTH:ATTN_WIDTH + 2 * KV_WIDTH],
                preferred_element_type=F32)
    for m in range(KV_WIDTH // LANES):
        sl = slice(LANES * m, LANES * (m + 1))
        kr = rope(k[:, sl])
        vm = v[:, sl]
        for blk in range(ROW_TILE // BLOCK):
            rows = slice(BLOCK * blk, BLOCK * (blk + 1))
            kbd_ref[0, blk, m, :BLOCK, :] = jnp.where(low, kr[rows], 0.0).astype(BF16)
            kbd_ref[0, blk, m, BLOCK:, :] = jnp.where(low, 0.0, kr[rows]).astype(BF16)
            vbd_ref[0, blk, m, :BLOCK, :] = jnp.where(low, vm[rows], 0.0).astype(BF16)
            vbd_ref[0, blk, m, BLOCK:, :] = jnp.where(low, 0.0, vm[rows]).astype(BF16)
    z = jnp.dot(h, w_ref[:, ATTN_WIDTH + 2 * KV_WIDTH:], preferred_element_type=F32)
    z_ref[0] = z.astype(BF16)


def _proj0(x, ctx, mod, norm_w, cos_tab, sin_tab, w_in):
    n_col = w_in.shape[1]
    blocks_per_tile = ROW_TILE // BLOCK
    kv_shape = jax.ShapeDtypeStruct((BATCH, N_BLOCKS, KV_WIDTH // LANES, 2 * BLOCK, LANES), BF16)
    kv_spec = pl.BlockSpec((1, blocks_per_tile, KV_WIDTH // LANES, 2 * BLOCK, LANES),
                           lambda b, i: (b, i, 0, 0, 0))
    row_spec = pl.BlockSpec((1, ROW_TILE, D_MODEL), lambda b, i: (b, i, 0))
    return pl.pallas_call(
        _proj0_kernel,
        grid=(BATCH, N_ROW_TILES),
        in_specs=[
            pl.BlockSpec((1, ROW_TILE, D_MODEL), lambda b, i: (b, jnp.maximum(i - 1, 0), 0)),
            pl.BlockSpec((1, ROW_TILE, D_MODEL), lambda b, i: (b, 0, 0)),
            pl.BlockSpec((1, 1, SUBLANES, D_MODEL), lambda b, i: (b, jnp.minimum(i, 1), 0, 0)),
            pl.BlockSpec((1, D_MODEL), lambda b, i: (0, 0)),
            pl.BlockSpec((ROW_TILE, LANES), lambda b, i: (i, 0)),
            pl.BlockSpec((ROW_TILE, LANES), lambda b, i: (i, 0)),
            pl.BlockSpec((D_MODEL, n_col), lambda b, i: (0, 0)),
        ],
        out_specs=[row_spec, row_spec, kv_spec, kv_spec],
        out_shape=[
            jax.ShapeDtypeStruct((BATCH, TOTAL, ATTN_WIDTH), BF16),
            jax.ShapeDtypeStruct((BATCH, TOTAL, ATTN_WIDTH), BF16),
            kv_shape, kv_shape,
        ],
        compiler_params=_params(("arbitrary", "arbitrary"), 48),
        name="attn_projection",
    )(x, ctx, mod, norm_w.reshape(1, D_MODEL), cos_tab, sin_tab, w_in)


def _attn_kernel(q_ref, z_ref, kl_ref, kc_ref, kr_ref, kx_ref, vl_ref, vc_ref, vr_ref, vx_ref,
                 sink_ref, x_ref, c_ref, mod_ref, wo_ref, o_ref):
    i = pl.program_id(1)
    is_lat = i >= N_CTX_BLOCKS
    n = i - N_CTX_BLOCKS
    far = 4 * BLOCK
    lim_left = jnp.where(jnp.logical_and(is_lat, n >= 1), 0, far)
    lim_right = jnp.where(jnp.logical_and(is_lat, n <= SEQ // BLOCK - 2), 0, -far)
    lim_center = jnp.where(is_lat, far, -far)
    row = lax.broadcasted_iota(jnp.int32, (BLOCK, 2 * BLOCK), 0)
    lane = lax.broadcasted_iota(jnp.int32, (BLOCK, 2 * BLOCK), 1)
    d = (lane % BLOCK) - row
    masks = [d >= lim_left, d <= lim_center, d <= lim_right, None, None]
    low = lax.broadcasted_iota(jnp.int32, (BLOCK, LANES), 1) < HEAD_DIM

    def kpiece(p, m):
        if p == 0:
            return kl_ref[0, 0, m], vl_ref[0, 0, m]
        if p == 1:
            return kc_ref[0, 0, m], vc_ref[0, 0, m]
        if p == 2:
            return kr_ref[0, 0, m], vr_ref[0, 0, m]
        return kx_ref[0, p - 3, m], vx_ref[0, p - 3, m]

    n_piece = 3 + N_CTX_BLOCKS
    outs = []
    for j in range(N_SLOTS):
        m = j // (N_SLOTS // (KV_WIDTH // LANES))
        sl = slice(LANES * j, LANES * (j + 1))
        qj = q_ref[0, :, sl]
        s_list = []
        for p in range(n_piece):
            kbd, _ = kpiece(p, m)
            s = lax.dot_general(qj, kbd, (((1,), (1,)), ((), ())), preferred_element_type=F32)
            if masks[p] is not None:
                s = jnp.where(masks[p], s, NEG_INF)
            s_list.append(s)
        mx = s_list[0]
        for s in s_list[1:]:
            mx = jnp.maximum(mx, s)
        sink_a = sink_ref[2 * j:2 * j + 1, 0:1]
        sink_b = sink_ref[2 * j + 1:2 * j + 2, 0:1]
        m_a = jnp.maximum(jnp.max(mx[:, :BLOCK], axis=1, keepdims=True), sink_a)
        m_b = jnp.maximum(jnp.max(mx[:, BLOCK:], axis=1, keepdims=True), sink_b)
        acc = jnp.zeros((BLOCK, LANES), F32)
        esum_a = jnp.zeros((BLOCK, BLOCK), F32)
        esum_b = jnp.zeros((BLOCK, BLOCK), F32)
        for p in range(n_piece):
            _, vbd = kpiece(p, m)
            e_a = jnp.exp(s_list[p][:, :BLOCK] - m_a)
            e_b = jnp.exp(s_list[p][:, BLOCK:] - m_b)
            esum_a = esum_a + e_a
            esum_b = esum_b + e_b
            pb = jnp.concatenate([e_a, e_b], axis=1).astype(BF16)
            acc = acc + jnp.dot(pb, vbd, preferred_element_type=F32)
        l_a = jnp.sum(esum_a, axis=1, keepdims=True) + jnp.exp(sink_a - m_a)
        l_b = jnp.sum(esum_b, axis=1, keepdims=True) + jnp.exp(sink_b - m_b)
        inv = jnp.where(low, 1.0 / l_a, 1.0 / l_b)
        zj = z_ref[0, :, sl].astype(F32)
        outs.append((acc * inv * (zj * jax.nn.sigmoid(zj))).astype(BF16))
    g = jnp.concatenate(outs, axis=1)
    y = jnp.dot(g, wo_ref[...], preferred_element_type=F32)
    resid = jnp.where(is_lat, x_ref[0], c_ref[0])
    o_ref[0] = resid + mod_ref[0, 0, 2:3, :] * y


def _attention(q, z, kbd, vbd, sink_tab, x, ctx, mod, w_out):
    last = N_BLOCKS - 1
    n_m = KV_WIDTH // LANES
    row_spec = pl.BlockSpec((1, BLOCK, ATTN_WIDTH), lambda b, i: (b, i, 0))

    def kv_spec(off):
        return pl.BlockSpec((1, 1, n_m, 2 * BLOCK, LANES),
                            lambda b, i: (b, jnp.clip(i + off, 0, last), 0, 0, 0))

    ctx_kv_spec = pl.BlockSpec((1, N_CTX_BLOCKS, n_m, 2 * BLOCK, LANES),
                               lambda b, i: (b, 0, 0, 0, 0))
    return pl.pallas_call(
        _attn_kernel,
        grid=(BATCH, N_BLOCKS),
        in_specs=[
            row_spec, row_spec,
            kv_spec(-1), kv_spec(0), kv_spec(1), ctx_kv_spec,
            kv_spec(-1), kv_spec(0), kv_spec(1), ctx_kv_spec,
            pl.BlockSpec((2 * N_SLOTS, LANES), lambda b, i: (0, 0)),
            pl.BlockSpec((1, BLOCK, D_MODEL),
                         lambda b, i: (b, jnp.maximum(i - N_CTX_BLOCKS, 0), 0)),
            pl.BlockSpec((1, BLOCK, D_MODEL),
                         lambda b, i: (b, jnp.minimum(i, N_CTX_BLOCKS - 1), 0)),
            pl.BlockSpec((1, 1, SUBLANES, D_MODEL),
                         lambda b, i: (b, jnp.minimum(i // N_CTX_BLOCKS, 1), 0, 0)),
            pl.BlockSpec((ATTN_WIDTH, D_MODEL), lambda b, i: (0, 0)),
        ],
        out_specs=pl.BlockSpec((1, BLOCK, D_MODEL), lambda b, i: (b, i, 0)),
        out_shape=jax.ShapeDtypeStruct((BATCH, TOTAL, D_MODEL), F32),
        compiler_params=_params(("arbitrary", "arbitrary"), 48),
        name="window_attention",
    )(q, z, kbd, kbd, kbd, kbd, vbd, vbd, vbd, vbd, sink_tab, x, ctx, mod, w_out)


def _proj1_kernel(x_ref, mod_ref, nw_ref, w_ref, u_ref, z_ref, scr_ref):
    h = jnp.concatenate(
        [_modulated_norm(x_ref[b], nw_ref[...], mod_ref, b).astype(BF16) for b in range(BATCH)],
        axis=0)
    z = jnp.dot(h, w_ref[:, D_MODEL:], preferred_element_type=F32)
    for b in range(BATCH):
        z_ref[b] = z[ROW_TILE * b:ROW_TILE * (b + 1)].astype(BF16)
    u = jnp.dot(h, w_ref[:, :D_MODEL], preferred_element_type=F32)
    blocks = [None] * (BATCH * CHUNKS_PER_TILE)
    for b in range(BATCH):
        for cc in range(CHUNKS_PER_TILE):
            r0 = ROW_TILE * b + CHUNK * cc
            blocks[_chunk_slot(b, cc)] = u[r0:r0 + CHUNK]
    u_perm = jnp.concatenate(blocks, axis=0)
    n_slab = D_MODEL // LANES
    for k in range(n_slab):
        scr_ref[k] = u_perm[:, LANES * k:LANES * (k + 1)]
    groups_per_slab = LANES // SSM_GROUP
    for k in range(n_slab):
        for pp in range(PAIRS_PER_TILE // 2):
            parts = []
            for p in (2 * pp, 2 * pp + 1):
                rows = [scr_ref[k, pl.ds(CHUNK * SUBLANES * p + s, SUBLANES, stride=CHUNK), :]
                        for s in range(CHUNK)]
                parts.append([_lane_block_transpose(rows[SUBLANES * m2:SUBLANES * (m2 + 1)])
                              for m2 in range(CHUNK // SUBLANES)])
            for m2 in range(CHUNK // SUBLANES):
                for gl in range(groups_per_slab):
                    val = jnp.concatenate([parts[0][m2][gl], parts[1][m2][gl]], axis=0)
                    u_ref[groups_per_slab * k + gl, 2 * SUBLANES * pp:2 * SUBLANES * (pp + 1),
                          LANES * m2:LANES * (m2 + 1)] = val.astype(BF16)


def _proj1(xc, mod, norm_w, w_in):
    row_spec = pl.BlockSpec((BATCH, ROW_TILE, D_MODEL), lambda i: (0, i, 0))
    tile_rows = PAIRS_PER_TILE * SUBLANES
    return pl.pallas_call(
        _proj1_kernel,
        grid=(N_ROW_TILES,),
        in_specs=[
            row_spec,
            pl.BlockSpec((BATCH, 1, SUBLANES, D_MODEL), lambda i: (0, jnp.minimum(i, 1), 0, 0)),
            pl.BlockSpec((1, D_MODEL), lambda i: (0, 0)),
            pl.BlockSpec((D_MODEL, 2 * D_MODEL), lambda i: (0, 0)),
        ],
        out_specs=[pl.BlockSpec((SSM_GROUPS, tile_rows, CHUNK_W), lambda i: (0, i, 0)), row_spec],
        out_shape=[jax.ShapeDtypeStruct((SSM_GROUPS, SCAN_ROWS, CHUNK_W), BF16),
                   jax.ShapeDtypeStruct((BATCH, TOTAL, D_MODEL), BF16)],
        scratch_shapes=[pltpu.VMEM((D_MODEL // LANES, BATCH * ROW_TILE, LANES), F32)],
        compiler_params=_params(("arbitrary",), 56),
        name="ssm_projection",
    )(xc, mod, norm_w.reshape(1, D_MODEL), w_in)


def _s5_kernel(u_ref, m_ref, ws_ref, wy_ref, cst_ref, y_ref, s4_ref, xp_ref):
    for g in range(GROUP_BATCH):
        s4_ref[g] = jnp.dot(u_ref[g], ws_ref[g], preferred_element_type=F32)
    even = lax.broadcasted_iota(jnp.int32, (SUBLANES, STATE_W), 0) % 2 == 0
    down = 1
    up = SUBLANES - 1
    fwd = slice(0, STATE_W)
    bwd = slice(STATE_W, 2 * STATE_W)
    fwd_sw = slice(2 * STATE_W, 3 * STATE_W)
    bwd_sw = slice(3 * STATE_W, 4 * STATE_W)

    def step(j, carry):
        jb = jnp.where(j < N_CTX_SCAN_BLOCKS, N_CTX_SCAN_BLOCKS - 1 - j,
                       N_SCAN_BLOCKS - 1 + N_CTX_SCAN_BLOCKS - j)
        rf = pl.ds(pl.multiple_of(j * SUBLANES, SUBLANES), SUBLANES)
        rb = pl.ds(pl.multiple_of(jb * SUBLANES, SUBLANES), SUBLANES)
        new = []
        for g in range(GROUP_BATCH):
            cf, cfs, cb, cbs = carry[4 * g:4 * g + 4]
            p1f, p2f, q1f, q2f = cst_ref[g, 0], cst_ref[g, 1], cst_ref[g, 2], cst_ref[g, 3]
            p1b, p2b, q1b, q2b = cst_ref[g, 4], cst_ref[g, 5], cst_ref[g, 6], cst_ref[g, 7]
            zf = s4_ref[g, rf, fwd]
            zfs = s4_ref[g, rf, fwd_sw]
            rzf = pltpu.roll(zf, down, 0)
            rzfs = pltpu.roll(zfs, down, 0)
            xf = p1f * cf + p2f * cfs + (zf + q1f * rzf + q2f * rzfs)
            xfs = p1f * cfs - p2f * cf + (zfs + q1f * rzfs - q2f * rzf)
            xp_ref[g, rf, fwd] = jnp.where(even, cf, pltpu.roll(xf, down, 0))
            new += [jnp.where(even, pltpu.roll(xf, up, 0), xf),
                    jnp.where(even, pltpu.roll(xfs, up, 0), xfs)]
            zb = s4_ref[g, rb, bwd]
            zbs = s4_ref[g, rb, bwd_sw]
            rzb = pltpu.roll(zb, up, 0)
            rzbs = pltpu.roll(zbs, up, 0)
            xb = p1b * cb + p2b * cbs + (zb + q1b * rzb + q2b * rzbs)
            xbs = p1b * cbs - p2b * cb + (zbs + q1b * rzbs - q2b * rzb)
            xp_ref[g, rb, bwd] = jnp.where(even, pltpu.roll(xb, up, 0), cb)
            new += [jnp.where(even, xb, pltpu.roll(xb, down, 0)),
                    jnp.where(even, xbs, pltpu.roll(xbs, down, 0))]
        return tuple(new)

    zero = jnp.zeros((SUBLANES, STATE_W), F32)
    lax.fori_loop(0, N_SCAN_BLOCKS, step, (zero,) * (4 * GROUP_BATCH))
    for g in range(GROUP_BATCH):
        y_ref[g] = (
            jnp.dot(u_ref[g, CTX_SCAN_ROWS:, :], m_ref[g], preferred_element_type=F32)
            + jnp.dot(xp_ref[g, CTX_SCAN_ROWS:, :].astype(BF16), wy_ref[g],
                      preferred_element_type=F32))


def _s5_core(u_g, m_mat, ws_mat, wy_mat, consts):
    lat_rows = SCAN_ROWS - CTX_SCAN_ROWS

    def gspec(*tail):
        return pl.BlockSpec((GROUP_BATCH,) + tail, lambda i: (i,) + (0,) * len(tail))

    return pl.pallas_call(
        _s5_kernel,
        grid=(SSM_GROUPS // GROUP_BATCH,),
        in_specs=[
            gspec(SCAN_ROWS, CHUNK_W),
            gspec(CHUNK_W, CHUNK_W),
            gspec(CHUNK_W, 4 * STATE_W),
            gspec(2 * STATE_W, CHUNK_W),
            gspec(8, SUBLANES, STATE_W),
        ],
        out_specs=gspec(lat_rows, CHUNK_W),
        out_shape=jax.ShapeDtypeStruct((SSM_GROUPS, lat_rows, CHUNK_W), F32),
        scratch_shapes=[
            pltpu.VMEM((GROUP_BATCH, SCAN_ROWS, 4 * STATE_W), F32),
            pltpu.VMEM((GROUP_BATCH, SCAN_ROWS, 2 * STATE_W), F32),
        ],
        compiler_params=_params(("arbitrary",), 48),
        name="s5_scan",
    )(u_g, m_mat, ws_mat, wy_mat, consts)


def _out1_kernel(y_ref, z_ref, x_ref, mod_ref, wg_ref, wo_ref, fnw_ref, o_ref, scr_ref):
    n_slab = D_MODEL // LANES
    groups_per_slab = LANES // SSM_GROUP
    for k in range(n_slab):
        for p in range(PAIRS_PER_TILE):
            for m2 in range(CHUNK // SUBLANES):
                vals = [y_ref[groups_per_slab * k + gl, SUBLANES * p:SUBLANES * (p + 1),
                              LANES * m2:LANES * (m2 + 1)] for gl in range(groups_per_slab)]
                steps = _lane_block_transpose(vals)
                for s2 in range(SUBLANES):
                    t_idx = SUBLANES * m2 + s2
                    scr_ref[k, pl.ds(CHUNK * SUBLANES * p + t_idx, SUBLANES, stride=CHUNK), :] = (
                        steps[s2])
    y_perm = jnp.concatenate([scr_ref[k] for k in range(n_slab)], axis=1)
    for b in range(BATCH):
        y = jnp.concatenate(
            [y_perm[CHUNK * _chunk_slot(b, cc):CHUNK * (_chunk_slot(b, cc) + 1)]
             for cc in range(CHUNKS_PER_TILE)], axis=0)
        g = (0.5 * y * (1.0 + lax.erf(y * (2.0 ** -0.5)))).astype(BF16)
        t = jnp.dot(g, wg_ref[...], preferred_element_type=F32)
        z = z_ref[b].astype(F32)
        r = t[:, :D_MODEL] * jax.nn.sigmoid(t[:, D_MODEL:]) * (z * jax.nn.sigmoid(z))
        o = jnp.dot(r.astype(BF16), wo_ref[...], preferred_element_type=F32)
        x2 = x_ref[b] + mod_ref[b, 0, 2:3, :] * o
        ms = jnp.mean(x2 * x2, axis=-1, keepdims=True)
        o_ref[b] = x2 * lax.rsqrt(ms + NORM_EPS) * fnw_ref[...]


def _out1(y_g, z, xc, mod, w_glu, w_out, final_norm_w):
    ctx_tiles = CTX_LEN // ROW_TILE
    tile_rows = PAIRS_PER_TILE * SUBLANES
    lat_spec = pl.BlockSpec((BATCH, ROW_TILE, D_MODEL), lambda i: (0, i, 0))
    all_spec = pl.BlockSpec((BATCH, ROW_TILE, D_MODEL), lambda i: (0, i + ctx_tiles, 0))
    return pl.pallas_call(
        _out1_kernel,
        grid=(SEQ // ROW_TILE,),
        in_specs=[
            pl.BlockSpec((SSM_GROUPS, tile_rows, CHUNK_W), lambda i: (0, i, 0)),
            all_spec, all_spec,
            pl.BlockSpec((BATCH, 1, SUBLANES, D_MODEL), lambda i: (0, 1, 0, 0)),
            pl.BlockSpec((D_MODEL, 2 * D_MODEL), lambda i: (0, 0)),
            pl.BlockSpec((D_MODEL, D_MODEL), lambda i: (0, 0)),
            pl.BlockSpec((1, D_MODEL), lambda i: (0, 0)),
        ],
        out_specs=lat_spec,
        out_shape=jax.ShapeDtypeStruct((BATCH, SEQ, D_MODEL), F32),
        scratch_shapes=[pltpu.VMEM((D_MODEL // LANES, BATCH * ROW_TILE, LANES), F32)],
        compiler_params=_params(("arbitrary",), 56),
        name="ssm_output",
    )(y_g, z, xc, mod, w_glu, w_out, final_norm_w.reshape(1, D_MODEL))


def _head_dim_perm():
    w = np.arange(HEAD_DIM)
    half, axis, f = w // 32, (w % 32) // 16, w % 16
    return axis * 32 + half * 16 + f


def _attn_layout():
    perm = _head_dim_perm()
    n_m = KV_WIDTH // LANES
    per_m = N_SLOTS // n_m
    q_cols, o_cols, heads = [], [], []
    for j in range(N_SLOTS):
        m, gi = j // per_m, j % per_m
        for hs in range(2):
            head = (N_HEADS // n_m) * m + (N_HEADS // N_KV_HEADS) * hs + gi
            heads.append(head)
            q_cols.append(head * HEAD_DIM + perm)
            o_cols.append(head * HEAD_DIM + np.arange(HEAD_DIM))
    k_cols = [ATTN_WIDTH + kh * HEAD_DIM + perm for kh in range(N_KV_HEADS)]
    q_cols = np.concatenate(q_cols)
    o_cols = np.concatenate(o_cols)
    k_cols = np.concatenate(k_cols)
    v_cols = ATTN_WIDTH + KV_WIDTH + np.arange(KV_WIDTH)
    z_cols = ATTN_WIDTH + 2 * KV_WIDTH + o_cols
    w_cols = np.concatenate([q_cols, k_cols, v_cols, z_cols])
    return w_cols, o_cols, np.array(heads)


def _rope_tables():
    inv = ROPE_BASE ** (-jnp.arange(ROPE_FREQS, dtype=F32) / ROPE_FREQS)
    pos = jnp.arange(SEQ)
    row = (pos // GRID_W).astype(F32)[:, None] * inv
    col = (pos % GRID_W).astype(F32)[:, None] * inv
    lane = np.arange(LANES)
    w = lane % HEAD_DIM
    half, axis, f = w // 32, (w % 32) // 16, w % 16
    ang = jnp.where(jnp.asarray(axis == 0)[None, :], row[:, f], col[:, f])
    sign = jnp.asarray(np.where(half == 0, -1.0, 1.0), F32)[None, :]
    cos = jnp.concatenate([jnp.ones((CTX_LEN, LANES), F32), jnp.cos(ang)], axis=0)
    sin = jnp.concatenate([jnp.zeros((CTX_LEN, LANES), F32), jnp.sin(ang) * sign], axis=0)
    return cos, sin


def _s5_operators(lam_re, lam_im, log_dt, b_re, b_im, c_re, c_im, d_skip):
    hi = lax.Precision.HIGHEST
    t_len = CHUNK
    lam = lax.complex(lam_re.astype(F32), lam_im.astype(F32))
    dt = jnp.exp(log_dt.astype(F32))[..., None]
    log_a = lam * dt
    a_bar = jnp.exp(log_a)
    b_bar = ((a_bar - 1.0) / lam)[..., None] * lax.complex(b_re.astype(F32), b_im.astype(F32))
    c_mat = lax.complex(c_re.astype(F32), c_im.astype(F32))
    ks = jnp.arange(2 * t_len + 1, dtype=F32)
    a_pow = jnp.exp(log_a[..., None] * ks)

    def split(zc):
        return jnp.real(zc), jnp.imag(zc)

    cr, ci = split(c_mat)
    pbr, pbi = split(a_pow[..., :t_len, None] * b_bar[:, :, :, None, :])
    kern = (jnp.einsum('dghp,dgpkj->dgkhj', cr, pbr, precision=hi)
            - jnp.einsum('dghp,dgpkj->dgkhj', ci, pbi, precision=hi))
    s_idx = np.arange(t_len)[:, None]
    t_idx = np.arange(t_len)[None, :]
    lag_f = t_idx - s_idx
    lag_b = s_idx - t_idx
    kf = kern[0][:, np.clip(lag_f, 0, t_len - 1)] * jnp.asarray(lag_f >= 0, F32)[None, :, :, None, None]
    kb = kern[1][:, np.clip(lag_b, 0, t_len - 1)] * jnp.asarray(lag_b >= 0, F32)[None, :, :, None, None]
    m5 = jnp.transpose(kf + kb, (0, 1, 4, 2, 3))
    eye_t = jnp.eye(t_len, dtype=F32)
    eye_h = jnp.eye(SSM_GROUP, dtype=F32)
    d_g = d_skip.astype(F32).reshape(SSM_GROUPS, SSM_GROUP)
    m5 = m5 + (eye_t[None, :, None, :, None] * eye_h[None, None, :, None, :]
               * d_g[:, None, None, None, :])
    m_mat = m5.reshape(SSM_GROUPS, CHUNK_W, CHUNK_W)

    pow_f = a_pow[0][..., :t_len][..., ::-1]
    pow_b = a_pow[1][..., :t_len]
    wsf = jnp.transpose(pow_f[..., None] * b_bar[0][:, :, None, :], (0, 2, 3, 1))
    wsb = jnp.transpose(pow_b[..., None] * b_bar[1][:, :, None, :], (0, 2, 3, 1))
    fr, fi = split(wsf)
    br, bi = split(wsb)
    ws_mat = jnp.concatenate([fr, fi, br, bi, fi, fr, bi, br], axis=-1)
    ws_mat = ws_mat.reshape(SSM_GROUPS, CHUNK_W, 4 * STATE_W)

    cf = c_mat[0][:, :, :, None] * a_pow[0][:, None, :, 1:t_len + 1]
    cb = c_mat[1][:, :, :, None] * a_pow[1][:, None, :, 1:t_len + 1][..., ::-1]
    cfr, cfi = split(jnp.transpose(cf, (0, 2, 3, 1)))
    cbr, cbi = split(jnp.transpose(cb, (0, 2, 3, 1)))
    wy_mat = jnp.concatenate([cfr, -cfi, cbr, -cbi], axis=1)
    wy_mat = wy_mat.reshape(SSM_GROUPS, 2 * STATE_W, CHUNK_W)

    at1 = a_pow[..., t_len]
    at2 = a_pow[..., 2 * t_len]
    zero = jnp.zeros_like(at1)

    def lanes(zc):
        re, im = split(zc)
        return (jnp.concatenate([re, re], axis=-1)[:, None, :],
                jnp.concatenate([-im, im], axis=-1)[:, None, :])

    def rows(top, bot):
        t1, t2 = lanes(top)
        b1, b2 = lanes(bot)
        rep = (1, SUBLANES // 2, 1)
        return (jnp.tile(jnp.concatenate([t1, b1], axis=1), rep),
                jnp.tile(jnp.concatenate([t2, b2], axis=1), rep))

    p1f, p2f = rows(at1[0], at2[0])
    q1f, q2f = rows(zero[0], at1[0])
    p1b, p2b = rows(at2[1], at1[1])
    q1b, q2b = rows(at1[1], zero[1])
    consts = jnp.stack([p1f, p2f, q1f, q2f, p1b, p2b, q1b, q2b], axis=1)
    return m_mat.astype(BF16), ws_mat.astype(BF16), wy_mat.astype(BF16), consts


def kernel(x, c, ctx, c_ctx, norm_w, w_ada, b_ada, attn_w_in, attn_sink, attn_w_out,
           ssm_w_in, ssm_lam_re, ssm_lam_im, ssm_log_dt, ssm_b_re, ssm_b_im, ssm_c_re, ssm_c_im,
           ssm_d, ssm_w_glu, ssm_w_out, final_norm_w):
    mod0, mod1 = _modulation(c, c_ctx, w_ada, b_ada)

    w_cols, o_cols, heads = _attn_layout()
    w_in0 = attn_w_in[0][:, w_cols].astype(BF16)
    w_out0 = attn_w_out[0][o_cols, :].astype(BF16)
    sink_tab = jnp.broadcast_to(attn_sink[0].astype(F32)[heads][:, None], (2 * N_SLOTS, LANES))
    cos_tab, sin_tab = _rope_tables()
    q, z0, kbd, vbd = _proj0(x, ctx, mod0, norm_w[0], cos_tab, sin_tab, w_in0)
    xc1 = _attention(q, z0, kbd, vbd, sink_tab, x, ctx, mod0, w_out0)

    u_g, z1 = _proj1(xc1, mod1, norm_w[1], ssm_w_in[0].astype(BF16))
    m_mat, ws_mat, wy_mat, consts = _s5_operators(
        ssm_lam_re[0], ssm_lam_im[0], ssm_log_dt[0], ssm_b_re[0], ssm_b_im[0],
        ssm_c_re[0], ssm_c_im[0], ssm_d[0])
    y_g = _s5_core(u_g, m_mat, ws_mat, wy_mat, consts)
    return _out1(y_g, z1, xc1, mod1, ssm_w_glu[0].astype(BF16), ssm_w_out[0].astype(BF16),
                 final_norm_w)
```

```python
import functools
import math

import jax
import jax.numpy as jnp
import numpy as np
from jax import lax
from jax.experimental import pallas as pl
from jax.experimental.pallas import tpu as pltpu

F32 = jnp.float32
BF16 = jnp.bfloat16

D_MODEL = 1024
BATCH = 4
SEQ = 4096
GRID_W = 64
CTX_LEN = 256
TOTAL = CTX_LEN + SEQ
HEAD_DIM = 64
N_HEADS = 16
N_KV_HEADS = 4
ATTN_WIDTH = N_HEADS * HEAD_DIM
KV_WIDTH = N_KV_HEADS * HEAD_DIM
BLOCK = 128
N_BLOCKS = TOTAL // BLOCK
N_CTX_BLOCKS = CTX_LEN // BLOCK
ROPE_BASE = 10000.0
ROPE_FREQS = HEAD_DIM // 4
SSM_GROUP = 16
SSM_GROUPS = D_MODEL // SSM_GROUP
SSM_STATE = 64
NORM_EPS = 1e-6
NEG_INF = -1e30

LANES = 128
SUBLANES = 8
N_SLOTS = ATTN_WIDTH // LANES
N_KV_PAIRS = KV_WIDTH // LANES
SLOTS_PER_M = N_SLOTS // N_KV_PAIRS
LOG2E = math.log2(math.e)
Q_SCALE = HEAD_DIM ** -0.5 * LOG2E
ROW_TILE = 256
N_ROW_TILES = TOTAL // ROW_TILE
CHUNK = 16
N_CHUNKS = TOTAL // CHUNK
N_CTX_CHUNKS = CTX_LEN // CHUNK
CHUNK_W = CHUNK * SSM_GROUP
CHUNKS_PER_TILE = ROW_TILE // CHUNK
PAIRS_PER_TILE = CHUNKS_PER_TILE // 2
STATE_W = 2 * SSM_STATE
SCAN_ROWS = N_CHUNKS * BATCH
CTX_SCAN_ROWS = N_CTX_CHUNKS * BATCH
N_SCAN_BLOCKS = SCAN_ROWS // SUBLANES
N_CTX_SCAN_BLOCKS = CTX_SCAN_ROWS // SUBLANES
GROUP_BATCH = 4

assert BATCH * 2 == SUBLANES


def _params(semantics, vmem_mb):
    return pltpu.CompilerParams(dimension_semantics=semantics,
                                vmem_limit_bytes=vmem_mb * 1024 * 1024)


def _mod_kernel(c_ref, w_ref, b_ref, o_ref):
    c = c_ref[...]
    a = c * jax.nn.sigmoid(c)
    o_ref[0] = jnp.dot(a, w_ref[0], preferred_element_type=F32,
                       precision=lax.Precision.HIGHEST) + b_ref[0]


def _modulation(c, c_ctx, w_ada, b_ada):
    depth = w_ada.shape[0]
    rows = jnp.zeros((SUBLANES, D_MODEL), F32).at[:BATCH].set(c).at[BATCH].set(c_ctx)
    n_col = 3
    out = pl.pallas_call(
        _mod_kernel,
        grid=(depth, n_col),
        in_specs=[
            pl.BlockSpec((SUBLANES, D_MODEL), lambda l, j: (0, 0)),
            pl.BlockSpec((1, D_MODEL, D_MODEL), lambda l, j: (l, 0, j)),
            pl.BlockSpec((1, 1, D_MODEL), lambda l, j: (l, 0, j)),
        ],
        out_specs=pl.BlockSpec((1, SUBLANES, D_MODEL), lambda l, j: (l, 0, j)),
        out_shape=jax.ShapeDtypeStruct((depth, SUBLANES, 3 * D_MODEL), F32),
        compiler_params=_params(("arbitrary", "arbitrary"), 32),
        name="adaln_modulation",
    )(rows, w_ada, b_ada.reshape(depth, 1, 3 * D_MODEL))
    tabs = []
    for l in range(depth):
        lat = out[l, :BATCH].reshape(BATCH, 3, D_MODEL)
        cx = jnp.broadcast_to(out[l, BATCH].reshape(1, 3, D_MODEL), (BATCH, 3, D_MODEL))
        tab = jnp.stack([cx, lat], axis=1)
        tabs.append(jnp.pad(tab, ((0, 0), (0, 0), (0, SUBLANES - 3), (0, 0))))
    return tabs


def _modulated_norm(xt, nw, mod_ref, b=0):
    ms = jnp.mean(xt * xt, axis=-1, keepdims=True)
    y = xt * lax.rsqrt(ms + NORM_EPS) * nw
    return y * (1.0 + mod_ref[b, 0, 1:2, :]) + mod_ref[b, 0, 0:1, :]


def _lane_block_transpose(vs):
    n = len(vs)
    width = LANES // n
    blk = lax.broadcasted_iota(jnp.int32, vs[0].shape, 1) // width
    x = list(vs)
    d = n // 2
    while d >= 1:
        clear = (blk & d) == 0
        y = list(x)
        for i in range(n):
            if i & d == 0:
                a, b = x[i], x[i + d]
                y[i] = jnp.where(clear, a, pltpu.roll(b, width * d, 1))
                y[i + d] = jnp.where(clear, pltpu.roll(a, LANES - width * d, 1), b)
        x = y
        d //= 2
    return x


def _chunk_slot(b, cc):
    return (cc // 2) * (2 * BATCH) + 2 * b + (cc % 2)


def _proj0_kernel(x_ref, c_ref, mod_ref, nw_ref, cos_ref, sin_ref, w_ref,
                  q_ref, z_ref, kbd_ref, vbd_ref):
    i = pl.program_id(1)
    xt = jnp.where(i == 0, c_ref[0], x_ref[0])
    h = _modulated_norm(xt, nw_ref[...], mod_ref).astype(BF16)
    cos = cos_ref[...]
    sin = sin_ref[...]
    lane = lax.broadcasted_iota(jnp.int32, (ROW_TILE, LANES), 1)
    first_half = (lane % HEAD_DIM) < (HEAD_DIM // 2)
    low = lax.broadcasted_iota(jnp.int32, (BLOCK, LANES), 1) < HEAD_DIM

    def rope(t):
        partner = jnp.where(first_half, pltpu.roll(t, LANES - HEAD_DIM // 2, 1),
                            pltpu.roll(t, HEAD_DIM // 2, 1))
        return t * cos + partner * sin

    q = jnp.dot(h, w_ref[:, :ATTN_WIDTH], preferred_element_type=F32)
    for j in range(N_SLOTS):
        m, gi = divmod(j, SLOTS_PER_M)
        qj = (rope(q[:, LANES * j:LANES * (j + 1)]) * Q_SCALE).astype(BF16)
        for blk in range(ROW_TILE // BLOCK):
            q_ref[0, blk, m, BLOCK * gi:BLOCK * (gi + 1), :] = qj[BLOCK * blk:BLOCK * (blk + 1)]
    k = jnp.dot(h, w_ref[:, ATTN_WIDTH:ATTN_WIDTH + KV_WIDTH], preferred_element_type=F32)
    v = jnp.dot(h, w_ref[:, ATTN_WIDTH + KV_WIDTH:ATTN_WIDTH + 2 * KV_WIDTH],
                preferred_element_type=F32)
    for m in range(KV_WIDTH // LANES):
        sl = slice(LANES * m, LANES * (m + 1))
        kr = rope(k[:, sl])
        vm = v[:, sl]
        for blk in range(ROW_TILE // BLOCK):
            rows = slice(BLOCK * blk, BLOCK * (blk + 1))
            kbd_ref[0, blk, m, :BLOCK, :] = jnp.where(low, kr[rows], 0.0).astype(BF16)
            kbd_ref[0, blk, m, BLOCK:, :] = jnp.where(low, 0.0, kr[rows]).astype(BF16)
            vbd_ref[0, blk, m, :BLOCK, :] = jnp.where(low, vm[rows], 0.0).astype(BF16)
            vbd_ref[0, blk, m, BLOCK:, :] = jnp.where(low, 0.0, vm[rows]).astype(BF16)
    z = jnp.dot(h, w_ref[:, ATTN_WIDTH + 2 * KV_WIDTH:], preferred_element_type=F32)
    z_ref[0] = z.astype(BF16)


def _proj0(x, ctx, mod, norm_w, cos_tab, sin_tab, w_in):
    n_col = w_in.shape[1]
    blocks_per_tile = ROW_TILE // BLOCK
    kv_shape = jax.ShapeDtypeStruct((BATCH, N_BLOCKS, KV_WIDTH // LANES, 2 * BLOCK, LANES), BF16)
    kv_spec = pl.BlockSpec((1, blocks_per_tile, KV_WIDTH // LANES, 2 * BLOCK, LANES),
                           lambda b, i: (b, i, 0, 0, 0))
    row_spec = pl.BlockSpec((1, ROW_TILE, D_MODEL), lambda b, i: (b, i, 0))
    return pl.pallas_call(
        _proj0_kernel,
        grid=(BATCH, N_ROW_TILES),
        in_specs=[
            pl.BlockSpec((1, ROW_TILE, D_MODEL), lambda b, i: (b, jnp.maximum(i - 1, 0), 0)),
            pl.BlockSpec((1, ROW_TILE, D_MODEL), lambda b, i: (b, 0, 0)),
            pl.BlockSpec((1, 1, SUBLANES, D_MODEL), lambda b, i: (b, jnp.minimum(i, 1), 0, 0)),
            pl.BlockSpec((1, D_MODEL), lambda b, i: (0, 0)),
            pl.BlockSpec((ROW_TILE, LANES), lambda b, i: (i, 0)),
            pl.BlockSpec((ROW_TILE, LANES), lambda b, i: (i, 0)),
            pl.BlockSpec((D_MODEL, n_col), lambda b, i: (0, 0)),
        ],
        out_specs=[
            pl.BlockSpec((1, blocks_per_tile, N_KV_PAIRS, SLOTS_PER_M * BLOCK, LANES),
                         lambda b, i: (b, i, 0, 0, 0)),
            row_spec, kv_spec, kv_spec],
        out_shape=[
            jax.ShapeDtypeStruct((BATCH, N_BLOCKS, N_KV_PAIRS, SLOTS_PER_M * BLOCK, LANES), BF16),
            jax.ShapeDtypeStruct((BATCH, TOTAL, ATTN_WIDTH), BF16),
            kv_shape, kv_shape,
        ],
        compiler_params=_params(("arbitrary", "arbitrary"), 48),
        name="attn_projection",
    )(x, ctx, mod, norm_w.reshape(1, D_MODEL), cos_tab, sin_tab, w_in)


def _attn_kernel(q_ref, z_ref, kl_ref, kc_ref, kr_ref, kx_ref, vl_ref, vc_ref, vr_ref, vx_ref,
                 sink_ref, x_ref, c_ref, mod_ref, wo_ref, o_ref):
    i = pl.program_id(1)
    is_lat = i >= N_CTX_BLOCKS
    n = i - N_CTX_BLOCKS
    far = 4 * BLOCK
    lim_left = jnp.where(jnp.logical_and(is_lat, n >= 1), 0, far)
    lim_right = jnp.where(jnp.logical_and(is_lat, n <= SEQ // BLOCK - 2), 0, -far)
    lim_center = jnp.where(is_lat, far, -far)
    q_rows = SLOTS_PER_M * BLOCK
    row = lax.broadcasted_iota(jnp.int32, (q_rows, 2 * BLOCK), 0)
    lane = lax.broadcasted_iota(jnp.int32, (q_rows, 2 * BLOCK), 1)
    d = (lane % BLOCK) - (row % BLOCK)
    masks = [d >= lim_left, d <= lim_center, d <= lim_right, None, None]
    low = lax.broadcasted_iota(jnp.int32, (BLOCK, LANES), 1) < HEAD_DIM

    def kpiece(p, m):
        if p == 0:
            return kl_ref[0, 0, m], vl_ref[0, 0, m]
        if p == 1:
            return kc_ref[0, 0, m], vc_ref[0, 0, m]
        if p == 2:
            return kr_ref[0, 0, m], vr_ref[0, 0, m]
        return kx_ref[0, p - 3, m], vx_ref[0, p - 3, m]

    n_piece = 3 + N_CTX_BLOCKS
    s_all = []
    for m in range(N_KV_PAIRS):
        qm = q_ref[0, 0, m]
        s_list = []
        for p in range(n_piece):
            kbd, _ = kpiece(p, m)
            s = lax.dot_general(qm, kbd, (((1,), (1,)), ((), ())), preferred_element_type=F32)
            if masks[p] is not None:
                s = jnp.where(masks[p], s, NEG_INF)
            s_list.append(s)
        s_all.append(s_list)
    outs = [None] * N_SLOTS
    for m in range(N_KV_PAIRS):
        s_list = s_all[m]
        mx = s_list[0]
        for s in s_list[1:]:
            mx = jnp.maximum(mx, s)

        def sink_col(hs):
            return jnp.concatenate(
                [jnp.broadcast_to(sink_ref[2 * (SLOTS_PER_M * m + gi) + hs:
                                           2 * (SLOTS_PER_M * m + gi) + hs + 1, 0:1], (BLOCK, 1))
                 for gi in range(SLOTS_PER_M)], axis=0)

        sink_a, sink_b = sink_col(0), sink_col(1)
        m_a = jnp.maximum(jnp.max(mx[:, :BLOCK], axis=1, keepdims=True), sink_a)
        m_b = jnp.maximum(jnp.max(mx[:, BLOCK:], axis=1, keepdims=True), sink_b)
        acc = jnp.zeros((q_rows, LANES), F32)
        esum_a = jnp.zeros((q_rows, BLOCK), F32)
        esum_b = jnp.zeros((q_rows, BLOCK), F32)
        for p in range(n_piece):
            _, vbd = kpiece(p, m)
            e_a = jnp.exp2(s_list[p][:, :BLOCK] - m_a)
            e_b = jnp.exp2(s_list[p][:, BLOCK:] - m_b)
            esum_a = esum_a + e_a
            esum_b = esum_b + e_b
            pb = jnp.concatenate([e_a, e_b], axis=1).astype(BF16)
            acc = acc + jnp.dot(pb, vbd, preferred_element_type=F32)
        inv_a = 1.0 / (jnp.sum(esum_a, axis=1, keepdims=True) + jnp.exp2(sink_a - m_a))
        inv_b = 1.0 / (jnp.sum(esum_b, axis=1, keepdims=True) + jnp.exp2(sink_b - m_b))
        for gi in range(SLOTS_PER_M):
            j = SLOTS_PER_M * m + gi
            rows = slice(BLOCK * gi, BLOCK * (gi + 1))
            inv = jnp.where(low, inv_a[rows], inv_b[rows])
            zj = z_ref[0, :, LANES * j:LANES * (j + 1)].astype(F32)
            outs[j] = (acc[rows] * inv * (zj * jax.nn.sigmoid(zj))).astype(BF16)
    g = jnp.concatenate(outs, axis=1)
    y = jnp.dot(g, wo_ref[...], preferred_element_type=F32)
    resid = jnp.where(is_lat, x_ref[0], c_ref[0])
    o_ref[0] = resid + mod_ref[0, 0, 2:3, :] * y


def _attention(q, z, kbd, vbd, sink_tab, x, ctx, mod, w_out):
    last = N_BLOCKS - 1
    n_m = KV_WIDTH // LANES
    row_spec = pl.BlockSpec((1, BLOCK, ATTN_WIDTH), lambda b, i: (b, i, 0))

    def kv_spec(off):
        return pl.BlockSpec((1, 1, n_m, 2 * BLOCK, LANES),
                            lambda b, i: (b, jnp.clip(i + off, 0, last), 0, 0, 0))

    ctx_kv_spec = pl.BlockSpec((1, N_CTX_BLOCKS, n_m, 2 * BLOCK, LANES),
                               lambda b, i: (b, 0, 0, 0, 0))
    return pl.pallas_call(
        _attn_kernel,
        grid=(BATCH, N_BLOCKS),
        in_specs=[
            pl.BlockSpec((1, 1, n_m, SLOTS_PER_M * BLOCK, LANES), lambda b, i: (b, i, 0, 0, 0)),
            row_spec,
            kv_spec(-1), kv_spec(0), kv_spec(1), ctx_kv_spec,
            kv_spec(-1), kv_spec(0), kv_spec(1), ctx_kv_spec,
            pl.BlockSpec((2 * N_SLOTS, LANES), lambda b, i: (0, 0)),
            pl.BlockSpec((1, BLOCK, D_MODEL),
                         lambda b, i: (b, jnp.maximum(i - N_CTX_BLOCKS, 0), 0)),
            pl.BlockSpec((1, BLOCK, D_MODEL),
                         lambda b, i: (b, jnp.minimum(i, N_CTX_BLOCKS - 1), 0)),
            pl.BlockSpec((1, 1, SUBLANES, D_MODEL),
                         lambda b, i: (b, jnp.minimum(i // N_CTX_BLOCKS, 1), 0, 0)),
            pl.BlockSpec((ATTN_WIDTH, D_MODEL), lambda b, i: (0, 0)),
        ],
        out_specs=pl.BlockSpec((1, BLOCK, D_MODEL), lambda b, i: (b, i, 0)),
        out_shape=jax.ShapeDtypeStruct((BATCH, TOTAL, D_MODEL), F32),
        compiler_params=_params(("arbitrary", "arbitrary"), 48),
        name="window_attention",
    )(q, z, kbd, kbd, kbd, kbd, vbd, vbd, vbd, vbd, sink_tab, x, ctx, mod, w_out)


def _proj1_kernel(x_ref, mod_ref, nw_ref, w_ref, u_ref, z_ref, scr_ref):
    h = jnp.concatenate(
        [_modulated_norm(x_ref[b], nw_ref[...], mod_ref, b).astype(BF16) for b in range(BATCH)],
        axis=0)
    z = jnp.dot(h, w_ref[:, D_MODEL:], preferred_element_type=F32)
    for b in range(BATCH):
        z_ref[b] = z[ROW_TILE * b:ROW_TILE * (b + 1)].astype(BF16)
    u = jnp.dot(h, w_ref[:, :D_MODEL], preferred_element_type=F32)
    blocks = [None] * (BATCH * CHUNKS_PER_TILE)
    for b in range(BATCH):
        for cc in range(CHUNKS_PER_TILE):
            r0 = ROW_TILE * b + CHUNK * cc
            blocks[_chunk_slot(b, cc)] = u[r0:r0 + CHUNK]
    u_perm = jnp.concatenate(blocks, axis=0)
    n_slab = D_MODEL // LANES
    for k in range(n_slab):
        scr_ref[k] = u_perm[:, LANES * k:LANES * (k + 1)]
    groups_per_slab = LANES // SSM_GROUP
    for k in range(n_slab):
        for pp in range(PAIRS_PER_TILE // 2):
            parts = []
            for p in (2 * pp, 2 * pp + 1):
                rows = [scr_ref[k, pl.ds(CHUNK * SUBLANES * p + s, SUBLANES, stride=CHUNK), :]
                        for s in range(CHUNK)]
                parts.append([_lane_block_transpose(rows[SUBLANES * m2:SUBLANES * (m2 + 1)])
                              for m2 in range(CHUNK // SUBLANES)])
            for m2 in range(CHUNK // SUBLANES):
                for gl in range(groups_per_slab):
                    val = jnp.concatenate([parts[0][m2][gl], parts[1][m2][gl]], axis=0)
                    u_ref[groups_per_slab * k + gl, 2 * SUBLANES * pp:2 * SUBLANES * (pp + 1),
                          LANES * m2:LANES * (m2 + 1)] = val.astype(BF16)


def _proj1(xc, mod, norm_w, w_in):
    row_spec = pl.BlockSpec((BATCH, ROW_TILE, D_MODEL), lambda i: (0, i, 0))
    tile_rows = PAIRS_PER_TILE * SUBLANES
    return pl.pallas_call(
        _proj1_kernel,
        grid=(N_ROW_TILES,),
        in_specs=[
            row_spec,
            pl.BlockSpec((BATCH, 1, SUBLANES, D_MODEL), lambda i: (0, jnp.minimum(i, 1), 0, 0)),
            pl.BlockSpec((1, D_MODEL), lambda i: (0, 0)),
            pl.BlockSpec((D_MODEL, 2 * D_MODEL), lambda i: (0, 0)),
        ],
        out_specs=[pl.BlockSpec((SSM_GROUPS, tile_rows, CHUNK_W), lambda i: (0, i, 0)), row_spec],
        out_shape=[jax.ShapeDtypeStruct((SSM_GROUPS, SCAN_ROWS, CHUNK_W), BF16),
                   jax.ShapeDtypeStruct((BATCH, TOTAL, D_MODEL), BF16)],
        scratch_shapes=[pltpu.VMEM((D_MODEL // LANES, BATCH * ROW_TILE, LANES), F32)],
        compiler_params=_params(("arbitrary",), 56),
        name="ssm_projection",
    )(xc, mod, norm_w.reshape(1, D_MODEL), w_in)


def _s5_kernel(u_ref, m_ref, ws_ref, wy_ref, cst_ref, y_ref, s4_ref, xp_ref):
    for g in range(GROUP_BATCH):
        s4_ref[g] = jnp.dot(u_ref[g], ws_ref[g], preferred_element_type=F32)
    even = lax.broadcasted_iota(jnp.int32, (SUBLANES, STATE_W), 0) % 2 == 0
    down = 1
    up = SUBLANES - 1
    fwd = slice(0, STATE_W)
    bwd = slice(STATE_W, 2 * STATE_W)
    fwd_sw = slice(2 * STATE_W, 3 * STATE_W)
    bwd_sw = slice(3 * STATE_W, 4 * STATE_W)

    def step(j, carry):
        jb = jnp.where(j < N_CTX_SCAN_BLOCKS, N_CTX_SCAN_BLOCKS - 1 - j,
                       N_SCAN_BLOCKS - 1 + N_CTX_SCAN_BLOCKS - j)
        rf = pl.ds(pl.multiple_of(j * SUBLANES, SUBLANES), SUBLANES)
        rb = pl.ds(pl.multiple_of(jb * SUBLANES, SUBLANES), SUBLANES)
        new = []
        for g in range(GROUP_BATCH):
            cf, cfs, cb, cbs = carry[4 * g:4 * g + 4]
            p1f, p2f, q1f, q2f = cst_ref[g, 0], cst_ref[g, 1], cst_ref[g, 2], cst_ref[g, 3]
            p1b, p2b, q1b, q2b = cst_ref[g, 4], cst_ref[g, 5], cst_ref[g, 6], cst_ref[g, 7]
            zf = s4_ref[g, rf, fwd]
            zfs = s4_ref[g, rf, fwd_sw]
            rzf = pltpu.roll(zf, down, 0)
            rzfs = pltpu.roll(zfs, down, 0)
            xf = p1f * cf + p2f * cfs + (zf + q1f * rzf + q2f * rzfs)
            xfs = p1f * cfs - p2f * cf + (zfs + q1f * rzfs - q2f * rzf)
            xp_ref[g, rf, fwd] = jnp.where(even, cf, pltpu.roll(xf, down, 0))
            new += [jnp.where(even, pltpu.roll(xf, up, 0), xf),
                    jnp.where(even, pltpu.roll(xfs, up, 0), xfs)]
            zb = s4_ref[g, rb, bwd]
            zbs = s4_ref[g, rb, bwd_sw]
            rzb = pltpu.roll(zb, up, 0)
            rzbs = pltpu.roll(zbs, up, 0)
            xb = p1b * cb + p2b * cbs + (zb + q1b * rzb + q2b * rzbs)
            xbs = p1b * cbs - p2b * cb + (zbs + q1b * rzbs - q2b * rzb)
            xp_ref[g, rb, bwd] = jnp.where(even, pltpu.roll(xb, up, 0), cb)
            new += [jnp.where(even, xb, pltpu.roll(xb, down, 0)),
                    jnp.where(even, xbs, pltpu.roll(xbs, down, 0))]
        return tuple(new)

    zero = jnp.zeros((SUBLANES, STATE_W), F32)
    lax.fori_loop(0, N_SCAN_BLOCKS, step, (zero,) * (4 * GROUP_BATCH))
    for g in range(GROUP_BATCH):
        y_ref[g] = (
            jnp.dot(u_ref[g, CTX_SCAN_ROWS:, :], m_ref[g], preferred_element_type=F32)
            + jnp.dot(xp_ref[g, CTX_SCAN_ROWS:, :].astype(BF16), wy_ref[g],
                      preferred_element_type=F32))


def _s5_core(u_g, m_mat, ws_mat, wy_mat, consts):
    lat_rows = SCAN_ROWS - CTX_SCAN_ROWS

    def gspec(*tail):
        return pl.BlockSpec((GROUP_BATCH,) + tail, lambda i: (i,) + (0,) * len(tail))

    return pl.pallas_call(
        _s5_kernel,
        grid=(SSM_GROUPS // GROUP_BATCH,),
        in_specs=[
            gspec(SCAN_ROWS, CHUNK_W),
            gspec(CHUNK_W, CHUNK_W),
            gspec(CHUNK_W, 4 * STATE_W),
            gspec(2 * STATE_W, CHUNK_W),
            gspec(8, SUBLANES, STATE_W),
        ],
        out_specs=gspec(lat_rows, CHUNK_W),
        out_shape=jax.ShapeDtypeStruct((SSM_GROUPS, lat_rows, CHUNK_W), F32),
        scratch_shapes=[
            pltpu.VMEM((GROUP_BATCH, SCAN_ROWS, 4 * STATE_W), F32),
            pltpu.VMEM((GROUP_BATCH, SCAN_ROWS, 2 * STATE_W), F32),
        ],
        compiler_params=_params(("arbitrary",), 48),
        name="s5_scan",
    )(u_g, m_mat, ws_mat, wy_mat, consts)


def _out1_kernel(y_ref, z_ref, x_ref, mod_ref, wg_ref, wo_ref, fnw_ref, o_ref, scr_ref):
    n_slab = D_MODEL // LANES
    groups_per_slab = LANES // SSM_GROUP
    for k in range(n_slab):
        for p in range(PAIRS_PER_TILE):
            for m2 in range(CHUNK // SUBLANES):
                vals = [y_ref[groups_per_slab * k + gl, SUBLANES * p:SUBLANES * (p + 1),
                              LANES * m2:LANES * (m2 + 1)] for gl in range(groups_per_slab)]
                steps = _lane_block_transpose(vals)
                for s2 in range(SUBLANES):
                    t_idx = SUBLANES * m2 + s2
                    scr_ref[k, pl.ds(CHUNK * SUBLANES * p + t_idx, SUBLANES, stride=CHUNK), :] = (
                        steps[s2])
    y_perm = jnp.concatenate([scr_ref[k] for k in range(n_slab)], axis=1)
    for b in range(BATCH):
        y = jnp.concatenate(
            [y_perm[CHUNK * _chunk_slot(b, cc):CHUNK * (_chunk_slot(b, cc) + 1)]
             for cc in range(CHUNKS_PER_TILE)], axis=0)
        g = (0.5 * y * (1.0 + lax.erf(y * (2.0 ** -0.5)))).astype(BF16)
        t = jnp.dot(g, wg_ref[...], preferred_element_type=F32)
        z = z_ref[b].astype(F32)
        r = t[:, :D_MODEL] * jax.nn.sigmoid(t[:, D_MODEL:]) * (z * jax.nn.sigmoid(z))
        o = jnp.dot(r.astype(BF16), wo_ref[...], preferred_element_type=F32)
        x2 = x_ref[b] + mod_ref[b, 0, 2:3, :] * o
        ms = jnp.mean(x2 * x2, axis=-1, keepdims=True)
        o_ref[b] = x2 * lax.rsqrt(ms + NORM_EPS) * fnw_ref[...]


def _out1(y_g, z, xc, mod, w_glu, w_out, final_norm_w):
    ctx_tiles = CTX_LEN // ROW_TILE
    tile_rows = PAIRS_PER_TILE * SUBLANES
    lat_spec = pl.BlockSpec((BATCH, ROW_TILE, D_MODEL), lambda i: (0, i, 0))
    all_spec = pl.BlockSpec((BATCH, ROW_TILE, D_MODEL), lambda i: (0, i + ctx_tiles, 0))
    return pl.pallas_call(
        _out1_kernel,
        grid=(SEQ // ROW_TILE,),
        in_specs=[
            pl.BlockSpec((SSM_GROUPS, tile_rows, CHUNK_W), lambda i: (0, i, 0)),
            all_spec, all_spec,
            pl.BlockSpec((BATCH, 1, SUBLANES, D_MODEL), lambda i: (0, 1, 0, 0)),
            pl.BlockSpec((D_MODEL, 2 * D_MODEL), lambda i: (0, 0)),
            pl.BlockSpec((D_MODEL, D_MODEL), lambda i: (0, 0)),
            pl.BlockSpec((1, D_MODEL), lambda i: (0, 0)),
        ],
        out_specs=lat_spec,
        out_shape=jax.ShapeDtypeStruct((BATCH, SEQ, D_MODEL), F32),
        scratch_shapes=[pltpu.VMEM((D_MODEL // LANES, BATCH * ROW_TILE, LANES), F32)],
        compiler_params=_params(("arbitrary",), 56),
        name="ssm_output",
    )(y_g, z, xc, mod, w_glu, w_out, final_norm_w.reshape(1, D_MODEL))


def _slot_order(t, lead):
    n_m = KV_WIDTH // LANES
    gq = N_HEADS // N_KV_HEADS
    shape = t.shape
    t = t.reshape(shape[:lead] + (n_m, 2, gq) + shape[lead + 1:])
    perm = tuple(range(lead)) + (lead, lead + 2, lead + 1) + tuple(range(lead + 3, t.ndim))
    return jnp.transpose(t, perm)


def _rope_order(t):
    shape = t.shape
    t = t.reshape(shape[:-1] + (2, 2, ROPE_FREQS))
    return jnp.swapaxes(t, -3, -2).reshape(shape)


def _attn_weights(w_in, w_out, sink):
    wq = w_in[:, :ATTN_WIDTH].reshape(D_MODEL, N_HEADS, HEAD_DIM)
    wq = _slot_order(_rope_order(wq), 1).reshape(D_MODEL, ATTN_WIDTH)
    wk = w_in[:, ATTN_WIDTH:ATTN_WIDTH + KV_WIDTH].reshape(D_MODEL, N_KV_HEADS, HEAD_DIM)
    wk = _rope_order(wk).reshape(D_MODEL, KV_WIDTH)
    wv = w_in[:, ATTN_WIDTH + KV_WIDTH:ATTN_WIDTH + 2 * KV_WIDTH]
    wz = w_in[:, ATTN_WIDTH + 2 * KV_WIDTH:].reshape(D_MODEL, N_HEADS, HEAD_DIM)
    wz = _slot_order(wz, 1).reshape(D_MODEL, ATTN_WIDTH)
    w_in_p = jnp.concatenate([wq, wk, wv, wz], axis=1).astype(BF16)
    wo = _slot_order(w_out.reshape(N_HEADS, HEAD_DIM, D_MODEL), 0).reshape(ATTN_WIDTH, D_MODEL)
    sink_p = _slot_order(sink.astype(F32).reshape(N_HEADS), 0).reshape(2 * N_SLOTS)
    sink_tab = jnp.broadcast_to((sink_p * LOG2E)[:, None], (2 * N_SLOTS, LANES))
    return w_in_p, wo.astype(BF16), sink_tab


def _rope_tables():
    inv = ROPE_BASE ** (-jnp.arange(ROPE_FREQS, dtype=F32) / ROPE_FREQS)
    pos = jnp.arange(SEQ)
    row = (pos // GRID_W).astype(F32)[:, None] * inv
    col = (pos % GRID_W).astype(F32)[:, None] * inv
    lane = np.arange(LANES)
    w = lane % HEAD_DIM
    half, axis, f = w // 32, (w % 32) // 16, w % 16
    ang = jnp.where(jnp.asarray(axis == 0)[None, :], row[:, f], col[:, f])
    sign = jnp.asarray(np.where(half == 0, -1.0, 1.0), F32)[None, :]
    cos = jnp.concatenate([jnp.ones((CTX_LEN, LANES), F32), jnp.cos(ang)], axis=0)
    sin = jnp.concatenate([jnp.zeros((CTX_LEN, LANES), F32), jnp.sin(ang) * sign], axis=0)
    return cos, sin


def _s5_operators(lam_re, lam_im, log_dt, b_re, b_im, c_re, c_im, d_skip):
    hi = lax.Precision.HIGHEST
    t_len = CHUNK
    lr, li = lam_re.astype(F32), lam_im.astype(F32)
    dt = jnp.exp(log_dt.astype(F32))[..., None]
    ks = jnp.arange(2 * t_len + 1, dtype=F32)
    mag = jnp.exp((lr * dt)[..., None] * ks)
    ph = (li * dt)[..., None] * ks
    pr, pi = mag * jnp.cos(ph), mag * jnp.sin(ph)
    ar1, ai1 = pr[..., 1] - 1.0, pi[..., 1]
    den = lr * lr + li * li
    gr, gi = (ar1 * lr + ai1 * li) / den, (ai1 * lr - ar1 * li) / den
    br_, bi_ = b_re.astype(F32), b_im.astype(F32)
    bbr = gr[..., None] * br_ - gi[..., None] * bi_
    bbi = gr[..., None] * bi_ + gi[..., None] * br_
    cr, ci = c_re.astype(F32), c_im.astype(F32)

    pk_r, pk_i = pr[..., :t_len, None], pi[..., :t_len, None]
    wr = pk_r * bbr[:, :, :, None, :] - pk_i * bbi[:, :, :, None, :]
    wi = pk_r * bbi[:, :, :, None, :] + pk_i * bbr[:, :, :, None, :]

    kern = jnp.einsum('dghp,dgpkj->dgkhj', jnp.concatenate([cr, -ci], axis=-1),
                      jnp.concatenate([wr, wi], axis=2), precision=hi)
    d_g = d_skip.astype(F32).reshape(SSM_GROUPS, SSM_GROUP)
    lag0 = (kern[0][:, :1] + kern[1][:, :1]
            + (jnp.eye(SSM_GROUP, dtype=F32) * d_g[:, :, None])[:, None])
    ext = jnp.concatenate(
        [lag0, kern[0][:, 1:], jnp.zeros_like(kern[0][:, :2]), kern[1][:, :0:-1]], axis=1)
    period = 2 * t_len + 1
    assert ext.shape[1] == period
    skew = jnp.tile(ext, (1, t_len, 1, 1))[:, :t_len * (period - 1)]
    skew = skew.reshape(SSM_GROUPS, t_len, period - 1, SSM_GROUP, SSM_GROUP)[:, :, :t_len]
    m_mat = jnp.transpose(skew, (0, 1, 4, 2, 3)).reshape(SSM_GROUPS, CHUNK_W, CHUNK_W)

    def state_rows(w, d, flip):
        w = w[d][:, :, ::-1] if flip else w[d]
        return jnp.transpose(w, (0, 2, 3, 1))

    fr, fi = state_rows(wr, 0, True), state_rows(wi, 0, True)
    br, bi = state_rows(wr, 1, False), state_rows(wi, 1, False)
    ws_mat = jnp.concatenate([fr, fi, br, bi, fi, fr, bi, br], axis=-1)
    ws_mat = ws_mat.reshape(SSM_GROUPS, CHUNK_W, 4 * STATE_W)

    def out_rows(d, flip):
        kr, ki = pr[d][:, :, 1:t_len + 1], pi[d][:, :, 1:t_len + 1]
        if flip:
            kr, ki = kr[..., ::-1], ki[..., ::-1]
        c_r = jnp.swapaxes(cr[d], 1, 2)[:, :, None, :]
        c_i = jnp.swapaxes(ci[d], 1, 2)[:, :, None, :]
        re = c_r * kr[..., None] - c_i * ki[..., None]
        im = c_r * ki[..., None] + c_i * kr[..., None]
        return re, -im

    wy_mat = jnp.concatenate(out_rows(0, False) + out_rows(1, True), axis=1)
    wy_mat = wy_mat.reshape(SSM_GROUPS, 2 * STATE_W, CHUNK_W)

    def lanes(d, k):
        re, im = pr[d][:, :, k], pi[d][:, :, k]
        return (jnp.concatenate([re, re], axis=-1)[:, None, :],
                jnp.concatenate([-im, im], axis=-1)[:, None, :])

    def rows(top, bot):
        rep = (1, SUBLANES // 2, 1)
        return [jnp.tile(jnp.concatenate([t, b], axis=1), rep) for t, b in zip(top, bot)]

    zero = (jnp.zeros((SSM_GROUPS, 1, STATE_W), F32),) * 2
    consts = jnp.stack(
        rows(lanes(0, t_len), lanes(0, 2 * t_len)) + rows(zero, lanes(0, t_len))
        + rows(lanes(1, 2 * t_len), lanes(1, t_len)) + rows(lanes(1, t_len), zero),
        axis=1)
    return m_mat.astype(BF16), ws_mat.astype(BF16), wy_mat.astype(BF16), consts


def kernel(x, c, ctx, c_ctx, norm_w, w_ada, b_ada, attn_w_in, attn_sink, attn_w_out,
           ssm_w_in, ssm_lam_re, ssm_lam_im, ssm_log_dt, ssm_b_re, ssm_b_im, ssm_c_re, ssm_c_im,
           ssm_d, ssm_w_glu, ssm_w_out, final_norm_w):
    mod0, mod1 = _modulation(c, c_ctx, w_ada, b_ada)

    w_in0, w_out0, sink_tab = _attn_weights(attn_w_in[0], attn_w_out[0], attn_sink[0])
    cos_tab, sin_tab = _rope_tables()
    q, z0, kbd, vbd = _proj0(x, ctx, mod0, norm_w[0], cos_tab, sin_tab, w_in0)
    xc1 = _attention(q, z0, kbd, vbd, sink_tab, x, ctx, mod0, w_out0)

    u_g, z1 = _proj1(xc1, mod1, norm_w[1], ssm_w_in[0].astype(BF16))
    m_mat, ws_mat, wy_mat, consts = _s5_operators(
        ssm_lam_re[0], ssm_lam_im[0], ssm_log_dt[0], ssm_b_re[0], ssm_b_im[0],
        ssm_c_re[0], ssm_c_im[0], ssm_d[0])
    y_g = _s5_core(u_g, m_mat, ws_mat, wy_mat, consts)
    return _out1(y_g, z1, xc1, mod1, ssm_w_glu[0].astype(BF16), ssm_w_out[0].astype(BF16),
                 final_norm_w)
```

```python
import functools
import math

import jax
import jax.numpy as jnp
import numpy as np
from jax import lax
from jax.experimental import pallas as pl
from jax.experimental.pallas import tpu as pltpu

F32 = jnp.float32
BF16 = jnp.bfloat16

D_MODEL = 1024
BATCH = 4
SEQ = 4096
GRID_W = 64
CTX_LEN = 256
TOTAL = CTX_LEN + SEQ
HEAD_DIM = 64
N_HEADS = 16
N_KV_HEADS = 4
ATTN_WIDTH = N_HEADS * HEAD_DIM
KV_WIDTH = N_KV_HEADS * HEAD_DIM
BLOCK = 128
N_BLOCKS = TOTAL // BLOCK
N_CTX_BLOCKS = CTX_LEN // BLOCK
ROPE_BASE = 10000.0
ROPE_FREQS = HEAD_DIM // 4
SSM_GROUP = 16
SSM_GROUPS = D_MODEL // SSM_GROUP
SSM_STATE = 64
NORM_EPS = 1e-6
NEG_INF = -1e30

LANES = 128
SUBLANES = 8
N_SLOTS = ATTN_WIDTH // LANES
N_KV_PAIRS = KV_WIDTH // LANES
SLOTS_PER_M = N_SLOTS // N_KV_PAIRS
UNIT_SLOTS = 2
LOG2E = math.log2(math.e)
Q_SCALE = HEAD_DIM ** -0.5 * LOG2E
ROW_TILE = 256
N_ROW_TILES = TOTAL // ROW_TILE
CHUNK = 16
N_CHUNKS = TOTAL // CHUNK
N_CTX_CHUNKS = CTX_LEN // CHUNK
CHUNK_W = CHUNK * SSM_GROUP
CHUNKS_PER_TILE = ROW_TILE // CHUNK
PAIRS_PER_TILE = CHUNKS_PER_TILE // 2
STATE_W = 2 * SSM_STATE
SCAN_ROWS = N_CHUNKS * BATCH
CTX_SCAN_ROWS = N_CTX_CHUNKS * BATCH
N_SCAN_BLOCKS = SCAN_ROWS // SUBLANES
N_CTX_SCAN_BLOCKS = CTX_SCAN_ROWS // SUBLANES
GROUP_BATCH = 4

assert BATCH * 2 == SUBLANES


def _params(semantics, vmem_mb):
    return pltpu.CompilerParams(dimension_semantics=semantics,
                                vmem_limit_bytes=vmem_mb * 1024 * 1024)


def _mod_kernel(c_ref, w_ref, b_ref, o_ref):
    c = c_ref[...]
    a = c * jax.nn.sigmoid(c)
    o_ref[0] = jnp.dot(a, w_ref[0], preferred_element_type=F32,
                       precision=lax.Precision.HIGHEST) + b_ref[0]


def _modulation(c, c_ctx, w_ada, b_ada):
    depth = w_ada.shape[0]
    rows = jnp.zeros((SUBLANES, D_MODEL), F32).at[:BATCH].set(c).at[BATCH].set(c_ctx)
    n_col = 3
    out = pl.pallas_call(
        _mod_kernel,
        grid=(depth, n_col),
        in_specs=[
            pl.BlockSpec((SUBLANES, D_MODEL), lambda l, j: (0, 0)),
            pl.BlockSpec((1, D_MODEL, D_MODEL), lambda l, j: (l, 0, j)),
            pl.BlockSpec((1, 1, D_MODEL), lambda l, j: (l, 0, j)),
        ],
        out_specs=pl.BlockSpec((1, SUBLANES, D_MODEL), lambda l, j: (l, 0, j)),
        out_shape=jax.ShapeDtypeStruct((depth, SUBLANES, 3 * D_MODEL), F32),
        compiler_params=_params(("arbitrary", "arbitrary"), 32),
        name="adaln_modulation",
    )(rows, w_ada, b_ada.reshape(depth, 1, 3 * D_MODEL))
    tabs = []
    for l in range(depth):
        lat = out[l, :BATCH].reshape(BATCH, 3, D_MODEL)
        cx = jnp.broadcast_to(out[l, BATCH].reshape(1, 3, D_MODEL), (BATCH, 3, D_MODEL))
        tab = jnp.stack([cx, lat], axis=1)
        tabs.append(jnp.pad(tab, ((0, 0), (0, 0), (0, SUBLANES - 3), (0, 0))))
    return tabs


def _modulated_norm(xt, nw, mod_ref, b=0):
    ms = jnp.mean(xt * xt, axis=-1, keepdims=True)
    y = xt * lax.rsqrt(ms + NORM_EPS) * nw
    return y * (1.0 + mod_ref[b, 0, 1:2, :]) + mod_ref[b, 0, 0:1, :]


def _lane_block_transpose(vs):
    n = len(vs)
    width = LANES // n
    blk = lax.broadcasted_iota(jnp.int32, vs[0].shape, 1) // width
    x = list(vs)
    d = n // 2
    while d >= 1:
        clear = (blk & d) == 0
        y = list(x)
        for i in range(n):
            if i & d == 0:
                a, b = x[i], x[i + d]
                y[i] = jnp.where(clear, a, pltpu.roll(b, width * d, 1))
                y[i + d] = jnp.where(clear, pltpu.roll(a, LANES - width * d, 1), b)
        x = y
        d //= 2
    return x


def _chunk_slot(b, cc):
    return (cc // 2) * (2 * BATCH) + 2 * b + (cc % 2)


def _proj0_kernel(x_ref, c_ref, mod_ref, nw_ref, cos_ref, sin_ref, w_ref,
                  q_ref, z_ref, kbd_ref, vbd_ref):
    i = pl.program_id(1)
    xt = jnp.where(i == 0, c_ref[0], x_ref[0])
    h = _modulated_norm(xt, nw_ref[...], mod_ref).astype(BF16)
    cos = cos_ref[...]
    sin = sin_ref[...]
    lane = lax.broadcasted_iota(jnp.int32, (ROW_TILE, LANES), 1)
    first_half = (lane % HEAD_DIM) < (HEAD_DIM // 2)
    low = lax.broadcasted_iota(jnp.int32, (BLOCK, LANES), 1) < HEAD_DIM

    def rope(t):
        partner = jnp.where(first_half, pltpu.roll(t, LANES - HEAD_DIM // 2, 1),
                            pltpu.roll(t, HEAD_DIM // 2, 1))
        return t * cos + partner * sin

    q = jnp.dot(h, w_ref[:, :ATTN_WIDTH], preferred_element_type=F32)
    for j in range(N_SLOTS):
        m, gi = divmod(j, SLOTS_PER_M)
        qj = (rope(q[:, LANES * j:LANES * (j + 1)]) * Q_SCALE).astype(BF16)
        for blk in range(ROW_TILE // BLOCK):
            q_ref[0, blk, m, BLOCK * gi:BLOCK * (gi + 1), :] = qj[BLOCK * blk:BLOCK * (blk + 1)]
    k = jnp.dot(h, w_ref[:, ATTN_WIDTH:ATTN_WIDTH + KV_WIDTH], preferred_element_type=F32)
    v = jnp.dot(h, w_ref[:, ATTN_WIDTH + KV_WIDTH:ATTN_WIDTH + 2 * KV_WIDTH],
                preferred_element_type=F32)
    for m in range(KV_WIDTH // LANES):
        sl = slice(LANES * m, LANES * (m + 1))
        kr = rope(k[:, sl])
        vm = v[:, sl]
        for blk in range(ROW_TILE // BLOCK):
            rows = slice(BLOCK * blk, BLOCK * (blk + 1))
            kbd_ref[0, blk, m, :BLOCK, :] = jnp.where(low, kr[rows], 0.0).astype(BF16)
            kbd_ref[0, blk, m, BLOCK:, :] = jnp.where(low, 0.0, kr[rows]).astype(BF16)
            vbd_ref[0, blk, m, :BLOCK, :] = jnp.where(low, vm[rows], 0.0).astype(BF16)
            vbd_ref[0, blk, m, BLOCK:, :] = jnp.where(low, 0.0, vm[rows]).astype(BF16)
    z = jnp.dot(h, w_ref[:, ATTN_WIDTH + 2 * KV_WIDTH:], preferred_element_type=F32)
    z_ref[0] = z.astype(BF16)


def _proj0(x, ctx, mod, norm_w, cos_tab, sin_tab, w_in):
    n_col = w_in.shape[1]
    blocks_per_tile = ROW_TILE // BLOCK
    kv_shape = jax.ShapeDtypeStruct((BATCH, N_BLOCKS, KV_WIDTH // LANES, 2 * BLOCK, LANES), BF16)
    kv_spec = pl.BlockSpec((1, blocks_per_tile, KV_WIDTH // LANES, 2 * BLOCK, LANES),
                           lambda b, i: (b, i, 0, 0, 0))
    row_spec = pl.BlockSpec((1, ROW_TILE, D_MODEL), lambda b, i: (b, i, 0))
    return pl.pallas_call(
        _proj0_kernel,
        grid=(BATCH, N_ROW_TILES),
        in_specs=[
            pl.BlockSpec((1, ROW_TILE, D_MODEL), lambda b, i: (b, jnp.maximum(i - 1, 0), 0)),
            pl.BlockSpec((1, ROW_TILE, D_MODEL), lambda b, i: (b, 0, 0)),
            pl.BlockSpec((1, 1, SUBLANES, D_MODEL), lambda b, i: (b, jnp.minimum(i, 1), 0, 0)),
            pl.BlockSpec((1, D_MODEL), lambda b, i: (0, 0)),
            pl.BlockSpec((ROW_TILE, LANES), lambda b, i: (i, 0)),
            pl.BlockSpec((ROW_TILE, LANES), lambda b, i: (i, 0)),
            pl.BlockSpec((D_MODEL, n_col), lambda b, i: (0, 0)),
        ],
        out_specs=[
            pl.BlockSpec((1, blocks_per_tile, N_KV_PAIRS, SLOTS_PER_M * BLOCK, LANES),
                         lambda b, i: (b, i, 0, 0, 0)),
            row_spec, kv_spec, kv_spec],
        out_shape=[
            jax.ShapeDtypeStruct((BATCH, N_BLOCKS, N_KV_PAIRS, SLOTS_PER_M * BLOCK, LANES), BF16),
            jax.ShapeDtypeStruct((BATCH, TOTAL, ATTN_WIDTH), BF16),
            kv_shape, kv_shape,
        ],
        compiler_params=_params(("arbitrary", "arbitrary"), 48),
        name="attn_projection",
    )(x, ctx, mod, norm_w.reshape(1, D_MODEL), cos_tab, sin_tab, w_in)


def _attn_kernel(q_ref, z_ref, kl_ref, kc_ref, kr_ref, kx_ref, vl_ref, vc_ref, vr_ref, vx_ref,
                 sink_ref, x_ref, c_ref, mod_ref, wo_ref, o_ref):
    i = pl.program_id(1)
    is_lat = i >= N_CTX_BLOCKS
    n = i - N_CTX_BLOCKS
    far = 4 * BLOCK
    lim_left = jnp.where(jnp.logical_and(is_lat, n >= 1), 0, far)
    lim_right = jnp.where(jnp.logical_and(is_lat, n <= SEQ // BLOCK - 2), 0, -far)
    lim_center = jnp.where(is_lat, far, -far)
    q_rows = UNIT_SLOTS * BLOCK
    row = lax.broadcasted_iota(jnp.int32, (q_rows, 2 * BLOCK), 0)
    lane = lax.broadcasted_iota(jnp.int32, (q_rows, 2 * BLOCK), 1)
    d = (lane % BLOCK) - (row % BLOCK)
    masks = [d >= lim_left, d <= lim_center, d <= lim_right, None, None]
    low = lax.broadcasted_iota(jnp.int32, (BLOCK, LANES), 1) < HEAD_DIM

    def kpiece(p, m):
        if p == 0:
            return kl_ref[0, 0, m], vl_ref[0, 0, m]
        if p == 1:
            return kc_ref[0, 0, m], vc_ref[0, 0, m]
        if p == 2:
            return kr_ref[0, 0, m], vr_ref[0, 0, m]
        return kx_ref[0, p - 3, m], vx_ref[0, p - 3, m]

    n_piece = 3 + N_CTX_BLOCKS
    units = [(m, h) for m in range(N_KV_PAIRS) for h in range(SLOTS_PER_M // UNIT_SLOTS)]

    def scores(m, h):
        qu = q_ref[0, 0, m, q_rows * h:q_rows * (h + 1), :]
        s_list = []
        for p in range(n_piece):
            kbd, _ = kpiece(p, m)
            s = lax.dot_general(qu, kbd, (((1,), (1,)), ((), ())), preferred_element_type=F32)
            if masks[p] is not None:
                s = jnp.where(masks[p], s, NEG_INF)
            s_list.append(s)
        return s_list

    def finish(m, h, s_list, y):
        slot0 = SLOTS_PER_M * m + UNIT_SLOTS * h
        mx = s_list[0]
        for s in s_list[1:]:
            mx = jnp.maximum(mx, s)

        def sink_col(hs):
            return jnp.concatenate(
                [jnp.broadcast_to(sink_ref[2 * (slot0 + gi) + hs:2 * (slot0 + gi) + hs + 1, 0:1],
                                  (BLOCK, 1)) for gi in range(UNIT_SLOTS)], axis=0)

        sink_a, sink_b = sink_col(0), sink_col(1)
        m_a = jnp.maximum(jnp.max(mx[:, :BLOCK], axis=1, keepdims=True), sink_a)
        m_b = jnp.maximum(jnp.max(mx[:, BLOCK:], axis=1, keepdims=True), sink_b)
        mb_a = jnp.broadcast_to(m_a, (q_rows, BLOCK))
        mb_b = jnp.broadcast_to(m_b, (q_rows, BLOCK))
        acc = jnp.zeros((q_rows, LANES), F32)
        esum_a = jnp.zeros((q_rows, BLOCK), F32)
        esum_b = jnp.zeros((q_rows, BLOCK), F32)
        for p in range(n_piece):
            _, vbd = kpiece(p, m)
            e_a = jnp.exp2(s_list[p][:, :BLOCK] - mb_a)
            e_b = jnp.exp2(s_list[p][:, BLOCK:] - mb_b)
            esum_a = esum_a + e_a
            esum_b = esum_b + e_b
            pb = jnp.concatenate([e_a, e_b], axis=1).astype(BF16)
            acc = acc + jnp.dot(pb, vbd, preferred_element_type=F32)
        inv_a = 1.0 / (jnp.sum(esum_a, axis=1, keepdims=True) + jnp.exp2(sink_a - m_a))
        inv_b = 1.0 / (jnp.sum(esum_b, axis=1, keepdims=True) + jnp.exp2(sink_b - m_b))
        outs = []
        for gi in range(UNIT_SLOTS):
            j = slot0 + gi
            rows = slice(BLOCK * gi, BLOCK * (gi + 1))
            inv = jnp.where(low, inv_a[rows], inv_b[rows])
            zj = z_ref[0, :, LANES * j:LANES * (j + 1)].astype(F32)
            outs.append((acc[rows] * inv * (zj * jax.nn.sigmoid(zj))).astype(BF16))
        g = jnp.concatenate(outs, axis=1)
        part = jnp.dot(g, wo_ref[LANES * slot0:LANES * (slot0 + UNIT_SLOTS), :],
                       preferred_element_type=F32)
        return part if y is None else y + part

    y = None
    pending = scores(*units[0])
    for u, (m, h) in enumerate(units):
        nxt = scores(*units[u + 1]) if u + 1 < len(units) else None
        y = finish(m, h, pending, y)
        pending = nxt
    resid = jnp.where(is_lat, x_ref[0], c_ref[0])
    o_ref[0] = resid + mod_ref[0, 0, 2:3, :] * y


def _attention(q, z, kbd, vbd, sink_tab, x, ctx, mod, w_out):
    last = N_BLOCKS - 1
    n_m = KV_WIDTH // LANES
    row_spec = pl.BlockSpec((1, BLOCK, ATTN_WIDTH), lambda b, i: (b, i, 0))

    def kv_spec(off):
        return pl.BlockSpec((1, 1, n_m, 2 * BLOCK, LANES),
                            lambda b, i: (b, jnp.clip(i + off, 0, last), 0, 0, 0))

    ctx_kv_spec = pl.BlockSpec((1, N_CTX_BLOCKS, n_m, 2 * BLOCK, LANES),
                               lambda b, i: (b, 0, 0, 0, 0))
    return pl.pallas_call(
        _attn_kernel,
        grid=(BATCH, N_BLOCKS),
        in_specs=[
            pl.BlockSpec((1, 1, n_m, SLOTS_PER_M * BLOCK, LANES), lambda b, i: (b, i, 0, 0, 0)),
            row_spec,
            kv_spec(-1), kv_spec(0), kv_spec(1), ctx_kv_spec,
            kv_spec(-1), kv_spec(0), kv_spec(1), ctx_kv_spec,
            pl.BlockSpec((2 * N_SLOTS, LANES), lambda b, i: (0, 0)),
            pl.BlockSpec((1, BLOCK, D_MODEL),
                         lambda b, i: (b, jnp.maximum(i - N_CTX_BLOCKS, 0), 0)),
            pl.BlockSpec((1, BLOCK, D_MODEL),
                         lambda b, i: (b, jnp.minimum(i, N_CTX_BLOCKS - 1), 0)),
            pl.BlockSpec((1, 1, SUBLANES, D_MODEL),
                         lambda b, i: (b, jnp.minimum(i // N_CTX_BLOCKS, 1), 0, 0)),
            pl.BlockSpec((ATTN_WIDTH, D_MODEL), lambda b, i: (0, 0)),
        ],
        out_specs=pl.BlockSpec((1, BLOCK, D_MODEL), lambda b, i: (b, i, 0)),
        out_shape=jax.ShapeDtypeStruct((BATCH, TOTAL, D_MODEL), F32),
        compiler_params=_params(("arbitrary", "arbitrary"), 48),
        name="window_attention",
    )(q, z, kbd, kbd, kbd, kbd, vbd, vbd, vbd, vbd, sink_tab, x, ctx, mod, w_out)


def _proj1_kernel(x_ref, mod_ref, nw_ref, w_ref, u_ref, z_ref, scr_ref):
    h = jnp.concatenate(
        [_modulated_norm(x_ref[b], nw_ref[...], mod_ref, b).astype(BF16) for b in range(BATCH)],
        axis=0)
    z = jnp.dot(h, w_ref[:, D_MODEL:], preferred_element_type=F32)
    for b in range(BATCH):
        z_ref[b] = z[ROW_TILE * b:ROW_TILE * (b + 1)].astype(BF16)
    u = jnp.dot(h, w_ref[:, :D_MODEL], preferred_element_type=F32)
    blocks = [None] * (BATCH * CHUNKS_PER_TILE)
    for b in range(BATCH):
        for cc in range(CHUNKS_PER_TILE):
            r0 = ROW_TILE * b + CHUNK * cc
            blocks[_chunk_slot(b, cc)] = u[r0:r0 + CHUNK]
    u_perm = jnp.concatenate(blocks, axis=0)
    n_slab = D_MODEL // LANES
    for k in range(n_slab):
        scr_ref[k] = u_perm[:, LANES * k:LANES * (k + 1)]
    groups_per_slab = LANES // SSM_GROUP
    for k in range(n_slab):
        for pp in range(PAIRS_PER_TILE // 2):
            parts = []
            for p in (2 * pp, 2 * pp + 1):
                rows = [scr_ref[k, pl.ds(CHUNK * SUBLANES * p + s, SUBLANES, stride=CHUNK), :]
                        for s in range(CHUNK)]
                parts.append([_lane_block_transpose(rows[SUBLANES * m2:SUBLANES * (m2 + 1)])
                              for m2 in range(CHUNK // SUBLANES)])
            for m2 in range(CHUNK // SUBLANES):
                for gl in range(groups_per_slab):
                    val = jnp.concatenate([parts[0][m2][gl], parts[1][m2][gl]], axis=0)
                    u_ref[groups_per_slab * k + gl, 2 * SUBLANES * pp:2 * SUBLANES * (pp + 1),
                          LANES * m2:LANES * (m2 + 1)] = val.astype(BF16)


def _proj1(xc, mod, norm_w, w_in):
    row_spec = pl.BlockSpec((BATCH, ROW_TILE, D_MODEL), lambda i: (0, i, 0))
    tile_rows = PAIRS_PER_TILE * SUBLANES
    return pl.pallas_call(
        _proj1_kernel,
        grid=(N_ROW_TILES,),
        in_specs=[
            row_spec,
            pl.BlockSpec((BATCH, 1, SUBLANES, D_MODEL), lambda i: (0, jnp.minimum(i, 1), 0, 0)),
            pl.BlockSpec((1, D_MODEL), lambda i: (0, 0)),
            pl.BlockSpec((D_MODEL, 2 * D_MODEL), lambda i: (0, 0)),
        ],
        out_specs=[pl.BlockSpec((SSM_GROUPS, tile_rows, CHUNK_W), lambda i: (0, i, 0)), row_spec],
        out_shape=[jax.ShapeDtypeStruct((SSM_GROUPS, SCAN_ROWS, CHUNK_W), BF16),
                   jax.ShapeDtypeStruct((BATCH, TOTAL, D_MODEL), BF16)],
        scratch_shapes=[pltpu.VMEM((D_MODEL // LANES, BATCH * ROW_TILE, LANES), F32)],
        compiler_params=_params(("arbitrary",), 56),
        name="ssm_projection",
    )(xc, mod, norm_w.reshape(1, D_MODEL), w_in)


def _s5_kernel(u_ref, m_ref, ws_ref, wy_ref, cst_ref, y_ref, s4_ref, xp_ref):
    for g in range(GROUP_BATCH):
        s4_ref[g] = jnp.dot(u_ref[g], ws_ref[g], preferred_element_type=F32)
    even = lax.broadcasted_iota(jnp.int32, (SUBLANES, STATE_W), 0) % 2 == 0
    down = 1
    up = SUBLANES - 1
    fwd = slice(0, STATE_W)
    bwd = slice(STATE_W, 2 * STATE_W)
    fwd_sw = slice(2 * STATE_W, 3 * STATE_W)
    bwd_sw = slice(3 * STATE_W, 4 * STATE_W)

    def step(j, carry):
        jb = jnp.where(j < N_CTX_SCAN_BLOCKS, N_CTX_SCAN_BLOCKS - 1 - j,
                       N_SCAN_BLOCKS - 1 + N_CTX_SCAN_BLOCKS - j)
        rf = pl.ds(pl.multiple_of(j * SUBLANES, SUBLANES), SUBLANES)
        rb = pl.ds(pl.multiple_of(jb * SUBLANES, SUBLANES), SUBLANES)
        new = []
        for g in range(GROUP_BATCH):
            cf, cfs, cb, cbs = carry[4 * g:4 * g + 4]
            p1f, p2f, q1f, q2f = cst_ref[g, 0], cst_ref[g, 1], cst_ref[g, 2], cst_ref[g, 3]
            p1b, p2b, q1b, q2b = cst_ref[g, 4], cst_ref[g, 5], cst_ref[g, 6], cst_ref[g, 7]
            zf = s4_ref[g, rf, fwd]
            zfs = s4_ref[g, rf, fwd_sw]
            rzf = pltpu.roll(zf, down, 0)
            rzfs = pltpu.roll(zfs, down, 0)
            xf = p1f * cf + p2f * cfs + (zf + q1f * rzf + q2f * rzfs)
            xfs = p1f * cfs - p2f * cf + (zfs + q1f * rzfs - q2f * rzf)
            xp_ref[g, rf, fwd] = jnp.where(even, cf, pltpu.roll(xf, down, 0))
            new += [jnp.where(even, pltpu.roll(xf, up, 0), xf),
                    jnp.where(even, pltpu.roll(xfs, up, 0), xfs)]
            zb = s4_ref[g, rb, bwd]
            zbs = s4_ref[g, rb, bwd_sw]
            rzb = pltpu.roll(zb, up, 0)
            rzbs = pltpu.roll(zbs, up, 0)
            xb = p1b * cb + p2b * cbs + (zb + q1b * rzb + q2b * rzbs)
            xbs = p1b * cbs - p2b * cb + (zbs + q1b * rzbs - q2b * rzb)
            xp_ref[g, rb, bwd] = jnp.where(even, pltpu.roll(xb, up, 0), cb)
            new += [jnp.where(even, xb, pltpu.roll(xb, down, 0)),
                    jnp.where(even, xbs, pltpu.roll(xbs, down, 0))]
        return tuple(new)

    zero = jnp.zeros((SUBLANES, STATE_W), F32)
    lax.fori_loop(0, N_SCAN_BLOCKS, step, (zero,) * (4 * GROUP_BATCH))
    for g in range(GROUP_BATCH):
        y_ref[g] = (
            jnp.dot(u_ref[g, CTX_SCAN_ROWS:, :], m_ref[g], preferred_element_type=F32)
            + jnp.dot(xp_ref[g, CTX_SCAN_ROWS:, :].astype(BF16), wy_ref[g],
                      preferred_element_type=F32))


def _s5_core(u_g, m_mat, ws_mat, wy_mat, consts):
    lat_rows = SCAN_ROWS - CTX_SCAN_ROWS

    def gspec(*tail):
        return pl.BlockSpec((GROUP_BATCH,) + tail, lambda i: (i,) + (0,) * len(tail))

    return pl.pallas_call(
        _s5_kernel,
        grid=(SSM_GROUPS // GROUP_BATCH,),
        in_specs=[
            gspec(SCAN_ROWS, CHUNK_W),
            gspec(CHUNK_W, CHUNK_W),
            gspec(CHUNK_W, 4 * STATE_W),
            gspec(2 * STATE_W, CHUNK_W),
            gspec(8, SUBLANES, STATE_W),
        ],
        out_specs=gspec(lat_rows, CHUNK_W),
        out_shape=jax.ShapeDtypeStruct((SSM_GROUPS, lat_rows, CHUNK_W), F32),
        scratch_shapes=[
            pltpu.VMEM((GROUP_BATCH, SCAN_ROWS, 4 * STATE_W), F32),
            pltpu.VMEM((GROUP_BATCH, SCAN_ROWS, 2 * STATE_W), F32),
        ],
        compiler_params=_params(("arbitrary",), 48),
        name="s5_scan",
    )(u_g, m_mat, ws_mat, wy_mat, consts)


def _out1_kernel(y_ref, z_ref, x_ref, mod_ref, wg_ref, wo_ref, fnw_ref, o_ref, scr_ref):
    n_slab = D_MODEL // LANES
    groups_per_slab = LANES // SSM_GROUP
    for k in range(n_slab):
        for p in range(PAIRS_PER_TILE):
            for m2 in range(CHUNK // SUBLANES):
                vals = [y_ref[groups_per_slab * k + gl, SUBLANES * p:SUBLANES * (p + 1),
                              LANES * m2:LANES * (m2 + 1)] for gl in range(groups_per_slab)]
                steps = _lane_block_transpose(vals)
                for s2 in range(SUBLANES):
                    t_idx = SUBLANES * m2 + s2
                    scr_ref[k, pl.ds(CHUNK * SUBLANES * p + t_idx, SUBLANES, stride=CHUNK), :] = (
                        steps[s2])
    y_perm = jnp.concatenate([scr_ref[k] for k in range(n_slab)], axis=1)
    for b in range(BATCH):
        y = jnp.concatenate(
            [y_perm[CHUNK * _chunk_slot(b, cc):CHUNK * (_chunk_slot(b, cc) + 1)]
             for cc in range(CHUNKS_PER_TILE)], axis=0)
        g = (0.5 * y * (1.0 + lax.erf(y * (2.0 ** -0.5)))).astype(BF16)
        t = jnp.dot(g, wg_ref[...], preferred_element_type=F32)
        z = z_ref[b].astype(F32)
        r = t[:, :D_MODEL] * jax.nn.sigmoid(t[:, D_MODEL:]) * (z * jax.nn.sigmoid(z))
        o = jnp.dot(r.astype(BF16), wo_ref[...], preferred_element_type=F32)
        x2 = x_ref[b] + mod_ref[b, 0, 2:3, :] * o
        ms = jnp.mean(x2 * x2, axis=-1, keepdims=True)
        o_ref[b] = x2 * lax.rsqrt(ms + NORM_EPS) * fnw_ref[...]


def _out1(y_g, z, xc, mod, w_glu, w_out, final_norm_w):
    ctx_tiles = CTX_LEN // ROW_TILE
    tile_rows = PAIRS_PER_TILE * SUBLANES
    lat_spec = pl.BlockSpec((BATCH, ROW_TILE, D_MODEL), lambda i: (0, i, 0))
    all_spec = pl.BlockSpec((BATCH, ROW_TILE, D_MODEL), lambda i: (0, i + ctx_tiles, 0))
    return pl.pallas_call(
        _out1_kernel,
        grid=(SEQ // ROW_TILE,),
        in_specs=[
            pl.BlockSpec((SSM_GROUPS, tile_rows, CHUNK_W), lambda i: (0, i, 0)),
            all_spec, all_spec,
            pl.BlockSpec((BATCH, 1, SUBLANES, D_MODEL), lambda i: (0, 1, 0, 0)),
            pl.BlockSpec((D_MODEL, 2 * D_MODEL), lambda i: (0, 0)),
            pl.BlockSpec((D_MODEL, D_MODEL), lambda i: (0, 0)),
            pl.BlockSpec((1, D_MODEL), lambda i: (0, 0)),
        ],
        out_specs=lat_spec,
        out_shape=jax.ShapeDtypeStruct((BATCH, SEQ, D_MODEL), F32),
        scratch_shapes=[pltpu.VMEM((D_MODEL // LANES, BATCH * ROW_TILE, LANES), F32)],
        compiler_params=_params(("arbitrary",), 56),
        name="ssm_output",
    )(y_g, z, xc, mod, w_glu, w_out, final_norm_w.reshape(1, D_MODEL))


def _slot_order(t, lead):
    n_m = KV_WIDTH // LANES
    gq = N_HEADS // N_KV_HEADS
    shape = t.shape
    t = t.reshape(shape[:lead] + (n_m, 2, gq) + shape[lead + 1:])
    perm = tuple(range(lead)) + (lead, lead + 2, lead + 1) + tuple(range(lead + 3, t.ndim))
    return jnp.transpose(t, perm)


def _rope_order(t):
    shape = t.shape
    t = t.reshape(shape[:-1] + (2, 2, ROPE_FREQS))
    return jnp.swapaxes(t, -3, -2).reshape(shape)


def _attn_weights(w_in, w_out, sink):
    wq = w_in[:, :ATTN_WIDTH].reshape(D_MODEL, N_HEADS, HEAD_DIM)
    wq = _slot_order(_rope_order(wq), 1).reshape(D_MODEL, ATTN_WIDTH)
    wk = w_in[:, ATTN_WIDTH:ATTN_WIDTH + KV_WIDTH].reshape(D_MODEL, N_KV_HEADS, HEAD_DIM)
    wk = _rope_order(wk).reshape(D_MODEL, KV_WIDTH)
    wv = w_in[:, ATTN_WIDTH + KV_WIDTH:ATTN_WIDTH + 2 * KV_WIDTH]
    wz = w_in[:, ATTN_WIDTH + 2 * KV_WIDTH:].reshape(D_MODEL, N_HEADS, HEAD_DIM)
    wz = _slot_order(wz, 1).reshape(D_MODEL, ATTN_WIDTH)
    w_in_p = jnp.concatenate([wq, wk, wv, wz], axis=1).astype(BF16)
    wo = _slot_order(w_out.reshape(N_HEADS, HEAD_DIM, D_MODEL), 0).reshape(ATTN_WIDTH, D_MODEL)
    sink_p = _slot_order(sink.astype(F32).reshape(N_HEADS), 0).reshape(2 * N_SLOTS)
    sink_tab = jnp.broadcast_to((sink_p * LOG2E)[:, None], (2 * N_SLOTS, LANES))
    return w_in_p, wo.astype(BF16), sink_tab


def _rope_tables():
    inv = ROPE_BASE ** (-jnp.arange(ROPE_FREQS, dtype=F32) / ROPE_FREQS)
    pos = jnp.arange(SEQ)
    row = (pos // GRID_W).astype(F32)[:, None] * inv
    col = (pos % GRID_W).astype(F32)[:, None] * inv
    lane = np.arange(LANES)
    w = lane % HEAD_DIM
    half, axis, f = w // 32, (w % 32) // 16, w % 16
    ang = jnp.where(jnp.asarray(axis == 0)[None, :], row[:, f], col[:, f])
    sign = jnp.asarray(np.where(half == 0, -1.0, 1.0), F32)[None, :]
    cos = jnp.concatenate([jnp.ones((CTX_LEN, LANES), F32), jnp.cos(ang)], axis=0)
    sin = jnp.concatenate([jnp.zeros((CTX_LEN, LANES), F32), jnp.sin(ang) * sign], axis=0)
    return cos, sin


def _s5_operators(lam_re, lam_im, log_dt, b_re, b_im, c_re, c_im, d_skip):
    hi = lax.Precision.HIGHEST
    t_len = CHUNK
    half_pi = 0.5 * math.pi
    lr, li = lam_re.astype(F32), lam_im.astype(F32)
    dt = jnp.exp(log_dt.astype(F32))[..., None]
    er, ei = lr * dt, li * dt
    ar1, ai1 = jnp.exp(er) * jnp.cos(ei) - 1.0, jnp.exp(er) * jnp.sin(ei)
    den = lr * lr + li * li
    gr, gi = (ar1 * lr + ai1 * li) / den, (ai1 * lr - ar1 * li) / den
    br_, bi_ = b_re.astype(F32), b_im.astype(F32)
    bbr = gr[..., None] * br_ - gi[..., None] * bi_
    bbi = gr[..., None] * bi_ + gi[..., None] * br_
    cr, ci = c_re.astype(F32), c_im.astype(F32)

    def rot(xr, xi, e_r, e_i, k, phi):
        th = e_i * k - phi
        return jnp.exp(e_r * k) * (xr * jnp.cos(th) - xi * jnp.sin(th))

    def pick(first, arr):
        sel = jnp.asarray(np.asarray(first, bool)).reshape((-1,) + (1,) * (arr.ndim - 1))
        return jnp.where(sel, arr[0][None], arr[1][None])

    k_lag = jnp.arange(t_len, dtype=F32)[:, None]
    e4 = (er[..., None, None], ei[..., None, None])
    wr = rot(bbr[:, :, :, None, :], bbi[:, :, :, None, :], *e4, k_lag, 0.0)
    wi = rot(bbr[:, :, :, None, :], bbi[:, :, :, None, :], *e4, k_lag, half_pi)
    kern = (jnp.einsum('dghp,dgpkj->dgkhj', cr, wr, precision=hi)
            - jnp.einsum('dghp,dgpkj->dgkhj', ci, wi, precision=hi))
    s_i = np.arange(t_len)[:, None]
    t_i = np.arange(t_len)[None, :]
    k_i = np.arange(t_len)[:, None, None]
    place = np.stack([(t_i - s_i)[None] == k_i, (s_i - t_i)[None] == k_i]).astype(np.float32)
    m5 = jnp.einsum('dgkhj,dkst->gsjth', kern, jnp.asarray(place), precision=hi)
    diag = (np.eye(t_len, dtype=np.float32)[:, None, :, None]
            * np.eye(SSM_GROUP, dtype=np.float32)[None, :, None, :])
    d_g = d_skip.astype(F32).reshape(SSM_GROUPS, 1, 1, 1, SSM_GROUP)
    m_mat = (m5 + jnp.asarray(diag)[None] * d_g).reshape(SSM_GROUPS, CHUNK_W, CHUNK_W)

    ws_fwd = [True, True, False, False, True, True, False, False]
    ws_phi = np.array([0, 1, 0, 1, 1, 0, 1, 0], np.float32) * half_pi
    s_ar = np.arange(t_len, dtype=np.float32)
    ws_k = np.where(np.array(ws_fwd)[:, None], t_len - 1 - s_ar[None], s_ar[None])
    e8r, e8i = pick(ws_fwd, er), pick(ws_fwd, ei)
    b8r, b8i = pick(ws_fwd, bbr), pick(ws_fwd, bbi)

    def gsjbp(x, has_j):
        x = jnp.moveaxis(x, 0, 2)
        if has_j:
            return jnp.transpose(x, (0, 3, 2, 1))[:, None]
        return jnp.transpose(x, (0, 2, 1))[:, None, None]

    ws_mat = rot(gsjbp(b8r, True), gsjbp(b8i, True), gsjbp(e8r, False), gsjbp(e8i, False),
                 jnp.asarray(ws_k.T)[None, :, None, :, None],
                 jnp.asarray(ws_phi)[None, None, None, :, None])
    ws_mat = ws_mat.reshape(SSM_GROUPS, CHUNK_W, 4 * STATE_W)

    wy_fwd = [True, True, False, False]
    wy_phi = np.array([0, -1, 0, -1], np.float32) * half_pi
    wy_k = np.where(np.array(wy_fwd)[:, None], s_ar[None] + 1, t_len - s_ar[None])
    c4r = jnp.transpose(pick(wy_fwd, cr), (1, 0, 3, 2))[:, :, :, None, :]
    c4i = jnp.transpose(pick(wy_fwd, ci), (1, 0, 3, 2))[:, :, :, None, :]
    e4r = jnp.moveaxis(pick(wy_fwd, er), 0, 1)[..., None, None]
    e4i = jnp.moveaxis(pick(wy_fwd, ei), 0, 1)[..., None, None]
    wy_mat = rot(c4r, c4i, e4r, e4i, jnp.asarray(wy_k)[None, :, None, :, None],
                 jnp.asarray(wy_phi)[None, :, None, None, None])
    wy_mat = wy_mat.reshape(SSM_GROUPS, 2 * STATE_W, CHUNK_W)

    cf_fwd = [True] * 4 + [False] * 4
    pw = np.array([[1, 2], [1, 2], [0, 1], [0, 1], [2, 1], [2, 1], [1, 0], [1, 0]], np.float32)
    pw_rows = np.tile(pw, (1, SUBLANES // 2))
    phi2 = np.where(np.arange(STATE_W) >= SSM_STATE, half_pi, -half_pi)
    cf_phi = np.stack([np.zeros(STATE_W), phi2] * 4).astype(np.float32)
    ec_r = jnp.moveaxis(pick(cf_fwd, er), 0, 1)
    ec_i = jnp.moveaxis(pick(cf_fwd, ei), 0, 1)
    ec_r = jnp.concatenate([ec_r, ec_r], axis=-1)[:, :, None, :]
    ec_i = jnp.concatenate([ec_i, ec_i], axis=-1)[:, :, None, :]
    consts = rot(1.0, 0.0, ec_r, ec_i, jnp.asarray(pw_rows * t_len)[None, :, :, None],
                 jnp.asarray(cf_phi)[None, :, None, :])
    consts = consts * jnp.asarray((pw_rows > 0).astype(np.float32))[None, :, :, None]
    return m_mat.astype(BF16), ws_mat.astype(BF16), wy_mat.astype(BF16), consts


def kernel(x, c, ctx, c_ctx, norm_w, w_ada, b_ada, attn_w_in, attn_sink, attn_w_out,
           ssm_w_in, ssm_lam_re, ssm_lam_im, ssm_log_dt, ssm_b_re, ssm_b_im, ssm_c_re, ssm_c_im,
           ssm_d, ssm_w_glu, ssm_w_out, final_norm_w):
    mod0, mod1 = _modulation(c, c_ctx, w_ada, b_ada)

    w_in0, w_out0, sink_tab = _attn_weights(attn_w_in[0], attn_w_out[0], attn_sink[0])
    cos_tab, sin_tab = _rope_tables()
    q, z0, kbd, vbd = _proj0(x, ctx, mod0, norm_w[0], cos_tab, sin_tab, w_in0)
    xc1 = _attention(q, z0, kbd, vbd, sink_tab, x, ctx, mod0, w_out0)

    u_g, z1 = _proj1(xc1, mod1, norm_w[1], ssm_w_in[0].astype(BF16))
    m_mat, ws_mat, wy_mat, consts = _s5_operators(
        ssm_lam_re[0], ssm_lam_im[0], ssm_log_dt[0], ssm_b_re[0], ssm_b_im[0],
        ssm_c_re[0], ssm_c_im[0], ssm_d[0])
    y_g = _s5_core(u_g, m_mat, ws_mat, wy_mat, consts)
    return _out1(y_g, z1, xc1, mod1, ssm_w_glu[0].astype(BF16), ssm_w_out[0].astype(BF16),
                 final_norm_w)
```

```python
import functools
import math

import jax
import jax.numpy as jnp
import numpy as np
from jax import lax
from jax.experimental import pallas as pl
from jax.experimental.pallas import tpu as pltpu

F32 = jnp.float32
BF16 = jnp.bfloat16

D_MODEL = 1024
BATCH = 4
SEQ = 4096
GRID_W = 64
CTX_LEN = 256
TOTAL = CTX_LEN + SEQ
HEAD_DIM = 64
N_HEADS = 16
N_KV_HEADS = 4
ATTN_WIDTH = N_HEADS * HEAD_DIM
KV_WIDTH = N_KV_HEADS * HEAD_DIM
BLOCK = 128
N_BLOCKS = TOTAL // BLOCK
N_CTX_BLOCKS = CTX_LEN // BLOCK
ROPE_BASE = 10000.0
ROPE_FREQS = HEAD_DIM // 4
SSM_GROUP = 16
SSM_GROUPS = D_MODEL // SSM_GROUP
SSM_STATE = 64
NORM_EPS = 1e-6
NEG_INF = -1e30

LANES = 128
SUBLANES = 8
N_SLOTS = ATTN_WIDTH // LANES
N_KV_PAIRS = KV_WIDTH // LANES
SLOTS_PER_M = N_SLOTS // N_KV_PAIRS
UNIT_SLOTS = 2
LOG2E = math.log2(math.e)
Q_SCALE = HEAD_DIM ** -0.5 * LOG2E
ROW_TILE = 256
N_ROW_TILES = TOTAL // ROW_TILE
CHUNK = 16
N_CHUNKS = TOTAL // CHUNK
N_CTX_CHUNKS = CTX_LEN // CHUNK
CHUNK_W = CHUNK * SSM_GROUP
CHUNKS_PER_TILE = ROW_TILE // CHUNK
PAIRS_PER_TILE = CHUNKS_PER_TILE // 2
STATE_W = 2 * SSM_STATE
SCAN_ROWS = N_CHUNKS * BATCH
CTX_SCAN_ROWS = N_CTX_CHUNKS * BATCH
N_SCAN_BLOCKS = SCAN_ROWS // SUBLANES
N_CTX_SCAN_BLOCKS = CTX_SCAN_ROWS // SUBLANES
GROUP_BATCH = 4
OP_GROUP_BATCH = 4
POW_ROWS = 48

assert BATCH * 2 == SUBLANES


def _params(semantics, vmem_mb):
    return pltpu.CompilerParams(dimension_semantics=semantics,
                                vmem_limit_bytes=vmem_mb * 1024 * 1024)


def _mod_kernel(c_ref, w_ref, b_ref, o_ref):
    c = c_ref[...]
    a = c * jax.nn.sigmoid(c)
    o_ref[0] = jnp.dot(a, w_ref[0], preferred_element_type=F32,
                       precision=lax.Precision.HIGHEST) + b_ref[0]


def _modulation(c, c_ctx, w_ada, b_ada):
    depth = w_ada.shape[0]
    rows = jnp.zeros((SUBLANES, D_MODEL), F32).at[:BATCH].set(c).at[BATCH].set(c_ctx)
    n_col = 3
    out = pl.pallas_call(
        _mod_kernel,
        grid=(depth, n_col),
        in_specs=[
            pl.BlockSpec((SUBLANES, D_MODEL), lambda l, j: (0, 0)),
            pl.BlockSpec((1, D_MODEL, D_MODEL), lambda l, j: (l, 0, j)),
            pl.BlockSpec((1, 1, D_MODEL), lambda l, j: (l, 0, j)),
        ],
        out_specs=pl.BlockSpec((1, SUBLANES, D_MODEL), lambda l, j: (l, 0, j)),
        out_shape=jax.ShapeDtypeStruct((depth, SUBLANES, 3 * D_MODEL), F32),
        compiler_params=_params(("arbitrary", "arbitrary"), 32),
        name="adaln_modulation",
    )(rows, w_ada, b_ada.reshape(depth, 1, 3 * D_MODEL))
    tabs = []
    for l in range(depth):
        lat = out[l, :BATCH].reshape(BATCH, 3, D_MODEL)
        cx = jnp.broadcast_to(out[l, BATCH].reshape(1, 3, D_MODEL), (BATCH, 3, D_MODEL))
        tab = jnp.stack([cx, lat], axis=1)
        tabs.append(jnp.pad(tab, ((0, 0), (0, 0), (0, SUBLANES - 3), (0, 0))))
    return tabs


def _modulated_norm(xt, nw, mod_ref, b=0):
    ms = jnp.mean(xt * xt, axis=-1, keepdims=True)
    y = xt * lax.rsqrt(ms + NORM_EPS) * nw
    return y * (1.0 + mod_ref[b, 0, 1:2, :]) + mod_ref[b, 0, 0:1, :]


def _lane_block_transpose(vs):
    n = len(vs)
    width = LANES // n
    blk = lax.broadcasted_iota(jnp.int32, vs[0].shape, 1) // width
    x = list(vs)
    d = n // 2
    while d >= 1:
        clear = (blk & d) == 0
        y = list(x)
        for i in range(n):
            if i & d == 0:
                a, b = x[i], x[i + d]
                y[i] = jnp.where(clear, a, pltpu.roll(b, width * d, 1))
                y[i + d] = jnp.where(clear, pltpu.roll(a, LANES - width * d, 1), b)
        x = y
        d //= 2
    return x


def _chunk_slot(b, cc):
    return (cc // 2) * (2 * BATCH) + 2 * b + (cc % 2)


def _proj0_kernel(x_ref, c_ref, mod_ref, nw_ref, cos_ref, sin_ref, w_ref,
                  q_ref, z_ref, kbd_ref, vbd_ref):
    i = pl.program_id(1)
    xt = jnp.where(i == 0, c_ref[0], x_ref[0])
    h = _modulated_norm(xt, nw_ref[...], mod_ref).astype(BF16)
    cos = cos_ref[...]
    sin = sin_ref[...]
    lane = lax.broadcasted_iota(jnp.int32, (ROW_TILE, LANES), 1)
    first_half = (lane % HEAD_DIM) < (HEAD_DIM // 2)
    low = lax.broadcasted_iota(jnp.int32, (BLOCK, LANES), 1) < HEAD_DIM

    def rope(t):
        partner = jnp.where(first_half, pltpu.roll(t, LANES - HEAD_DIM // 2, 1),
                            pltpu.roll(t, HEAD_DIM // 2, 1))
        return t * cos + partner * sin

    q = jnp.dot(h, w_ref[:, :ATTN_WIDTH], preferred_element_type=F32)
    for j in range(N_SLOTS):
        m, gi = divmod(j, SLOTS_PER_M)
        qj = (rope(q[:, LANES * j:LANES * (j + 1)]) * Q_SCALE).astype(BF16)
        for blk in range(ROW_TILE // BLOCK):
            q_ref[0, blk, m, BLOCK * gi:BLOCK * (gi + 1), :] = qj[BLOCK * blk:BLOCK * (blk + 1)]
    k = jnp.dot(h, w_ref[:, ATTN_WIDTH:ATTN_WIDTH + KV_WIDTH], preferred_element_type=F32)
    v = jnp.dot(h, w_ref[:, ATTN_WIDTH + KV_WIDTH:ATTN_WIDTH + 2 * KV_WIDTH],
                preferred_element_type=F32)
    for m in range(KV_WIDTH // LANES):
        sl = slice(LANES * m, LANES * (m + 1))
        kr = rope(k[:, sl])
        vm = v[:, sl]
        for blk in range(ROW_TILE // BLOCK):
            rows = slice(BLOCK * blk, BLOCK * (blk + 1))
            kbd_ref[0, blk, m, :BLOCK, :] = jnp.where(low, kr[rows], 0.0).astype(BF16)
            kbd_ref[0, blk, m, BLOCK:, :] = jnp.where(low, 0.0, kr[rows]).astype(BF16)
            vbd_ref[0, blk, m, :BLOCK, :] = jnp.where(low, vm[rows], 0.0).astype(BF16)
            vbd_ref[0, blk, m, BLOCK:, :] = jnp.where(low, 0.0, vm[rows]).astype(BF16)
    z = jnp.dot(h, w_ref[:, ATTN_WIDTH + 2 * KV_WIDTH:], preferred_element_type=F32)
    z_ref[0] = z.astype(BF16)


def _proj0(x, ctx, mod, norm_w, cos_tab, sin_tab, w_in):
    n_col = w_in.shape[1]
    blocks_per_tile = ROW_TILE // BLOCK
    kv_shape = jax.ShapeDtypeStruct((BATCH, N_BLOCKS, KV_WIDTH // LANES, 2 * BLOCK, LANES), BF16)
    kv_spec = pl.BlockSpec((1, blocks_per_tile, KV_WIDTH // LANES, 2 * BLOCK, LANES),
                           lambda b, i: (b, i, 0, 0, 0))
    row_spec = pl.BlockSpec((1, ROW_TILE, D_MODEL), lambda b, i: (b, i, 0))
    return pl.pallas_call(
        _proj0_kernel,
        grid=(BATCH, N_ROW_TILES),
        in_specs=[
            pl.BlockSpec((1, ROW_TILE, D_MODEL), lambda b, i: (b, jnp.maximum(i - 1, 0), 0)),
            pl.BlockSpec((1, ROW_TILE, D_MODEL), lambda b, i: (b, 0, 0)),
            pl.BlockSpec((1, 1, SUBLANES, D_MODEL), lambda b, i: (b, jnp.minimum(i, 1), 0, 0)),
            pl.BlockSpec((1, D_MODEL), lambda b, i: (0, 0)),
            pl.BlockSpec((ROW_TILE, LANES), lambda b, i: (i, 0)),
            pl.BlockSpec((ROW_TILE, LANES), lambda b, i: (i, 0)),
            pl.BlockSpec((D_MODEL, n_col), lambda b, i: (0, 0)),
        ],
        out_specs=[
            pl.BlockSpec((1, blocks_per_tile, N_KV_PAIRS, SLOTS_PER_M * BLOCK, LANES),
                         lambda b, i: (b, i, 0, 0, 0)),
            row_spec, kv_spec, kv_spec],
        out_shape=[
            jax.ShapeDtypeStruct((BATCH, N_BLOCKS, N_KV_PAIRS, SLOTS_PER_M * BLOCK, LANES), BF16),
            jax.ShapeDtypeStruct((BATCH, TOTAL, ATTN_WIDTH), BF16),
            kv_shape, kv_shape,
        ],
        compiler_params=_params(("arbitrary", "arbitrary"), 48),
        name="attn_projection",
    )(x, ctx, mod, norm_w.reshape(1, D_MODEL), cos_tab, sin_tab, w_in)


def _attn_kernel(q_ref, z_ref, kl_ref, kc_ref, kr_ref, kx_ref, vl_ref, vc_ref, vr_ref, vx_ref,
                 sink_ref, x_ref, c_ref, mod_ref, wo_ref, o_ref):
    i = pl.program_id(1)
    is_lat = i >= N_CTX_BLOCKS
    n = i - N_CTX_BLOCKS
    far = 4 * BLOCK
    lim_left = jnp.where(jnp.logical_and(is_lat, n >= 1), 0, far)
    lim_right = jnp.where(jnp.logical_and(is_lat, n <= SEQ // BLOCK - 2), 0, -far)
    lim_center = jnp.where(is_lat, far, -far)
    q_rows = UNIT_SLOTS * BLOCK
    row = lax.broadcasted_iota(jnp.int32, (q_rows, 2 * BLOCK), 0)
    lane = lax.broadcasted_iota(jnp.int32, (q_rows, 2 * BLOCK), 1)
    d = (lane % BLOCK) - (row % BLOCK)
    masks = [d >= lim_left, d <= lim_center, d <= lim_right, None, None]
    low = lax.broadcasted_iota(jnp.int32, (BLOCK, LANES), 1) < HEAD_DIM

    def kpiece(p, m):
        if p == 0:
            return kl_ref[0, 0, m], vl_ref[0, 0, m]
        if p == 1:
            return kc_ref[0, 0, m], vc_ref[0, 0, m]
        if p == 2:
            return kr_ref[0, 0, m], vr_ref[0, 0, m]
        return kx_ref[0, p - 3, m], vx_ref[0, p - 3, m]

    n_piece = 3 + N_CTX_BLOCKS
    units = [(m, h) for m in range(N_KV_PAIRS) for h in range(SLOTS_PER_M // UNIT_SLOTS)]

    def scores(m, h):
        qu = q_ref[0, 0, m, q_rows * h:q_rows * (h + 1), :]
        s_list = []
        for p in range(n_piece):
            kbd, _ = kpiece(p, m)
            s = lax.dot_general(qu, kbd, (((1,), (1,)), ((), ())), preferred_element_type=F32)
            if masks[p] is not None:
                s = jnp.where(masks[p], s, NEG_INF)
            s_list.append(s)
        return s_list

    def finish(m, h, s_list, y):
        slot0 = SLOTS_PER_M * m + UNIT_SLOTS * h
        mx = s_list[0]
        for s in s_list[1:]:
            mx = jnp.maximum(mx, s)

        def sink_col(hs):
            return jnp.concatenate(
                [jnp.broadcast_to(sink_ref[2 * (slot0 + gi) + hs:2 * (slot0 + gi) + hs + 1, 0:1],
                                  (BLOCK, 1)) for gi in range(UNIT_SLOTS)], axis=0)

        sink_a, sink_b = sink_col(0), sink_col(1)
        m_a = jnp.maximum(jnp.max(mx[:, :BLOCK], axis=1, keepdims=True), sink_a)
        m_b = jnp.maximum(jnp.max(mx[:, BLOCK:], axis=1, keepdims=True), sink_b)
        mb_a = jnp.broadcast_to(m_a, (q_rows, BLOCK))
        mb_b = jnp.broadcast_to(m_b, (q_rows, BLOCK))
        acc = jnp.zeros((q_rows, LANES), F32)
        esum_a = jnp.zeros((q_rows, BLOCK), F32)
        esum_b = jnp.zeros((q_rows, BLOCK), F32)
        for p in range(n_piece):
            _, vbd = kpiece(p, m)
            e_a = jnp.exp2(s_list[p][:, :BLOCK] - mb_a)
            e_b = jnp.exp2(s_list[p][:, BLOCK:] - mb_b)
            esum_a = esum_a + e_a
            esum_b = esum_b + e_b
            pb = jnp.concatenate([e_a, e_b], axis=1).astype(BF16)
            acc = acc + jnp.dot(pb, vbd, preferred_element_type=F32)
        inv_a = 1.0 / (jnp.sum(esum_a, axis=1, keepdims=True) + jnp.exp2(sink_a - m_a))
        inv_b = 1.0 / (jnp.sum(esum_b, axis=1, keepdims=True) + jnp.exp2(sink_b - m_b))
        outs = []
        for gi in range(UNIT_SLOTS):
            j = slot0 + gi
            rows = slice(BLOCK * gi, BLOCK * (gi + 1))
            inv = jnp.where(low, inv_a[rows], inv_b[rows])
            zj = z_ref[0, :, LANES * j:LANES * (j + 1)].astype(F32)
            outs.append((acc[rows] * inv * (zj * jax.nn.sigmoid(zj))).astype(BF16))
        g = jnp.concatenate(outs, axis=1)
        part = jnp.dot(g, wo_ref[LANES * slot0:LANES * (slot0 + UNIT_SLOTS), :],
                       preferred_element_type=F32)
        return part if y is None else y + part

    y = None
    pending = scores(*units[0])
    for u, (m, h) in enumerate(units):
        nxt = scores(*units[u + 1]) if u + 1 < len(units) else None
        y = finish(m, h, pending, y)
        pending = nxt
    resid = jnp.where(is_lat, x_ref[0], c_ref[0])
    o_ref[0] = resid + mod_ref[0, 0, 2:3, :] * y


def _attention(q, z, kbd, vbd, sink_tab, x, ctx, mod, w_out):
    last = N_BLOCKS - 1
    n_m = KV_WIDTH // LANES
    row_spec = pl.BlockSpec((1, BLOCK, ATTN_WIDTH), lambda b, i: (b, i, 0))

    def kv_spec(off):
        return pl.BlockSpec((1, 1, n_m, 2 * BLOCK, LANES),
                            lambda b, i: (b, jnp.clip(i + off, 0, last), 0, 0, 0))

    ctx_kv_spec = pl.BlockSpec((1, N_CTX_BLOCKS, n_m, 2 * BLOCK, LANES),
                               lambda b, i: (b, 0, 0, 0, 0))
    return pl.pallas_call(
        _attn_kernel,
        grid=(BATCH, N_BLOCKS),
        in_specs=[
            pl.BlockSpec((1, 1, n_m, SLOTS_PER_M * BLOCK, LANES), lambda b, i: (b, i, 0, 0, 0)),
            row_spec,
            kv_spec(-1), kv_spec(0), kv_spec(1), ctx_kv_spec,
            kv_spec(-1), kv_spec(0), kv_spec(1), ctx_kv_spec,
            pl.BlockSpec((2 * N_SLOTS, LANES), lambda b, i: (0, 0)),
            pl.BlockSpec((1, BLOCK, D_MODEL),
                         lambda b, i: (b, jnp.maximum(i - N_CTX_BLOCKS, 0), 0)),
            pl.BlockSpec((1, BLOCK, D_MODEL),
                         lambda b, i: (b, jnp.minimum(i, N_CTX_BLOCKS - 1), 0)),
            pl.BlockSpec((1, 1, SUBLANES, D_MODEL),
                         lambda b, i: (b, jnp.minimum(i // N_CTX_BLOCKS, 1), 0, 0)),
            pl.BlockSpec((ATTN_WIDTH, D_MODEL), lambda b, i: (0, 0)),
        ],
        out_specs=pl.BlockSpec((1, BLOCK, D_MODEL), lambda b, i: (b, i, 0)),
        out_shape=jax.ShapeDtypeStruct((BATCH, TOTAL, D_MODEL), F32),
        compiler_params=_params(("arbitrary", "arbitrary"), 48),
        name="window_attention",
    )(q, z, kbd, kbd, kbd, kbd, vbd, vbd, vbd, vbd, sink_tab, x, ctx, mod, w_out)


def _proj1_kernel(x_ref, mod_ref, nw_ref, w_ref, u_ref, z_ref, scr_ref):
    h = jnp.concatenate(
        [_modulated_norm(x_ref[b], nw_ref[...], mod_ref, b).astype(BF16) for b in range(BATCH)],
        axis=0)
    z = jnp.dot(h, w_ref[:, D_MODEL:], preferred_element_type=F32)
    for b in range(BATCH):
        z_ref[b] = z[ROW_TILE * b:ROW_TILE * (b + 1)].astype(BF16)
    u = jnp.dot(h, w_ref[:, :D_MODEL], preferred_element_type=F32)
    blocks = [None] * (BATCH * CHUNKS_PER_TILE)
    for b in range(BATCH):
        for cc in range(CHUNKS_PER_TILE):
            r0 = ROW_TILE * b + CHUNK * cc
            blocks[_chunk_slot(b, cc)] = u[r0:r0 + CHUNK]
    u_perm = jnp.concatenate(blocks, axis=0)
    n_slab = D_MODEL // LANES
    for k in range(n_slab):
        scr_ref[k] = u_perm[:, LANES * k:LANES * (k + 1)]
    groups_per_slab = LANES // SSM_GROUP
    for k in range(n_slab):
        for pp in range(PAIRS_PER_TILE // 2):
            parts = []
            for p in (2 * pp, 2 * pp + 1):
                rows = [scr_ref[k, pl.ds(CHUNK * SUBLANES * p + s, SUBLANES, stride=CHUNK), :]
                        for s in range(CHUNK)]
                parts.append([_lane_block_transpose(rows[SUBLANES * m2:SUBLANES * (m2 + 1)])
                              for m2 in range(CHUNK // SUBLANES)])
            for m2 in range(CHUNK // SUBLANES):
                for gl in range(groups_per_slab):
                    val = jnp.concatenate([parts[0][m2][gl], parts[1][m2][gl]], axis=0)
                    u_ref[groups_per_slab * k + gl, 2 * SUBLANES * pp:2 * SUBLANES * (pp + 1),
                          LANES * m2:LANES * (m2 + 1)] = val.astype(BF16)


def _proj1(xc, mod, norm_w, w_in):
    row_spec = pl.BlockSpec((BATCH, ROW_TILE, D_MODEL), lambda i: (0, i, 0))
    tile_rows = PAIRS_PER_TILE * SUBLANES
    return pl.pallas_call(
        _proj1_kernel,
        grid=(N_ROW_TILES,),
        in_specs=[
            row_spec,
            pl.BlockSpec((BATCH, 1, SUBLANES, D_MODEL), lambda i: (0, jnp.minimum(i, 1), 0, 0)),
            pl.BlockSpec((1, D_MODEL), lambda i: (0, 0)),
            pl.BlockSpec((D_MODEL, 2 * D_MODEL), lambda i: (0, 0)),
        ],
        out_specs=[pl.BlockSpec((SSM_GROUPS, tile_rows, CHUNK_W), lambda i: (0, i, 0)), row_spec],
        out_shape=[jax.ShapeDtypeStruct((SSM_GROUPS, SCAN_ROWS, CHUNK_W), BF16),
                   jax.ShapeDtypeStruct((BATCH, TOTAL, D_MODEL), BF16)],
        scratch_shapes=[pltpu.VMEM((D_MODEL // LANES, BATCH * ROW_TILE, LANES), F32)],
        compiler_params=_params(("arbitrary",), 56),
        name="ssm_projection",
    )(xc, mod, norm_w.reshape(1, D_MODEL), w_in)


def _s5_kernel(u_ref, m_ref, ws_ref, wy_ref, cst_ref, y_ref, s4_ref, xp_ref):
    for g in range(GROUP_BATCH):
        s4_ref[g] = jnp.dot(u_ref[g], ws_ref[g], preferred_element_type=F32)
    even = lax.broadcasted_iota(jnp.int32, (SUBLANES, STATE_W), 0) % 2 == 0
    down = 1
    up = SUBLANES - 1
    fwd = slice(0, STATE_W)
    bwd = slice(STATE_W, 2 * STATE_W)
    fwd_sw = slice(2 * STATE_W, 3 * STATE_W)
    bwd_sw = slice(3 * STATE_W, 4 * STATE_W)

    def step(j, carry):
        jb = jnp.where(j < N_CTX_SCAN_BLOCKS, N_CTX_SCAN_BLOCKS - 1 - j,
                       N_SCAN_BLOCKS - 1 + N_CTX_SCAN_BLOCKS - j)
        rf = pl.ds(pl.multiple_of(j * SUBLANES, SUBLANES), SUBLANES)
        rb = pl.ds(pl.multiple_of(jb * SUBLANES, SUBLANES), SUBLANES)
        new = []
        for g in range(GROUP_BATCH):
            cf, cfs, cb, cbs = carry[4 * g:4 * g + 4]
            p1f, p2f, q1f, q2f = cst_ref[g, 0], cst_ref[g, 1], cst_ref[g, 2], cst_ref[g, 3]
            p1b, p2b, q1b, q2b = cst_ref[g, 4], cst_ref[g, 5], cst_ref[g, 6], cst_ref[g, 7]
            zf = s4_ref[g, rf, fwd]
            zfs = s4_ref[g, rf, fwd_sw]
            rzf = pltpu.roll(zf, down, 0)
            rzfs = pltpu.roll(zfs, down, 0)
            xf = p1f * cf + p2f * cfs + (zf + q1f * rzf + q2f * rzfs)
            xfs = p1f * cfs - p2f * cf + (zfs + q1f * rzfs - q2f * rzf)
            xp_ref[g, rf, fwd] = jnp.where(even, cf, pltpu.roll(xf, down, 0))
            new += [jnp.where(even, pltpu.roll(xf, up, 0), xf),
                    jnp.where(even, pltpu.roll(xfs, up, 0), xfs)]
            zb = s4_ref[g, rb, bwd]
            zbs = s4_ref[g, rb, bwd_sw]
            rzb = pltpu.roll(zb, up, 0)
            rzbs = pltpu.roll(zbs, up, 0)
            xb = p1b * cb + p2b * cbs + (zb + q1b * rzb + q2b * rzbs)
            xbs = p1b * cbs - p2b * cb + (zbs + q1b * rzbs - q2b * rzb)
            xp_ref[g, rb, bwd] = jnp.where(even, pltpu.roll(xb, up, 0), cb)
            new += [jnp.where(even, xb, pltpu.roll(xb, down, 0)),
                    jnp.where(even, xbs, pltpu.roll(xbs, down, 0))]
        return tuple(new)

    zero = jnp.zeros((SUBLANES, STATE_W), F32)
    lax.fori_loop(0, N_SCAN_BLOCKS, step, (zero,) * (4 * GROUP_BATCH))
    for g in range(GROUP_BATCH):
        y_ref[g] = (
            jnp.dot(u_ref[g, CTX_SCAN_ROWS:, :], m_ref[g], preferred_element_type=F32)
            + jnp.dot(xp_ref[g, CTX_SCAN_ROWS:, :].astype(BF16), wy_ref[g],
                      preferred_element_type=F32))


def _s5_core(u_g, m_mat, ws_mat, wy_mat, consts):
    lat_rows = SCAN_ROWS - CTX_SCAN_ROWS

    def gspec(*tail):
        return pl.BlockSpec((GROUP_BATCH,) + tail, lambda i: (i,) + (0,) * len(tail))

    return pl.pallas_call(
        _s5_kernel,
        grid=(SSM_GROUPS // GROUP_BATCH,),
        in_specs=[
            gspec(SCAN_ROWS, CHUNK_W),
            gspec(CHUNK_W, CHUNK_W),
            gspec(CHUNK_W, 4 * STATE_W),
            gspec(2 * STATE_W, CHUNK_W),
            gspec(8, SUBLANES, STATE_W),
        ],
        out_specs=gspec(lat_rows, CHUNK_W),
        out_shape=jax.ShapeDtypeStruct((SSM_GROUPS, lat_rows, CHUNK_W), F32),
        scratch_shapes=[
            pltpu.VMEM((GROUP_BATCH, SCAN_ROWS, 4 * STATE_W), F32),
            pltpu.VMEM((GROUP_BATCH, SCAN_ROWS, 2 * STATE_W), F32),
        ],
        compiler_params=_params(("arbitrary",), 48),
        name="s5_scan",
    )(u_g, m_mat, ws_mat, wy_mat, consts)


def _out1_kernel(y_ref, z_ref, x_ref, mod_ref, wg_ref, wo_ref, fnw_ref, o_ref, scr_ref):
    n_slab = D_MODEL // LANES
    groups_per_slab = LANES // SSM_GROUP
    for k in range(n_slab):
        for p in range(PAIRS_PER_TILE):
            for m2 in range(CHUNK // SUBLANES):
                vals = [y_ref[groups_per_slab * k + gl, SUBLANES * p:SUBLANES * (p + 1),
                              LANES * m2:LANES * (m2 + 1)] for gl in range(groups_per_slab)]
                steps = _lane_block_transpose(vals)
                for s2 in range(SUBLANES):
                    t_idx = SUBLANES * m2 + s2
                    scr_ref[k, pl.ds(CHUNK * SUBLANES * p + t_idx, SUBLANES, stride=CHUNK), :] = (
                        steps[s2])
    y_perm = jnp.concatenate([scr_ref[k] for k in range(n_slab)], axis=1)
    for b in range(BATCH):
        y = jnp.concatenate(
            [y_perm[CHUNK * _chunk_slot(b, cc):CHUNK * (_chunk_slot(b, cc) + 1)]
             for cc in range(CHUNKS_PER_TILE)], axis=0)
        g = (0.5 * y * (1.0 + lax.erf(y * (2.0 ** -0.5)))).astype(BF16)
        t = jnp.dot(g, wg_ref[...], preferred_element_type=F32)
        z = z_ref[b].astype(F32)
        r = t[:, :D_MODEL] * jax.nn.sigmoid(t[:, D_MODEL:]) * (z * jax.nn.sigmoid(z))
        o = jnp.dot(r.astype(BF16), wo_ref[...], preferred_element_type=F32)
        x2 = x_ref[b] + mod_ref[b, 0, 2:3, :] * o
        ms = jnp.mean(x2 * x2, axis=-1, keepdims=True)
        o_ref[b] = x2 * lax.rsqrt(ms + NORM_EPS) * fnw_ref[...]


def _out1(y_g, z, xc, mod, w_glu, w_out, final_norm_w):
    ctx_tiles = CTX_LEN // ROW_TILE
    tile_rows = PAIRS_PER_TILE * SUBLANES
    lat_spec = pl.BlockSpec((BATCH, ROW_TILE, D_MODEL), lambda i: (0, i, 0))
    all_spec = pl.BlockSpec((BATCH, ROW_TILE, D_MODEL), lambda i: (0, i + ctx_tiles, 0))
    return pl.pallas_call(
        _out1_kernel,
        grid=(SEQ // ROW_TILE,),
        in_specs=[
            pl.BlockSpec((SSM_GROUPS, tile_rows, CHUNK_W), lambda i: (0, i, 0)),
            all_spec, all_spec,
            pl.BlockSpec((BATCH, 1, SUBLANES, D_MODEL), lambda i: (0, 1, 0, 0)),
            pl.BlockSpec((D_MODEL, 2 * D_MODEL), lambda i: (0, 0)),
            pl.BlockSpec((D_MODEL, D_MODEL), lambda i: (0, 0)),
            pl.BlockSpec((1, D_MODEL), lambda i: (0, 0)),
        ],
        out_specs=lat_spec,
        out_shape=jax.ShapeDtypeStruct((BATCH, SEQ, D_MODEL), F32),
        scratch_shapes=[pltpu.VMEM((D_MODEL // LANES, BATCH * ROW_TILE, LANES), F32)],
        compiler_params=_params(("arbitrary",), 56),
        name="ssm_output",
    )(y_g, z, xc, mod, w_glu, w_out, final_norm_w.reshape(1, D_MODEL))


def _slot_order(t, lead):
    n_m = KV_WIDTH // LANES
    gq = N_HEADS // N_KV_HEADS
    shape = t.shape
    t = t.reshape(shape[:lead] + (n_m, 2, gq) + shape[lead + 1:])
    perm = tuple(range(lead)) + (lead, lead + 2, lead + 1) + tuple(range(lead + 3, t.ndim))
    return jnp.transpose(t, perm)


def _rope_order(t):
    shape = t.shape
    t = t.reshape(shape[:-1] + (2, 2, ROPE_FREQS))
    return jnp.swapaxes(t, -3, -2).reshape(shape)


def _attn_weights(w_in, w_out, sink):
    wq = w_in[:, :ATTN_WIDTH].reshape(D_MODEL, N_HEADS, HEAD_DIM)
    wq = _slot_order(_rope_order(wq), 1).reshape(D_MODEL, ATTN_WIDTH)
    wk = w_in[:, ATTN_WIDTH:ATTN_WIDTH + KV_WIDTH].reshape(D_MODEL, N_KV_HEADS, HEAD_DIM)
    wk = _rope_order(wk).reshape(D_MODEL, KV_WIDTH)
    wv = w_in[:, ATTN_WIDTH + KV_WIDTH:ATTN_WIDTH + 2 * KV_WIDTH]
    wz = w_in[:, ATTN_WIDTH + 2 * KV_WIDTH:].reshape(D_MODEL, N_HEADS, HEAD_DIM)
    wz = _slot_order(wz, 1).reshape(D_MODEL, ATTN_WIDTH)
    w_in_p = jnp.concatenate([wq, wk, wv, wz], axis=1).astype(BF16)
    wo = _slot_order(w_out.reshape(N_HEADS, HEAD_DIM, D_MODEL), 0).reshape(ATTN_WIDTH, D_MODEL)
    sink_p = _slot_order(sink.astype(F32).reshape(N_HEADS), 0).reshape(2 * N_SLOTS)
    sink_tab = jnp.broadcast_to((sink_p * LOG2E)[:, None], (2 * N_SLOTS, LANES))
    return w_in_p, wo.astype(BF16), sink_tab


def _rope_tables():
    inv = ROPE_BASE ** (-jnp.arange(ROPE_FREQS, dtype=F32) / ROPE_FREQS)
    pos = jnp.arange(SEQ)
    row = (pos // GRID_W).astype(F32)[:, None] * inv
    col = (pos % GRID_W).astype(F32)[:, None] * inv
    lane = np.arange(LANES)
    w = lane % HEAD_DIM
    half, axis, f = w // 32, (w % 32) // 16, w % 16
    ang = jnp.where(jnp.asarray(axis == 0)[None, :], row[:, f], col[:, f])
    sign = jnp.asarray(np.where(half == 0, -1.0, 1.0), F32)[None, :]
    cos = jnp.concatenate([jnp.ones((CTX_LEN, LANES), F32), jnp.cos(ang)], axis=0)
    sin = jnp.concatenate([jnp.zeros((CTX_LEN, LANES), F32), jnp.sin(ang) * sign], axis=0)
    return cos, sin


def _s5_operators(lam_re, lam_im, log_dt, b_re, b_im, c_re, c_im, d_skip):
    t_len = CHUNK
    n_pow = 2 * t_len + 1
    lr, li = lam_re.astype(F32), lam_im.astype(F32)
    dt = jnp.exp(log_dt.astype(F32))[..., None]
    ks = jnp.arange(n_pow, dtype=F32)
    mag = jnp.exp((lr * dt)[..., None] * ks)
    ph = (li * dt)[..., None] * ks
    pr, pi = mag * jnp.cos(ph), mag * jnp.sin(ph)
    ar1, ai1 = pr[..., 1] - 1.0, pi[..., 1]
    den = lr * lr + li * li
    gr, gi = (ar1 * lr + ai1 * li) / den, (ai1 * lr - ar1 * li) / den
    br_, bi_ = b_re.astype(F32), b_im.astype(F32)
    bbr = gr[..., None] * br_ - gi[..., None] * bi_
    bbi = gr[..., None] * bi_ + gi[..., None] * br_
    pad_k = POW_ROWS - n_pow
    prt, pit = jnp.swapaxes(pr, 2, 3), jnp.swapaxes(pi, 2, 3)
    pw = jnp.pad(jnp.concatenate([prt, prt, pit, pit], axis=-1),
                 ((0, 0), (0, 0), (0, pad_k), (0, 0)))
    pwc = jnp.stack([jnp.concatenate([pr, pr], axis=2), jnp.concatenate([pi, pi], axis=2)], axis=2)
    pwc = jnp.pad(pwc, ((0, 0),) * 4 + ((0, pad_k),))
    brt, bit = jnp.swapaxes(bbr, 2, 3), jnp.swapaxes(bbi, 2, 3)
    bb = jnp.concatenate([brt, bit, -bit, brt, bit, brt, brt, -bit], axis=-1)
    crt = jnp.swapaxes(c_re.astype(F32), 2, 3)
    cit = jnp.swapaxes(c_im.astype(F32), 2, 3)
    cm = jnp.stack([jnp.concatenate([crt, -cit], axis=2),
                    jnp.concatenate([-cit, -crt], axis=2)], axis=2)
    dv = jnp.tile(d_skip.astype(F32).reshape(SSM_GROUPS, 1, SSM_GROUP), (1, 1, t_len))
    return _s5_operator_call(pw, pwc, bb, cm, dv)


def _operator_constants():
    t_len = CHUNK
    s_of_row = np.arange(CHUNK_W) // SSM_GROUP
    k_ar = np.arange(POW_ROWS)
    oh_f = (k_ar[None, :] == (t_len - 1 - s_of_row)[:, None]).astype(np.float32)
    oh_b = (k_ar[None, :] == s_of_row[:, None]).astype(np.float32)
    t_of_lane = np.arange(CHUNK_W) // SSM_GROUP
    expo = [t_of_lane + 1, t_len - t_of_lane, t_of_lane, t_len - 1 - t_of_lane]
    sel = np.stack([(k_ar[:, None] == e[None, :]) for e in expo]).astype(np.float32)
    h_of_lane = np.arange(CHUNK_W) % SSM_GROUP
    tl = (np.arange(SSM_GROUP)[:, None] == h_of_lane[None, :]).astype(np.float32)
    dmask = ((s_of_row[:, None] == t_of_lane[None, :])
             & ((np.arange(CHUNK_W) % SSM_GROUP)[:, None] == h_of_lane[None, :])).astype(np.float32)
    return oh_f, oh_b, sel, tl, dmask


def _s5_op_kernel(*refs):
    for gg in range(OP_GROUP_BATCH):
        _s5_op_group(gg, *refs)


def _s5_op_group(gg, pw_ref, pwc_ref, bb_ref, cm_ref, dv_ref, ohf_ref, ohb_ref, sel_ref, tl_ref,
                 dmask_ref, m_ref, ws_ref, wy_ref, cst_ref):
    t_len = CHUNK

    def mm(a, b):
        return jnp.dot(a, b, preferred_element_type=F32, precision=lax.Precision.HIGHEST)

    def split(a):
        hi = a.astype(BF16)
        return hi, (a - hi.astype(F32)).astype(BF16)

    def pick_rows(onehot, table):
        hi, lo = split(table)
        return (jnp.dot(onehot, hi, preferred_element_type=F32)
                + jnp.dot(onehot, lo, preferred_element_type=F32))

    def spread_lanes(table, onehot):
        hi, lo = split(table)
        return (jnp.dot(hi, onehot, preferred_element_type=F32)
                + jnp.dot(lo, onehot, preferred_element_type=F32))

    main, swapped = [], []
    for d, oh_ref in ((0, ohf_ref), (1, ohb_ref)):
        pp = pick_rows(oh_ref[...], pw_ref[d, gg])
        p_re, p_im = pp[:, :STATE_W], pp[:, STATE_W:]
        b0, b1, b2, b3 = [jnp.concatenate([bb_ref[d, gg, :, STATE_W * i:STATE_W * (i + 1)]] * t_len,
                                          axis=0) for i in range(4)]
        main.append(p_re * b0 + p_im * b1)
        swapped.append(p_re * b2 + p_im * b3)
    ws_ref[gg] = jnp.concatenate(main + swapped, axis=1).astype(BF16)

    c_tiled = [[spread_lanes(cm_ref[d, gg, i], tl_ref[...]) for i in range(2)] for d in range(2)]

    def block(d, pat):
        sel = sel_ref[pat]
        return (spread_lanes(pwc_ref[d, gg, 0], sel) * c_tiled[d][0]
                + spread_lanes(pwc_ref[d, gg, 1], sel) * c_tiled[d][1])

    wy_ref[gg] = jnp.concatenate([block(0, 0), block(1, 1)], axis=0).astype(BF16)

    kt_f = mm(bb_ref[0, gg, :, :STATE_W], block(0, 2))
    kt_b = mm(bb_ref[1, gg, :, :STATE_W], block(1, 3))
    lane = lax.broadcasted_iota(jnp.int32, (SSM_GROUP, CHUNK_W), 1)
    skip = dv_ref[gg]
    for s in range(t_len):
        fwd = kt_f if s == 0 else pltpu.roll(kt_f, SSM_GROUP * s, 1)
        back = t_len - 1 - s
        bwd = kt_b if back == 0 else pltpu.roll(kt_b, CHUNK_W - SSM_GROUP * back, 1)
        rows = (jnp.where(lane >= SSM_GROUP * s, fwd, 0.0)
                + jnp.where(lane < SSM_GROUP * (s + 1), bwd, 0.0)
                + dmask_ref[SSM_GROUP * s:SSM_GROUP * (s + 1), :] * skip)
        m_ref[gg, SSM_GROUP * s:SSM_GROUP * (s + 1), :] = rows.astype(BF16)

    even = lax.broadcasted_iota(jnp.int32, (SUBLANES, STATE_W), 0) % 2 == 0
    sign = jnp.where(lax.broadcasted_iota(jnp.int32, (1, STATE_W), 1) < SSM_STATE, -1.0, 1.0)

    def w12(d, k):
        row = pw_ref[d, gg, k:k + 1, :]
        return row[:, :STATE_W], row[:, STATE_W:] * sign

    zero = (jnp.zeros((1, STATE_W), F32),) * 2
    pairs = [(w12(0, t_len), w12(0, 2 * t_len)), (zero, w12(0, t_len)),
             (w12(1, 2 * t_len), w12(1, t_len)), (w12(1, t_len), zero)]
    idx = 0
    for top, bot in pairs:
        for part in range(2):
            cst_ref[gg, idx] = jnp.where(even, jnp.broadcast_to(top[part], (SUBLANES, STATE_W)),
                                         jnp.broadcast_to(bot[part], (SUBLANES, STATE_W)))
            idx += 1


def _s5_operator_call(pw, pwc, bb, cm, dv):
    oh_f, oh_b, sel, tl, dmask = _operator_constants()
    consts = [jnp.asarray(oh_f, BF16), jnp.asarray(oh_b, BF16), jnp.asarray(sel, BF16),
              jnp.asarray(tl, BF16), jnp.asarray(dmask)]

    def per_group(*tail):
        n = len(tail)
        return pl.BlockSpec((2, OP_GROUP_BATCH) + tail, lambda g: (0, g) + (0,) * n)

    def whole(a):
        return pl.BlockSpec(a.shape, lambda g: (0,) * a.ndim)

    def out(*tail):
        return pl.BlockSpec((OP_GROUP_BATCH,) + tail, lambda g: (g,) + (0,) * len(tail))

    return pl.pallas_call(
        _s5_op_kernel,
        grid=(SSM_GROUPS // OP_GROUP_BATCH,),
        in_specs=[per_group(POW_ROWS, 2 * STATE_W), per_group(2, STATE_W, POW_ROWS),
                  per_group(SSM_GROUP, 4 * STATE_W), per_group(2, STATE_W, SSM_GROUP),
                  pl.BlockSpec((OP_GROUP_BATCH, 1, CHUNK_W), lambda g: (g, 0, 0))]
                 + [whole(a) for a in consts],
        out_specs=[out(CHUNK_W, CHUNK_W), out(CHUNK_W, 4 * STATE_W), out(2 * STATE_W, CHUNK_W),
                   out(8, SUBLANES, STATE_W)],
        out_shape=[jax.ShapeDtypeStruct((SSM_GROUPS, CHUNK_W, CHUNK_W), BF16),
                   jax.ShapeDtypeStruct((SSM_GROUPS, CHUNK_W, 4 * STATE_W), BF16),
                   jax.ShapeDtypeStruct((SSM_GROUPS, 2 * STATE_W, CHUNK_W), BF16),
                   jax.ShapeDtypeStruct((SSM_GROUPS, 8, SUBLANES, STATE_W), F32)],
        compiler_params=_params(("arbitrary",), 32),
        name="s5_operators",
    )(pw, pwc, bb, cm, dv, *consts)


def kernel(x, c, ctx, c_ctx, norm_w, w_ada, b_ada, attn_w_in, attn_sink, attn_w_out,
           ssm_w_in, ssm_lam_re, ssm_lam_im, ssm_log_dt, ssm_b_re, ssm_b_im, ssm_c_re, ssm_c_im,
           ssm_d, ssm_w_glu, ssm_w_out, final_norm_w):
    mod0, mod1 = _modulation(c, c_ctx, w_ada, b_ada)

    w_in0, w_out0, sink_tab = _attn_weights(attn_w_in[0], attn_w_out[0], attn_sink[0])
    cos_tab, sin_tab = _rope_tables()
    q, z0, kbd, vbd = _proj0(x, ctx, mod0, norm_w[0], cos_tab, sin_tab, w_in0)
    xc1 = _attention(q, z0, kbd, vbd, sink_tab, x, ctx, mod0, w_out0)

    u_g, z1 = _proj1(xc1, mod1, norm_w[1], ssm_w_in[0].astype(BF16))
    m_mat, ws_mat, wy_mat, consts = _s5_operators(
        ssm_lam_re[0], ssm_lam_im[0], ssm_log_dt[0], ssm_b_re[0], ssm_b_im[0],
        ssm_c_re[0], ssm_c_im[0], ssm_d[0])
    y_g = _s5_core(u_g, m_mat, ws_mat, wy_mat, consts)
    return _out1(y_g, z1, xc1, mod1, ssm_w_glu[0].astype(BF16), ssm_w_out[0].astype(BF16),
                 final_norm_w)
```

```python
import functools
import math

import jax
import jax.numpy as jnp
import numpy as np
from jax import lax
from jax.experimental import pallas as pl
from jax.experimental.pallas import tpu as pltpu

F32 = jnp.float32
BF16 = jnp.bfloat16

D_MODEL = 1024
BATCH = 4
SEQ = 4096
GRID_W = 64
CTX_LEN = 256
TOTAL = CTX_LEN + SEQ
HEAD_DIM = 64
N_HEADS = 16
N_KV_HEADS = 4
ATTN_WIDTH = N_HEADS * HEAD_DIM
KV_WIDTH = N_KV_HEADS * HEAD_DIM
BLOCK = 128
N_BLOCKS = TOTAL // BLOCK
N_CTX_BLOCKS = CTX_LEN // BLOCK
ROPE_BASE = 10000.0
ROPE_FREQS = HEAD_DIM // 4
SSM_GROUP = 16
SSM_GROUPS = D_MODEL // SSM_GROUP
SSM_STATE = 64
NORM_EPS = 1e-6
NEG_INF = -1e30

LANES = 128
SUBLANES = 8
N_SLOTS = ATTN_WIDTH // LANES
N_KV_PAIRS = KV_WIDTH // LANES
SLOTS_PER_M = N_SLOTS // N_KV_PAIRS
UNIT_SLOTS = 2
LOG2E = math.log2(math.e)
Q_SCALE = HEAD_DIM ** -0.5 * LOG2E
ROW_TILE = 256
N_ROW_TILES = TOTAL // ROW_TILE
CHUNK = 16
N_CHUNKS = TOTAL // CHUNK
N_CTX_CHUNKS = CTX_LEN // CHUNK
CHUNK_W = CHUNK * SSM_GROUP
CHUNKS_PER_TILE = ROW_TILE // CHUNK
PAIRS_PER_TILE = CHUNKS_PER_TILE // 2
SLOT_PITCH = 24
OUT_PHASES = 2
assert SLOT_PITCH >= CHUNK and SLOT_PITCH % SUBLANES == 0
STATE_W = 2 * SSM_STATE
SCAN_ROWS = N_CHUNKS * BATCH
CTX_SCAN_ROWS = N_CTX_CHUNKS * BATCH
N_SCAN_BLOCKS = SCAN_ROWS // SUBLANES
N_CTX_SCAN_BLOCKS = CTX_SCAN_ROWS // SUBLANES
GROUP_BATCH = 4
OP_GROUP_BATCH = 4
POW_ROWS = 48

assert BATCH * 2 == SUBLANES


def _params(semantics, vmem_mb):
    return pltpu.CompilerParams(dimension_semantics=semantics,
                                vmem_limit_bytes=vmem_mb * 1024 * 1024)


def _mod_kernel(c_ref, w_ref, b_ref, o_ref):
    c = c_ref[...]
    a = c * jax.nn.sigmoid(c)
    o_ref[0] = jnp.dot(a, w_ref[0], preferred_element_type=F32,
                       precision=lax.Precision.HIGHEST) + b_ref[0]


def _modulation(c, c_ctx, w_ada, b_ada):
    depth = w_ada.shape[0]
    rows = jnp.zeros((SUBLANES, D_MODEL), F32).at[:BATCH].set(c).at[BATCH].set(c_ctx)
    n_col = 3
    out = pl.pallas_call(
        _mod_kernel,
        grid=(depth, n_col),
        in_specs=[
            pl.BlockSpec((SUBLANES, D_MODEL), lambda l, j: (0, 0)),
            pl.BlockSpec((1, D_MODEL, D_MODEL), lambda l, j: (l, 0, j)),
            pl.BlockSpec((1, 1, D_MODEL), lambda l, j: (l, 0, j)),
        ],
        out_specs=pl.BlockSpec((1, SUBLANES, D_MODEL), lambda l, j: (l, 0, j)),
        out_shape=jax.ShapeDtypeStruct((depth, SUBLANES, 3 * D_MODEL), F32),
        compiler_params=_params(("arbitrary", "arbitrary"), 32),
        name="adaln_modulation",
    )(rows, w_ada, b_ada.reshape(depth, 1, 3 * D_MODEL))
    tabs = []
    for l in range(depth):
        lat = out[l, :BATCH].reshape(BATCH, 3, D_MODEL)
        cx = jnp.broadcast_to(out[l, BATCH].reshape(1, 3, D_MODEL), (BATCH, 3, D_MODEL))
        tab = jnp.stack([cx, lat], axis=1)
        tabs.append(jnp.pad(tab, ((0, 0), (0, 0), (0, SUBLANES - 3), (0, 0))))
    return tabs


def _modulated_norm(xt, nw, mod_ref, b=0):
    ms = jnp.mean(xt * xt, axis=-1, keepdims=True)
    y = xt * lax.rsqrt(ms + NORM_EPS) * nw
    return y * (1.0 + mod_ref[b, 0, 1:2, :]) + mod_ref[b, 0, 0:1, :]


def _lane_block_transpose(vs):
    n = len(vs)
    width = LANES // n
    blk = lax.broadcasted_iota(jnp.int32, vs[0].shape, 1) // width
    x = list(vs)
    d = n // 2
    while d >= 1:
        clear = (blk & d) == 0
        y = list(x)
        for i in range(n):
            if i & d == 0:
                a, b = x[i], x[i + d]
                y[i] = jnp.where(clear, a, pltpu.roll(b, width * d, 1))
                y[i + d] = jnp.where(clear, pltpu.roll(a, LANES - width * d, 1), b)
        x = y
        d //= 2
    return x


def _chunk_slot(b, cc):
    return (cc // 2) * (2 * BATCH) + 2 * b + (cc % 2)


def _proj0_kernel(x_ref, c_ref, mod_ref, nw_ref, cos_ref, sin_ref, w_ref,
                  q_ref, z_ref, kbd_ref, vbd_ref):
    is_ctx = pl.program_id(0) == 0
    h = jnp.concatenate(
        [_modulated_norm(jnp.where(is_ctx, c_ref[b], x_ref[b]), nw_ref[...], mod_ref, b).astype(BF16)
         for b in range(BATCH)], axis=0)
    cos = cos_ref[...]
    sin = sin_ref[...]
    lane = lax.broadcasted_iota(jnp.int32, (ROW_TILE, LANES), 1)
    first_half = (lane % HEAD_DIM) < (HEAD_DIM // 2)
    low = lax.broadcasted_iota(jnp.int32, (BLOCK, LANES), 1) < HEAD_DIM

    def rope(t):
        partner = jnp.where(first_half, pltpu.roll(t, LANES - HEAD_DIM // 2, 1),
                            pltpu.roll(t, HEAD_DIM // 2, 1))
        return t * cos + partner * sin

    q = jnp.dot(h, w_ref[:, :ATTN_WIDTH], preferred_element_type=F32)
    k = jnp.dot(h, w_ref[:, ATTN_WIDTH:ATTN_WIDTH + KV_WIDTH], preferred_element_type=F32)
    v = jnp.dot(h, w_ref[:, ATTN_WIDTH + KV_WIDTH:ATTN_WIDTH + 2 * KV_WIDTH],
                preferred_element_type=F32)
    z = jnp.dot(h, w_ref[:, ATTN_WIDTH + 2 * KV_WIDTH:], preferred_element_type=F32)
    for b in range(BATCH):
        tile = slice(ROW_TILE * b, ROW_TILE * (b + 1))
        z_ref[b] = z[tile].astype(BF16)
        for j in range(N_SLOTS):
            m, gi = divmod(j, SLOTS_PER_M)
            qj = (rope(q[tile, LANES * j:LANES * (j + 1)]) * Q_SCALE).astype(BF16)
            for blk in range(ROW_TILE // BLOCK):
                q_ref[b, blk, m, BLOCK * gi:BLOCK * (gi + 1), :] = qj[BLOCK * blk:BLOCK * (blk + 1)]
        for m in range(N_KV_PAIRS):
            sl = slice(LANES * m, LANES * (m + 1))
            kr = rope(k[tile, sl])
            vm = v[tile, sl]
            for blk in range(ROW_TILE // BLOCK):
                rows = slice(BLOCK * blk, BLOCK * (blk + 1))
                kbd_ref[b, blk, m, :BLOCK, :] = jnp.where(low, kr[rows], 0.0).astype(BF16)
                kbd_ref[b, blk, m, BLOCK:, :] = jnp.where(low, 0.0, kr[rows]).astype(BF16)
                vbd_ref[b, blk, m, :BLOCK, :] = jnp.where(low, vm[rows], 0.0).astype(BF16)
                vbd_ref[b, blk, m, BLOCK:, :] = jnp.where(low, 0.0, vm[rows]).astype(BF16)


def _proj0(x, ctx, mod, norm_w, cos_tab, sin_tab, w_in):
    n_col = w_in.shape[1]
    blocks_per_tile = ROW_TILE // BLOCK
    kv_shape = jax.ShapeDtypeStruct((BATCH, N_BLOCKS, N_KV_PAIRS, 2 * BLOCK, LANES), BF16)
    kv_spec = pl.BlockSpec((BATCH, blocks_per_tile, N_KV_PAIRS, 2 * BLOCK, LANES),
                           lambda i: (0, i, 0, 0, 0))
    row_spec = pl.BlockSpec((BATCH, ROW_TILE, D_MODEL), lambda i: (0, i, 0))
    once = pl.Buffered(1)
    return pl.pallas_call(
        _proj0_kernel,
        grid=(N_ROW_TILES,),
        in_specs=[
            pl.BlockSpec((BATCH, ROW_TILE, D_MODEL), lambda i: (0, jnp.maximum(i - 1, 0), 0)),
            pl.BlockSpec((BATCH, ROW_TILE, D_MODEL), lambda i: (0, 0, 0), pipeline_mode=once),
            pl.BlockSpec((BATCH, 1, SUBLANES, D_MODEL), lambda i: (0, jnp.minimum(i, 1), 0, 0)),
            pl.BlockSpec((1, D_MODEL), lambda i: (0, 0)),
            pl.BlockSpec((ROW_TILE, LANES), lambda i: (i, 0)),
            pl.BlockSpec((ROW_TILE, LANES), lambda i: (i, 0)),
            pl.BlockSpec((D_MODEL, n_col), lambda i: (0, 0), pipeline_mode=once),
        ],
        out_specs=[
            pl.BlockSpec((BATCH, blocks_per_tile, N_KV_PAIRS, SLOTS_PER_M * BLOCK, LANES),
                         lambda i: (0, i, 0, 0, 0)),
            row_spec, kv_spec, kv_spec],
        out_shape=[
            jax.ShapeDtypeStruct((BATCH, N_BLOCKS, N_KV_PAIRS, SLOTS_PER_M * BLOCK, LANES), BF16),
            jax.ShapeDtypeStruct((BATCH, TOTAL, ATTN_WIDTH), BF16),
            kv_shape, kv_shape,
        ],
        compiler_params=_params(("arbitrary",), 56),
        name="attn_projection",
    )(x, ctx, mod, norm_w.reshape(1, D_MODEL), cos_tab, sin_tab, w_in)


def _attn_kernel(q_ref, z_ref, kl_ref, kc_ref, kr_ref, kx_ref, vl_ref, vc_ref, vr_ref, vx_ref,
                 sink_ref, x_ref, c_ref, mod_ref, wo_ref, o_ref):
    i = pl.program_id(1)
    is_lat = i >= N_CTX_BLOCKS
    n = i - N_CTX_BLOCKS
    far = 4 * BLOCK
    lim_left = jnp.where(jnp.logical_and(is_lat, n >= 1), 0, far)
    lim_right = jnp.where(jnp.logical_and(is_lat, n <= SEQ // BLOCK - 2), 0, -far)
    lim_center = jnp.where(is_lat, far, -far)
    q_rows = UNIT_SLOTS * BLOCK
    row = lax.broadcasted_iota(jnp.int32, (q_rows, 2 * BLOCK), 0)
    lane = lax.broadcasted_iota(jnp.int32, (q_rows, 2 * BLOCK), 1)
    d = (lane % BLOCK) - (row % BLOCK)
    masks = [d >= lim_left, d <= lim_center, d <= lim_right, None, None]
    low = lax.broadcasted_iota(jnp.int32, (BLOCK, LANES), 1) < HEAD_DIM

    def kpiece(p, m):
        if p == 0:
            return kl_ref[0, 0, m], vl_ref[0, 0, m]
        if p == 1:
            return kc_ref[0, 0, m], vc_ref[0, 0, m]
        if p == 2:
            return kr_ref[0, 0, m], vr_ref[0, 0, m]
        return kx_ref[0, p - 3, m], vx_ref[0, p - 3, m]

    n_piece = 3 + N_CTX_BLOCKS
    units = [(m, h) for m in range(N_KV_PAIRS) for h in range(SLOTS_PER_M // UNIT_SLOTS)]

    def scores(m, h):
        qu = q_ref[0, 0, m, q_rows * h:q_rows * (h + 1), :]
        s_list = []
        for p in range(n_piece):
            kbd, _ = kpiece(p, m)
            s = lax.dot_general(qu, kbd, (((1,), (1,)), ((), ())), preferred_element_type=F32)
            if masks[p] is not None:
                s = jnp.where(masks[p], s, NEG_INF)
            s_list.append(s)
        return s_list

    def finish(m, h, s_list, y):
        slot0 = SLOTS_PER_M * m + UNIT_SLOTS * h
        mx = s_list[0]
        for s in s_list[1:]:
            mx = jnp.maximum(mx, s)

        def sink_col(hs):
            return jnp.concatenate(
                [jnp.broadcast_to(sink_ref[2 * (slot0 + gi) + hs:2 * (slot0 + gi) + hs + 1, 0:1],
                                  (BLOCK, 1)) for gi in range(UNIT_SLOTS)], axis=0)

        sink_a, sink_b = sink_col(0), sink_col(1)
        m_a = jnp.maximum(jnp.max(mx[:, :BLOCK], axis=1, keepdims=True), sink_a)
        m_b = jnp.maximum(jnp.max(mx[:, BLOCK:], axis=1, keepdims=True), sink_b)
        mb_a = jnp.broadcast_to(m_a, (q_rows, BLOCK))
        mb_b = jnp.broadcast_to(m_b, (q_rows, BLOCK))
        acc = jnp.zeros((q_rows, LANES), F32)
        esum_a = jnp.zeros((q_rows, BLOCK), F32)
        esum_b = jnp.zeros((q_rows, BLOCK), F32)
        for p in range(n_piece):
            _, vbd = kpiece(p, m)
            e_a = jnp.exp2(s_list[p][:, :BLOCK] - mb_a)
            e_b = jnp.exp2(s_list[p][:, BLOCK:] - mb_b)
            esum_a = esum_a + e_a
            esum_b = esum_b + e_b
            pb = jnp.concatenate([e_a, e_b], axis=1).astype(BF16)
            acc = acc + jnp.dot(pb, vbd, preferred_element_type=F32)
        inv_a = 1.0 / (jnp.sum(esum_a, axis=1, keepdims=True) + jnp.exp2(sink_a - m_a))
        inv_b = 1.0 / (jnp.sum(esum_b, axis=1, keepdims=True) + jnp.exp2(sink_b - m_b))
        outs = []
        for gi in range(UNIT_SLOTS):
            j = slot0 + gi
            rows = slice(BLOCK * gi, BLOCK * (gi + 1))
            inv = jnp.where(low, inv_a[rows], inv_b[rows])
            zj = z_ref[0, :, LANES * j:LANES * (j + 1)].astype(F32)
            outs.append((acc[rows] * inv * (zj * jax.nn.sigmoid(zj))).astype(BF16))
        g = jnp.concatenate(outs, axis=1)
        part = jnp.dot(g, wo_ref[LANES * slot0:LANES * (slot0 + UNIT_SLOTS), :],
                       preferred_element_type=F32)
        return part if y is None else y + part

    y = None
    pending = scores(*units[0])
    for u, (m, h) in enumerate(units):
        nxt = scores(*units[u + 1]) if u + 1 < len(units) else None
        y = finish(m, h, pending, y)
        pending = nxt
    resid = jnp.where(is_lat, x_ref[0], c_ref[0])
    o_ref[0] = resid + mod_ref[0, 0, 2:3, :] * y


def _attention(q, z, kbd, vbd, sink_tab, x, ctx, mod, w_out):
    last = N_BLOCKS - 1
    n_m = KV_WIDTH // LANES
    row_spec = pl.BlockSpec((1, BLOCK, ATTN_WIDTH), lambda b, i: (b, i, 0))

    def kv_spec(off):
        return pl.BlockSpec((1, 1, n_m, 2 * BLOCK, LANES),
                            lambda b, i: (b, jnp.clip(i + off, 0, last), 0, 0, 0))

    ctx_kv_spec = pl.BlockSpec((1, N_CTX_BLOCKS, n_m, 2 * BLOCK, LANES),
                               lambda b, i: (b, 0, 0, 0, 0))
    return pl.pallas_call(
        _attn_kernel,
        grid=(BATCH, N_BLOCKS),
        in_specs=[
            pl.BlockSpec((1, 1, n_m, SLOTS_PER_M * BLOCK, LANES), lambda b, i: (b, i, 0, 0, 0)),
            row_spec,
            kv_spec(-1), kv_spec(0), kv_spec(1), ctx_kv_spec,
            kv_spec(-1), kv_spec(0), kv_spec(1), ctx_kv_spec,
            pl.BlockSpec((2 * N_SLOTS, LANES), lambda b, i: (0, 0)),
            pl.BlockSpec((1, BLOCK, D_MODEL),
                         lambda b, i: (b, jnp.maximum(i - N_CTX_BLOCKS, 0), 0)),
            pl.BlockSpec((1, BLOCK, D_MODEL),
                         lambda b, i: (b, jnp.minimum(i, N_CTX_BLOCKS - 1), 0)),
            pl.BlockSpec((1, 1, SUBLANES, D_MODEL),
                         lambda b, i: (b, jnp.minimum(i // N_CTX_BLOCKS, 1), 0, 0)),
            pl.BlockSpec((ATTN_WIDTH, D_MODEL), lambda b, i: (0, 0)),
        ],
        out_specs=pl.BlockSpec((1, BLOCK, D_MODEL), lambda b, i: (b, i, 0)),
        out_shape=jax.ShapeDtypeStruct((BATCH, TOTAL, D_MODEL), F32),
        compiler_params=_params(("arbitrary", "arbitrary"), 48),
        name="window_attention",
    )(q, z, kbd, kbd, kbd, kbd, vbd, vbd, vbd, vbd, sink_tab, x, ctx, mod, w_out)


def _proj1_kernel(x_ref, mod_ref, nw_ref, w_ref, u_ref, z_ref, scr_ref):
    h = jnp.concatenate(
        [_modulated_norm(x_ref[b], nw_ref[...], mod_ref, b).astype(BF16) for b in range(BATCH)],
        axis=0)
    u = jnp.dot(h, w_ref[:, :D_MODEL], preferred_element_type=F32)
    z = jnp.dot(h, w_ref[:, D_MODEL:], preferred_element_type=F32)
    for b in range(BATCH):
        z_ref[b] = z[ROW_TILE * b:ROW_TILE * (b + 1)].astype(BF16)
    n_slab = D_MODEL // LANES
    for b in range(BATCH):
        for cc in range(CHUNKS_PER_TILE):
            r0 = ROW_TILE * b + CHUNK * cc
            q0 = SLOT_PITCH * _chunk_slot(b, cc)
            for k in range(n_slab):
                scr_ref[k, q0:q0 + CHUNK, :] = u[r0:r0 + CHUNK, LANES * k:LANES * (k + 1)]
    groups_per_slab = LANES // SSM_GROUP
    for k in range(n_slab):
        for pp in range(PAIRS_PER_TILE // 2):
            parts = []
            for p in (2 * pp, 2 * pp + 1):
                rows = [scr_ref[k, pl.ds(SLOT_PITCH * SUBLANES * p + s, SUBLANES,
                                         stride=SLOT_PITCH), :] for s in range(CHUNK)]
                parts.append([_lane_block_transpose(rows[SUBLANES * m2:SUBLANES * (m2 + 1)])
                              for m2 in range(CHUNK // SUBLANES)])
            for m2 in range(CHUNK // SUBLANES):
                for gl in range(groups_per_slab):
                    val = jnp.concatenate([parts[0][m2][gl], parts[1][m2][gl]], axis=0)
                    u_ref[groups_per_slab * k + gl, 2 * SUBLANES * pp:2 * SUBLANES * (pp + 1),
                          LANES * m2:LANES * (m2 + 1)] = val.astype(BF16)


def _proj1(xc, mod, norm_w, w_in):
    row_spec = pl.BlockSpec((BATCH, ROW_TILE, D_MODEL), lambda i: (0, i, 0))
    tile_rows = PAIRS_PER_TILE * SUBLANES
    return pl.pallas_call(
        _proj1_kernel,
        grid=(N_ROW_TILES,),
        in_specs=[
            row_spec,
            pl.BlockSpec((BATCH, 1, SUBLANES, D_MODEL), lambda i: (0, jnp.minimum(i, 1), 0, 0)),
            pl.BlockSpec((1, D_MODEL), lambda i: (0, 0)),
            pl.BlockSpec((D_MODEL, 2 * D_MODEL), lambda i: (0, 0)),
        ],
        out_specs=[pl.BlockSpec((SSM_GROUPS, tile_rows, CHUNK_W), lambda i: (0, i, 0)), row_spec],
        out_shape=[jax.ShapeDtypeStruct((SSM_GROUPS, SCAN_ROWS, CHUNK_W), BF16),
                   jax.ShapeDtypeStruct((BATCH, TOTAL, D_MODEL), BF16)],
        scratch_shapes=[pltpu.VMEM((D_MODEL // LANES, BATCH * CHUNKS_PER_TILE * SLOT_PITCH, LANES),
                                   F32)],
        compiler_params=_params(("arbitrary",), 56),
        name="ssm_projection",
    )(xc, mod, norm_w.reshape(1, D_MODEL), w_in)


def _s5_kernel(u_ref, m_ref, ws_ref, wy_ref, cst_ref, y_ref, s4_ref, xp_ref):
    for g in range(GROUP_BATCH):
        s4_ref[g] = jnp.dot(u_ref[g], ws_ref[g], preferred_element_type=F32)
    even = lax.broadcasted_iota(jnp.int32, (SUBLANES, STATE_W), 0) % 2 == 0
    down = 1
    up = SUBLANES - 1
    fwd = slice(0, STATE_W)
    bwd = slice(STATE_W, 2 * STATE_W)
    fwd_sw = slice(2 * STATE_W, 3 * STATE_W)
    bwd_sw = slice(3 * STATE_W, 4 * STATE_W)

    def step(j, carry):
        jb = jnp.where(j < N_CTX_SCAN_BLOCKS, N_CTX_SCAN_BLOCKS - 1 - j,
                       N_SCAN_BLOCKS - 1 + N_CTX_SCAN_BLOCKS - j)
        rf = pl.ds(pl.multiple_of(j * SUBLANES, SUBLANES), SUBLANES)
        rb = pl.ds(pl.multiple_of(jb * SUBLANES, SUBLANES), SUBLANES)
        new = []
        for g in range(GROUP_BATCH):
            cf, cfs, cb, cbs = carry[4 * g:4 * g + 4]
            p1f, p2f, q1f, q2f = cst_ref[g, 0], cst_ref[g, 1], cst_ref[g, 2], cst_ref[g, 3]
            p1b, p2b, q1b, q2b = cst_ref[g, 4], cst_ref[g, 5], cst_ref[g, 6], cst_ref[g, 7]
            zf = s4_ref[g, rf, fwd]
            zfs = s4_ref[g, rf, fwd_sw]
            rzf = pltpu.roll(zf, down, 0)
            rzfs = pltpu.roll(zfs, down, 0)
            xf = p1f * cf + p2f * cfs + (zf + q1f * rzf + q2f * rzfs)
            xfs = p1f * cfs - p2f * cf + (zfs + q1f * rzfs - q2f * rzf)
            xp_ref[g, rf, fwd] = jnp.where(even, cf, pltpu.roll(xf, down, 0))
            new += [jnp.where(even, pltpu.roll(xf, up, 0), xf),
                    jnp.where(even, pltpu.roll(xfs, up, 0), xfs)]
            zb = s4_ref[g, rb, bwd]
            zbs = s4_ref[g, rb, bwd_sw]
            rzb = pltpu.roll(zb, up, 0)
            rzbs = pltpu.roll(zbs, up, 0)
            xb = p1b * cb + p2b * cbs + (zb + q1b * rzb + q2b * rzbs)
            xbs = p1b * cbs - p2b * cb + (zbs + q1b * rzbs - q2b * rzb)
            xp_ref[g, rb, bwd] = jnp.where(even, pltpu.roll(xb, up, 0), cb)
            new += [jnp.where(even, xb, pltpu.roll(xb, down, 0)),
                    jnp.where(even, xbs, pltpu.roll(xbs, down, 0))]
        return tuple(new)

    zero = jnp.zeros((SUBLANES, STATE_W), F32)
    lax.fori_loop(0, N_SCAN_BLOCKS, step, (zero,) * (4 * GROUP_BATCH))
    for g in range(GROUP_BATCH):
        y_ref[g] = (
            jnp.dot(u_ref[g, CTX_SCAN_ROWS:, :], m_ref[g], preferred_element_type=F32)
            + jnp.dot(xp_ref[g, CTX_SCAN_ROWS:, :].astype(BF16), wy_ref[g],
                      preferred_element_type=F32))


def _s5_core(u_g, m_mat, ws_mat, wy_mat, consts):
    lat_rows = SCAN_ROWS - CTX_SCAN_ROWS

    def gspec(*tail):
        return pl.BlockSpec((GROUP_BATCH,) + tail, lambda i: (i,) + (0,) * len(tail))

    return pl.pallas_call(
        _s5_kernel,
        grid=(SSM_GROUPS // GROUP_BATCH,),
        in_specs=[
            gspec(SCAN_ROWS, CHUNK_W),
            gspec(CHUNK_W, CHUNK_W),
            gspec(CHUNK_W, 4 * STATE_W),
            gspec(2 * STATE_W, CHUNK_W),
            gspec(8, SUBLANES, STATE_W),
        ],
        out_specs=gspec(lat_rows, CHUNK_W),
        out_shape=jax.ShapeDtypeStruct((SSM_GROUPS, lat_rows, CHUNK_W), F32),
        scratch_shapes=[
            pltpu.VMEM((GROUP_BATCH, SCAN_ROWS, 4 * STATE_W), F32),
            pltpu.VMEM((GROUP_BATCH, SCAN_ROWS, 2 * STATE_W), F32),
        ],
        compiler_params=_params(("arbitrary",), 48),
        name="s5_scan",
    )(u_g, m_mat, ws_mat, wy_mat, consts)


def _out1_kernel(y_ref, z_ref, x_ref, mod_ref, wg_ref, wo_ref, fnw_ref, o_ref, scr_ref):
    n_slab = D_MODEL // LANES
    groups_per_slab = LANES // SSM_GROUP
    pairs_per_phase = PAIRS_PER_TILE // OUT_PHASES
    rows_per_phase = ROW_TILE // OUT_PHASES

    def relayout(ph):
        for k in range(n_slab):
            for p in range(pairs_per_phase * ph, pairs_per_phase * (ph + 1)):
                for m2 in range(CHUNK // SUBLANES):
                    vals = [y_ref[groups_per_slab * k + gl, SUBLANES * p:SUBLANES * (p + 1),
                                  LANES * m2:LANES * (m2 + 1)] for gl in range(groups_per_slab)]
                    steps = _lane_block_transpose(vals)
                    for s2 in range(SUBLANES):
                        t_idx = SUBLANES * m2 + s2
                        scr_ref[k, pl.ds(SLOT_PITCH * SUBLANES * p + t_idx, SUBLANES,
                                         stride=SLOT_PITCH), :] = steps[s2]

    def compute(ph):
        chunks = range(2 * pairs_per_phase * ph, 2 * pairs_per_phase * (ph + 1))
        y = jnp.concatenate(
            [jnp.concatenate(
                [scr_ref[k, SLOT_PITCH * _chunk_slot(b, cc):SLOT_PITCH * _chunk_slot(b, cc) + CHUNK, :]
                 for k in range(n_slab)], axis=1)
             for b in range(BATCH) for cc in chunks], axis=0)
        g = (0.5 * y * (1.0 + lax.erf(y * (2.0 ** -0.5)))).astype(BF16)
        t = jnp.dot(g, wg_ref[...], preferred_element_type=F32)
        rows = slice(rows_per_phase * ph, rows_per_phase * (ph + 1))
        z = jnp.concatenate([z_ref[b, rows, :] for b in range(BATCH)], axis=0).astype(F32)
        r = (t[:, :D_MODEL] * jax.nn.sigmoid(t[:, D_MODEL:]) * (z * jax.nn.sigmoid(z))).astype(BF16)
        o = jnp.dot(r, wo_ref[...], preferred_element_type=F32)
        for b in range(BATCH):
            x2 = (x_ref[b, rows, :]
                  + mod_ref[b, 0, 2:3, :] * o[rows_per_phase * b:rows_per_phase * (b + 1)])
            ms = jnp.mean(x2 * x2, axis=-1, keepdims=True)
            o_ref[b, rows, :] = x2 * lax.rsqrt(ms + NORM_EPS) * fnw_ref[...]

    relayout(0)
    for ph in range(OUT_PHASES):
        if ph + 1 < OUT_PHASES:
            relayout(ph + 1)
        compute(ph)


def _out1(y_g, z, xc, mod, w_glu, w_out, final_norm_w):
    ctx_tiles = CTX_LEN // ROW_TILE
    tile_rows = PAIRS_PER_TILE * SUBLANES
    lat_spec = pl.BlockSpec((BATCH, ROW_TILE, D_MODEL), lambda i: (0, i, 0))
    all_spec = pl.BlockSpec((BATCH, ROW_TILE, D_MODEL), lambda i: (0, i + ctx_tiles, 0))
    return pl.pallas_call(
        _out1_kernel,
        grid=(SEQ // ROW_TILE,),
        in_specs=[
            pl.BlockSpec((SSM_GROUPS, tile_rows, CHUNK_W), lambda i: (0, i, 0)),
            all_spec, all_spec,
            pl.BlockSpec((BATCH, 1, SUBLANES, D_MODEL), lambda i: (0, 1, 0, 0)),
            pl.BlockSpec((D_MODEL, 2 * D_MODEL), lambda i: (0, 0)),
            pl.BlockSpec((D_MODEL, D_MODEL), lambda i: (0, 0)),
            pl.BlockSpec((1, D_MODEL), lambda i: (0, 0)),
        ],
        out_specs=lat_spec,
        out_shape=jax.ShapeDtypeStruct((BATCH, SEQ, D_MODEL), F32),
        scratch_shapes=[pltpu.VMEM((D_MODEL // LANES, BATCH * CHUNKS_PER_TILE * SLOT_PITCH, LANES),
                                   F32)],
        compiler_params=_params(("arbitrary",), 56),
        name="ssm_output",
    )(y_g, z, xc, mod, w_glu, w_out, final_norm_w.reshape(1, D_MODEL))


def _slot_order(t, lead):
    n_m = KV_WIDTH // LANES
    gq = N_HEADS // N_KV_HEADS
    shape = t.shape
    t = t.reshape(shape[:lead] + (n_m, 2, gq) + shape[lead + 1:])
    perm = tuple(range(lead)) + (lead, lead + 2, lead + 1) + tuple(range(lead + 3, t.ndim))
    return jnp.transpose(t, perm)


def _rope_order(t):
    shape = t.shape
    t = t.reshape(shape[:-1] + (2, 2, ROPE_FREQS))
    return jnp.swapaxes(t, -3, -2).reshape(shape)


def _attn_weights(w_in, w_out, sink):
    wq = w_in[:, :ATTN_WIDTH].reshape(D_MODEL, N_HEADS, HEAD_DIM)
    wq = _slot_order(_rope_order(wq), 1).reshape(D_MODEL, ATTN_WIDTH)
    wk = w_in[:, ATTN_WIDTH:ATTN_WIDTH + KV_WIDTH].reshape(D_MODEL, N_KV_HEADS, HEAD_DIM)
    wk = _rope_order(wk).reshape(D_MODEL, KV_WIDTH)
    wv = w_in[:, ATTN_WIDTH + KV_WIDTH:ATTN_WIDTH + 2 * KV_WIDTH]
    wz = w_in[:, ATTN_WIDTH + 2 * KV_WIDTH:].reshape(D_MODEL, N_HEADS, HEAD_DIM)
    wz = _slot_order(wz, 1).reshape(D_MODEL, ATTN_WIDTH)
    w_in_p = jnp.concatenate([wq, wk, wv, wz], axis=1).astype(BF16)
    wo = _slot_order(w_out.reshape(N_HEADS, HEAD_DIM, D_MODEL), 0).reshape(ATTN_WIDTH, D_MODEL)
    sink_p = _slot_order(sink.astype(F32).reshape(N_HEADS), 0).reshape(2 * N_SLOTS)
    sink_tab = jnp.broadcast_to((sink_p * LOG2E)[:, None], (2 * N_SLOTS, LANES))
    return w_in_p, wo.astype(BF16), sink_tab


def _rope_tables():
    inv = ROPE_BASE ** (-jnp.arange(ROPE_FREQS, dtype=F32) / ROPE_FREQS)
    pos = jnp.arange(SEQ)
    row = (pos // GRID_W).astype(F32)[:, None] * inv
    col = (pos % GRID_W).astype(F32)[:, None] * inv
    lane = np.arange(LANES)
    w = lane % HEAD_DIM
    half, axis, f = w // 32, (w % 32) // 16, w % 16
    ang = jnp.where(jnp.asarray(axis == 0)[None, :], row[:, f], col[:, f])
    sign = jnp.asarray(np.where(half == 0, -1.0, 1.0), F32)[None, :]
    cos = jnp.concatenate([jnp.ones((CTX_LEN, LANES), F32), jnp.cos(ang)], axis=0)
    sin = jnp.concatenate([jnp.zeros((CTX_LEN, LANES), F32), jnp.sin(ang) * sign], axis=0)
    return cos, sin


def _s5_operators(lam_re, lam_im, log_dt, b_re, b_im, c_re, c_im, d_skip):
    t_len = CHUNK
    n_pow = 2 * t_len + 1
    lr, li = lam_re.astype(F32), lam_im.astype(F32)
    dt = jnp.exp(log_dt.astype(F32))[..., None]
    ks = jnp.arange(n_pow, dtype=F32)
    mag = jnp.exp((lr * dt)[..., None] * ks)
    ph = (li * dt)[..., None] * ks
    pr, pi = mag * jnp.cos(ph), mag * jnp.sin(ph)
    ar1, ai1 = pr[..., 1] - 1.0, pi[..., 1]
    den = lr * lr + li * li
    gr, gi = (ar1 * lr + ai1 * li) / den, (ai1 * lr - ar1 * li) / den
    br_, bi_ = b_re.astype(F32), b_im.astype(F32)
    bbr = gr[..., None] * br_ - gi[..., None] * bi_
    bbi = gr[..., None] * bi_ + gi[..., None] * br_
    pad_k = POW_ROWS - n_pow
    prt, pit = jnp.swapaxes(pr, 2, 3), jnp.swapaxes(pi, 2, 3)
    pw = jnp.pad(jnp.concatenate([prt, prt, pit, pit], axis=-1),
                 ((0, 0), (0, 0), (0, pad_k), (0, 0)))
    pwc = jnp.stack([jnp.concatenate([pr, pr], axis=2), jnp.concatenate([pi, pi], axis=2)], axis=2)
    pwc = jnp.pad(pwc, ((0, 0),) * 4 + ((0, pad_k),))
    brt, bit = jnp.swapaxes(bbr, 2, 3), jnp.swapaxes(bbi, 2, 3)
    bb = jnp.concatenate([brt, bit, -bit, brt, bit, brt, brt, -bit], axis=-1)
    crt = jnp.swapaxes(c_re.astype(F32), 2, 3)
    cit = jnp.swapaxes(c_im.astype(F32), 2, 3)
    cm = jnp.stack([jnp.concatenate([crt, -cit], axis=2),
                    jnp.concatenate([-cit, -crt], axis=2)], axis=2)
    dv = jnp.tile(d_skip.astype(F32).reshape(SSM_GROUPS, 1, SSM_GROUP), (1, 1, t_len))
    return _s5_operator_call(pw, pwc, bb, cm, dv)


def _operator_constants():
    t_len = CHUNK
    s_of_row = np.arange(CHUNK_W) // SSM_GROUP
    k_ar = np.arange(POW_ROWS)
    oh_f = (k_ar[None, :] == (t_len - 1 - s_of_row)[:, None]).astype(np.float32)
    oh_b = (k_ar[None, :] == s_of_row[:, None]).astype(np.float32)
    t_of_lane = np.arange(CHUNK_W) // SSM_GROUP
    expo = [t_of_lane + 1, t_len - t_of_lane, t_of_lane, t_len - 1 - t_of_lane]
    sel = np.stack([(k_ar[:, None] == e[None, :]) for e in expo]).astype(np.float32)
    h_of_lane = np.arange(CHUNK_W) % SSM_GROUP
    tl = (np.arange(SSM_GROUP)[:, None] == h_of_lane[None, :]).astype(np.float32)
    dmask = ((s_of_row[:, None] == t_of_lane[None, :])
             & ((np.arange(CHUNK_W) % SSM_GROUP)[:, None] == h_of_lane[None, :])).astype(np.float32)
    return oh_f, oh_b, sel, tl, dmask


def _s5_op_kernel(*refs):
    for gg in range(OP_GROUP_BATCH):
        _s5_op_group(gg, *refs)


def _s5_op_group(gg, pw_ref, pwc_ref, bb_ref, cm_ref, dv_ref, ohf_ref, ohb_ref, sel_ref, tl_ref,
                 dmask_ref, m_ref, ws_ref, wy_ref, cst_ref):
    t_len = CHUNK

    def mm(a, b):
        return jnp.dot(a, b, preferred_element_type=F32, precision=lax.Precision.HIGHEST)

    def split(a):
        hi = a.astype(BF16)
        return hi, (a - hi.astype(F32)).astype(BF16)

    def pick_rows(onehot, table):
        hi, lo = split(table)
        return (jnp.dot(onehot, hi, preferred_element_type=F32)
                + jnp.dot(onehot, lo, preferred_element_type=F32))

    def spread_lanes(table, onehot):
        hi, lo = split(table)
        return (jnp.dot(hi, onehot, preferred_element_type=F32)
                + jnp.dot(lo, onehot, preferred_element_type=F32))

    main, swapped = [], []
    for d, oh_ref in ((0, ohf_ref), (1, ohb_ref)):
        pp = pick_rows(oh_ref[...], pw_ref[d, gg])
        p_re, p_im = pp[:, :STATE_W], pp[:, STATE_W:]
        b0, b1, b2, b3 = [jnp.concatenate([bb_ref[d, gg, :, STATE_W * i:STATE_W * (i + 1)]] * t_len,
                                          axis=0) for i in range(4)]
        main.append(p_re * b0 + p_im * b1)
        swapped.append(p_re * b2 + p_im * b3)
    ws_ref[gg] = jnp.concatenate(main + swapped, axis=1).astype(BF16)

    c_tiled = [[spread_lanes(cm_ref[d, gg, i], tl_ref[...]) for i in range(2)] for d in range(2)]

    def block(d, pat):
        sel = sel_ref[pat]
        return (spread_lanes(pwc_ref[d, gg, 0], sel) * c_tiled[d][0]
                + spread_lanes(pwc_ref[d, gg, 1], sel) * c_tiled[d][1])

    wy_ref[gg] = jnp.concatenate([block(0, 0), block(1, 1)], axis=0).astype(BF16)

    kt_f = mm(bb_ref[0, gg, :, :STATE_W], block(0, 2))
    kt_b = mm(bb_ref[1, gg, :, :STATE_W], block(1, 3))
    lane = lax.broadcasted_iota(jnp.int32, (SSM_GROUP, CHUNK_W), 1)
    skip = dv_ref[gg]
    for s in range(t_len):
        fwd = kt_f if s == 0 else pltpu.roll(kt_f, SSM_GROUP * s, 1)
        back = t_len - 1 - s
        bwd = kt_b if back == 0 else pltpu.roll(kt_b, CHUNK_W - SSM_GROUP * back, 1)
        rows = (jnp.where(lane >= SSM_GROUP * s, fwd, 0.0)
                + jnp.where(lane < SSM_GROUP * (s + 1), bwd, 0.0)
                + dmask_ref[SSM_GROUP * s:SSM_GROUP * (s + 1), :] * skip)
        m_ref[gg, SSM_GROUP * s:SSM_GROUP * (s + 1), :] = rows.astype(BF16)

    even = lax.broadcasted_iota(jnp.int32, (SUBLANES, STATE_W), 0) % 2 == 0
    sign = jnp.where(lax.broadcasted_iota(jnp.int32, (1, STATE_W), 1) < SSM_STATE, -1.0, 1.0)

    def w12(d, k):
        row = pw_ref[d, gg, k:k + 1, :]
        return row[:, :STATE_W], row[:, STATE_W:] * sign

    zero = (jnp.zeros((1, STATE_W), F32),) * 2
    pairs = [(w12(0, t_len), w12(0, 2 * t_len)), (zero, w12(0, t_len)),
             (w12(1, 2 * t_len), w12(1, t_len)), (w12(1, t_len), zero)]
    idx = 0
    for top, bot in pairs:
        for part in range(2):
            cst_ref[gg, idx] = jnp.where(even, jnp.broadcast_to(top[part], (SUBLANES, STATE_W)),
                                         jnp.broadcast_to(bot[part], (SUBLANES, STATE_W)))
            idx += 1


def _s5_operator_call(pw, pwc, bb, cm, dv):
    oh_f, oh_b, sel, tl, dmask = _operator_constants()
    consts = [jnp.asarray(oh_f, BF16), jnp.asarray(oh_b, BF16), jnp.asarray(sel, BF16),
              jnp.asarray(tl, BF16), jnp.asarray(dmask)]

    def per_group(*tail):
        n = len(tail)
        return pl.BlockSpec((2, OP_GROUP_BATCH) + tail, lambda g: (0, g) + (0,) * n)

    def whole(a):
        return pl.BlockSpec(a.shape, lambda g: (0,) * a.ndim)

    def out(*tail):
        return pl.BlockSpec((OP_GROUP_BATCH,) + tail, lambda g: (g,) + (0,) * len(tail))

    return pl.pallas_call(
        _s5_op_kernel,
        grid=(SSM_GROUPS // OP_GROUP_BATCH,),
        in_specs=[per_group(POW_ROWS, 2 * STATE_W), per_group(2, STATE_W, POW_ROWS),
                  per_group(SSM_GROUP, 4 * STATE_W), per_group(2, STATE_W, SSM_GROUP),
                  pl.BlockSpec((OP_GROUP_BATCH, 1, CHUNK_W), lambda g: (g, 0, 0))]
                 + [whole(a) for a in consts],
        out_specs=[out(CHUNK_W, CHUNK_W), out(CHUNK_W, 4 * STATE_W), out(2 * STATE_W, CHUNK_W),
                   out(8, SUBLANES, STATE_W)],
        out_shape=[jax.ShapeDtypeStruct((SSM_GROUPS, CHUNK_W, CHUNK_W), BF16),
                   jax.ShapeDtypeStruct((SSM_GROUPS, CHUNK_W, 4 * STATE_W), BF16),
                   jax.ShapeDtypeStruct((SSM_GROUPS, 2 * STATE_W, CHUNK_W), BF16),
                   jax.ShapeDtypeStruct((SSM_GROUPS, 8, SUBLANES, STATE_W), F32)],
        compiler_params=_params(("arbitrary",), 32),
        name="s5_operators",
    )(pw, pwc, bb, cm, dv, *consts)


def kernel(x, c, ctx, c_ctx, norm_w, w_ada, b_ada, attn_w_in, attn_sink, attn_w_out,
           ssm_w_in, ssm_lam_re, ssm_lam_im, ssm_log_dt, ssm_b_re, ssm_b_im, ssm_c_re, ssm_c_im,
           ssm_d, ssm_w_glu, ssm_w_out, final_norm_w):
    mod0, mod1 = _modulation(c, c_ctx, w_ada, b_ada)

    w_in0, w_out0, sink_tab = _attn_weights(attn_w_in[0], attn_w_out[0], attn_sink[0])
    cos_tab, sin_tab = _rope_tables()
    q, z0, kbd, vbd = _proj0(x, ctx, mod0, norm_w[0], cos_tab, sin_tab, w_in0)
    xc1 = _attention(q, z0, kbd, vbd, sink_tab, x, ctx, mod0, w_out0)

    u_g, z1 = _proj1(xc1, mod1, norm_w[1], ssm_w_in[0].astype(BF16))
    m_mat, ws_mat, wy_mat, consts = _s5_operators(
        ssm_lam_re[0], ssm_lam_im[0], ssm_log_dt[0], ssm_b_re[0], ssm_b_im[0],
        ssm_c_re[0], ssm_c_im[0], ssm_d[0])
    y_g = _s5_core(u_g, m_mat, ws_mat, wy_mat, consts)
    return _out1(y_g, z1, xc1, mod1, ssm_w_glu[0].astype(BF16), ssm_w_out[0].astype(BF16),
                 final_norm_w)
```

```python
import functools
import math

import jax
import jax.numpy as jnp
import numpy as np
from jax import lax
from jax.experimental import pallas as pl
from jax.experimental.pallas import tpu as pltpu

F32 = jnp.float32
BF16 = jnp.bfloat16

D_MODEL = 1024
BATCH = 4
SEQ = 4096
GRID_W = 64
CTX_LEN = 256
TOTAL = CTX_LEN + SEQ
HEAD_DIM = 64
N_HEADS = 16
N_KV_HEADS = 4
ATTN_WIDTH = N_HEADS * HEAD_DIM
KV_WIDTH = N_KV_HEADS * HEAD_DIM
BLOCK = 128
N_BLOCKS = TOTAL // BLOCK
N_CTX_BLOCKS = CTX_LEN // BLOCK
ROPE_BASE = 10000.0
ROPE_FREQS = HEAD_DIM // 4
SSM_GROUP = 16
SSM_GROUPS = D_MODEL // SSM_GROUP
SSM_STATE = 64
NORM_EPS = 1e-6
NEG_INF = -1e30

LANES = 128
SUBLANES = 8
N_SLOTS = ATTN_WIDTH // LANES
N_KV_PAIRS = KV_WIDTH // LANES
SLOTS_PER_M = N_SLOTS // N_KV_PAIRS
UNIT_SLOTS = 2
LOG2E = math.log2(math.e)
Q_SCALE = HEAD_DIM ** -0.5 * LOG2E
ROW_TILE = 256
N_ROW_TILES = TOTAL // ROW_TILE
CHUNK = 16
N_CHUNKS = TOTAL // CHUNK
N_CTX_CHUNKS = CTX_LEN // CHUNK
CHUNK_W = CHUNK * SSM_GROUP
CHUNKS_PER_TILE = ROW_TILE // CHUNK
PAIRS_PER_TILE = CHUNKS_PER_TILE // 2
SLOT_PITCH = 24
OUT_PHASES = 4
assert SLOT_PITCH >= CHUNK and SLOT_PITCH % SUBLANES == 0
STATE_W = 2 * SSM_STATE
SCAN_ROWS = N_CHUNKS * BATCH
CTX_SCAN_ROWS = N_CTX_CHUNKS * BATCH
N_SCAN_BLOCKS = SCAN_ROWS // SUBLANES
N_CTX_SCAN_BLOCKS = CTX_SCAN_ROWS // SUBLANES
GROUP_BATCH = 4
OP_GROUP_BATCH = 4
POW_ROWS = 48

assert BATCH * 2 == SUBLANES


def _params(semantics, vmem_mb):
    return pltpu.CompilerParams(dimension_semantics=semantics,
                                vmem_limit_bytes=vmem_mb * 1024 * 1024)


def _mod_kernel(c_ref, w_ref, b_ref, o_ref):
    c = c_ref[...]
    a = c * jax.nn.sigmoid(c)
    o_ref[0] = jnp.dot(a, w_ref[0], preferred_element_type=F32,
                       precision=lax.Precision.HIGHEST) + b_ref[0]


def _modulation(c, c_ctx, w_ada, b_ada):
    depth = w_ada.shape[0]
    rows = jnp.zeros((SUBLANES, D_MODEL), F32).at[:BATCH].set(c).at[BATCH].set(c_ctx)
    n_col = 3
    out = pl.pallas_call(
        _mod_kernel,
        grid=(depth, n_col),
        in_specs=[
            pl.BlockSpec((SUBLANES, D_MODEL), lambda l, j: (0, 0)),
            pl.BlockSpec((1, D_MODEL, D_MODEL), lambda l, j: (l, 0, j)),
            pl.BlockSpec((1, 1, D_MODEL), lambda l, j: (l, 0, j)),
        ],
        out_specs=pl.BlockSpec((1, SUBLANES, D_MODEL), lambda l, j: (l, 0, j)),
        out_shape=jax.ShapeDtypeStruct((depth, SUBLANES, 3 * D_MODEL), F32),
        compiler_params=_params(("arbitrary", "arbitrary"), 32),
        name="adaln_modulation",
    )(rows, w_ada, b_ada.reshape(depth, 1, 3 * D_MODEL))
    tabs = []
    for l in range(depth):
        lat = out[l, :BATCH].reshape(BATCH, 3, D_MODEL)
        cx = jnp.broadcast_to(out[l, BATCH].reshape(1, 3, D_MODEL), (BATCH, 3, D_MODEL))
        tab = jnp.stack([cx, lat], axis=1)
        tabs.append(jnp.pad(tab, ((0, 0), (0, 0), (0, SUBLANES - 3), (0, 0))))
    return tabs


def _modulated_norm(xt, nw, mod_ref, b=0):
    ms = jnp.mean(xt * xt, axis=-1, keepdims=True)
    y = xt * lax.rsqrt(ms + NORM_EPS) * nw
    return y * (1.0 + mod_ref[b, 0, 1:2, :]) + mod_ref[b, 0, 0:1, :]


def _lane_block_transpose(vs):
    n = len(vs)
    width = LANES // n
    blk = lax.broadcasted_iota(jnp.int32, vs[0].shape, 1) // width
    x = list(vs)
    d = n // 2
    while d >= 1:
        clear = (blk & d) == 0
        y = list(x)
        for i in range(n):
            if i & d == 0:
                a, b = x[i], x[i + d]
                y[i] = jnp.where(clear, a, pltpu.roll(b, width * d, 1))
                y[i + d] = jnp.where(clear, pltpu.roll(a, LANES - width * d, 1), b)
        x = y
        d //= 2
    return x


def _chunk_slot(b, cc):
    return (cc // 2) * (2 * BATCH) + 2 * b + (cc % 2)


def _proj0_kernel(x_ref, c_ref, mod_ref, nw_ref, cos_ref, sin_ref, w_ref,
                  q_ref, z_ref, kbd_ref, vbd_ref):
    is_ctx = pl.program_id(0) == 0
    h = jnp.concatenate(
        [_modulated_norm(jnp.where(is_ctx, c_ref[b], x_ref[b]), nw_ref[...], mod_ref, b).astype(BF16)
         for b in range(BATCH)], axis=0)
    cos = cos_ref[...]
    sin = sin_ref[...]
    lane = lax.broadcasted_iota(jnp.int32, (ROW_TILE, LANES), 1)
    first_half = (lane % HEAD_DIM) < (HEAD_DIM // 2)
    low = lax.broadcasted_iota(jnp.int32, (BLOCK, LANES), 1) < HEAD_DIM

    def rope(t):
        partner = jnp.where(first_half, pltpu.roll(t, LANES - HEAD_DIM // 2, 1),
                            pltpu.roll(t, HEAD_DIM // 2, 1))
        return t * cos + partner * sin

    q = jnp.dot(h, w_ref[:, :ATTN_WIDTH], preferred_element_type=F32)
    k = jnp.dot(h, w_ref[:, ATTN_WIDTH:ATTN_WIDTH + KV_WIDTH], preferred_element_type=F32)
    v = jnp.dot(h, w_ref[:, ATTN_WIDTH + KV_WIDTH:ATTN_WIDTH + 2 * KV_WIDTH],
                preferred_element_type=F32)
    z = jnp.dot(h, w_ref[:, ATTN_WIDTH + 2 * KV_WIDTH:], preferred_element_type=F32)
    for b in range(BATCH):
        tile = slice(ROW_TILE * b, ROW_TILE * (b + 1))
        z_ref[b] = z[tile].astype(BF16)
        for j in range(N_SLOTS):
            m, gi = divmod(j, SLOTS_PER_M)
            qj = (rope(q[tile, LANES * j:LANES * (j + 1)]) * Q_SCALE).astype(BF16)
            for blk in range(ROW_TILE // BLOCK):
                q_ref[b, blk, m, BLOCK * gi:BLOCK * (gi + 1), :] = qj[BLOCK * blk:BLOCK * (blk + 1)]
        for m in range(N_KV_PAIRS):
            sl = slice(LANES * m, LANES * (m + 1))
            kr = rope(k[tile, sl])
            vm = v[tile, sl]
            for blk in range(ROW_TILE // BLOCK):
                rows = slice(BLOCK * blk, BLOCK * (blk + 1))
                kbd_ref[b, blk, m, :BLOCK, :] = jnp.where(low, kr[rows], 0.0).astype(BF16)
                kbd_ref[b, blk, m, BLOCK:, :] = jnp.where(low, 0.0, kr[rows]).astype(BF16)
                vbd_ref[b, blk, m, :BLOCK, :] = jnp.where(low, vm[rows], 0.0).astype(BF16)
                vbd_ref[b, blk, m, BLOCK:, :] = jnp.where(low, 0.0, vm[rows]).astype(BF16)


def _proj0(x, ctx, mod, norm_w, cos_tab, sin_tab, w_in):
    n_col = w_in.shape[1]
    blocks_per_tile = ROW_TILE // BLOCK
    kv_shape = jax.ShapeDtypeStruct((BATCH, N_BLOCKS, N_KV_PAIRS, 2 * BLOCK, LANES), BF16)
    kv_spec = pl.BlockSpec((BATCH, blocks_per_tile, N_KV_PAIRS, 2 * BLOCK, LANES),
                           lambda i: (0, i, 0, 0, 0))
    row_spec = pl.BlockSpec((BATCH, ROW_TILE, D_MODEL), lambda i: (0, i, 0))
    once = pl.Buffered(1)
    return pl.pallas_call(
        _proj0_kernel,
        grid=(N_ROW_TILES,),
        in_specs=[
            pl.BlockSpec((BATCH, ROW_TILE, D_MODEL), lambda i: (0, jnp.maximum(i - 1, 0), 0)),
            pl.BlockSpec((BATCH, ROW_TILE, D_MODEL), lambda i: (0, 0, 0), pipeline_mode=once),
            pl.BlockSpec((BATCH, 1, SUBLANES, D_MODEL), lambda i: (0, jnp.minimum(i, 1), 0, 0)),
            pl.BlockSpec((1, D_MODEL), lambda i: (0, 0)),
            pl.BlockSpec((ROW_TILE, LANES), lambda i: (i, 0)),
            pl.BlockSpec((ROW_TILE, LANES), lambda i: (i, 0)),
            pl.BlockSpec((D_MODEL, n_col), lambda i: (0, 0), pipeline_mode=once),
        ],
        out_specs=[
            pl.BlockSpec((BATCH, blocks_per_tile, N_KV_PAIRS, SLOTS_PER_M * BLOCK, LANES),
                         lambda i: (0, i, 0, 0, 0)),
            row_spec, kv_spec, kv_spec],
        out_shape=[
            jax.ShapeDtypeStruct((BATCH, N_BLOCKS, N_KV_PAIRS, SLOTS_PER_M * BLOCK, LANES), BF16),
            jax.ShapeDtypeStruct((BATCH, TOTAL, ATTN_WIDTH), BF16),
            kv_shape, kv_shape,
        ],
        compiler_params=_params(("arbitrary",), 56),
        name="attn_projection",
    )(x, ctx, mod, norm_w.reshape(1, D_MODEL), cos_tab, sin_tab, w_in)


def _attn_kernel(q_ref, z_ref, kl_ref, kc_ref, kr_ref, kx_ref, vl_ref, vc_ref, vr_ref, vx_ref,
                 sink_ref, x_ref, c_ref, mod_ref, wo_ref, eye_ref, tri_ref, o_ref):
    i = pl.program_id(1)
    is_lat = i >= N_CTX_BLOCKS
    n = i - N_CTX_BLOCKS
    q_rows = UNIT_SLOTS * BLOCK
    blocked = tri_ref[2]
    bias = [jnp.where(jnp.logical_and(is_lat, n >= 1), tri_ref[0], blocked),
            jnp.where(is_lat, tri_ref[3], blocked),
            jnp.where(jnp.logical_and(is_lat, n <= SEQ // BLOCK - 2), tri_ref[1], blocked),
            None, None]
    low = lax.broadcasted_iota(jnp.int32, (BLOCK, LANES), 1) < HEAD_DIM

    def kpiece(p, m):
        if p == 0:
            return kl_ref[0, 0, m], vl_ref[0, 0, m]
        if p == 1:
            return kc_ref[0, 0, m], vc_ref[0, 0, m]
        if p == 2:
            return kr_ref[0, 0, m], vr_ref[0, 0, m]
        return kx_ref[0, p - 3, m], vx_ref[0, p - 3, m]

    n_piece = 3 + N_CTX_BLOCKS
    units = [(m, h) for m in range(N_KV_PAIRS) for h in range(SLOTS_PER_M // UNIT_SLOTS)]

    nt = (((1,), (1,)), ((), ()))

    def scores(m, h):
        qu = q_ref[0, 0, m, q_rows * h:q_rows * (h + 1), :]
        qu_masked = jnp.concatenate([qu, eye_ref[...]], axis=1)
        s_list = []
        for p in range(n_piece):
            kbd, _ = kpiece(p, m)
            if bias[p] is None:
                s = lax.dot_general(qu, kbd, nt, preferred_element_type=F32)
            else:
                s = lax.dot_general(qu_masked, jnp.concatenate([kbd, bias[p]], axis=1), nt,
                                    preferred_element_type=F32)
            s_list.append(s)
        return s_list

    def finish(m, h, s_list, y):
        slot0 = SLOTS_PER_M * m + UNIT_SLOTS * h
        mx = s_list[0]
        for s in s_list[1:]:
            mx = jnp.maximum(mx, s)

        def sink_col(hs):
            return jnp.concatenate(
                [jnp.broadcast_to(sink_ref[2 * (slot0 + gi) + hs:2 * (slot0 + gi) + hs + 1, 0:1],
                                  (BLOCK, 1)) for gi in range(UNIT_SLOTS)], axis=0)

        sink_a, sink_b = sink_col(0), sink_col(1)
        m_a = jnp.maximum(jnp.max(mx[:, :BLOCK], axis=1, keepdims=True), sink_a)
        m_b = jnp.maximum(jnp.max(mx[:, BLOCK:], axis=1, keepdims=True), sink_b)
        mb_a = jnp.broadcast_to(m_a, (q_rows, BLOCK))
        mb_b = jnp.broadcast_to(m_b, (q_rows, BLOCK))
        acc = None
        esum_a = jnp.zeros((q_rows, BLOCK), F32)
        esum_b = jnp.zeros((q_rows, BLOCK), F32)
        for p in range(n_piece):
            e_a = jnp.exp2(s_list[p][:, :BLOCK] - mb_a)
            e_b = jnp.exp2(s_list[p][:, BLOCK:] - mb_b)
            esum_a = esum_a + e_a
            esum_b = esum_b + e_b
            pb = jnp.concatenate([e_a, e_b], axis=1).astype(BF16)
            part = jnp.dot(pb, kpiece(p, m)[1], preferred_element_type=F32)
            acc = part if acc is None else acc + part
        inv_a = 1.0 / (jnp.sum(esum_a, axis=1, keepdims=True) + jnp.exp2(sink_a - m_a))
        inv_b = 1.0 / (jnp.sum(esum_b, axis=1, keepdims=True) + jnp.exp2(sink_b - m_b))
        outs = []
        for gi in range(UNIT_SLOTS):
            j = slot0 + gi
            rows = slice(BLOCK * gi, BLOCK * (gi + 1))
            inv = jnp.where(low, inv_a[rows], inv_b[rows])
            zj = z_ref[0, :, LANES * j:LANES * (j + 1)].astype(F32)
            outs.append((acc[rows, :LANES] * inv * (zj * jax.nn.sigmoid(zj))).astype(BF16))
        g = jnp.concatenate(outs, axis=1)
        part = jnp.dot(g, wo_ref[LANES * slot0:LANES * (slot0 + UNIT_SLOTS), :],
                       preferred_element_type=F32)
        return part if y is None else y + part

    y = None
    pending = scores(*units[0])
    for u, (m, h) in enumerate(units):
        nxt = scores(*units[u + 1]) if u + 1 < len(units) else None
        y = finish(m, h, pending, y)
        pending = nxt
    resid = jnp.where(is_lat, x_ref[0], c_ref[0])
    o_ref[0] = resid + mod_ref[0, 0, 2:3, :] * y


def _attention(q, z, kbd, vbd, sink_tab, x, ctx, mod, w_out):
    last = N_BLOCKS - 1
    n_m = KV_WIDTH // LANES
    row_spec = pl.BlockSpec((1, BLOCK, ATTN_WIDTH), lambda b, i: (b, i, 0))

    def kv_spec(off):
        return pl.BlockSpec((1, 1, n_m, 2 * BLOCK, LANES),
                            lambda b, i: (b, jnp.clip(i + off, 0, last), 0, 0, 0))

    ctx_kv_spec = pl.BlockSpec((1, N_CTX_BLOCKS, n_m, 2 * BLOCK, LANES),
                               lambda b, i: (b, 0, 0, 0, 0))
    off = np.arange(BLOCK)
    eye = np.tile(np.eye(BLOCK, dtype=np.float32), (UNIT_SLOTS, 1))
    key_ge = np.where(off[:, None] >= off[None, :], 0.0, NEG_INF)
    key_le = np.where(off[:, None] <= off[None, :], 0.0, NEG_INF)
    tri = np.stack([np.tile(t, (2, 1)) for t in
                    (key_ge, key_le, np.full((BLOCK, BLOCK), NEG_INF), np.zeros((BLOCK, BLOCK)))])
    consts = [jnp.asarray(a, BF16) for a in (eye, tri)]
    return pl.pallas_call(
        _attn_kernel,
        grid=(BATCH, N_BLOCKS),
        in_specs=[
            pl.BlockSpec((1, 1, n_m, SLOTS_PER_M * BLOCK, LANES), lambda b, i: (b, i, 0, 0, 0)),
            row_spec,
            kv_spec(-1), kv_spec(0), kv_spec(1), ctx_kv_spec,
            kv_spec(-1), kv_spec(0), kv_spec(1), ctx_kv_spec,
            pl.BlockSpec((2 * N_SLOTS, LANES), lambda b, i: (0, 0)),
            pl.BlockSpec((1, BLOCK, D_MODEL),
                         lambda b, i: (b, jnp.maximum(i - N_CTX_BLOCKS, 0), 0)),
            pl.BlockSpec((1, BLOCK, D_MODEL),
                         lambda b, i: (b, jnp.minimum(i, N_CTX_BLOCKS - 1), 0)),
            pl.BlockSpec((1, 1, SUBLANES, D_MODEL),
                         lambda b, i: (b, jnp.minimum(i // N_CTX_BLOCKS, 1), 0, 0)),
            pl.BlockSpec((ATTN_WIDTH, D_MODEL), lambda b, i: (0, 0)),
        ] + [pl.BlockSpec(a.shape, lambda b, i, nd=a.ndim: (0,) * nd) for a in consts],
        out_specs=pl.BlockSpec((1, BLOCK, D_MODEL), lambda b, i: (b, i, 0)),
        out_shape=jax.ShapeDtypeStruct((BATCH, TOTAL, D_MODEL), F32),
        compiler_params=_params(("arbitrary", "arbitrary"), 48),
        name="window_attention",
    )(q, z, kbd, kbd, kbd, kbd, vbd, vbd, vbd, vbd, sink_tab, x, ctx, mod, w_out, *consts)


def _proj1_kernel(x_ref, mod_ref, nw_ref, w_ref, u_ref, z_ref, scr_ref):
    h = jnp.concatenate(
        [_modulated_norm(x_ref[b], nw_ref[...], mod_ref, b).astype(BF16) for b in range(BATCH)],
        axis=0)
    u = jnp.dot(h, w_ref[:, :D_MODEL], preferred_element_type=F32)
    z = jnp.dot(h, w_ref[:, D_MODEL:], preferred_element_type=F32)
    for b in range(BATCH):
        z_ref[b] = z[ROW_TILE * b:ROW_TILE * (b + 1)].astype(BF16)
    n_slab = D_MODEL // LANES
    for b in range(BATCH):
        for cc in range(CHUNKS_PER_TILE):
            r0 = ROW_TILE * b + CHUNK * cc
            q0 = SLOT_PITCH * _chunk_slot(b, cc)
            for k in range(n_slab):
                scr_ref[k, q0:q0 + CHUNK, :] = u[r0:r0 + CHUNK, LANES * k:LANES * (k + 1)]
    groups_per_slab = LANES // SSM_GROUP
    for k in range(n_slab):
        for pp in range(PAIRS_PER_TILE // 2):
            parts = []
            for p in (2 * pp, 2 * pp + 1):
                rows = [scr_ref[k, pl.ds(SLOT_PITCH * SUBLANES * p + s, SUBLANES,
                                         stride=SLOT_PITCH), :] for s in range(CHUNK)]
                parts.append([_lane_block_transpose(rows[SUBLANES * m2:SUBLANES * (m2 + 1)])
                              for m2 in range(CHUNK // SUBLANES)])
            for m2 in range(CHUNK // SUBLANES):
                for gl in range(groups_per_slab):
                    val = jnp.concatenate([parts[0][m2][gl], parts[1][m2][gl]], axis=0)
                    u_ref[groups_per_slab * k + gl, 2 * SUBLANES * pp:2 * SUBLANES * (pp + 1),
                          LANES * m2:LANES * (m2 + 1)] = val.astype(BF16)


def _proj1(xc, mod, norm_w, w_in):
    row_spec = pl.BlockSpec((BATCH, ROW_TILE, D_MODEL), lambda i: (0, i, 0))
    tile_rows = PAIRS_PER_TILE * SUBLANES
    return pl.pallas_call(
        _proj1_kernel,
        grid=(N_ROW_TILES,),
        in_specs=[
            row_spec,
            pl.BlockSpec((BATCH, 1, SUBLANES, D_MODEL), lambda i: (0, jnp.minimum(i, 1), 0, 0)),
            pl.BlockSpec((1, D_MODEL), lambda i: (0, 0)),
            pl.BlockSpec((D_MODEL, 2 * D_MODEL), lambda i: (0, 0)),
        ],
        out_specs=[pl.BlockSpec((SSM_GROUPS, tile_rows, CHUNK_W), lambda i: (0, i, 0)), row_spec],
        out_shape=[jax.ShapeDtypeStruct((SSM_GROUPS, SCAN_ROWS, CHUNK_W), BF16),
                   jax.ShapeDtypeStruct((BATCH, TOTAL, D_MODEL), BF16)],
        scratch_shapes=[pltpu.VMEM((D_MODEL // LANES, BATCH * CHUNKS_PER_TILE * SLOT_PITCH, LANES),
                                   F32)],
        compiler_params=_params(("arbitrary",), 56),
        name="ssm_projection",
    )(xc, mod, norm_w.reshape(1, D_MODEL), w_in)


def _s5_kernel(u_ref, m_ref, ws_ref, wy_ref, cst_ref, y_ref, s4_ref, xp_ref):
    for g in range(GROUP_BATCH):
        s4_ref[g] = jnp.dot(u_ref[g], ws_ref[g], preferred_element_type=F32)
    even = lax.broadcasted_iota(jnp.int32, (SUBLANES, STATE_W), 0) % 2 == 0
    down = 1
    up = SUBLANES - 1
    fwd = slice(0, STATE_W)
    bwd = slice(STATE_W, 2 * STATE_W)
    fwd_sw = slice(2 * STATE_W, 3 * STATE_W)
    bwd_sw = slice(3 * STATE_W, 4 * STATE_W)

    def step(j, carry):
        jb = jnp.where(j < N_CTX_SCAN_BLOCKS, N_CTX_SCAN_BLOCKS - 1 - j,
                       N_SCAN_BLOCKS - 1 + N_CTX_SCAN_BLOCKS - j)
        rf = pl.ds(pl.multiple_of(j * SUBLANES, SUBLANES), SUBLANES)
        rb = pl.ds(pl.multiple_of(jb * SUBLANES, SUBLANES), SUBLANES)
        new = []
        for g in range(GROUP_BATCH):
            cf, cfs, cb, cbs = carry[4 * g:4 * g + 4]
            p1f, p2f, q1f, q2f = cst_ref[g, 0], cst_ref[g, 1], cst_ref[g, 2], cst_ref[g, 3]
            p1b, p2b, q1b, q2b = cst_ref[g, 4], cst_ref[g, 5], cst_ref[g, 6], cst_ref[g, 7]
            zf = s4_ref[g, rf, fwd]
            zfs = s4_ref[g, rf, fwd_sw]
            rzf = pltpu.roll(zf, down, 0)
            rzfs = pltpu.roll(zfs, down, 0)
            xf = p1f * cf + p2f * cfs + (zf + q1f * rzf + q2f * rzfs)
            xfs = p1f * cfs - p2f * cf + (zfs + q1f * rzfs - q2f * rzf)
            xp_ref[g, rf, fwd] = jnp.where(even, cf, pltpu.roll(xf, down, 0))
            new += [jnp.where(even, pltpu.roll(xf, up, 0), xf),
                    jnp.where(even, pltpu.roll(xfs, up, 0), xfs)]
            zb = s4_ref[g, rb, bwd]
            zbs = s4_ref[g, rb, bwd_sw]
            rzb = pltpu.roll(zb, up, 0)
            rzbs = pltpu.roll(zbs, up, 0)
            xb = p1b * cb + p2b * cbs + (zb + q1b * rzb + q2b * rzbs)
            xbs = p1b * cbs - p2b * cb + (zbs + q1b * rzbs - q2b * rzb)
            xp_ref[g, rb, bwd] = jnp.where(even, pltpu.roll(xb, up, 0), cb)
            new += [jnp.where(even, xb, pltpu.roll(xb, down, 0)),
                    jnp.where(even, xbs, pltpu.roll(xbs, down, 0))]
        return tuple(new)

    zero = jnp.zeros((SUBLANES, STATE_W), F32)
    lax.fori_loop(0, N_SCAN_BLOCKS, step, (zero,) * (4 * GROUP_BATCH))
    for g in range(GROUP_BATCH):
        y_ref[g] = (
            jnp.dot(u_ref[g, CTX_SCAN_ROWS:, :], m_ref[g], preferred_element_type=F32)
            + jnp.dot(xp_ref[g, CTX_SCAN_ROWS:, :].astype(BF16), wy_ref[g],
                      preferred_element_type=F32))


def _s5_core(u_g, m_mat, ws_mat, wy_mat, consts):
    lat_rows = SCAN_ROWS - CTX_SCAN_ROWS

    def gspec(*tail):
        return pl.BlockSpec((GROUP_BATCH,) + tail, lambda i: (i,) + (0,) * len(tail))

    return pl.pallas_call(
        _s5_kernel,
        grid=(SSM_GROUPS // GROUP_BATCH,),
        in_specs=[
            gspec(SCAN_ROWS, CHUNK_W),
            gspec(CHUNK_W, CHUNK_W),
            gspec(CHUNK_W, 4 * STATE_W),
            gspec(2 * STATE_W, CHUNK_W),
            gspec(8, SUBLANES, STATE_W),
        ],
        out_specs=gspec(lat_rows, CHUNK_W),
        out_shape=jax.ShapeDtypeStruct((SSM_GROUPS, lat_rows, CHUNK_W), F32),
        scratch_shapes=[
            pltpu.VMEM((GROUP_BATCH, SCAN_ROWS, 4 * STATE_W), F32),
            pltpu.VMEM((GROUP_BATCH, SCAN_ROWS, 2 * STATE_W), F32),
        ],
        compiler_params=_params(("arbitrary",), 48),
        name="s5_scan",
    )(u_g, m_mat, ws_mat, wy_mat, consts)


def _out1_kernel(y_ref, z_ref, x_ref, mod_ref, wg_ref, wo_ref, fnw_ref, o_ref, scr_ref):
    n_slab = D_MODEL // LANES
    groups_per_slab = LANES // SSM_GROUP
    pairs_per_phase = PAIRS_PER_TILE // OUT_PHASES
    rows_per_phase = ROW_TILE // OUT_PHASES

    def relayout(ph):
        for k in range(n_slab):
            for p in range(pairs_per_phase * ph, pairs_per_phase * (ph + 1)):
                for m2 in range(CHUNK // SUBLANES):
                    vals = [y_ref[groups_per_slab * k + gl, SUBLANES * p:SUBLANES * (p + 1),
                                  LANES * m2:LANES * (m2 + 1)] for gl in range(groups_per_slab)]
                    steps = _lane_block_transpose(vals)
                    for s2 in range(SUBLANES):
                        t_idx = SUBLANES * m2 + s2
                        scr_ref[k, pl.ds(SLOT_PITCH * SUBLANES * p + t_idx, SUBLANES,
                                         stride=SLOT_PITCH), :] = steps[s2]

    def compute(ph):
        chunks = range(2 * pairs_per_phase * ph, 2 * pairs_per_phase * (ph + 1))
        y = jnp.concatenate(
            [jnp.concatenate(
                [scr_ref[k, SLOT_PITCH * _chunk_slot(b, cc):SLOT_PITCH * _chunk_slot(b, cc) + CHUNK, :]
                 for k in range(n_slab)], axis=1)
             for b in range(BATCH) for cc in chunks], axis=0)
        g = (0.5 * y * (1.0 + lax.erf(y * (2.0 ** -0.5)))).astype(BF16)
        t = jnp.dot(g, wg_ref[...], preferred_element_type=F32)
        rows = slice(rows_per_phase * ph, rows_per_phase * (ph + 1))
        z = jnp.concatenate([z_ref[b, rows, :] for b in range(BATCH)], axis=0).astype(F32)
        r = (t[:, :D_MODEL] * jax.nn.sigmoid(t[:, D_MODEL:]) * (z * jax.nn.sigmoid(z))).astype(BF16)
        o = jnp.dot(r, wo_ref[...], preferred_element_type=F32)
        for b in range(BATCH):
            x2 = (x_ref[b, rows, :]
                  + mod_ref[b, 0, 2:3, :] * o[rows_per_phase * b:rows_per_phase * (b + 1)])
            ms = jnp.mean(x2 * x2, axis=-1, keepdims=True)
            o_ref[b, rows, :] = x2 * lax.rsqrt(ms + NORM_EPS) * fnw_ref[...]

    relayout(0)
    for ph in range(OUT_PHASES):
        if ph + 1 < OUT_PHASES:
            relayout(ph + 1)
        compute(ph)


def _out1(y_g, z, xc, mod, w_glu, w_out, final_norm_w):
    ctx_tiles = CTX_LEN // ROW_TILE
    tile_rows = PAIRS_PER_TILE * SUBLANES
    lat_spec = pl.BlockSpec((BATCH, ROW_TILE, D_MODEL), lambda i: (0, i, 0))
    all_spec = pl.BlockSpec((BATCH, ROW_TILE, D_MODEL), lambda i: (0, i + ctx_tiles, 0))
    return pl.pallas_call(
        _out1_kernel,
        grid=(SEQ // ROW_TILE,),
        in_specs=[
            pl.BlockSpec((SSM_GROUPS, tile_rows, CHUNK_W), lambda i: (0, i, 0)),
            all_spec, all_spec,
            pl.BlockSpec((BATCH, 1, SUBLANES, D_MODEL), lambda i: (0, 1, 0, 0)),
            pl.BlockSpec((D_MODEL, 2 * D_MODEL), lambda i: (0, 0)),
            pl.BlockSpec((D_MODEL, D_MODEL), lambda i: (0, 0)),
            pl.BlockSpec((1, D_MODEL), lambda i: (0, 0)),
        ],
        out_specs=lat_spec,
        out_shape=jax.ShapeDtypeStruct((BATCH, SEQ, D_MODEL), F32),
        scratch_shapes=[pltpu.VMEM((D_MODEL // LANES, BATCH * CHUNKS_PER_TILE * SLOT_PITCH, LANES),
                                   F32)],
        compiler_params=_params(("arbitrary",), 56),
        name="ssm_output",
    )(y_g, z, xc, mod, w_glu, w_out, final_norm_w.reshape(1, D_MODEL))


def _slot_order(t, lead):
    n_m = KV_WIDTH // LANES
    gq = N_HEADS // N_KV_HEADS
    shape = t.shape
    t = t.reshape(shape[:lead] + (n_m, 2, gq) + shape[lead + 1:])
    perm = tuple(range(lead)) + (lead, lead + 2, lead + 1) + tuple(range(lead + 3, t.ndim))
    return jnp.transpose(t, perm)


def _rope_order(t):
    shape = t.shape
    t = t.reshape(shape[:-1] + (2, 2, ROPE_FREQS))
    return jnp.swapaxes(t, -3, -2).reshape(shape)


def _attn_weights(w_in, w_out, sink):
    wq = w_in[:, :ATTN_WIDTH].reshape(D_MODEL, N_HEADS, HEAD_DIM)
    wq = _slot_order(_rope_order(wq), 1).reshape(D_MODEL, ATTN_WIDTH)
    wk = w_in[:, ATTN_WIDTH:ATTN_WIDTH + KV_WIDTH].reshape(D_MODEL, N_KV_HEADS, HEAD_DIM)
    wk = _rope_order(wk).reshape(D_MODEL, KV_WIDTH)
    wv = w_in[:, ATTN_WIDTH + KV_WIDTH:ATTN_WIDTH + 2 * KV_WIDTH]
    wz = w_in[:, ATTN_WIDTH + 2 * KV_WIDTH:].reshape(D_MODEL, N_HEADS, HEAD_DIM)
    wz = _slot_order(wz, 1).reshape(D_MODEL, ATTN_WIDTH)
    w_in_p = jnp.concatenate([wq, wk, wv, wz], axis=1).astype(BF16)
    wo = _slot_order(w_out.reshape(N_HEADS, HEAD_DIM, D_MODEL), 0).reshape(ATTN_WIDTH, D_MODEL)
    sink_p = _slot_order(sink.astype(F32).reshape(N_HEADS), 0).reshape(2 * N_SLOTS)
    sink_tab = jnp.broadcast_to((sink_p * LOG2E)[:, None], (2 * N_SLOTS, LANES))
    return w_in_p, wo.astype(BF16), sink_tab


def _rope_tables():
    inv = ROPE_BASE ** (-np.arange(ROPE_FREQS, dtype=np.float64) / ROPE_FREQS)
    pos = np.arange(SEQ)
    row = (pos // GRID_W)[:, None] * inv
    col = (pos % GRID_W)[:, None] * inv
    w = np.arange(LANES) % HEAD_DIM
    half, axis, f = w // 32, (w % 32) // 16, w % 16
    ang = np.where((axis == 0)[None, :], row[:, f], col[:, f])
    sign = np.where(half == 0, -1.0, 1.0)[None, :]
    cos = np.concatenate([np.ones((CTX_LEN, LANES)), np.cos(ang)], axis=0)
    sin = np.concatenate([np.zeros((CTX_LEN, LANES)), np.sin(ang) * sign], axis=0)
    return jnp.asarray(cos, F32), jnp.asarray(sin, F32)


def _s5_operators(lam_re, lam_im, log_dt, b_re, b_im, c_re, c_im, d_skip):
    t_len = CHUNK
    n_pow = 2 * t_len + 1
    lr, li = lam_re.astype(F32), lam_im.astype(F32)
    dt = jnp.exp(log_dt.astype(F32))[..., None]
    ks = jnp.arange(n_pow, dtype=F32)
    mag = jnp.exp((lr * dt)[..., None] * ks)
    ph = (li * dt)[..., None] * ks
    pr, pi = mag * jnp.cos(ph), mag * jnp.sin(ph)
    ar1, ai1 = pr[..., 1] - 1.0, pi[..., 1]
    den = lr * lr + li * li
    gr, gi = (ar1 * lr + ai1 * li) / den, (ai1 * lr - ar1 * li) / den
    br_, bi_ = b_re.astype(F32), b_im.astype(F32)
    bbr = gr[..., None] * br_ - gi[..., None] * bi_
    bbi = gr[..., None] * bi_ + gi[..., None] * br_
    pad_k = POW_ROWS - n_pow
    prt, pit = jnp.swapaxes(pr, 2, 3), jnp.swapaxes(pi, 2, 3)
    pw = jnp.pad(jnp.concatenate([prt, prt, pit, pit], axis=-1),
                 ((0, 0), (0, 0), (0, pad_k), (0, 0)))
    pwc = jnp.stack([jnp.concatenate([pr, pr], axis=2), jnp.concatenate([pi, pi], axis=2)], axis=2)
    pwc = jnp.pad(pwc, ((0, 0),) * 4 + ((0, pad_k),))
    brt, bit = jnp.swapaxes(bbr, 2, 3), jnp.swapaxes(bbi, 2, 3)
    bb = jnp.concatenate([brt, bit, -bit, brt, bit, brt, brt, -bit], axis=-1)
    crt = jnp.swapaxes(c_re.astype(F32), 2, 3)
    cit = jnp.swapaxes(c_im.astype(F32), 2, 3)
    cm = jnp.stack([jnp.concatenate([crt, -cit], axis=2),
                    jnp.concatenate([-cit, -crt], axis=2)], axis=2)
    dv = jnp.tile(d_skip.astype(F32).reshape(SSM_GROUPS, 1, SSM_GROUP), (1, 1, t_len))
    return _s5_operator_call(pw, pwc, bb, cm, dv)


def _operator_constants():
    t_len = CHUNK
    s_of_row = np.arange(CHUNK_W) // SSM_GROUP
    k_ar = np.arange(POW_ROWS)
    oh_f = (k_ar[None, :] == (t_len - 1 - s_of_row)[:, None]).astype(np.float32)
    oh_b = (k_ar[None, :] == s_of_row[:, None]).astype(np.float32)
    t_of_lane = np.arange(CHUNK_W) // SSM_GROUP
    expo = [t_of_lane + 1, t_len - t_of_lane, t_of_lane, t_len - 1 - t_of_lane]
    sel = np.stack([(k_ar[:, None] == e[None, :]) for e in expo]).astype(np.float32)
    h_of_lane = np.arange(CHUNK_W) % SSM_GROUP
    tl = (np.arange(SSM_GROUP)[:, None] == h_of_lane[None, :]).astype(np.float32)
    dmask = ((s_of_row[:, None] == t_of_lane[None, :])
             & ((np.arange(CHUNK_W) % SSM_GROUP)[:, None] == h_of_lane[None, :])).astype(np.float32)
    return oh_f, oh_b, sel, tl, dmask


def _s5_op_kernel(*refs):
    for gg in range(OP_GROUP_BATCH):
        _s5_op_group(gg, *refs)


def _s5_op_group(gg, pw_ref, pwc_ref, bb_ref, cm_ref, dv_ref, ohf_ref, ohb_ref, sel_ref, tl_ref,
                 dmask_ref, m_ref, ws_ref, wy_ref, cst_ref):
    t_len = CHUNK

    def mm(a, b):
        return jnp.dot(a, b, preferred_element_type=F32, precision=lax.Precision.HIGHEST)

    def split(a):
        hi = a.astype(BF16)
        return hi, (a - hi.astype(F32)).astype(BF16)

    def pick_rows(onehot, table):
        hi, lo = split(table)
        return (jnp.dot(onehot, hi, preferred_element_type=F32)
                + jnp.dot(onehot, lo, preferred_element_type=F32))

    def spread_lanes(table, onehot):
        hi, lo = split(table)
        return (jnp.dot(hi, onehot, preferred_element_type=F32)
                + jnp.dot(lo, onehot, preferred_element_type=F32))

    main, swapped = [], []
    for d, oh_ref in ((0, ohf_ref), (1, ohb_ref)):
        pp = pick_rows(oh_ref[...], pw_ref[d, gg])
        p_re, p_im = pp[:, :STATE_W], pp[:, STATE_W:]
        b0, b1, b2, b3 = [jnp.concatenate([bb_ref[d, gg, :, STATE_W * i:STATE_W * (i + 1)]] * t_len,
                                          axis=0) for i in range(4)]
        main.append(p_re * b0 + p_im * b1)
        swapped.append(p_re * b2 + p_im * b3)
    ws_ref[gg] = jnp.concatenate(main + swapped, axis=1).astype(BF16)

    c_tiled = [[spread_lanes(cm_ref[d, gg, i], tl_ref[...]) for i in range(2)] for d in range(2)]

    def block(d, pat):
        sel = sel_ref[pat]
        return (spread_lanes(pwc_ref[d, gg, 0], sel) * c_tiled[d][0]
                + spread_lanes(pwc_ref[d, gg, 1], sel) * c_tiled[d][1])

    wy_ref[gg] = jnp.concatenate([block(0, 0), block(1, 1)], axis=0).astype(BF16)

    kt_f = mm(bb_ref[0, gg, :, :STATE_W], block(0, 2))
    kt_b = mm(bb_ref[1, gg, :, :STATE_W], block(1, 3))
    lane = lax.broadcasted_iota(jnp.int32, (SSM_GROUP, CHUNK_W), 1)
    skip = dv_ref[gg]
    for s in range(t_len):
        fwd = kt_f if s == 0 else pltpu.roll(kt_f, SSM_GROUP * s, 1)
        back = t_len - 1 - s
        bwd = kt_b if back == 0 else pltpu.roll(kt_b, CHUNK_W - SSM_GROUP * back, 1)
        rows = (jnp.where(lane >= SSM_GROUP * s, fwd, 0.0)
                + jnp.where(lane < SSM_GROUP * (s + 1), bwd, 0.0)
                + dmask_ref[SSM_GROUP * s:SSM_GROUP * (s + 1), :] * skip)
        m_ref[gg, SSM_GROUP * s:SSM_GROUP * (s + 1), :] = rows.astype(BF16)

    even = lax.broadcasted_iota(jnp.int32, (SUBLANES, STATE_W), 0) % 2 == 0
    sign = jnp.where(lax.broadcasted_iota(jnp.int32, (1, STATE_W), 1) < SSM_STATE, -1.0, 1.0)

    def w12(d, k):
        row = pw_ref[d, gg, k:k + 1, :]
        return row[:, :STATE_W], row[:, STATE_W:] * sign

    zero = (jnp.zeros((1, STATE_W), F32),) * 2
    pairs = [(w12(0, t_len), w12(0, 2 * t_len)), (zero, w12(0, t_len)),
             (w12(1, 2 * t_len), w12(1, t_len)), (w12(1, t_len), zero)]
    idx = 0
    for top, bot in pairs:
        for part in range(2):
            cst_ref[gg, idx] = jnp.where(even, jnp.broadcast_to(top[part], (SUBLANES, STATE_W)),
                                         jnp.broadcast_to(bot[part], (SUBLANES, STATE_W)))
            idx += 1


def _s5_operator_call(pw, pwc, bb, cm, dv):
    oh_f, oh_b, sel, tl, dmask = _operator_constants()
    consts = [jnp.asarray(oh_f, BF16), jnp.asarray(oh_b, BF16), jnp.asarray(sel, BF16),
              jnp.asarray(tl, BF16), jnp.asarray(dmask)]

    def per_group(*tail):
        n = len(tail)
        return pl.BlockSpec((2, OP_GROUP_BATCH) + tail, lambda g: (0, g) + (0,) * n)

    def whole(a):
        return pl.BlockSpec(a.shape, lambda g: (0,) * a.ndim)

    def out(*tail):
        return pl.BlockSpec((OP_GROUP_BATCH,) + tail, lambda g: (g,) + (0,) * len(tail))

    return pl.pallas_call(
        _s5_op_kernel,
        grid=(SSM_GROUPS // OP_GROUP_BATCH,),
        in_specs=[per_group(POW_ROWS, 2 * STATE_W), per_group(2, STATE_W, POW_ROWS),
                  per_group(SSM_GROUP, 4 * STATE_W), per_group(2, STATE_W, SSM_GROUP),
                  pl.BlockSpec((OP_GROUP_BATCH, 1, CHUNK_W), lambda g: (g, 0, 0))]
                 + [whole(a) for a in consts],
        out_specs=[out(CHUNK_W, CHUNK_W), out(CHUNK_W, 4 * STATE_W), out(2 * STATE_W, CHUNK_W),
                   out(8, SUBLANES, STATE_W)],
        out_shape=[jax.ShapeDtypeStruct((SSM_GROUPS, CHUNK_W, CHUNK_W), BF16),
                   jax.ShapeDtypeStruct((SSM_GROUPS, CHUNK_W, 4 * STATE_W), BF16),
                   jax.ShapeDtypeStruct((SSM_GROUPS, 2 * STATE_W, CHUNK_W), BF16),
                   jax.ShapeDtypeStruct((SSM_GROUPS, 8, SUBLANES, STATE_W), F32)],
        compiler_params=_params(("arbitrary",), 32),
        name="s5_operators",
    )(pw, pwc, bb, cm, dv, *consts)


def kernel(x, c, ctx, c_ctx, norm_w, w_ada, b_ada, attn_w_in, attn_sink, attn_w_out,
           ssm_w_in, ssm_lam_re, ssm_lam_im, ssm_log_dt, ssm_b_re, ssm_b_im, ssm_c_re, ssm_c_im,
           ssm_d, ssm_w_glu, ssm_w_out, final_norm_w):
    mod0, mod1 = _modulation(c, c_ctx, w_ada, b_ada)

    w_in0, w_out0, sink_tab = _attn_weights(attn_w_in[0], attn_w_out[0], attn_sink[0])
    cos_tab, sin_tab = _rope_tables()
    q, z0, kbd, vbd = _proj0(x, ctx, mod0, norm_w[0], cos_tab, sin_tab, w_in0)
    xc1 = _attention(q, z0, kbd, vbd, sink_tab, x, ctx, mod0, w_out0)

    u_g, z1 = _proj1(xc1, mod1, norm_w[1], ssm_w_in[0].astype(BF16))
    m_mat, ws_mat, wy_mat, consts = _s5_operators(
        ssm_lam_re[0], ssm_lam_im[0], ssm_log_dt[0], ssm_b_re[0], ssm_b_im[0],
        ssm_c_re[0], ssm_c_im[0], ssm_d[0])
    y_g = _s5_core(u_g, m_mat, ws_mat, wy_mat, consts)
    return _out1(y_g, z1, xc1, mod1, ssm_w_glu[0].astype(BF16), ssm_w_out[0].astype(BF16),
                 final_norm_w)
```

```python
import functools
import math

import jax
import jax.numpy as jnp
import numpy as np
from jax import lax
from jax.experimental import pallas as pl
from jax.experimental.pallas import tpu as pltpu

F32 = jnp.float32
BF16 = jnp.bfloat16

D_MODEL = 1024
BATCH = 4
SEQ = 4096
GRID_W = 64
CTX_LEN = 256
TOTAL = CTX_LEN + SEQ
HEAD_DIM = 64
N_HEADS = 16
N_KV_HEADS = 4
ATTN_WIDTH = N_HEADS * HEAD_DIM
KV_WIDTH = N_KV_HEADS * HEAD_DIM
BLOCK = 128
N_BLOCKS = TOTAL // BLOCK
N_CTX_BLOCKS = CTX_LEN // BLOCK
ROPE_BASE = 10000.0
ROPE_FREQS = HEAD_DIM // 4
SSM_GROUP = 16
SSM_GROUPS = D_MODEL // SSM_GROUP
SSM_STATE = 64
NORM_EPS = 1e-6
NEG_INF = -1e30

LANES = 128
SUBLANES = 8
N_SLOTS = ATTN_WIDTH // LANES
N_KV_PAIRS = KV_WIDTH // LANES
SLOTS_PER_M = N_SLOTS // N_KV_PAIRS
ATTN_QBLOCKS = 2
assert N_CTX_BLOCKS % ATTN_QBLOCKS == 0 and N_BLOCKS % ATTN_QBLOCKS == 0
UNIT_SLOTS = 2
LOG2E = math.log2(math.e)
Q_SCALE = HEAD_DIM ** -0.5 * LOG2E
ROW_TILE = 256
N_ROW_TILES = TOTAL // ROW_TILE
CHUNK = 16
N_CHUNKS = TOTAL // CHUNK
N_CTX_CHUNKS = CTX_LEN // CHUNK
CHUNK_W = CHUNK * SSM_GROUP
CHUNKS_PER_TILE = ROW_TILE // CHUNK
PAIRS_PER_TILE = CHUNKS_PER_TILE // 2
SLOT_PITCH = 24
OUT_PHASES = 4
assert SLOT_PITCH >= CHUNK and SLOT_PITCH % SUBLANES == 0
STATE_W = 2 * SSM_STATE
SCAN_ROWS = N_CHUNKS * BATCH
CTX_SCAN_ROWS = N_CTX_CHUNKS * BATCH
N_SCAN_BLOCKS = SCAN_ROWS // SUBLANES
N_CTX_SCAN_BLOCKS = CTX_SCAN_ROWS // SUBLANES
GROUP_BATCH = 4
OP_GROUP_BATCH = 4
POW_ROWS = 48

assert BATCH * 2 == SUBLANES


def _params(semantics, vmem_mb):
    return pltpu.CompilerParams(dimension_semantics=semantics,
                                vmem_limit_bytes=vmem_mb * 1024 * 1024)


def _mod_kernel(c_ref, w_ref, b_ref, o_ref):
    c = c_ref[...]
    a = c * jax.nn.sigmoid(c)
    o_ref[0] = jnp.dot(a, w_ref[0], preferred_element_type=F32,
                       precision=lax.Precision.HIGHEST) + b_ref[0]


def _modulation(c, c_ctx, w_ada, b_ada):
    depth = w_ada.shape[0]
    rows = jnp.zeros((SUBLANES, D_MODEL), F32).at[:BATCH].set(c).at[BATCH].set(c_ctx)
    n_col = 3
    out = pl.pallas_call(
        _mod_kernel,
        grid=(depth, n_col),
        in_specs=[
            pl.BlockSpec((SUBLANES, D_MODEL), lambda l, j: (0, 0)),
            pl.BlockSpec((1, D_MODEL, D_MODEL), lambda l, j: (l, 0, j)),
            pl.BlockSpec((1, 1, D_MODEL), lambda l, j: (l, 0, j)),
        ],
        out_specs=pl.BlockSpec((1, SUBLANES, D_MODEL), lambda l, j: (l, 0, j)),
        out_shape=jax.ShapeDtypeStruct((depth, SUBLANES, 3 * D_MODEL), F32),
        compiler_params=_params(("arbitrary", "arbitrary"), 32),
        name="adaln_modulation",
    )(rows, w_ada, b_ada.reshape(depth, 1, 3 * D_MODEL))
    tabs = []
    for l in range(depth):
        lat = out[l, :BATCH].reshape(BATCH, 3, D_MODEL)
        cx = jnp.broadcast_to(out[l, BATCH].reshape(1, 3, D_MODEL), (BATCH, 3, D_MODEL))
        tab = jnp.stack([cx, lat], axis=1)
        tabs.append(jnp.pad(tab, ((0, 0), (0, 0), (0, SUBLANES - 3), (0, 0))))
    return tabs


def _modulated_norm(xt, nw, mod_ref, b=0):
    ms = jnp.mean(xt * xt, axis=-1, keepdims=True)
    y = xt * lax.rsqrt(ms + NORM_EPS) * nw
    return y * (1.0 + mod_ref[b, 0, 1:2, :]) + mod_ref[b, 0, 0:1, :]


def _lane_block_transpose(vs):
    n = len(vs)
    width = LANES // n
    blk = lax.broadcasted_iota(jnp.int32, vs[0].shape, 1) // width
    x = list(vs)
    d = n // 2
    while d >= 1:
        clear = (blk & d) == 0
        y = list(x)
        for i in range(n):
            if i & d == 0:
                a, b = x[i], x[i + d]
                y[i] = jnp.where(clear, a, pltpu.roll(b, width * d, 1))
                y[i + d] = jnp.where(clear, pltpu.roll(a, LANES - width * d, 1), b)
        x = y
        d //= 2
    return x


def _chunk_slot(b, cc):
    return (cc // 2) * (2 * BATCH) + 2 * b + (cc % 2)


def _proj0_kernel(x_ref, c_ref, mod_ref, nw_ref, cos_ref, sin_ref, w_ref,
                  q_ref, z_ref, kbd_ref, vbd_ref):
    is_ctx = pl.program_id(0) == 0
    h = jnp.concatenate(
        [_modulated_norm(jnp.where(is_ctx, c_ref[b], x_ref[b]), nw_ref[...], mod_ref, b).astype(BF16)
         for b in range(BATCH)], axis=0)
    cos = cos_ref[...]
    sin = sin_ref[...]
    lane = lax.broadcasted_iota(jnp.int32, (ROW_TILE, LANES), 1)
    first_half = (lane % HEAD_DIM) < (HEAD_DIM // 2)
    low = lax.broadcasted_iota(jnp.int32, (BLOCK, LANES), 1) < HEAD_DIM

    def rope(t):
        partner = jnp.where(first_half, pltpu.roll(t, LANES - HEAD_DIM // 2, 1),
                            pltpu.roll(t, HEAD_DIM // 2, 1))
        return t * cos + partner * sin

    q = jnp.dot(h, w_ref[:, :ATTN_WIDTH], preferred_element_type=F32)
    k = jnp.dot(h, w_ref[:, ATTN_WIDTH:ATTN_WIDTH + KV_WIDTH], preferred_element_type=F32)
    v = jnp.dot(h, w_ref[:, ATTN_WIDTH + KV_WIDTH:ATTN_WIDTH + 2 * KV_WIDTH],
                preferred_element_type=F32)
    z = jnp.dot(h, w_ref[:, ATTN_WIDTH + 2 * KV_WIDTH:], preferred_element_type=F32)
    for b in range(BATCH):
        tile = slice(ROW_TILE * b, ROW_TILE * (b + 1))
        z_ref[b] = z[tile].astype(BF16)
        for j in range(N_SLOTS):
            m, gi = divmod(j, SLOTS_PER_M)
            qj = (rope(q[tile, LANES * j:LANES * (j + 1)]) * Q_SCALE).astype(BF16)
            for blk in range(ROW_TILE // BLOCK):
                q_ref[b, blk, m, BLOCK * gi:BLOCK * (gi + 1), :] = qj[BLOCK * blk:BLOCK * (blk + 1)]
        for m in range(N_KV_PAIRS):
            sl = slice(LANES * m, LANES * (m + 1))
            kr = rope(k[tile, sl])
            vm = v[tile, sl]
            for blk in range(ROW_TILE // BLOCK):
                rows = slice(BLOCK * blk, BLOCK * (blk + 1))
                kbd_ref[b, blk, m, :BLOCK, :] = jnp.where(low, kr[rows], 0.0).astype(BF16)
                kbd_ref[b, blk, m, BLOCK:, :] = jnp.where(low, 0.0, kr[rows]).astype(BF16)
                vbd_ref[b, blk, m, :BLOCK, :] = jnp.where(low, vm[rows], 0.0).astype(BF16)
                vbd_ref[b, blk, m, BLOCK:, :] = jnp.where(low, 0.0, vm[rows]).astype(BF16)


def _proj0(x, ctx, mod, norm_w, cos_tab, sin_tab, w_in):
    n_col = w_in.shape[1]
    blocks_per_tile = ROW_TILE // BLOCK
    kv_shape = jax.ShapeDtypeStruct((BATCH, N_BLOCKS, N_KV_PAIRS, 2 * BLOCK, LANES), BF16)
    kv_spec = pl.BlockSpec((BATCH, blocks_per_tile, N_KV_PAIRS, 2 * BLOCK, LANES),
                           lambda i: (0, i, 0, 0, 0))
    row_spec = pl.BlockSpec((BATCH, ROW_TILE, D_MODEL), lambda i: (0, i, 0))
    once = pl.Buffered(1)
    return pl.pallas_call(
        _proj0_kernel,
        grid=(N_ROW_TILES,),
        in_specs=[
            pl.BlockSpec((BATCH, ROW_TILE, D_MODEL), lambda i: (0, jnp.maximum(i - 1, 0), 0)),
            pl.BlockSpec((BATCH, ROW_TILE, D_MODEL), lambda i: (0, 0, 0), pipeline_mode=once),
            pl.BlockSpec((BATCH, 1, SUBLANES, D_MODEL), lambda i: (0, jnp.minimum(i, 1), 0, 0)),
            pl.BlockSpec((1, D_MODEL), lambda i: (0, 0)),
            pl.BlockSpec((ROW_TILE, LANES), lambda i: (i, 0)),
            pl.BlockSpec((ROW_TILE, LANES), lambda i: (i, 0)),
            pl.BlockSpec((D_MODEL, n_col), lambda i: (0, 0), pipeline_mode=once),
        ],
        out_specs=[
            pl.BlockSpec((BATCH, blocks_per_tile, N_KV_PAIRS, SLOTS_PER_M * BLOCK, LANES),
                         lambda i: (0, i, 0, 0, 0)),
            row_spec, kv_spec, kv_spec],
        out_shape=[
            jax.ShapeDtypeStruct((BATCH, N_BLOCKS, N_KV_PAIRS, SLOTS_PER_M * BLOCK, LANES), BF16),
            jax.ShapeDtypeStruct((BATCH, TOTAL, ATTN_WIDTH), BF16),
            kv_shape, kv_shape,
        ],
        compiler_params=_params(("arbitrary",), 56),
        name="attn_projection",
    )(x, ctx, mod, norm_w.reshape(1, D_MODEL), cos_tab, sin_tab, w_in)


def _attn_kernel(q_ref, z_ref, k0_ref, k1_ref, k2_ref, k3_ref, kx_ref,
                 v0_ref, v1_ref, v2_ref, v3_ref, vx_ref,
                 sink_ref, x_ref, c_ref, mod_ref, wo_ref, eye_ref, tri_ref, o_ref):
    step = pl.program_id(1)
    is_lat = step >= N_CTX_BLOCKS // ATTN_QBLOCKS
    n_first = ATTN_QBLOCKS * step - N_CTX_BLOCKS
    q_rows = UNIT_SLOTS * BLOCK
    blocked = tri_ref[2]
    bias = []
    for qb in range(ATTN_QBLOCKS):
        n = n_first + qb
        bias.append([jnp.where(jnp.logical_and(is_lat, n >= 1), tri_ref[0], blocked),
                     jnp.where(is_lat, tri_ref[3], blocked),
                     jnp.where(jnp.logical_and(is_lat, n <= SEQ // BLOCK - 2), tri_ref[1], blocked),
                     None, None])
    low = lax.broadcasted_iota(jnp.int32, (BLOCK, LANES), 1) < HEAD_DIM
    k_win = (k0_ref, k1_ref, k2_ref, k3_ref)
    v_win = (v0_ref, v1_ref, v2_ref, v3_ref)

    def kpiece(qb, p, m):
        if p < 3:
            return k_win[qb + p][0, 0, m], v_win[qb + p][0, 0, m]
        return kx_ref[0, p - 3, m], vx_ref[0, p - 3, m]

    n_piece = 3 + N_CTX_BLOCKS
    units = [(qb, m, h) for qb in range(ATTN_QBLOCKS) for m in range(N_KV_PAIRS)
             for h in range(SLOTS_PER_M // UNIT_SLOTS)]

    nt = (((1,), (1,)), ((), ()))

    def scores(qb, m, h):
        qu = q_ref[0, qb, m, q_rows * h:q_rows * (h + 1), :]
        qu_masked = jnp.concatenate([qu, eye_ref[...]], axis=1)
        s_list = []
        for p in range(n_piece):
            kbd, _ = kpiece(qb, p, m)
            if bias[qb][p] is None:
                s = lax.dot_general(qu, kbd, nt, preferred_element_type=F32)
            else:
                s = lax.dot_general(qu_masked, jnp.concatenate([kbd, bias[qb][p]], axis=1), nt,
                                    preferred_element_type=F32)
            s_list.append(s)
        return s_list

    def finish(qb, m, h, s_list, y):
        slot0 = SLOTS_PER_M * m + UNIT_SLOTS * h
        mx = s_list[0]
        for s in s_list[1:]:
            mx = jnp.maximum(mx, s)

        def sink_col(hs):
            return jnp.concatenate(
                [jnp.broadcast_to(sink_ref[2 * (slot0 + gi) + hs:2 * (slot0 + gi) + hs + 1, 0:1],
                                  (BLOCK, 1)) for gi in range(UNIT_SLOTS)], axis=0)

        sink_a, sink_b = sink_col(0), sink_col(1)
        m_a = jnp.maximum(jnp.max(mx[:, :BLOCK], axis=1, keepdims=True), sink_a)
        m_b = jnp.maximum(jnp.max(mx[:, BLOCK:], axis=1, keepdims=True), sink_b)
        mb_a = jnp.broadcast_to(m_a, (q_rows, BLOCK))
        mb_b = jnp.broadcast_to(m_b, (q_rows, BLOCK))
        acc = None
        esum_a = jnp.zeros((q_rows, BLOCK), F32)
        esum_b = jnp.zeros((q_rows, BLOCK), F32)
        for p in range(n_piece):
            e_a = jnp.exp2(s_list[p][:, :BLOCK] - mb_a)
            e_b = jnp.exp2(s_list[p][:, BLOCK:] - mb_b)
            esum_a = esum_a + e_a
            esum_b = esum_b + e_b
            pb = jnp.concatenate([e_a, e_b], axis=1).astype(BF16)
            part = jnp.dot(pb, kpiece(qb, p, m)[1], preferred_element_type=F32)
            acc = part if acc is None else acc + part
        inv_a = 1.0 / (jnp.sum(esum_a, axis=1, keepdims=True) + jnp.exp2(sink_a - m_a))
        inv_b = 1.0 / (jnp.sum(esum_b, axis=1, keepdims=True) + jnp.exp2(sink_b - m_b))
        outs = []
        for gi in range(UNIT_SLOTS):
            j = slot0 + gi
            rows = slice(BLOCK * gi, BLOCK * (gi + 1))
            inv = jnp.where(low, inv_a[rows], inv_b[rows])
            zj = z_ref[0, BLOCK * qb:BLOCK * (qb + 1), LANES * j:LANES * (j + 1)].astype(F32)
            outs.append((acc[rows] * inv * (zj * jax.nn.sigmoid(zj))).astype(BF16))
        g = jnp.concatenate(outs, axis=1)
        part = jnp.dot(g, wo_ref[LANES * slot0:LANES * (slot0 + UNIT_SLOTS), :],
                       preferred_element_type=F32)
        return part if y is None else y + part

    y = [None] * ATTN_QBLOCKS
    pending = scores(*units[0])
    for u, (qb, m, h) in enumerate(units):
        nxt = scores(*units[u + 1]) if u + 1 < len(units) else None
        y[qb] = finish(qb, m, h, pending, y[qb])
        pending = nxt
    resid = jnp.where(is_lat, x_ref[0], c_ref[0])
    gate = mod_ref[0, 0, 2:3, :]
    for qb in range(ATTN_QBLOCKS):
        rows = slice(BLOCK * qb, BLOCK * (qb + 1))
        o_ref[0, rows, :] = resid[rows] + gate * y[qb]


def _attention(q, z, kbd, vbd, sink_tab, x, ctx, mod, w_out):
    last = N_BLOCKS - 1
    n_m = KV_WIDTH // LANES
    nq = ATTN_QBLOCKS
    ctx_steps = N_CTX_BLOCKS // nq
    row_spec = pl.BlockSpec((1, nq * BLOCK, ATTN_WIDTH), lambda b, i: (b, i, 0))

    def kv_spec(off):
        return pl.BlockSpec((1, 1, n_m, 2 * BLOCK, LANES),
                            lambda b, i: (b, jnp.clip(nq * i + off, 0, last), 0, 0, 0))

    ctx_kv_spec = pl.BlockSpec((1, N_CTX_BLOCKS, n_m, 2 * BLOCK, LANES),
                               lambda b, i: (b, 0, 0, 0, 0))
    off = np.arange(BLOCK)
    eye = np.tile(np.eye(BLOCK, dtype=np.float32), (UNIT_SLOTS, 1))
    key_ge = np.where(off[:, None] >= off[None, :], 0.0, NEG_INF)
    key_le = np.where(off[:, None] <= off[None, :], 0.0, NEG_INF)
    tri = np.stack([np.tile(t, (2, 1)) for t in
                    (key_ge, key_le, np.full((BLOCK, BLOCK), NEG_INF), np.zeros((BLOCK, BLOCK)))])
    consts = [jnp.asarray(a, BF16) for a in (eye, tri)]
    return pl.pallas_call(
        _attn_kernel,
        grid=(BATCH, N_BLOCKS // nq),
        in_specs=[
            pl.BlockSpec((1, nq, n_m, SLOTS_PER_M * BLOCK, LANES), lambda b, i: (b, i, 0, 0, 0)),
            row_spec,
            kv_spec(-1), kv_spec(0), kv_spec(1), kv_spec(2), ctx_kv_spec,
            kv_spec(-1), kv_spec(0), kv_spec(1), kv_spec(2), ctx_kv_spec,
            pl.BlockSpec((2 * N_SLOTS, LANES), lambda b, i: (0, 0)),
            pl.BlockSpec((1, nq * BLOCK, D_MODEL),
                         lambda b, i: (b, jnp.maximum(i - ctx_steps, 0), 0)),
            pl.BlockSpec((1, nq * BLOCK, D_MODEL),
                         lambda b, i: (b, jnp.minimum(i, ctx_steps - 1), 0)),
            pl.BlockSpec((1, 1, SUBLANES, D_MODEL),
                         lambda b, i: (b, jnp.minimum(i // ctx_steps, 1), 0, 0)),
            pl.BlockSpec((ATTN_WIDTH, D_MODEL), lambda b, i: (0, 0)),
        ] + [pl.BlockSpec(a.shape, lambda b, i, nd=a.ndim: (0,) * nd) for a in consts],
        out_specs=pl.BlockSpec((1, nq * BLOCK, D_MODEL), lambda b, i: (b, i, 0)),
        out_shape=jax.ShapeDtypeStruct((BATCH, TOTAL, D_MODEL), F32),
        compiler_params=_params(("arbitrary", "arbitrary"), 48),
        name="window_attention",
    )(q, z, kbd, kbd, kbd, kbd, kbd, vbd, vbd, vbd, vbd, vbd, sink_tab, x, ctx, mod, w_out,
      *consts)


def _proj1_kernel(x_ref, mod_ref, nw_ref, w_ref, u_ref, z_ref, scr_ref):
    h = jnp.concatenate(
        [_modulated_norm(x_ref[b], nw_ref[...], mod_ref, b).astype(BF16) for b in range(BATCH)],
        axis=0)
    u = jnp.dot(h, w_ref[:, :D_MODEL], preferred_element_type=F32)
    z = jnp.dot(h, w_ref[:, D_MODEL:], preferred_element_type=F32)
    for b in range(BATCH):
        z_ref[b] = z[ROW_TILE * b:ROW_TILE * (b + 1)].astype(BF16)
    n_slab = D_MODEL // LANES
    for b in range(BATCH):
        for cc in range(CHUNKS_PER_TILE):
            r0 = ROW_TILE * b + CHUNK * cc
            q0 = SLOT_PITCH * _chunk_slot(b, cc)
            for k in range(n_slab):
                scr_ref[k, q0:q0 + CHUNK, :] = u[r0:r0 + CHUNK, LANES * k:LANES * (k + 1)]
    groups_per_slab = LANES // SSM_GROUP
    for k in range(n_slab):
        for pp in range(PAIRS_PER_TILE // 2):
            parts = []
            for p in (2 * pp, 2 * pp + 1):
                rows = [scr_ref[k, pl.ds(SLOT_PITCH * SUBLANES * p + s, SUBLANES,
                                         stride=SLOT_PITCH), :] for s in range(CHUNK)]
                parts.append([_lane_block_transpose(rows[SUBLANES * m2:SUBLANES * (m2 + 1)])
                              for m2 in range(CHUNK // SUBLANES)])
            for m2 in range(CHUNK // SUBLANES):
                for gl in range(groups_per_slab):
                    val = jnp.concatenate([parts[0][m2][gl], parts[1][m2][gl]], axis=0)
                    u_ref[groups_per_slab * k + gl, 2 * SUBLANES * pp:2 * SUBLANES * (pp + 1),
                          LANES * m2:LANES * (m2 + 1)] = val.astype(BF16)


def _proj1(xc, mod, norm_w, w_in):
    row_spec = pl.BlockSpec((BATCH, ROW_TILE, D_MODEL), lambda i: (0, i, 0))
    tile_rows = PAIRS_PER_TILE * SUBLANES
    return pl.pallas_call(
        _proj1_kernel,
        grid=(N_ROW_TILES,),
        in_specs=[
            row_spec,
            pl.BlockSpec((BATCH, 1, SUBLANES, D_MODEL), lambda i: (0, jnp.minimum(i, 1), 0, 0)),
            pl.BlockSpec((1, D_MODEL), lambda i: (0, 0)),
            pl.BlockSpec((D_MODEL, 2 * D_MODEL), lambda i: (0, 0)),
        ],
        out_specs=[pl.BlockSpec((SSM_GROUPS, tile_rows, CHUNK_W), lambda i: (0, i, 0)), row_spec],
        out_shape=[jax.ShapeDtypeStruct((SSM_GROUPS, SCAN_ROWS, CHUNK_W), BF16),
                   jax.ShapeDtypeStruct((BATCH, TOTAL, D_MODEL), BF16)],
        scratch_shapes=[pltpu.VMEM((D_MODEL // LANES, BATCH * CHUNKS_PER_TILE * SLOT_PITCH, LANES),
                                   F32)],
        compiler_params=_params(("arbitrary",), 56),
        name="ssm_projection",
    )(xc, mod, norm_w.reshape(1, D_MODEL), w_in)


def _s5_kernel(u_ref, m_ref, ws_ref, wy_ref, cst_ref, y_ref, s4_ref, xp_ref):
    for g in range(GROUP_BATCH):
        s4_ref[g] = jnp.dot(u_ref[g], ws_ref[g], preferred_element_type=F32)
    even = lax.broadcasted_iota(jnp.int32, (SUBLANES, STATE_W), 0) % 2 == 0
    down = 1
    up = SUBLANES - 1
    fwd = slice(0, STATE_W)
    bwd = slice(STATE_W, 2 * STATE_W)
    fwd_sw = slice(2 * STATE_W, 3 * STATE_W)
    bwd_sw = slice(3 * STATE_W, 4 * STATE_W)

    def step(j, carry):
        jb = jnp.where(j < N_CTX_SCAN_BLOCKS, N_CTX_SCAN_BLOCKS - 1 - j,
                       N_SCAN_BLOCKS - 1 + N_CTX_SCAN_BLOCKS - j)
        rf = pl.ds(pl.multiple_of(j * SUBLANES, SUBLANES), SUBLANES)
        rb = pl.ds(pl.multiple_of(jb * SUBLANES, SUBLANES), SUBLANES)
        new = []
        for g in range(GROUP_BATCH):
            cf, cfs, cb, cbs = carry[4 * g:4 * g + 4]
            p1f, p2f, q1f, q2f = cst_ref[g, 0], cst_ref[g, 1], cst_ref[g, 2], cst_ref[g, 3]
            p1b, p2b, q1b, q2b = cst_ref[g, 4], cst_ref[g, 5], cst_ref[g, 6], cst_ref[g, 7]
            zf = s4_ref[g, rf, fwd]
            zfs = s4_ref[g, rf, fwd_sw]
            rzf = pltpu.roll(zf, down, 0)
            rzfs = pltpu.roll(zfs, down, 0)
            xf = p1f * cf + p2f * cfs + (zf + q1f * rzf + q2f * rzfs)
            xfs = p1f * cfs - p2f * cf + (zfs + q1f * rzfs - q2f * rzf)
            xp_ref[g, rf, fwd] = jnp.where(even, cf, pltpu.roll(xf, down, 0))
            new += [jnp.where(even, pltpu.roll(xf, up, 0), xf),
                    jnp.where(even, pltpu.roll(xfs, up, 0), xfs)]
            zb = s4_ref[g, rb, bwd]
            zbs = s4_ref[g, rb, bwd_sw]
            rzb = pltpu.roll(zb, up, 0)
            rzbs = pltpu.roll(zbs, up, 0)
            xb = p1b * cb + p2b * cbs + (zb + q1b * rzb + q2b * rzbs)
            xbs = p1b * cbs - p2b * cb + (zbs + q1b * rzbs - q2b * rzb)
            xp_ref[g, rb, bwd] = jnp.where(even, pltpu.roll(xb, up, 0), cb)
            new += [jnp.where(even, xb, pltpu.roll(xb, down, 0)),
                    jnp.where(even, xbs, pltpu.roll(xbs, down, 0))]
        return tuple(new)

    zero = jnp.zeros((SUBLANES, STATE_W), F32)
    lax.fori_loop(0, N_SCAN_BLOCKS, step, (zero,) * (4 * GROUP_BATCH))
    for g in range(GROUP_BATCH):
        y_ref[g] = (
            jnp.dot(u_ref[g, CTX_SCAN_ROWS:, :], m_ref[g], preferred_element_type=F32)
            + jnp.dot(xp_ref[g, CTX_SCAN_ROWS:, :].astype(BF16), wy_ref[g],
                      preferred_element_type=F32))


def _s5_core(u_g, m_mat, ws_mat, wy_mat, consts):
    lat_rows = SCAN_ROWS - CTX_SCAN_ROWS

    def gspec(*tail):
        return pl.BlockSpec((GROUP_BATCH,) + tail, lambda i: (i,) + (0,) * len(tail))

    return pl.pallas_call(
        _s5_kernel,
        grid=(SSM_GROUPS // GROUP_BATCH,),
        in_specs=[
            gspec(SCAN_ROWS, CHUNK_W),
            gspec(CHUNK_W, CHUNK_W),
            gspec(CHUNK_W, 4 * STATE_W),
            gspec(2 * STATE_W, CHUNK_W),
            gspec(8, SUBLANES, STATE_W),
        ],
        out_specs=gspec(lat_rows, CHUNK_W),
        out_shape=jax.ShapeDtypeStruct((SSM_GROUPS, lat_rows, CHUNK_W), F32),
        scratch_shapes=[
            pltpu.VMEM((GROUP_BATCH, SCAN_ROWS, 4 * STATE_W), F32),
            pltpu.VMEM((GROUP_BATCH, SCAN_ROWS, 2 * STATE_W), F32),
        ],
        compiler_params=_params(("arbitrary",), 48),
        name="s5_scan",
    )(u_g, m_mat, ws_mat, wy_mat, consts)


def _out1_kernel(y_ref, z_ref, x_ref, mod_ref, wg_ref, wo_ref, fnw_ref, o_ref, scr_ref):
    n_slab = D_MODEL // LANES
    groups_per_slab = LANES // SSM_GROUP
    pairs_per_phase = PAIRS_PER_TILE // OUT_PHASES
    rows_per_phase = ROW_TILE // OUT_PHASES

    def relayout(ph):
        for k in range(n_slab):
            for p in range(pairs_per_phase * ph, pairs_per_phase * (ph + 1)):
                for m2 in range(CHUNK // SUBLANES):
                    vals = [y_ref[groups_per_slab * k + gl, SUBLANES * p:SUBLANES * (p + 1),
                                  LANES * m2:LANES * (m2 + 1)] for gl in range(groups_per_slab)]
                    steps = _lane_block_transpose(vals)
                    for s2 in range(SUBLANES):
                        t_idx = SUBLANES * m2 + s2
                        scr_ref[k, pl.ds(SLOT_PITCH * SUBLANES * p + t_idx, SUBLANES,
                                         stride=SLOT_PITCH), :] = steps[s2]

    def compute(ph):
        chunks = range(2 * pairs_per_phase * ph, 2 * pairs_per_phase * (ph + 1))
        y = jnp.concatenate(
            [jnp.concatenate(
                [scr_ref[k, SLOT_PITCH * _chunk_slot(b, cc):SLOT_PITCH * _chunk_slot(b, cc) + CHUNK, :]
                 for k in range(n_slab)], axis=1)
             for b in range(BATCH) for cc in chunks], axis=0)
        g = (0.5 * y * (1.0 + lax.erf(y * (2.0 ** -0.5)))).astype(BF16)
        t = jnp.dot(g, wg_ref[...], preferred_element_type=F32)
        rows = slice(rows_per_phase * ph, rows_per_phase * (ph + 1))
        z = jnp.concatenate([z_ref[b, rows, :] for b in range(BATCH)], axis=0).astype(F32)
        r = (t[:, :D_MODEL] * jax.nn.sigmoid(t[:, D_MODEL:]) * (z * jax.nn.sigmoid(z))).astype(BF16)
        o = jnp.dot(r, wo_ref[...], preferred_element_type=F32)
        for b in range(BATCH):
            x2 = (x_ref[b, rows, :]
                  + mod_ref[b, 0, 2:3, :] * o[rows_per_phase * b:rows_per_phase * (b + 1)])
            ms = jnp.mean(x2 * x2, axis=-1, keepdims=True)
            o_ref[b, rows, :] = x2 * lax.rsqrt(ms + NORM_EPS) * fnw_ref[...]

    relayout(0)
    for ph in range(OUT_PHASES):
        if ph + 1 < OUT_PHASES:
            relayout(ph + 1)
        compute(ph)


def _out1(y_g, z, xc, mod, w_glu, w_out, final_norm_w):
    ctx_tiles = CTX_LEN // ROW_TILE
    tile_rows = PAIRS_PER_TILE * SUBLANES
    lat_spec = pl.BlockSpec((BATCH, ROW_TILE, D_MODEL), lambda i: (0, i, 0))
    all_spec = pl.BlockSpec((BATCH, ROW_TILE, D_MODEL), lambda i: (0, i + ctx_tiles, 0))
    return pl.pallas_call(
        _out1_kernel,
        grid=(SEQ // ROW_TILE,),
        in_specs=[
            pl.BlockSpec((SSM_GROUPS, tile_rows, CHUNK_W), lambda i: (0, i, 0)),
            all_spec, all_spec,
            pl.BlockSpec((BATCH, 1, SUBLANES, D_MODEL), lambda i: (0, 1, 0, 0)),
            pl.BlockSpec((D_MODEL, 2 * D_MODEL), lambda i: (0, 0)),
            pl.BlockSpec((D_MODEL, D_MODEL), lambda i: (0, 0)),
            pl.BlockSpec((1, D_MODEL), lambda i: (0, 0)),
        ],
        out_specs=lat_spec,
        out_shape=jax.ShapeDtypeStruct((BATCH, SEQ, D_MODEL), F32),
        scratch_shapes=[pltpu.VMEM((D_MODEL // LANES, BATCH * CHUNKS_PER_TILE * SLOT_PITCH, LANES),
                                   F32)],
        compiler_params=_params(("arbitrary",), 56),
        name="ssm_output",
    )(y_g, z, xc, mod, w_glu, w_out, final_norm_w.reshape(1, D_MODEL))


def _slot_order(t, lead):
    n_m = KV_WIDTH // LANES
    gq = N_HEADS // N_KV_HEADS
    shape = t.shape
    t = t.reshape(shape[:lead] + (n_m, 2, gq) + shape[lead + 1:])
    perm = tuple(range(lead)) + (lead, lead + 2, lead + 1) + tuple(range(lead + 3, t.ndim))
    return jnp.transpose(t, perm)


def _rope_order(t):
    shape = t.shape
    t = t.reshape(shape[:-1] + (2, 2, ROPE_FREQS))
    return jnp.swapaxes(t, -3, -2).reshape(shape)


def _attn_weights(w_in, w_out, sink):
    wq = w_in[:, :ATTN_WIDTH].reshape(D_MODEL, N_HEADS, HEAD_DIM)
    wq = _slot_order(_rope_order(wq), 1).reshape(D_MODEL, ATTN_WIDTH)
    wk = w_in[:, ATTN_WIDTH:ATTN_WIDTH + KV_WIDTH].reshape(D_MODEL, N_KV_HEADS, HEAD_DIM)
    wk = _rope_order(wk).reshape(D_MODEL, KV_WIDTH)
    wv = w_in[:, ATTN_WIDTH + KV_WIDTH:ATTN_WIDTH + 2 * KV_WIDTH]
    wz = w_in[:, ATTN_WIDTH + 2 * KV_WIDTH:].reshape(D_MODEL, N_HEADS, HEAD_DIM)
    wz = _slot_order(wz, 1).reshape(D_MODEL, ATTN_WIDTH)
    w_in_p = jnp.concatenate([wq, wk, wv, wz], axis=1).astype(BF16)
    wo = _slot_order(w_out.reshape(N_HEADS, HEAD_DIM, D_MODEL), 0).reshape(ATTN_WIDTH, D_MODEL)
    sink_p = _slot_order(sink.astype(F32).reshape(N_HEADS), 0).reshape(2 * N_SLOTS)
    sink_tab = jnp.broadcast_to((sink_p * LOG2E)[:, None], (2 * N_SLOTS, LANES))
    return w_in_p, wo.astype(BF16), sink_tab


def _rope_tables():
    inv = ROPE_BASE ** (-np.arange(ROPE_FREQS, dtype=np.float64) / ROPE_FREQS)
    pos = np.arange(SEQ)
    row = (pos // GRID_W)[:, None] * inv
    col = (pos % GRID_W)[:, None] * inv
    w = np.arange(LANES) % HEAD_DIM
    half, axis, f = w // 32, (w % 32) // 16, w % 16
    ang = np.where((axis == 0)[None, :], row[:, f], col[:, f])
    sign = np.where(half == 0, -1.0, 1.0)[None, :]
    cos = np.concatenate([np.ones((CTX_LEN, LANES)), np.cos(ang)], axis=0)
    sin = np.concatenate([np.zeros((CTX_LEN, LANES)), np.sin(ang) * sign], axis=0)
    return jnp.asarray(cos, F32), jnp.asarray(sin, F32)


def _s5_operators(lam_re, lam_im, log_dt, b_re, b_im, c_re, c_im, d_skip):
    t_len = CHUNK
    n_pow = 2 * t_len + 1
    lr, li = lam_re.astype(F32), lam_im.astype(F32)
    dt = jnp.exp(log_dt.astype(F32))[..., None]
    ks = jnp.arange(n_pow, dtype=F32)
    mag = jnp.exp((lr * dt)[..., None] * ks)
    ph = (li * dt)[..., None] * ks
    pr, pi = mag * jnp.cos(ph), mag * jnp.sin(ph)
    ar1, ai1 = pr[..., 1] - 1.0, pi[..., 1]
    den = lr * lr + li * li
    gr, gi = (ar1 * lr + ai1 * li) / den, (ai1 * lr - ar1 * li) / den
    br_, bi_ = b_re.astype(F32), b_im.astype(F32)
    bbr = gr[..., None] * br_ - gi[..., None] * bi_
    bbi = gr[..., None] * bi_ + gi[..., None] * br_
    pad_k = POW_ROWS - n_pow
    prt, pit = jnp.swapaxes(pr, 2, 3), jnp.swapaxes(pi, 2, 3)
    pw = jnp.pad(jnp.concatenate([prt, prt, pit, pit], axis=-1),
                 ((0, 0), (0, 0), (0, pad_k), (0, 0)))
    pwc = jnp.stack([jnp.concatenate([pr, pr], axis=2), jnp.concatenate([pi, pi], axis=2)], axis=2)
    pwc = jnp.pad(pwc, ((0, 0),) * 4 + ((0, pad_k),))
    brt, bit = jnp.swapaxes(bbr, 2, 3), jnp.swapaxes(bbi, 2, 3)
    bb = jnp.concatenate([brt, bit, -bit, brt, bit, brt, brt, -bit], axis=-1)
    crt = jnp.swapaxes(c_re.astype(F32), 2, 3)
    cit = jnp.swapaxes(c_im.astype(F32), 2, 3)
    cm = jnp.stack([jnp.concatenate([crt, -cit], axis=2),
                    jnp.concatenate([-cit, -crt], axis=2)], axis=2)
    dv = jnp.tile(d_skip.astype(F32).reshape(SSM_GROUPS, 1, SSM_GROUP), (1, 1, t_len))
    return _s5_operator_call(pw, pwc, bb, cm, dv)


def _operator_constants():
    t_len = CHUNK
    s_of_row = np.arange(CHUNK_W) // SSM_GROUP
    k_ar = np.arange(POW_ROWS)
    oh_f = (k_ar[None, :] == (t_len - 1 - s_of_row)[:, None]).astype(np.float32)
    oh_b = (k_ar[None, :] == s_of_row[:, None]).astype(np.float32)
    t_of_lane = np.arange(CHUNK_W) // SSM_GROUP
    expo = [t_of_lane + 1, t_len - t_of_lane, t_of_lane, t_len - 1 - t_of_lane]
    sel = np.stack([(k_ar[:, None] == e[None, :]) for e in expo]).astype(np.float32)
    h_of_lane = np.arange(CHUNK_W) % SSM_GROUP
    tl = (np.arange(SSM_GROUP)[:, None] == h_of_lane[None, :]).astype(np.float32)
    dmask = ((s_of_row[:, None] == t_of_lane[None, :])
             & ((np.arange(CHUNK_W) % SSM_GROUP)[:, None] == h_of_lane[None, :])).astype(np.float32)
    return oh_f, oh_b, sel, tl, dmask


def _s5_op_kernel(*refs):
    for gg in range(OP_GROUP_BATCH):
        _s5_op_group(gg, *refs)


def _s5_op_group(gg, pw_ref, pwc_ref, bb_ref, cm_ref, dv_ref, ohf_ref, ohb_ref, sel_ref, tl_ref,
                 dmask_ref, m_ref, ws_ref, wy_ref, cst_ref):
    t_len = CHUNK

    def mm(a, b):
        return jnp.dot(a, b, preferred_element_type=F32, precision=lax.Precision.HIGHEST)

    def split(a):
        hi = a.astype(BF16)
        return hi, (a - hi.astype(F32)).astype(BF16)

    def pick_rows(onehot, table):
        hi, lo = split(table)
        return (jnp.dot(onehot, hi, preferred_element_type=F32)
                + jnp.dot(onehot, lo, preferred_element_type=F32))

    def spread_lanes(table, onehot):
        hi, lo = split(table)
        return (jnp.dot(hi, onehot, preferred_element_type=F32)
                + jnp.dot(lo, onehot, preferred_element_type=F32))

    main, swapped = [], []
    for d, oh_ref in ((0, ohf_ref), (1, ohb_ref)):
        pp = pick_rows(oh_ref[...], pw_ref[d, gg])
        p_re, p_im = pp[:, :STATE_W], pp[:, STATE_W:]
        b0, b1, b2, b3 = [jnp.concatenate([bb_ref[d, gg, :, STATE_W * i:STATE_W * (i + 1)]] * t_len,
                                          axis=0) for i in range(4)]
        main.append(p_re * b0 + p_im * b1)
        swapped.append(p_re * b2 + p_im * b3)
    ws_ref[gg] = jnp.concatenate(main + swapped, axis=1).astype(BF16)

    c_tiled = [[spread_lanes(cm_ref[d, gg, i], tl_ref[...]) for i in range(2)] for d in range(2)]

    def block(d, pat):
        sel = sel_ref[pat]
        return (spread_lanes(pwc_ref[d, gg, 0], sel) * c_tiled[d][0]
                + spread_lanes(pwc_ref[d, gg, 1], sel) * c_tiled[d][1])

    wy_ref[gg] = jnp.concatenate([block(0, 0), block(1, 1)], axis=0).astype(BF16)

    kt_f = mm(bb_ref[0, gg, :, :STATE_W], block(0, 2))
    kt_b = mm(bb_ref[1, gg, :, :STATE_W], block(1, 3))
    lane = lax.broadcasted_iota(jnp.int32, (SSM_GROUP, CHUNK_W), 1)
    skip = dv_ref[gg]
    for s in range(t_len):
        fwd = kt_f if s == 0 else pltpu.roll(kt_f, SSM_GROUP * s, 1)
        back = t_len - 1 - s
        bwd = kt_b if back == 0 else pltpu.roll(kt_b, CHUNK_W - SSM_GROUP * back, 1)
        rows = (jnp.where(lane >= SSM_GROUP * s, fwd, 0.0)
                + jnp.where(lane < SSM_GROUP * (s + 1), bwd, 0.0)
                + dmask_ref[SSM_GROUP * s:SSM_GROUP * (s + 1), :] * skip)
        m_ref[gg, SSM_GROUP * s:SSM_GROUP * (s + 1), :] = rows.astype(BF16)

    even = lax.broadcasted_iota(jnp.int32, (SUBLANES, STATE_W), 0) % 2 == 0
    sign = jnp.where(lax.broadcasted_iota(jnp.int32, (1, STATE_W), 1) < SSM_STATE, -1.0, 1.0)

    def w12(d, k):
        row = pw_ref[d, gg, k:k + 1, :]
        return row[:, :STATE_W], row[:, STATE_W:] * sign

    zero = (jnp.zeros((1, STATE_W), F32),) * 2
    pairs = [(w12(0, t_len), w12(0, 2 * t_len)), (zero, w12(0, t_len)),
             (w12(1, 2 * t_len), w12(1, t_len)), (w12(1, t_len), zero)]
    idx = 0
    for top, bot in pairs:
        for part in range(2):
            cst_ref[gg, idx] = jnp.where(even, jnp.broadcast_to(top[part], (SUBLANES, STATE_W)),
                                         jnp.broadcast_to(bot[part], (SUBLANES, STATE_W)))
            idx += 1


def _s5_operator_call(pw, pwc, bb, cm, dv):
    oh_f, oh_b, sel, tl, dmask = _operator_constants()
    consts = [jnp.asarray(oh_f, BF16), jnp.asarray(oh_b, BF16), jnp.asarray(sel, BF16),
              jnp.asarray(tl, BF16), jnp.asarray(dmask)]

    def per_group(*tail):
        n = len(tail)
        return pl.BlockSpec((2, OP_GROUP_BATCH) + tail, lambda g: (0, g) + (0,) * n)

    def whole(a):
        return pl.BlockSpec(a.shape, lambda g: (0,) * a.ndim)

    def out(*tail):
        return pl.BlockSpec((OP_GROUP_BATCH,) + tail, lambda g: (g,) + (0,) * len(tail))

    return pl.pallas_call(
        _s5_op_kernel,
        grid=(SSM_GROUPS // OP_GROUP_BATCH,),
        in_specs=[per_group(POW_ROWS, 2 * STATE_W), per_group(2, STATE_W, POW_ROWS),
                  per_group(SSM_GROUP, 4 * STATE_W), per_group(2, STATE_W, SSM_GROUP),
                  pl.BlockSpec((OP_GROUP_BATCH, 1, CHUNK_W), lambda g: (g, 0, 0))]
                 + [whole(a) for a in consts],
        out_specs=[out(CHUNK_W, CHUNK_W), out(CHUNK_W, 4 * STATE_W), out(2 * STATE_W, CHUNK_W),
                   out(8, SUBLANES, STATE_W)],
        out_shape=[jax.ShapeDtypeStruct((SSM_GROUPS, CHUNK_W, CHUNK_W), BF16),
                   jax.ShapeDtypeStruct((SSM_GROUPS, CHUNK_W, 4 * STATE_W), BF16),
                   jax.ShapeDtypeStruct((SSM_GROUPS, 2 * STATE_W, CHUNK_W), BF16),
                   jax.ShapeDtypeStruct((SSM_GROUPS, 8, SUBLANES, STATE_W), F32)],
        compiler_params=_params(("arbitrary",), 32),
        name="s5_operators",
    )(pw, pwc, bb, cm, dv, *consts)


def kernel(x, c, ctx, c_ctx, norm_w, w_ada, b_ada, attn_w_in, attn_sink, attn_w_out,
           ssm_w_in, ssm_lam_re, ssm_lam_im, ssm_log_dt, ssm_b_re, ssm_b_im, ssm_c_re, ssm_c_im,
           ssm_d, ssm_w_glu, ssm_w_out, final_norm_w):
    mod0, mod1 = _modulation(c, c_ctx, w_ada, b_ada)

    w_in0, w_out0, sink_tab = _attn_weights(attn_w_in[0], attn_w_out[0], attn_sink[0])
    cos_tab, sin_tab = _rope_tables()
    q, z0, kbd, vbd = _proj0(x, ctx, mod0, norm_w[0], cos_tab, sin_tab, w_in0)
    xc1 = _attention(q, z0, kbd, vbd, sink_tab, x, ctx, mod0, w_out0)

    u_g, z1 = _proj1(xc1, mod1, norm_w[1], ssm_w_in[0].astype(BF16))
    m_mat, ws_mat, wy_mat, consts = _s5_operators(
        ssm_lam_re[0], ssm_lam_im[0], ssm_log_dt[0], ssm_b_re[0], ssm_b_im[0],
        ssm_c_re[0], ssm_c_im[0], ssm_d[0])
    y_g = _s5_core(u_g, m_mat, ws_mat, wy_mat, consts)
    return _out1(y_g, z1, xc1, mod1, ssm_w_glu[0].astype(BF16), ssm_w_out[0].astype(BF16),
                 final_norm_w)
```

```python
import functools
import math

import jax
import jax.numpy as jnp
import numpy as np
from jax import lax
from jax.experimental import pallas as pl
from jax.experimental.pallas import tpu as pltpu

F32 = jnp.float32
BF16 = jnp.bfloat16

D_MODEL = 1024
BATCH = 4
SEQ = 4096
GRID_W = 64
CTX_LEN = 256
TOTAL = CTX_LEN + SEQ
HEAD_DIM = 64
N_HEADS = 16
N_KV_HEADS = 4
ATTN_WIDTH = N_HEADS * HEAD_DIM
KV_WIDTH = N_KV_HEADS * HEAD_DIM
BLOCK = 128
N_BLOCKS = TOTAL // BLOCK
N_CTX_BLOCKS = CTX_LEN // BLOCK
ROPE_BASE = 10000.0
ROPE_FREQS = HEAD_DIM // 4
SSM_GROUP = 16
SSM_GROUPS = D_MODEL // SSM_GROUP
SSM_STATE = 64
NORM_EPS = 1e-6
NEG_INF = -1e30

LANES = 128
SUBLANES = 8
N_SLOTS = ATTN_WIDTH // LANES
N_KV_PAIRS = KV_WIDTH // LANES
SLOTS_PER_M = N_SLOTS // N_KV_PAIRS
ATTN_QBLOCKS = 2
assert N_CTX_BLOCKS % ATTN_QBLOCKS == 0 and N_BLOCKS % ATTN_QBLOCKS == 0
UNIT_SLOTS = 2
LOG2E = math.log2(math.e)
Q_SCALE = HEAD_DIM ** -0.5 * LOG2E
ROW_TILE = 256
N_ROW_TILES = TOTAL // ROW_TILE
CHUNK = 16
N_CHUNKS = TOTAL // CHUNK
N_CTX_CHUNKS = CTX_LEN // CHUNK
CHUNK_W = CHUNK * SSM_GROUP
CHUNKS_PER_TILE = ROW_TILE // CHUNK
PAIRS_PER_TILE = CHUNKS_PER_TILE // 2
SLOT_PITCH = 24
OUT_PHASES = 4
assert SLOT_PITCH >= CHUNK and SLOT_PITCH % SUBLANES == 0
STATE_W = 2 * SSM_STATE
SCAN_ROWS = N_CHUNKS * BATCH
CTX_SCAN_ROWS = N_CTX_CHUNKS * BATCH
N_SCAN_BLOCKS = SCAN_ROWS // SUBLANES
N_CTX_SCAN_BLOCKS = CTX_SCAN_ROWS // SUBLANES
GROUP_BATCH = 4
OP_GROUP_BATCH = 4
POW_ROWS = 48

assert BATCH * 2 == SUBLANES


def _params(semantics, vmem_mb):
    return pltpu.CompilerParams(dimension_semantics=semantics,
                                vmem_limit_bytes=vmem_mb * 1024 * 1024)


def _mod_kernel(c_ref, w_ref, b_ref, o_ref):
    c = c_ref[...]
    a = c * jax.nn.sigmoid(c)
    o_ref[0] = jnp.dot(a, w_ref[0], preferred_element_type=F32,
                       precision=lax.Precision.HIGHEST) + b_ref[0]


def _modulation(c, c_ctx, w_ada, b_ada):
    depth = w_ada.shape[0]
    rows = jnp.zeros((SUBLANES, D_MODEL), F32).at[:BATCH].set(c).at[BATCH].set(c_ctx)
    n_col = 3
    out = pl.pallas_call(
        _mod_kernel,
        grid=(depth, n_col),
        in_specs=[
            pl.BlockSpec((SUBLANES, D_MODEL), lambda l, j: (0, 0)),
            pl.BlockSpec((1, D_MODEL, D_MODEL), lambda l, j: (l, 0, j)),
            pl.BlockSpec((1, 1, D_MODEL), lambda l, j: (l, 0, j)),
        ],
        out_specs=pl.BlockSpec((1, SUBLANES, D_MODEL), lambda l, j: (l, 0, j)),
        out_shape=jax.ShapeDtypeStruct((depth, SUBLANES, 3 * D_MODEL), F32),
        compiler_params=_params(("arbitrary", "arbitrary"), 32),
        name="adaln_modulation",
    )(rows, w_ada, b_ada.reshape(depth, 1, 3 * D_MODEL))
    tabs = []
    for l in range(depth):
        lat = out[l, :BATCH].reshape(BATCH, 3, D_MODEL)
        cx = jnp.broadcast_to(out[l, BATCH].reshape(1, 3, D_MODEL), (BATCH, 3, D_MODEL))
        tab = jnp.stack([cx, lat], axis=1)
        tabs.append(jnp.pad(tab, ((0, 0), (0, 0), (0, SUBLANES - 3), (0, 0))))
    return tabs


def _modulated_norm(xt, nw, mod_ref, b=0):
    ms = jnp.mean(xt * xt, axis=-1, keepdims=True)
    y = xt * lax.rsqrt(ms + NORM_EPS) * nw
    return y * (1.0 + mod_ref[b, 0, 1:2, :]) + mod_ref[b, 0, 0:1, :]


def _lane_block_transpose(vs):
    n = len(vs)
    width = LANES // n
    blk = lax.broadcasted_iota(jnp.int32, vs[0].shape, 1) // width
    x = list(vs)
    d = n // 2
    while d >= 1:
        clear = (blk & d) == 0
        y = list(x)
        for i in range(n):
            if i & d == 0:
                a, b = x[i], x[i + d]
                y[i] = jnp.where(clear, a, pltpu.roll(b, width * d, 1))
                y[i + d] = jnp.where(clear, pltpu.roll(a, LANES - width * d, 1), b)
        x = y
        d //= 2
    return x


def _chunk_slot(b, cc):
    return (cc // 2) * (2 * BATCH) + 2 * b + (cc % 2)


def _proj0_kernel(x_ref, c_ref, mod_ref, nw_ref, cos_ref, sin_ref, w_ref,
                  q_ref, z_ref, kbd_ref, vbd_ref):
    is_ctx = pl.program_id(0) == 0
    h = jnp.concatenate(
        [_modulated_norm(jnp.where(is_ctx, c_ref[b], x_ref[b]), nw_ref[...], mod_ref, b).astype(BF16)
         for b in range(BATCH)], axis=0)
    cos = cos_ref[...]
    sin = sin_ref[...]
    lane = lax.broadcasted_iota(jnp.int32, (ROW_TILE, LANES), 1)
    first_half = (lane % HEAD_DIM) < (HEAD_DIM // 2)
    low = lax.broadcasted_iota(jnp.int32, (BLOCK, LANES), 1) < HEAD_DIM

    def rope(t):
        partner = jnp.where(first_half, pltpu.roll(t, LANES - HEAD_DIM // 2, 1),
                            pltpu.roll(t, HEAD_DIM // 2, 1))
        return t * cos + partner * sin

    q = jnp.dot(h, w_ref[:, :ATTN_WIDTH], preferred_element_type=F32)
    k = jnp.dot(h, w_ref[:, ATTN_WIDTH:ATTN_WIDTH + KV_WIDTH], preferred_element_type=F32)
    v = jnp.dot(h, w_ref[:, ATTN_WIDTH + KV_WIDTH:ATTN_WIDTH + 2 * KV_WIDTH],
                preferred_element_type=F32)
    z = jnp.dot(h, w_ref[:, ATTN_WIDTH + 2 * KV_WIDTH:], preferred_element_type=F32)
    for b in range(BATCH):
        tile = slice(ROW_TILE * b, ROW_TILE * (b + 1))
        z_ref[b] = z[tile].astype(BF16)
        for j in range(N_SLOTS):
            m, gi = divmod(j, SLOTS_PER_M)
            qj = (rope(q[tile, LANES * j:LANES * (j + 1)]) * Q_SCALE).astype(BF16)
            for blk in range(ROW_TILE // BLOCK):
                q_ref[b, blk, m, BLOCK * gi:BLOCK * (gi + 1), :] = qj[BLOCK * blk:BLOCK * (blk + 1)]
        for m in range(N_KV_PAIRS):
            sl = slice(LANES * m, LANES * (m + 1))
            kr = rope(k[tile, sl])
            vm = v[tile, sl]
            for blk in range(ROW_TILE // BLOCK):
                rows = slice(BLOCK * blk, BLOCK * (blk + 1))
                kbd_ref[b, blk, m, :BLOCK, :] = jnp.where(low, kr[rows], 0.0).astype(BF16)
                kbd_ref[b, blk, m, BLOCK:, :] = jnp.where(low, 0.0, kr[rows]).astype(BF16)
                vbd_ref[b, blk, m, :BLOCK, :] = jnp.where(low, vm[rows], 0.0).astype(BF16)
                vbd_ref[b, blk, m, BLOCK:, :] = jnp.where(low, 0.0, vm[rows]).astype(BF16)


def _proj0(x, ctx, mod, norm_w, cos_tab, sin_tab, w_in):
    n_col = w_in.shape[1]
    blocks_per_tile = ROW_TILE // BLOCK
    kv_shape = jax.ShapeDtypeStruct((BATCH, N_BLOCKS, N_KV_PAIRS, 2 * BLOCK, LANES), BF16)
    kv_spec = pl.BlockSpec((BATCH, blocks_per_tile, N_KV_PAIRS, 2 * BLOCK, LANES),
                           lambda i: (0, i, 0, 0, 0))
    row_spec = pl.BlockSpec((BATCH, ROW_TILE, D_MODEL), lambda i: (0, i, 0))
    once = pl.Buffered(1)
    return pl.pallas_call(
        _proj0_kernel,
        grid=(N_ROW_TILES,),
        in_specs=[
            pl.BlockSpec((BATCH, ROW_TILE, D_MODEL), lambda i: (0, jnp.maximum(i - 1, 0), 0)),
            pl.BlockSpec((BATCH, ROW_TILE, D_MODEL), lambda i: (0, 0, 0), pipeline_mode=once),
            pl.BlockSpec((BATCH, 1, SUBLANES, D_MODEL), lambda i: (0, jnp.minimum(i, 1), 0, 0)),
            pl.BlockSpec((1, D_MODEL), lambda i: (0, 0)),
            pl.BlockSpec((ROW_TILE, LANES), lambda i: (i, 0)),
            pl.BlockSpec((ROW_TILE, LANES), lambda i: (i, 0)),
            pl.BlockSpec((D_MODEL, n_col), lambda i: (0, 0), pipeline_mode=once),
        ],
        out_specs=[
            pl.BlockSpec((BATCH, blocks_per_tile, N_KV_PAIRS, SLOTS_PER_M * BLOCK, LANES),
                         lambda i: (0, i, 0, 0, 0)),
            row_spec, kv_spec, kv_spec],
        out_shape=[
            jax.ShapeDtypeStruct((BATCH, N_BLOCKS, N_KV_PAIRS, SLOTS_PER_M * BLOCK, LANES), BF16),
            jax.ShapeDtypeStruct((BATCH, TOTAL, ATTN_WIDTH), BF16),
            kv_shape, kv_shape,
        ],
        compiler_params=_params(("arbitrary",), 56),
        name="attn_projection",
    )(x, ctx, mod, norm_w.reshape(1, D_MODEL), cos_tab, sin_tab, w_in)


def _attn_kernel(q_ref, z_ref, k0_ref, k1_ref, k2_ref, k3_ref, kx_ref,
                 v0_ref, v1_ref, v2_ref, v3_ref, vx_ref,
                 sink_ref, x_ref, c_ref, mod_ref, wo_ref, eye_ref, tri_ref, o_ref):
    step = pl.program_id(1)
    is_lat = step >= N_CTX_BLOCKS // ATTN_QBLOCKS
    n_first = ATTN_QBLOCKS * step - N_CTX_BLOCKS
    q_rows = UNIT_SLOTS * BLOCK
    blocked = tri_ref[2]
    bias = []
    for qb in range(ATTN_QBLOCKS):
        n = n_first + qb
        bias.append([jnp.where(jnp.logical_and(is_lat, n >= 1), tri_ref[0], blocked),
                     jnp.where(is_lat, tri_ref[3], blocked),
                     jnp.where(jnp.logical_and(is_lat, n <= SEQ // BLOCK - 2), tri_ref[1], blocked),
                     None, None])
    low = lax.broadcasted_iota(jnp.int32, (BLOCK, LANES), 1) < HEAD_DIM
    k_win = (k0_ref, k1_ref, k2_ref, k3_ref)
    v_win = (v0_ref, v1_ref, v2_ref, v3_ref)

    def kpiece(qb, p, m):
        if p < 3:
            return k_win[qb + p][0, 0, m], v_win[qb + p][0, 0, m]
        return kx_ref[0, p - 3, m], vx_ref[0, p - 3, m]

    n_piece = 3 + N_CTX_BLOCKS
    units = [(qb, m, h) for qb in range(ATTN_QBLOCKS) for m in range(N_KV_PAIRS)
             for h in range(SLOTS_PER_M // UNIT_SLOTS)]

    nt = (((1,), (1,)), ((), ()))

    def scores(qb, m, h):
        qu = q_ref[0, qb, m, q_rows * h:q_rows * (h + 1), :]
        qu_masked = jnp.concatenate([qu, eye_ref[...]], axis=1)
        s_list = []
        for p in range(n_piece):
            kbd, _ = kpiece(qb, p, m)
            if bias[qb][p] is None:
                s = lax.dot_general(qu, kbd, nt, preferred_element_type=F32)
            else:
                s = lax.dot_general(qu_masked, jnp.concatenate([kbd, bias[qb][p]], axis=1), nt,
                                    preferred_element_type=F32)
            s_list.append(s)
        return s_list

    def finish(qb, m, h, s_list, y):
        slot0 = SLOTS_PER_M * m + UNIT_SLOTS * h
        mx = s_list[0]
        for s in s_list[1:]:
            mx = jnp.maximum(mx, s)

        def sink_col(hs):
            return jnp.concatenate(
                [jnp.broadcast_to(sink_ref[2 * (slot0 + gi) + hs:2 * (slot0 + gi) + hs + 1, 0:1],
                                  (BLOCK, 1)) for gi in range(UNIT_SLOTS)], axis=0)

        sink_a, sink_b = sink_col(0), sink_col(1)
        m_a = jnp.maximum(jnp.max(mx[:, :BLOCK], axis=1, keepdims=True), sink_a)
        m_b = jnp.maximum(jnp.max(mx[:, BLOCK:], axis=1, keepdims=True), sink_b)
        mb_a = jnp.broadcast_to(m_a, (q_rows, BLOCK))
        mb_b = jnp.broadcast_to(m_b, (q_rows, BLOCK))
        acc = None
        esum_a = jnp.zeros((q_rows, BLOCK), F32)
        esum_b = jnp.zeros((q_rows, BLOCK), F32)
        for p in range(n_piece):
            e_a = jnp.exp2(s_list[p][:, :BLOCK] - mb_a)
            e_b = jnp.exp2(s_list[p][:, BLOCK:] - mb_b)
            esum_a = esum_a + e_a
            esum_b = esum_b + e_b
            pb = jnp.concatenate([e_a, e_b], axis=1).astype(BF16)
            part = jnp.dot(pb, kpiece(qb, p, m)[1], preferred_element_type=F32)
            acc = part if acc is None else acc + part
        inv_a = 1.0 / (jnp.sum(esum_a, axis=1, keepdims=True) + jnp.exp2(sink_a - m_a))
        inv_b = 1.0 / (jnp.sum(esum_b, axis=1, keepdims=True) + jnp.exp2(sink_b - m_b))
        outs = []
        for gi in range(UNIT_SLOTS):
            j = slot0 + gi
            rows = slice(BLOCK * gi, BLOCK * (gi + 1))
            inv = jnp.where(low, inv_a[rows], inv_b[rows])
            zj = z_ref[0, BLOCK * qb:BLOCK * (qb + 1), LANES * j:LANES * (j + 1)].astype(F32)
            outs.append((acc[rows] * inv * (zj * jax.nn.sigmoid(zj))).astype(BF16))
        g = jnp.concatenate(outs, axis=1)
        part = jnp.dot(g, wo_ref[LANES * slot0:LANES * (slot0 + UNIT_SLOTS), :],
                       preferred_element_type=F32)
        return part if y is None else y + part

    y = [None] * ATTN_QBLOCKS
    pending = scores(*units[0])
    for u, (qb, m, h) in enumerate(units):
        nxt = scores(*units[u + 1]) if u + 1 < len(units) else None
        y[qb] = finish(qb, m, h, pending, y[qb])
        pending = nxt
    resid = jnp.where(is_lat, x_ref[0], c_ref[0])
    gate = mod_ref[0, 0, 2:3, :]
    for qb in range(ATTN_QBLOCKS):
        rows = slice(BLOCK * qb, BLOCK * (qb + 1))
        o_ref[0, rows, :] = resid[rows] + gate * y[qb]


def _attention(q, z, kbd, vbd, sink_tab, x, ctx, mod, w_out):
    last = N_BLOCKS - 1
    n_m = KV_WIDTH // LANES
    nq = ATTN_QBLOCKS
    ctx_steps = N_CTX_BLOCKS // nq
    row_spec = pl.BlockSpec((1, nq * BLOCK, ATTN_WIDTH), lambda b, i: (b, i, 0))

    def kv_spec(off):
        return pl.BlockSpec((1, 1, n_m, 2 * BLOCK, LANES),
                            lambda b, i: (b, jnp.clip(nq * i + off, 0, last), 0, 0, 0))

    ctx_kv_spec = pl.BlockSpec((1, N_CTX_BLOCKS, n_m, 2 * BLOCK, LANES),
                               lambda b, i: (b, 0, 0, 0, 0))
    off = np.arange(BLOCK)
    eye = np.tile(np.eye(BLOCK, dtype=np.float32), (UNIT_SLOTS, 1))
    key_ge = np.where(off[:, None] >= off[None, :], 0.0, NEG_INF)
    key_le = np.where(off[:, None] <= off[None, :], 0.0, NEG_INF)
    tri = np.stack([np.tile(t, (2, 1)) for t in
                    (key_ge, key_le, np.full((BLOCK, BLOCK), NEG_INF), np.zeros((BLOCK, BLOCK)))])
    consts = [jnp.asarray(a, BF16) for a in (eye, tri)]
    return pl.pallas_call(
        _attn_kernel,
        grid=(BATCH, N_BLOCKS // nq),
        in_specs=[
            pl.BlockSpec((1, nq, n_m, SLOTS_PER_M * BLOCK, LANES), lambda b, i: (b, i, 0, 0, 0)),
            row_spec,
            kv_spec(-1), kv_spec(0), kv_spec(1), kv_spec(2), ctx_kv_spec,
            kv_spec(-1), kv_spec(0), kv_spec(1), kv_spec(2), ctx_kv_spec,
            pl.BlockSpec((2 * N_SLOTS, LANES), lambda b, i: (0, 0)),
            pl.BlockSpec((1, nq * BLOCK, D_MODEL),
                         lambda b, i: (b, jnp.maximum(i - ctx_steps, 0), 0)),
            pl.BlockSpec((1, nq * BLOCK, D_MODEL),
                         lambda b, i: (b, jnp.minimum(i, ctx_steps - 1), 0)),
            pl.BlockSpec((1, 1, SUBLANES, D_MODEL),
                         lambda b, i: (b, jnp.minimum(i // ctx_steps, 1), 0, 0)),
            pl.BlockSpec((ATTN_WIDTH, D_MODEL), lambda b, i: (0, 0)),
        ] + [pl.BlockSpec(a.shape, lambda b, i, nd=a.ndim: (0,) * nd) for a in consts],
        out_specs=pl.BlockSpec((1, nq * BLOCK, D_MODEL), lambda b, i: (b, i, 0)),
        out_shape=jax.ShapeDtypeStruct((BATCH, TOTAL, D_MODEL), F32),
        compiler_params=_params(("arbitrary", "arbitrary"), 48),
        name="window_attention",
    )(q, z, kbd, kbd, kbd, kbd, kbd, vbd, vbd, vbd, vbd, vbd, sink_tab, x, ctx, mod, w_out,
      *consts)


def _proj1_kernel(x_ref, mod_ref, nw_ref, w_ref, u_ref, z_ref, scr_ref):
    h = jnp.concatenate(
        [_modulated_norm(x_ref[b], nw_ref[...], mod_ref, b).astype(BF16) for b in range(BATCH)],
        axis=0)
    u = jnp.dot(h, w_ref[:, :D_MODEL], preferred_element_type=F32)
    z = jnp.dot(h, w_ref[:, D_MODEL:], preferred_element_type=F32)
    for b in range(BATCH):
        z_ref[b] = z[ROW_TILE * b:ROW_TILE * (b + 1)].astype(BF16)
    n_slab = D_MODEL // LANES
    for b in range(BATCH):
        for cc in range(CHUNKS_PER_TILE):
            r0 = ROW_TILE * b + CHUNK * cc
            q0 = SLOT_PITCH * _chunk_slot(b, cc)
            for k in range(n_slab):
                scr_ref[k, q0:q0 + CHUNK, :] = u[r0:r0 + CHUNK, LANES * k:LANES * (k + 1)]
    groups_per_slab = LANES // SSM_GROUP
    for k in range(n_slab):
        for pp in range(PAIRS_PER_TILE // 2):
            parts = []
            for p in (2 * pp, 2 * pp + 1):
                rows = [scr_ref[k, pl.ds(SLOT_PITCH * SUBLANES * p + s, SUBLANES,
                                         stride=SLOT_PITCH), :] for s in range(CHUNK)]
                parts.append([_lane_block_transpose(rows[SUBLANES * m2:SUBLANES * (m2 + 1)])
                              for m2 in range(CHUNK // SUBLANES)])
            for m2 in range(CHUNK // SUBLANES):
                for gl in range(groups_per_slab):
                    val = jnp.concatenate([parts[0][m2][gl], parts[1][m2][gl]], axis=0)
                    u_ref[groups_per_slab * k + gl, 2 * SUBLANES * pp:2 * SUBLANES * (pp + 1),
                          LANES * m2:LANES * (m2 + 1)] = val.astype(BF16)


def _proj1(xc, mod, norm_w, w_in):
    row_spec = pl.BlockSpec((BATCH, ROW_TILE, D_MODEL), lambda i: (0, i, 0))
    tile_rows = PAIRS_PER_TILE * SUBLANES
    return pl.pallas_call(
        _proj1_kernel,
        grid=(N_ROW_TILES,),
        in_specs=[
            row_spec,
            pl.BlockSpec((BATCH, 1, SUBLANES, D_MODEL), lambda i: (0, jnp.minimum(i, 1), 0, 0)),
            pl.BlockSpec((1, D_MODEL), lambda i: (0, 0)),
            pl.BlockSpec((D_MODEL, 2 * D_MODEL), lambda i: (0, 0)),
        ],
        out_specs=[pl.BlockSpec((SSM_GROUPS, tile_rows, CHUNK_W), lambda i: (0, i, 0)), row_spec],
        out_shape=[jax.ShapeDtypeStruct((SSM_GROUPS, SCAN_ROWS, CHUNK_W), BF16),
                   jax.ShapeDtypeStruct((BATCH, TOTAL, D_MODEL), BF16)],
        scratch_shapes=[pltpu.VMEM((D_MODEL // LANES, BATCH * CHUNKS_PER_TILE * SLOT_PITCH, LANES),
                                   F32)],
        compiler_params=_params(("arbitrary",), 56),
        name="ssm_projection",
    )(xc, mod, norm_w.reshape(1, D_MODEL), w_in)


def _s5_kernel(u_ref, m_ref, ws_ref, wy_ref, cst_ref, y_ref, s4_ref, xp_ref):
    for g in range(GROUP_BATCH):
        s4_ref[g] = jnp.dot(u_ref[g], ws_ref[g], preferred_element_type=F32)
    even = lax.broadcasted_iota(jnp.int32, (SUBLANES, STATE_W), 0) % 2 == 0
    down = 1
    up = SUBLANES - 1
    fwd = slice(0, STATE_W)
    bwd = slice(STATE_W, 2 * STATE_W)
    fwd_sw = slice(2 * STATE_W, 3 * STATE_W)
    bwd_sw = slice(3 * STATE_W, 4 * STATE_W)

    def step(j, carry):
        jb = jnp.where(j < N_CTX_SCAN_BLOCKS, N_CTX_SCAN_BLOCKS - 1 - j,
                       N_SCAN_BLOCKS - 1 + N_CTX_SCAN_BLOCKS - j)
        rf = pl.ds(pl.multiple_of(j * SUBLANES, SUBLANES), SUBLANES)
        rb = pl.ds(pl.multiple_of(jb * SUBLANES, SUBLANES), SUBLANES)
        new = []
        for g in range(GROUP_BATCH):
            cf, cfs, cb, cbs = carry[4 * g:4 * g + 4]
            p1f, p2f, q1f, q2f = cst_ref[g, 0], cst_ref[g, 1], cst_ref[g, 2], cst_ref[g, 3]
            p1b, p2b, q1b, q2b = cst_ref[g, 4], cst_ref[g, 5], cst_ref[g, 6], cst_ref[g, 7]
            zf = s4_ref[g, rf, fwd]
            zfs = s4_ref[g, rf, fwd_sw]
            rzf = pltpu.roll(zf, down, 0)
            rzfs = pltpu.roll(zfs, down, 0)
            xf = p1f * cf + p2f * cfs + (zf + q1f * rzf + q2f * rzfs)
            xfs = p1f * cfs - p2f * cf + (zfs + q1f * rzfs - q2f * rzf)
            xp_ref[g, rf, fwd] = jnp.where(even, cf, pltpu.roll(xf, down, 0))
            new += [jnp.where(even, pltpu.roll(xf, up, 0), xf),
                    jnp.where(even, pltpu.roll(xfs, up, 0), xfs)]
            zb = s4_ref[g, rb, bwd]
            zbs = s4_ref[g, rb, bwd_sw]
            rzb = pltpu.roll(zb, up, 0)
            rzbs = pltpu.roll(zbs, up, 0)
            xb = p1b * cb + p2b * cbs + (zb + q1b * rzb + q2b * rzbs)
            xbs = p1b * cbs - p2b * cb + (zbs + q1b * rzbs - q2b * rzb)
            xp_ref[g, rb, bwd] = jnp.where(even, pltpu.roll(xb, up, 0), cb)
            new += [jnp.where(even, xb, pltpu.roll(xb, down, 0)),
                    jnp.where(even, xbs, pltpu.roll(xbs, down, 0))]
        return tuple(new)

    zero = jnp.zeros((SUBLANES, STATE_W), F32)
    lax.fori_loop(0, N_SCAN_BLOCKS, step, (zero,) * (4 * GROUP_BATCH))
    for g in range(GROUP_BATCH):
        y_ref[g] = (
            jnp.dot(u_ref[g, CTX_SCAN_ROWS:, :], m_ref[g], preferred_element_type=F32)
            + jnp.dot(xp_ref[g, CTX_SCAN_ROWS:, :].astype(BF16), wy_ref[g],
                      preferred_element_type=F32))


def _s5_core(u_g, m_mat, ws_mat, wy_mat, consts):
    lat_rows = SCAN_ROWS - CTX_SCAN_ROWS

    def gspec(*tail):
        return pl.BlockSpec((GROUP_BATCH,) + tail, lambda i: (i,) + (0,) * len(tail))

    return pl.pallas_call(
        _s5_kernel,
        grid=(SSM_GROUPS // GROUP_BATCH,),
        in_specs=[
            gspec(SCAN_ROWS, CHUNK_W),
            gspec(CHUNK_W, CHUNK_W),
            gspec(CHUNK_W, 4 * STATE_W),
            gspec(2 * STATE_W, CHUNK_W),
            gspec(8, SUBLANES, STATE_W),
        ],
        out_specs=gspec(lat_rows, CHUNK_W),
        out_shape=jax.ShapeDtypeStruct((SSM_GROUPS, lat_rows, CHUNK_W), F32),
        scratch_shapes=[
            pltpu.VMEM((GROUP_BATCH, SCAN_ROWS, 4 * STATE_W), F32),
            pltpu.VMEM((GROUP_BATCH, SCAN_ROWS, 2 * STATE_W), F32),
        ],
        compiler_params=_params(("arbitrary",), 48),
        name="s5_scan",
    )(u_g, m_mat, ws_mat, wy_mat, consts)


def _out1_kernel(y_ref, z_ref, x_ref, mod_ref, wg_ref, wo_ref, fnw_ref, o_ref, scr_ref):
    n_slab = D_MODEL // LANES
    groups_per_slab = LANES // SSM_GROUP
    pairs_per_phase = PAIRS_PER_TILE // OUT_PHASES
    rows_per_phase = ROW_TILE // OUT_PHASES

    def relayout(ph):
        for k in range(n_slab):
            for p in range(pairs_per_phase * ph, pairs_per_phase * (ph + 1)):
                for m2 in range(CHUNK // SUBLANES):
                    vals = [y_ref[groups_per_slab * k + gl, SUBLANES * p:SUBLANES * (p + 1),
                                  LANES * m2:LANES * (m2 + 1)] for gl in range(groups_per_slab)]
                    steps = _lane_block_transpose(vals)
                    for s2 in range(SUBLANES):
                        t_idx = SUBLANES * m2 + s2
                        scr_ref[k, pl.ds(SLOT_PITCH * SUBLANES * p + t_idx, SUBLANES,
                                         stride=SLOT_PITCH), :] = steps[s2]

    def compute(ph):
        chunks = range(2 * pairs_per_phase * ph, 2 * pairs_per_phase * (ph + 1))
        y = jnp.concatenate(
            [jnp.concatenate(
                [scr_ref[k, SLOT_PITCH * _chunk_slot(b, cc):SLOT_PITCH * _chunk_slot(b, cc) + CHUNK, :]
                 for k in range(n_slab)], axis=1)
             for b in range(BATCH) for cc in chunks], axis=0)
        g = (0.5 * y * (1.0 + lax.erf(y * (2.0 ** -0.5)))).astype(BF16)
        t = jnp.dot(g, wg_ref[...], preferred_element_type=F32)
        rows = slice(rows_per_phase * ph, rows_per_phase * (ph + 1))
        z = jnp.concatenate([z_ref[b, rows, :] for b in range(BATCH)], axis=0).astype(F32)
        r = (t[:, :D_MODEL] * jax.nn.sigmoid(t[:, D_MODEL:]) * (z * jax.nn.sigmoid(z))).astype(BF16)
        o = jnp.dot(r, wo_ref[...], preferred_element_type=F32)
        for b in range(BATCH):
            x2 = (x_ref[b, rows, :]
                  + mod_ref[b, 0, 2:3, :] * o[rows_per_phase * b:rows_per_phase * (b + 1)])
            ms = jnp.mean(x2 * x2, axis=-1, keepdims=True)
            o_ref[b, rows, :] = x2 * lax.rsqrt(ms + NORM_EPS) * fnw_ref[...]

    relayout(0)
    for ph in range(OUT_PHASES):
        if ph + 1 < OUT_PHASES:
            relayout(ph + 1)
        compute(ph)


def _out1(y_g, z, xc, mod, w_glu, w_out, final_norm_w):
    ctx_tiles = CTX_LEN // ROW_TILE
    tile_rows = PAIRS_PER_TILE * SUBLANES
    lat_spec = pl.BlockSpec((BATCH, ROW_TILE, D_MODEL), lambda i: (0, i, 0))
    all_spec = pl.BlockSpec((BATCH, ROW_TILE, D_MODEL), lambda i: (0, i + ctx_tiles, 0))
    return pl.pallas_call(
        _out1_kernel,
        grid=(SEQ // ROW_TILE,),
        in_specs=[
            pl.BlockSpec((SSM_GROUPS, tile_rows, CHUNK_W), lambda i: (0, i, 0)),
            all_spec, all_spec,
            pl.BlockSpec((BATCH, 1, SUBLANES, D_MODEL), lambda i: (0, 1, 0, 0)),
            pl.BlockSpec((D_MODEL, 2 * D_MODEL), lambda i: (0, 0)),
            pl.BlockSpec((D_MODEL, D_MODEL), lambda i: (0, 0)),
            pl.BlockSpec((1, D_MODEL), lambda i: (0, 0)),
        ],
        out_specs=lat_spec,
        out_shape=jax.ShapeDtypeStruct((BATCH, SEQ, D_MODEL), F32),
        scratch_shapes=[pltpu.VMEM((D_MODEL // LANES, BATCH * CHUNKS_PER_TILE * SLOT_PITCH, LANES),
                                   F32)],
        compiler_params=_params(("arbitrary",), 56),
        name="ssm_output",
    )(y_g, z, xc, mod, w_glu, w_out, final_norm_w.reshape(1, D_MODEL))


def _slot_order(t, lead):
    n_m = KV_WIDTH // LANES
    gq = N_HEADS // N_KV_HEADS
    shape = t.shape
    t = t.reshape(shape[:lead] + (n_m, 2, gq) + shape[lead + 1:])
    perm = tuple(range(lead)) + (lead, lead + 2, lead + 1) + tuple(range(lead + 3, t.ndim))
    return jnp.transpose(t, perm)


def _rope_order(t):
    shape = t.shape
    t = t.reshape(shape[:-1] + (2, 2, ROPE_FREQS))
    return jnp.swapaxes(t, -3, -2).reshape(shape)


def _attn_weights(w_in, w_out, sink):
    wq = w_in[:, :ATTN_WIDTH].reshape(D_MODEL, N_HEADS, HEAD_DIM)
    wq = _slot_order(_rope_order(wq), 1).reshape(D_MODEL, ATTN_WIDTH)
    wk = w_in[:, ATTN_WIDTH:ATTN_WIDTH + KV_WIDTH].reshape(D_MODEL, N_KV_HEADS, HEAD_DIM)
    wk = _rope_order(wk).reshape(D_MODEL, KV_WIDTH)
    wv = w_in[:, ATTN_WIDTH + KV_WIDTH:ATTN_WIDTH + 2 * KV_WIDTH]
    wz = w_in[:, ATTN_WIDTH + 2 * KV_WIDTH:].reshape(D_MODEL, N_HEADS, HEAD_DIM)
    wz = _slot_order(wz, 1).reshape(D_MODEL, ATTN_WIDTH)
    w_in_p = jnp.concatenate([wq, wk, wv, wz], axis=1).astype(BF16)
    wo = _slot_order(w_out.reshape(N_HEADS, HEAD_DIM, D_MODEL), 0).reshape(ATTN_WIDTH, D_MODEL)
    sink_p = _slot_order(sink.astype(F32).reshape(N_HEADS), 0).reshape(2 * N_SLOTS)
    sink_tab = jnp.broadcast_to((sink_p * LOG2E)[:, None], (2 * N_SLOTS, LANES))
    return w_in_p, wo.astype(BF16), sink_tab


def _rope_tables():
    inv = ROPE_BASE ** (-np.arange(ROPE_FREQS, dtype=np.float64) / ROPE_FREQS)
    pos = np.arange(SEQ)
    row = (pos // GRID_W)[:, None] * inv
    col = (pos % GRID_W)[:, None] * inv
    w = np.arange(LANES) % HEAD_DIM
    half, axis, f = w // 32, (w % 32) // 16, w % 16
    ang = np.where((axis == 0)[None, :], row[:, f], col[:, f])
    sign = np.where(half == 0, -1.0, 1.0)[None, :]
    cos = np.concatenate([np.ones((CTX_LEN, LANES)), np.cos(ang)], axis=0)
    sin = np.concatenate([np.zeros((CTX_LEN, LANES)), np.sin(ang) * sign], axis=0)
    return jnp.asarray(cos, F32), jnp.asarray(sin, F32)


def _s5_operators(lam_re, lam_im, log_dt, b_re, b_im, c_re, c_im, d_skip):
    t_len = CHUNK
    n_pow = 2 * t_len + 1
    lr, li = lam_re.astype(F32), lam_im.astype(F32)
    dt = jnp.exp(log_dt.astype(F32))[..., None]
    mag = jnp.exp(lr * dt)
    sq = [(mag * jnp.cos(li * dt), mag * jnp.sin(li * dt))]
    while 2 ** len(sq) < n_pow:
        r, i = sq[-1]
        sq.append((r * r - i * i, 2.0 * r * i))
    ks = np.arange(n_pow)
    pr = jnp.ones(lr.shape + (n_pow,), F32)
    pi = jnp.zeros(lr.shape + (n_pow,), F32)
    for bit, (r, i) in enumerate(sq):
        on = jnp.asarray((ks >> bit) & 1 == 1)
        fr = jnp.where(on, r[..., None], 1.0)
        fi = jnp.where(on, i[..., None], 0.0)
        pr, pi = pr * fr - pi * fi, pr * fi + pi * fr
    ar1, ai1 = sq[0][0] - 1.0, sq[0][1]
    den = lr * lr + li * li
    gr, gi = (ar1 * lr + ai1 * li) / den, (ai1 * lr - ar1 * li) / den
    br_, bi_ = b_re.astype(F32), b_im.astype(F32)
    bbr = gr[..., None] * br_ - gi[..., None] * bi_
    bbi = gr[..., None] * bi_ + gi[..., None] * br_
    pad_k = POW_ROWS - n_pow
    prt, pit = jnp.swapaxes(pr, 2, 3), jnp.swapaxes(pi, 2, 3)
    pw = jnp.pad(jnp.concatenate([prt, prt, pit, pit], axis=-1),
                 ((0, 0), (0, 0), (0, pad_k), (0, 0)))
    pwc = jnp.stack([jnp.concatenate([pr, pr], axis=2), jnp.concatenate([pi, pi], axis=2)], axis=2)
    pwc = jnp.pad(pwc, ((0, 0),) * 4 + ((0, pad_k),))
    brt, bit = jnp.swapaxes(bbr, 2, 3), jnp.swapaxes(bbi, 2, 3)
    bb = jnp.concatenate([brt, bit, -bit, brt, bit, brt, brt, -bit], axis=-1)
    crt = jnp.swapaxes(c_re.astype(F32), 2, 3)
    cit = jnp.swapaxes(c_im.astype(F32), 2, 3)
    cm = jnp.stack([jnp.concatenate([crt, -cit], axis=2),
                    jnp.concatenate([-cit, -crt], axis=2)], axis=2)
    dv = jnp.tile(d_skip.astype(F32).reshape(SSM_GROUPS, 1, SSM_GROUP), (1, 1, t_len))
    return _s5_operator_call(pw, pwc, bb, cm, dv)


def _operator_constants():
    t_len = CHUNK
    s_of_row = np.arange(CHUNK_W) // SSM_GROUP
    k_ar = np.arange(POW_ROWS)
    oh_f = (k_ar[None, :] == (t_len - 1 - s_of_row)[:, None]).astype(np.float32)
    oh_b = (k_ar[None, :] == s_of_row[:, None]).astype(np.float32)
    t_of_lane = np.arange(CHUNK_W) // SSM_GROUP
    expo = [t_of_lane + 1, t_len - t_of_lane, t_of_lane, t_len - 1 - t_of_lane]
    sel = np.stack([(k_ar[:, None] == e[None, :]) for e in expo]).astype(np.float32)
    h_of_lane = np.arange(CHUNK_W) % SSM_GROUP
    tl = (np.arange(SSM_GROUP)[:, None] == h_of_lane[None, :]).astype(np.float32)
    dmask = ((s_of_row[:, None] == t_of_lane[None, :])
             & ((np.arange(CHUNK_W) % SSM_GROUP)[:, None] == h_of_lane[None, :])).astype(np.float32)
    return oh_f, oh_b, sel, tl, dmask


def _s5_op_kernel(*refs):
    for gg in range(OP_GROUP_BATCH):
        _s5_op_group(gg, *refs)


def _s5_op_group(gg, pw_ref, pwc_ref, bb_ref, cm_ref, dv_ref, ohf_ref, ohb_ref, sel_ref, tl_ref,
                 dmask_ref, m_ref, ws_ref, wy_ref, cst_ref):
    t_len = CHUNK

    def mm(a, b):
        return jnp.dot(a, b, preferred_element_type=F32, precision=lax.Precision.HIGHEST)

    def split(a):
        hi = a.astype(BF16)
        return hi, (a - hi.astype(F32)).astype(BF16)

    def pick_rows(onehot, table):
        hi, lo = split(table)
        return (jnp.dot(onehot, hi, preferred_element_type=F32)
                + jnp.dot(onehot, lo, preferred_element_type=F32))

    def spread_lanes(table, onehot):
        hi, lo = split(table)
        return (jnp.dot(hi, onehot, preferred_element_type=F32)
                + jnp.dot(lo, onehot, preferred_element_type=F32))

    main, swapped = [], []
    for d, oh_ref in ((0, ohf_ref), (1, ohb_ref)):
        pp = pick_rows(oh_ref[...], pw_ref[d, gg])
        p_re, p_im = pp[:, :STATE_W], pp[:, STATE_W:]
        b0, b1, b2, b3 = [jnp.concatenate([bb_ref[d, gg, :, STATE_W * i:STATE_W * (i + 1)]] * t_len,
                                          axis=0) for i in range(4)]
        main.append(p_re * b0 + p_im * b1)
        swapped.append(p_re * b2 + p_im * b3)
    ws_ref[gg] = jnp.concatenate(main + swapped, axis=1).astype(BF16)

    c_tiled = [[spread_lanes(cm_ref[d, gg, i], tl_ref[...]) for i in range(2)] for d in range(2)]

    def block(d, pat):
        sel = sel_ref[pat]
        return (spread_lanes(pwc_ref[d, gg, 0], sel) * c_tiled[d][0]
                + spread_lanes(pwc_ref[d, gg, 1], sel) * c_tiled[d][1])

    wy_ref[gg] = jnp.concatenate([block(0, 0), block(1, 1)], axis=0).astype(BF16)

    kt_f = mm(bb_ref[0, gg, :, :STATE_W], block(0, 2))
    kt_b = mm(bb_ref[1, gg, :, :STATE_W], block(1, 3))
    lane = lax.broadcasted_iota(jnp.int32, (SSM_GROUP, CHUNK_W), 1)
    skip = dv_ref[gg]
    for s in range(t_len):
        fwd = kt_f if s == 0 else pltpu.roll(kt_f, SSM_GROUP * s, 1)
        back = t_len - 1 - s
        bwd = kt_b if back == 0 else pltpu.roll(kt_b, CHUNK_W - SSM_GROUP * back, 1)
        rows = (jnp.where(lane >= SSM_GROUP * s, fwd, 0.0)
                + jnp.where(lane < SSM_GROUP * (s + 1), bwd, 0.0)
                + dmask_ref[SSM_GROUP * s:SSM_GROUP * (s + 1), :] * skip)
        m_ref[gg, SSM_GROUP * s:SSM_GROUP * (s + 1), :] = rows.astype(BF16)

    even = lax.broadcasted_iota(jnp.int32, (SUBLANES, STATE_W), 0) % 2 == 0
    sign = jnp.where(lax.broadcasted_iota(jnp.int32, (1, STATE_W), 1) < SSM_STATE, -1.0, 1.0)

    def w12(d, k):
        row = pw_ref[d, gg, k:k + 1, :]
        return row[:, :STATE_W], row[:, STATE_W:] * sign

    zero = (jnp.zeros((1, STATE_W), F32),) * 2
    pairs = [(w12(0, t_len), w12(0, 2 * t_len)), (zero, w12(0, t_len)),
             (w12(1, 2 * t_len), w12(1, t_len)), (w12(1, t_len), zero)]
    idx = 0
    for top, bot in pairs:
        for part in range(2):
            cst_ref[gg, idx] = jnp.where(even, jnp.broadcast_to(top[part], (SUBLANES, STATE_W)),
                                         jnp.broadcast_to(bot[part], (SUBLANES, STATE_W)))
            idx += 1


def _s5_operator_call(pw, pwc, bb, cm, dv):
    oh_f, oh_b, sel, tl, dmask = _operator_constants()
    consts = [jnp.asarray(oh_f, BF16), jnp.asarray(oh_b, BF16), jnp.asarray(sel, BF16),
              jnp.asarray(tl, BF16), jnp.asarray(dmask)]

    def per_group(*tail):
        n = len(tail)
        return pl.BlockSpec((2, OP_GROUP_BATCH) + tail, lambda g: (0, g) + (0,) * n)

    def whole(a):
        return pl.BlockSpec(a.shape, lambda g: (0,) * a.ndim)

    def out(*tail):
        return pl.BlockSpec((OP_GROUP_BATCH,) + tail, lambda g: (g,) + (0,) * len(tail))

    return pl.pallas_call(
        _s5_op_kernel,
        grid=(SSM_GROUPS // OP_GROUP_BATCH,),
        in_specs=[per_group(POW_ROWS, 2 * STATE_W), per_group(2, STATE_W, POW_ROWS),
                  per_group(SSM_GROUP, 4 * STATE_W), per_group(2, STATE_W, SSM_GROUP),
                  pl.BlockSpec((OP_GROUP_BATCH, 1, CHUNK_W), lambda g: (g, 0, 0))]
                 + [whole(a) for a in consts],
        out_specs=[out(CHUNK_W, CHUNK_W), out(CHUNK_W, 4 * STATE_W), out(2 * STATE_W, CHUNK_W),
                   out(8, SUBLANES, STATE_W)],
        out_shape=[jax.ShapeDtypeStruct((SSM_GROUPS, CHUNK_W, CHUNK_W), BF16),
                   jax.ShapeDtypeStruct((SSM_GROUPS, CHUNK_W, 4 * STATE_W), BF16),
                   jax.ShapeDtypeStruct((SSM_GROUPS, 2 * STATE_W, CHUNK_W), BF16),
                   jax.ShapeDtypeStruct((SSM_GROUPS, 8, SUBLANES, STATE_W), F32)],
        compiler_params=_params(("arbitrary",), 32),
        name="s5_operators",
    )(pw, pwc, bb, cm, dv, *consts)


def kernel(x, c, ctx, c_ctx, norm_w, w_ada, b_ada, attn_w_in, attn_sink, attn_w_out,
           ssm_w_in, ssm_lam_re, ssm_lam_im, ssm_log_dt, ssm_b_re, ssm_b_im, ssm_c_re, ssm_c_im,
           ssm_d, ssm_w_glu, ssm_w_out, final_norm_w):
    mod0, mod1 = _modulation(c, c_ctx, w_ada, b_ada)

    w_in0, w_out0, sink_tab = _attn_weights(attn_w_in[0], attn_w_out[0], attn_sink[0])
    cos_tab, sin_tab = _rope_tables()
    q, z0, kbd, vbd = _proj0(x, ctx, mod0, norm_w[0], cos_tab, sin_tab, w_in0)
    xc1 = _attention(q, z0, kbd, vbd, sink_tab, x, ctx, mod0, w_out0)

    u_g, z1 = _proj1(xc1, mod1, norm_w[1], ssm_w_in[0].astype(BF16))
    m_mat, ws_mat, wy_mat, consts = _s5_operators(
        ssm_lam_re[0], ssm_lam_im[0], ssm_log_dt[0], ssm_b_re[0], ssm_b_im[0],
        ssm_c_re[0], ssm_c_im[0], ssm_d[0])
    y_g = _s5_core(u_g, m_mat, ws_mat, wy_mat, consts)
    return _out1(y_g, z1, xc1, mod1, ssm_w_glu[0].astype(BF16), ssm_w_out[0].astype(BF16),
                 final_norm_w)
```

```python
import functools
import math

import jax
import jax.numpy as jnp
import numpy as np
from jax import lax
from jax.experimental import pallas as pl
from jax.experimental.pallas import tpu as pltpu

F32 = jnp.float32
BF16 = jnp.bfloat16

D_MODEL = 1024
BATCH = 4
SEQ = 4096
GRID_W = 64
CTX_LEN = 256
TOTAL = CTX_LEN + SEQ
HEAD_DIM = 64
N_HEADS = 16
N_KV_HEADS = 4
ATTN_WIDTH = N_HEADS * HEAD_DIM
KV_WIDTH = N_KV_HEADS * HEAD_DIM
BLOCK = 128
N_BLOCKS = TOTAL // BLOCK
N_CTX_BLOCKS = CTX_LEN // BLOCK
ROPE_BASE = 10000.0
ROPE_FREQS = HEAD_DIM // 4
SSM_GROUP = 16
SSM_GROUPS = D_MODEL // SSM_GROUP
SSM_STATE = 64
NORM_EPS = 1e-6
NEG_INF = -1e30

LANES = 128
SUBLANES = 8
N_SLOTS = ATTN_WIDTH // LANES
N_KV_PAIRS = KV_WIDTH // LANES
SLOTS_PER_M = N_SLOTS // N_KV_PAIRS
ATTN_QBLOCKS = 2
assert N_CTX_BLOCKS % ATTN_QBLOCKS == 0 and N_BLOCKS % ATTN_QBLOCKS == 0
UNIT_SLOTS = 2
LOG2E = math.log2(math.e)
Q_SCALE = HEAD_DIM ** -0.5 * LOG2E
ROW_TILE = 256
N_ROW_TILES = TOTAL // ROW_TILE
CHUNK = 16
N_CHUNKS = TOTAL // CHUNK
N_CTX_CHUNKS = CTX_LEN // CHUNK
CHUNK_W = CHUNK * SSM_GROUP
CHUNKS_PER_TILE = ROW_TILE // CHUNK
PAIRS_PER_TILE = CHUNKS_PER_TILE // 2
SLOT_PITCH = 24
OUT_PHASES = 4
assert SLOT_PITCH >= CHUNK and SLOT_PITCH % SUBLANES == 0
STATE_W = 2 * SSM_STATE
SCAN_ROWS = N_CHUNKS * BATCH
CTX_SCAN_ROWS = N_CTX_CHUNKS * BATCH
N_SCAN_BLOCKS = SCAN_ROWS // SUBLANES
N_CTX_SCAN_BLOCKS = CTX_SCAN_ROWS // SUBLANES
GROUP_BATCH = 4
OP_GROUP_BATCH = 4
POW_ROWS = 48

assert BATCH * 2 == SUBLANES


def _params(semantics, vmem_mb):
    return pltpu.CompilerParams(dimension_semantics=semantics,
                                vmem_limit_bytes=vmem_mb * 1024 * 1024)


def _mod_kernel(c_ref, w_ref, b_ref, o_ref):
    c = c_ref[...]
    a = c * jax.nn.sigmoid(c)
    o_ref[0] = jnp.dot(a, w_ref[0], preferred_element_type=F32,
                       precision=lax.Precision.HIGHEST) + b_ref[0]


def _modulation(c, c_ctx, w_ada, b_ada):
    depth = w_ada.shape[0]
    rows = jnp.zeros((SUBLANES, D_MODEL), F32).at[:BATCH].set(c).at[BATCH].set(c_ctx)
    n_col = 3
    out = pl.pallas_call(
        _mod_kernel,
        grid=(depth, n_col),
        in_specs=[
            pl.BlockSpec((SUBLANES, D_MODEL), lambda l, j: (0, 0)),
            pl.BlockSpec((1, D_MODEL, D_MODEL), lambda l, j: (l, 0, j)),
            pl.BlockSpec((1, 1, D_MODEL), lambda l, j: (l, 0, j)),
        ],
        out_specs=pl.BlockSpec((1, SUBLANES, D_MODEL), lambda l, j: (l, 0, j)),
        out_shape=jax.ShapeDtypeStruct((depth, SUBLANES, 3 * D_MODEL), F32),
        compiler_params=_params(("arbitrary", "arbitrary"), 32),
        name="adaln_modulation",
    )(rows, w_ada, b_ada.reshape(depth, 1, 3 * D_MODEL))
    tabs = []
    for l in range(depth):
        lat = out[l, :BATCH].reshape(BATCH, 3, D_MODEL)
        cx = jnp.broadcast_to(out[l, BATCH].reshape(1, 3, D_MODEL), (BATCH, 3, D_MODEL))
        tab = jnp.stack([cx, lat], axis=1)
        tabs.append(jnp.pad(tab, ((0, 0), (0, 0), (0, SUBLANES - 3), (0, 0))))
    return tabs


def _modulated_norm(xt, nw, mod_ref, b=0):
    ms = jnp.mean(xt * xt, axis=-1, keepdims=True)
    y = xt * lax.rsqrt(ms + NORM_EPS) * nw
    return y * (1.0 + mod_ref[b, 0, 1:2, :]) + mod_ref[b, 0, 0:1, :]


def _lane_block_transpose(vs):
    n = len(vs)
    width = LANES // n
    blk = lax.broadcasted_iota(jnp.int32, vs[0].shape, 1) // width
    x = list(vs)
    d = n // 2
    while d >= 1:
        clear = (blk & d) == 0
        y = list(x)
        for i in range(n):
            if i & d == 0:
                a, b = x[i], x[i + d]
                y[i] = jnp.where(clear, a, pltpu.roll(b, width * d, 1))
                y[i + d] = jnp.where(clear, pltpu.roll(a, LANES - width * d, 1), b)
        x = y
        d //= 2
    return x


def _chunk_slot(b, cc):
    return (cc // 2) * (2 * BATCH) + 2 * b + (cc % 2)


def _proj0_kernel(x_ref, c_ref, mod_ref, nw_ref, cos_ref, sin_ref, w_ref,
                  q_ref, z_ref, kbd_ref, vbd_ref):
    is_ctx = pl.program_id(0) == 0
    h = jnp.concatenate(
        [_modulated_norm(jnp.where(is_ctx, c_ref[b], x_ref[b]), nw_ref[...], mod_ref, b).astype(BF16)
         for b in range(BATCH)], axis=0)
    cos = cos_ref[...]
    sin = sin_ref[...]
    lane = lax.broadcasted_iota(jnp.int32, (ROW_TILE, LANES), 1)
    first_half = (lane % HEAD_DIM) < (HEAD_DIM // 2)
    low = lax.broadcasted_iota(jnp.int32, (BLOCK, LANES), 1) < HEAD_DIM

    def rope(t):
        partner = jnp.where(first_half, pltpu.roll(t, LANES - HEAD_DIM // 2, 1),
                            pltpu.roll(t, HEAD_DIM // 2, 1))
        return t * cos + partner * sin

    q = jnp.dot(h, w_ref[:, :ATTN_WIDTH], preferred_element_type=F32)
    k = jnp.dot(h, w_ref[:, ATTN_WIDTH:ATTN_WIDTH + KV_WIDTH], preferred_element_type=F32)
    v = jnp.dot(h, w_ref[:, ATTN_WIDTH + KV_WIDTH:ATTN_WIDTH + 2 * KV_WIDTH],
                preferred_element_type=F32)
    z = jnp.dot(h, w_ref[:, ATTN_WIDTH + 2 * KV_WIDTH:], preferred_element_type=F32)
    for b in range(BATCH):
        tile = slice(ROW_TILE * b, ROW_TILE * (b + 1))
        z_ref[b] = z[tile].astype(BF16)
        for j in range(N_SLOTS):
            m, gi = divmod(j, SLOTS_PER_M)
            qj = (rope(q[tile, LANES * j:LANES * (j + 1)]) * Q_SCALE).astype(BF16)
            for blk in range(ROW_TILE // BLOCK):
                q_ref[b, blk, m, BLOCK * gi:BLOCK * (gi + 1), :] = qj[BLOCK * blk:BLOCK * (blk + 1)]
        for m in range(N_KV_PAIRS):
            sl = slice(LANES * m, LANES * (m + 1))
            kr = rope(k[tile, sl])
            vm = v[tile, sl]
            for blk in range(ROW_TILE // BLOCK):
                rows = slice(BLOCK * blk, BLOCK * (blk + 1))
                kbd_ref[b, blk, m, :BLOCK, :] = jnp.where(low, kr[rows], 0.0).astype(BF16)
                kbd_ref[b, blk, m, BLOCK:, :] = jnp.where(low, 0.0, kr[rows]).astype(BF16)
                vbd_ref[b, blk, m, :BLOCK, :] = jnp.where(low, vm[rows], 0.0).astype(BF16)
                vbd_ref[b, blk, m, BLOCK:, :] = jnp.where(low, 0.0, vm[rows]).astype(BF16)


def _proj0(x, ctx, mod, norm_w, cos_tab, sin_tab, w_in):
    n_col = w_in.shape[1]
    blocks_per_tile = ROW_TILE // BLOCK
    kv_shape = jax.ShapeDtypeStruct((BATCH, N_BLOCKS, N_KV_PAIRS, 2 * BLOCK, LANES), BF16)
    kv_spec = pl.BlockSpec((BATCH, blocks_per_tile, N_KV_PAIRS, 2 * BLOCK, LANES),
                           lambda i: (0, i, 0, 0, 0))
    row_spec = pl.BlockSpec((BATCH, ROW_TILE, D_MODEL), lambda i: (0, i, 0))
    once = pl.Buffered(1)
    return pl.pallas_call(
        _proj0_kernel,
        grid=(N_ROW_TILES,),
        in_specs=[
            pl.BlockSpec((BATCH, ROW_TILE, D_MODEL), lambda i: (0, jnp.maximum(i - 1, 0), 0)),
            pl.BlockSpec((BATCH, ROW_TILE, D_MODEL), lambda i: (0, 0, 0), pipeline_mode=once),
            pl.BlockSpec((BATCH, 1, SUBLANES, D_MODEL), lambda i: (0, jnp.minimum(i, 1), 0, 0)),
            pl.BlockSpec((1, D_MODEL), lambda i: (0, 0)),
            pl.BlockSpec((ROW_TILE, LANES), lambda i: (i, 0)),
            pl.BlockSpec((ROW_TILE, LANES), lambda i: (i, 0)),
            pl.BlockSpec((D_MODEL, n_col), lambda i: (0, 0), pipeline_mode=once),
        ],
        out_specs=[
            pl.BlockSpec((BATCH, blocks_per_tile, N_KV_PAIRS, SLOTS_PER_M * BLOCK, LANES),
                         lambda i: (0, i, 0, 0, 0)),
            row_spec, kv_spec, kv_spec],
        out_shape=[
            jax.ShapeDtypeStruct((BATCH, N_BLOCKS, N_KV_PAIRS, SLOTS_PER_M * BLOCK, LANES), BF16),
            jax.ShapeDtypeStruct((BATCH, TOTAL, ATTN_WIDTH), BF16),
            kv_shape, kv_shape,
        ],
        compiler_params=_params(("arbitrary",), 56),
        name="attn_projection",
    )(x, ctx, mod, norm_w.reshape(1, D_MODEL), cos_tab, sin_tab, w_in)


def _attn_kernel(q_ref, z_ref, k0_ref, k1_ref, k2_ref, k3_ref, kx_ref,
                 v0_ref, v1_ref, v2_ref, v3_ref, vx_ref,
                 sink_ref, x_ref, c_ref, mod_ref, wo_ref, eye_ref, tri_ref, o_ref):
    step = pl.program_id(1)
    is_lat = step >= N_CTX_BLOCKS // ATTN_QBLOCKS
    n_first = ATTN_QBLOCKS * step - N_CTX_BLOCKS
    q_rows = UNIT_SLOTS * BLOCK
    blocked = tri_ref[2]
    bias = []
    for qb in range(ATTN_QBLOCKS):
        n = n_first + qb
        bias.append([jnp.where(jnp.logical_and(is_lat, n >= 1), tri_ref[0], blocked),
                     jnp.where(is_lat, tri_ref[3], blocked),
                     jnp.where(jnp.logical_and(is_lat, n <= SEQ // BLOCK - 2), tri_ref[1], blocked),
                     None, None])
    low = lax.broadcasted_iota(jnp.int32, (BLOCK, LANES), 1) < HEAD_DIM
    k_win = (k0_ref, k1_ref, k2_ref, k3_ref)
    v_win = (v0_ref, v1_ref, v2_ref, v3_ref)

    def kpiece(qb, p, m):
        if p < 3:
            return k_win[qb + p][0, 0, m], v_win[qb + p][0, 0, m]
        return kx_ref[0, p - 3, m], vx_ref[0, p - 3, m]

    n_piece = 3 + N_CTX_BLOCKS
    units = [(qb, m, h) for qb in range(ATTN_QBLOCKS) for m in range(N_KV_PAIRS)
             for h in range(SLOTS_PER_M // UNIT_SLOTS)]

    nt = (((1,), (1,)), ((), ()))

    def scores(qb, m, h):
        qu = q_ref[0, qb, m, q_rows * h:q_rows * (h + 1), :]
        qu_masked = jnp.concatenate([qu, eye_ref[...]], axis=1)
        s_list = []
        for p in range(n_piece):
            kbd, _ = kpiece(qb, p, m)
            if bias[qb][p] is None:
                s = lax.dot_general(qu, kbd, nt, preferred_element_type=F32)
            else:
                s = lax.dot_general(qu_masked, jnp.concatenate([kbd, bias[qb][p]], axis=1), nt,
                                    preferred_element_type=F32)
            s_list.append(s)
        return s_list

    def finish(qb, m, h, s_list, y):
        slot0 = SLOTS_PER_M * m + UNIT_SLOTS * h
        mx = s_list[0]
        for s in s_list[1:]:
            mx = jnp.maximum(mx, s)

        def sink_col(hs):
            return jnp.concatenate(
                [jnp.broadcast_to(sink_ref[2 * (slot0 + gi) + hs:2 * (slot0 + gi) + hs + 1, 0:1],
                                  (BLOCK, 1)) for gi in range(UNIT_SLOTS)], axis=0)

        sink_a, sink_b = sink_col(0), sink_col(1)
        m_a = jnp.maximum(jnp.max(mx[:, :BLOCK], axis=1, keepdims=True), sink_a)
        m_b = jnp.maximum(jnp.max(mx[:, BLOCK:], axis=1, keepdims=True), sink_b)
        mb_a = jnp.broadcast_to(m_a, (q_rows, BLOCK))
        mb_b = jnp.broadcast_to(m_b, (q_rows, BLOCK))
        acc = None
        esum_a = jnp.zeros((q_rows, BLOCK), F32)
        esum_b = jnp.zeros((q_rows, BLOCK), F32)
        for p in range(n_piece):
            e_a = jnp.exp2(s_list[p][:, :BLOCK] - mb_a)
            e_b = jnp.exp2(s_list[p][:, BLOCK:] - mb_b)
            esum_a = esum_a + e_a
            esum_b = esum_b + e_b
            pb = jnp.concatenate([e_a, e_b], axis=1).astype(BF16)
            part = jnp.dot(pb, kpiece(qb, p, m)[1], preferred_element_type=F32)
            acc = part if acc is None else acc + part
        inv_a = 1.0 / (jnp.sum(esum_a, axis=1, keepdims=True) + jnp.exp2(sink_a - m_a))
        inv_b = 1.0 / (jnp.sum(esum_b, axis=1, keepdims=True) + jnp.exp2(sink_b - m_b))
        outs = []
        for gi in range(UNIT_SLOTS):
            j = slot0 + gi
            rows = slice(BLOCK * gi, BLOCK * (gi + 1))
            inv = jnp.where(low, inv_a[rows], inv_b[rows])
            zj = z_ref[0, BLOCK * qb:BLOCK * (qb + 1), LANES * j:LANES * (j + 1)].astype(F32)
            outs.append((acc[rows] * inv * (zj * jax.nn.sigmoid(zj))).astype(BF16))
        g = jnp.concatenate(outs, axis=1)
        part = jnp.dot(g, wo_ref[LANES * slot0:LANES * (slot0 + UNIT_SLOTS), :],
                       preferred_element_type=F32)
        return part if y is None else y + part

    y = [None] * ATTN_QBLOCKS
    pending = scores(*units[0])
    for u, (qb, m, h) in enumerate(units):
        nxt = scores(*units[u + 1]) if u + 1 < len(units) else None
        y[qb] = finish(qb, m, h, pending, y[qb])
        pending = nxt
    resid = jnp.where(is_lat, x_ref[0], c_ref[0])
    gate = mod_ref[0, 0, 2:3, :]
    for qb in range(ATTN_QBLOCKS):
        rows = slice(BLOCK * qb, BLOCK * (qb + 1))
        o_ref[0, rows, :] = resid[rows] + gate * y[qb]


def _attention(q, z, kbd, vbd, sink_tab, x, ctx, mod, w_out):
    last = N_BLOCKS - 1
    n_m = KV_WIDTH // LANES
    nq = ATTN_QBLOCKS
    ctx_steps = N_CTX_BLOCKS // nq
    row_spec = pl.BlockSpec((1, nq * BLOCK, ATTN_WIDTH), lambda b, i: (b, i, 0))

    def kv_spec(off):
        return pl.BlockSpec((1, 1, n_m, 2 * BLOCK, LANES),
                            lambda b, i: (b, jnp.clip(nq * i + off, 0, last), 0, 0, 0))

    ctx_kv_spec = pl.BlockSpec((1, N_CTX_BLOCKS, n_m, 2 * BLOCK, LANES),
                               lambda b, i: (b, 0, 0, 0, 0))
    off = np.arange(BLOCK)
    eye = np.tile(np.eye(BLOCK, dtype=np.float32), (UNIT_SLOTS, 1))
    key_ge = np.where(off[:, None] >= off[None, :], 0.0, NEG_INF)
    key_le = np.where(off[:, None] <= off[None, :], 0.0, NEG_INF)
    tri = np.stack([np.tile(t, (2, 1)) for t in
                    (key_ge, key_le, np.full((BLOCK, BLOCK), NEG_INF), np.zeros((BLOCK, BLOCK)))])
    consts = [jnp.asarray(a, BF16) for a in (eye, tri)]
    return pl.pallas_call(
        _attn_kernel,
        grid=(BATCH, N_BLOCKS // nq),
        in_specs=[
            pl.BlockSpec((1, nq, n_m, SLOTS_PER_M * BLOCK, LANES), lambda b, i: (b, i, 0, 0, 0)),
            row_spec,
            kv_spec(-1), kv_spec(0), kv_spec(1), kv_spec(2), ctx_kv_spec,
            kv_spec(-1), kv_spec(0), kv_spec(1), kv_spec(2), ctx_kv_spec,
            pl.BlockSpec((2 * N_SLOTS, LANES), lambda b, i: (0, 0)),
            pl.BlockSpec((1, nq * BLOCK, D_MODEL),
                         lambda b, i: (b, jnp.maximum(i - ctx_steps, 0), 0)),
            pl.BlockSpec((1, nq * BLOCK, D_MODEL),
                         lambda b, i: (b, jnp.minimum(i, ctx_steps - 1), 0)),
            pl.BlockSpec((1, 1, SUBLANES, D_MODEL),
                         lambda b, i: (b, jnp.minimum(i // ctx_steps, 1), 0, 0)),
            pl.BlockSpec((ATTN_WIDTH, D_MODEL), lambda b, i: (0, 0)),
        ] + [pl.BlockSpec(a.shape, lambda b, i, nd=a.ndim: (0,) * nd) for a in consts],
        out_specs=pl.BlockSpec((1, nq * BLOCK, D_MODEL), lambda b, i: (b, i, 0)),
        out_shape=jax.ShapeDtypeStruct((BATCH, TOTAL, D_MODEL), F32),
        compiler_params=_params(("arbitrary", "arbitrary"), 48),
        name="window_attention",
    )(q, z, kbd, kbd, kbd, kbd, kbd, vbd, vbd, vbd, vbd, vbd, sink_tab, x, ctx, mod, w_out,
      *consts)


def _proj1_kernel(x_ref, mod_ref, nw_ref, w_ref, u_ref, z_ref, scr_ref):
    h = jnp.concatenate(
        [_modulated_norm(x_ref[b], nw_ref[...], mod_ref, b).astype(BF16) for b in range(BATCH)],
        axis=0)
    u = jnp.dot(h, w_ref[:, :D_MODEL], preferred_element_type=F32)
    z = jnp.dot(h, w_ref[:, D_MODEL:], preferred_element_type=F32)
    for b in range(BATCH):
        z_ref[b] = z[ROW_TILE * b:ROW_TILE * (b + 1)].astype(BF16)
    n_slab = D_MODEL // LANES
    for b in range(BATCH):
        for cc in range(CHUNKS_PER_TILE):
            r0 = ROW_TILE * b + CHUNK * cc
            q0 = SLOT_PITCH * _chunk_slot(b, cc)
            for k in range(n_slab):
                scr_ref[k, q0:q0 + CHUNK, :] = u[r0:r0 + CHUNK, LANES * k:LANES * (k + 1)]
    groups_per_slab = LANES // SSM_GROUP
    for k in range(n_slab):
        for pp in range(PAIRS_PER_TILE // 2):
            parts = []
            for p in (2 * pp, 2 * pp + 1):
                rows = [scr_ref[k, pl.ds(SLOT_PITCH * SUBLANES * p + s, SUBLANES,
                                         stride=SLOT_PITCH), :] for s in range(CHUNK)]
                parts.append([_lane_block_transpose(rows[SUBLANES * m2:SUBLANES * (m2 + 1)])
                              for m2 in range(CHUNK // SUBLANES)])
            for m2 in range(CHUNK // SUBLANES):
                for gl in range(groups_per_slab):
                    val = jnp.concatenate([parts[0][m2][gl], parts[1][m2][gl]], axis=0)
                    u_ref[groups_per_slab * k + gl, 2 * SUBLANES * pp:2 * SUBLANES * (pp + 1),
                          LANES * m2:LANES * (m2 + 1)] = val.astype(BF16)


def _proj1(xc, mod, norm_w, w_in):
    row_spec = pl.BlockSpec((BATCH, ROW_TILE, D_MODEL), lambda i: (0, i, 0))
    tile_rows = PAIRS_PER_TILE * SUBLANES
    return pl.pallas_call(
        _proj1_kernel,
        grid=(N_ROW_TILES,),
        in_specs=[
            row_spec,
            pl.BlockSpec((BATCH, 1, SUBLANES, D_MODEL), lambda i: (0, jnp.minimum(i, 1), 0, 0)),
            pl.BlockSpec((1, D_MODEL), lambda i: (0, 0)),
            pl.BlockSpec((D_MODEL, 2 * D_MODEL), lambda i: (0, 0)),
        ],
        out_specs=[pl.BlockSpec((SSM_GROUPS, tile_rows, CHUNK_W), lambda i: (0, i, 0)), row_spec],
        out_shape=[jax.ShapeDtypeStruct((SSM_GROUPS, SCAN_ROWS, CHUNK_W), BF16),
                   jax.ShapeDtypeStruct((BATCH, TOTAL, D_MODEL), BF16)],
        scratch_shapes=[pltpu.VMEM((D_MODEL // LANES, BATCH * CHUNKS_PER_TILE * SLOT_PITCH, LANES),
                                   F32)],
        compiler_params=_params(("arbitrary",), 56),
        name="ssm_projection",
    )(xc, mod, norm_w.reshape(1, D_MODEL), w_in)


def _s5_kernel(u_ref, m_ref, ws_ref, wy_ref, cst_ref, y_ref, s4_ref, xp_ref):
    for g in range(GROUP_BATCH):
        s4_ref[g] = jnp.dot(u_ref[g], ws_ref[g], preferred_element_type=F32)
    even = lax.broadcasted_iota(jnp.int32, (SUBLANES, STATE_W), 0) % 2 == 0
    down = 1
    up = SUBLANES - 1
    fwd = slice(0, STATE_W)
    bwd = slice(STATE_W, 2 * STATE_W)
    fwd_sw = slice(2 * STATE_W, 3 * STATE_W)
    bwd_sw = slice(3 * STATE_W, 4 * STATE_W)

    def step(j, carry):
        jb = jnp.where(j < N_CTX_SCAN_BLOCKS, N_CTX_SCAN_BLOCKS - 1 - j,
                       N_SCAN_BLOCKS - 1 + N_CTX_SCAN_BLOCKS - j)
        rf = pl.ds(pl.multiple_of(j * SUBLANES, SUBLANES), SUBLANES)
        rb = pl.ds(pl.multiple_of(jb * SUBLANES, SUBLANES), SUBLANES)
        new = []
        for g in range(GROUP_BATCH):
            cf, cfs, cb, cbs = carry[4 * g:4 * g + 4]
            p1f, p2f, q1f, q2f = cst_ref[g, 0], cst_ref[g, 1], cst_ref[g, 2], cst_ref[g, 3]
            p1b, p2b, q1b, q2b = cst_ref[g, 4], cst_ref[g, 5], cst_ref[g, 6], cst_ref[g, 7]
            zf = s4_ref[g, rf, fwd]
            zfs = s4_ref[g, rf, fwd_sw]
            rzf = pltpu.roll(zf, down, 0)
            rzfs = pltpu.roll(zfs, down, 0)
            xf = p1f * cf + p2f * cfs + (zf + q1f * rzf + q2f * rzfs)
            xfs = p1f * cfs - p2f * cf + (zfs + q1f * rzfs - q2f * rzf)
            xp_ref[g, rf, fwd] = jnp.where(even, cf, pltpu.roll(xf, down, 0))
            new += [jnp.where(even, pltpu.roll(xf, up, 0), xf),
                    jnp.where(even, pltpu.roll(xfs, up, 0), xfs)]
            zb = s4_ref[g, rb, bwd]
            zbs = s4_ref[g, rb, bwd_sw]
            rzb = pltpu.roll(zb, up, 0)
            rzbs = pltpu.roll(zbs, up, 0)
            xb = p1b * cb + p2b * cbs + (zb + q1b * rzb + q2b * rzbs)
            xbs = p1b * cbs - p2b * cb + (zbs + q1b * rzbs - q2b * rzb)
            xp_ref[g, rb, bwd] = jnp.where(even, pltpu.roll(xb, up, 0), cb)
            new += [jnp.where(even, xb, pltpu.roll(xb, down, 0)),
                    jnp.where(even, xbs, pltpu.roll(xbs, down, 0))]
        return tuple(new)

    zero = jnp.zeros((SUBLANES, STATE_W), F32)
    lax.fori_loop(0, N_SCAN_BLOCKS, step, (zero,) * (4 * GROUP_BATCH))
    for g in range(GROUP_BATCH):
        y_ref[g] = (
            jnp.dot(u_ref[g, CTX_SCAN_ROWS:, :], m_ref[g], preferred_element_type=F32)
            + lax.dot_general(xp_ref[g, CTX_SCAN_ROWS:, :].astype(BF16), wy_ref[g],
                              (((1,), (1,)), ((), ())), preferred_element_type=F32))


def _s5_core(u_g, m_mat, ws_mat, wy_mat, consts):
    lat_rows = SCAN_ROWS - CTX_SCAN_ROWS

    def gspec(*tail):
        return pl.BlockSpec((GROUP_BATCH,) + tail, lambda i: (i,) + (0,) * len(tail))

    return pl.pallas_call(
        _s5_kernel,
        grid=(SSM_GROUPS // GROUP_BATCH,),
        in_specs=[
            gspec(SCAN_ROWS, CHUNK_W),
            gspec(CHUNK_W, CHUNK_W),
            gspec(CHUNK_W, 4 * STATE_W),
            gspec(CHUNK_W, 2 * STATE_W),
            gspec(8, SUBLANES, STATE_W),
        ],
        out_specs=gspec(lat_rows, CHUNK_W),
        out_shape=jax.ShapeDtypeStruct((SSM_GROUPS, lat_rows, CHUNK_W), F32),
        scratch_shapes=[
            pltpu.VMEM((GROUP_BATCH, SCAN_ROWS, 4 * STATE_W), F32),
            pltpu.VMEM((GROUP_BATCH, SCAN_ROWS, 2 * STATE_W), F32),
        ],
        compiler_params=_params(("arbitrary",), 48),
        name="s5_scan",
    )(u_g, m_mat, ws_mat, wy_mat, consts)


def _out1_kernel(y_ref, z_ref, x_ref, mod_ref, wg_ref, wo_ref, fnw_ref, o_ref, scr_ref):
    n_slab = D_MODEL // LANES
    groups_per_slab = LANES // SSM_GROUP
    pairs_per_phase = PAIRS_PER_TILE // OUT_PHASES
    rows_per_phase = ROW_TILE // OUT_PHASES

    def relayout(ph):
        for k in range(n_slab):
            for p in range(pairs_per_phase * ph, pairs_per_phase * (ph + 1)):
                for m2 in range(CHUNK // SUBLANES):
                    vals = [y_ref[groups_per_slab * k + gl, SUBLANES * p:SUBLANES * (p + 1),
                                  LANES * m2:LANES * (m2 + 1)] for gl in range(groups_per_slab)]
                    steps = _lane_block_transpose(vals)
                    for s2 in range(SUBLANES):
                        t_idx = SUBLANES * m2 + s2
                        scr_ref[k, pl.ds(SLOT_PITCH * SUBLANES * p + t_idx, SUBLANES,
                                         stride=SLOT_PITCH), :] = steps[s2]

    def compute(ph):
        chunks = range(2 * pairs_per_phase * ph, 2 * pairs_per_phase * (ph + 1))
        y = jnp.concatenate(
            [jnp.concatenate(
                [scr_ref[k, SLOT_PITCH * _chunk_slot(b, cc):SLOT_PITCH * _chunk_slot(b, cc) + CHUNK, :]
                 for k in range(n_slab)], axis=1)
             for b in range(BATCH) for cc in chunks], axis=0)
        g = (0.5 * y * (1.0 + lax.erf(y * (2.0 ** -0.5)))).astype(BF16)
        t = jnp.dot(g, wg_ref[...], preferred_element_type=F32)
        rows = slice(rows_per_phase * ph, rows_per_phase * (ph + 1))
        z = jnp.concatenate([z_ref[b, rows, :] for b in range(BATCH)], axis=0).astype(F32)
        r = (t[:, :D_MODEL] * jax.nn.sigmoid(t[:, D_MODEL:]) * (z * jax.nn.sigmoid(z))).astype(BF16)
        o = jnp.dot(r, wo_ref[...], preferred_element_type=F32)
        for b in range(BATCH):
            x2 = (x_ref[b, rows, :]
                  + mod_ref[b, 0, 2:3, :] * o[rows_per_phase * b:rows_per_phase * (b + 1)])
            ms = jnp.mean(x2 * x2, axis=-1, keepdims=True)
            o_ref[b, rows, :] = x2 * lax.rsqrt(ms + NORM_EPS) * fnw_ref[...]

    relayout(0)
    for ph in range(OUT_PHASES):
        if ph + 1 < OUT_PHASES:
            relayout(ph + 1)
        compute(ph)


def _out1(y_g, z, xc, mod, w_glu, w_out, final_norm_w):
    ctx_tiles = CTX_LEN // ROW_TILE
    tile_rows = PAIRS_PER_TILE * SUBLANES
    lat_spec = pl.BlockSpec((BATCH, ROW_TILE, D_MODEL), lambda i: (0, i, 0))
    all_spec = pl.BlockSpec((BATCH, ROW_TILE, D_MODEL), lambda i: (0, i + ctx_tiles, 0))
    return pl.pallas_call(
        _out1_kernel,
        grid=(SEQ // ROW_TILE,),
        in_specs=[
            pl.BlockSpec((SSM_GROUPS, tile_rows, CHUNK_W), lambda i: (0, i, 0)),
            all_spec, all_spec,
            pl.BlockSpec((BATCH, 1, SUBLANES, D_MODEL), lambda i: (0, 1, 0, 0)),
            pl.BlockSpec((D_MODEL, 2 * D_MODEL), lambda i: (0, 0)),
            pl.BlockSpec((D_MODEL, D_MODEL), lambda i: (0, 0)),
            pl.BlockSpec((1, D_MODEL), lambda i: (0, 0)),
        ],
        out_specs=lat_spec,
        out_shape=jax.ShapeDtypeStruct((BATCH, SEQ, D_MODEL), F32),
        scratch_shapes=[pltpu.VMEM((D_MODEL // LANES, BATCH * CHUNKS_PER_TILE * SLOT_PITCH, LANES),
                                   F32)],
        compiler_params=_params(("arbitrary",), 56),
        name="ssm_output",
    )(y_g, z, xc, mod, w_glu, w_out, final_norm_w.reshape(1, D_MODEL))


def _slot_order(t, lead):
    n_m = KV_WIDTH // LANES
    gq = N_HEADS // N_KV_HEADS
    shape = t.shape
    t = t.reshape(shape[:lead] + (n_m, 2, gq) + shape[lead + 1:])
    perm = tuple(range(lead)) + (lead, lead + 2, lead + 1) + tuple(range(lead + 3, t.ndim))
    return jnp.transpose(t, perm)


def _rope_order(t):
    shape = t.shape
    t = t.reshape(shape[:-1] + (2, 2, ROPE_FREQS))
    return jnp.swapaxes(t, -3, -2).reshape(shape)


def _attn_weights(w_in, w_out, sink):
    wq = w_in[:, :ATTN_WIDTH].reshape(D_MODEL, N_HEADS, HEAD_DIM)
    wq = _slot_order(_rope_order(wq), 1).reshape(D_MODEL, ATTN_WIDTH)
    wk = w_in[:, ATTN_WIDTH:ATTN_WIDTH + KV_WIDTH].reshape(D_MODEL, N_KV_HEADS, HEAD_DIM)
    wk = _rope_order(wk).reshape(D_MODEL, KV_WIDTH)
    wv = w_in[:, ATTN_WIDTH + KV_WIDTH:ATTN_WIDTH + 2 * KV_WIDTH]
    wz = w_in[:, ATTN_WIDTH + 2 * KV_WIDTH:].reshape(D_MODEL, N_HEADS, HEAD_DIM)
    wz = _slot_order(wz, 1).reshape(D_MODEL, ATTN_WIDTH)
    w_in_p = jnp.concatenate([wq, wk, wv, wz], axis=1).astype(BF16)
    wo = _slot_order(w_out.reshape(N_HEADS, HEAD_DIM, D_MODEL), 0).reshape(ATTN_WIDTH, D_MODEL)
    sink_p = _slot_order(sink.astype(F32).reshape(N_HEADS), 0).reshape(2 * N_SLOTS)
    sink_tab = jnp.broadcast_to((sink_p * LOG2E)[:, None], (2 * N_SLOTS, LANES))
    return w_in_p, wo.astype(BF16), sink_tab


def _rope_tables():
    inv = ROPE_BASE ** (-np.arange(ROPE_FREQS, dtype=np.float64) / ROPE_FREQS)
    pos = np.arange(SEQ)
    row = (pos // GRID_W)[:, None] * inv
    col = (pos % GRID_W)[:, None] * inv
    w = np.arange(LANES) % HEAD_DIM
    half, axis, f = w // 32, (w % 32) // 16, w % 16
    ang = np.where((axis == 0)[None, :], row[:, f], col[:, f])
    sign = np.where(half == 0, -1.0, 1.0)[None, :]
    cos = np.concatenate([np.ones((CTX_LEN, LANES)), np.cos(ang)], axis=0)
    sin = np.concatenate([np.zeros((CTX_LEN, LANES)), np.sin(ang) * sign], axis=0)
    return jnp.asarray(cos, F32), jnp.asarray(sin, F32)


def _s5_operators(lam_re, lam_im, log_dt, b_re, b_im, c_re, c_im, d_skip):
    t_len = CHUNK
    n_pow = 2 * t_len + 1
    lr, li = lam_re.astype(F32), lam_im.astype(F32)
    dt = jnp.exp(log_dt.astype(F32))[..., None]
    mag = jnp.exp(lr * dt)
    sq = [(mag * jnp.cos(li * dt), mag * jnp.sin(li * dt))]
    while 2 ** len(sq) < n_pow:
        r, i = sq[-1]
        sq.append((r * r - i * i, 2.0 * r * i))
    ks = np.arange(n_pow)
    pr = jnp.ones(lr.shape + (n_pow,), F32)
    pi = jnp.zeros(lr.shape + (n_pow,), F32)
    for bit, (r, i) in enumerate(sq):
        on = jnp.asarray((ks >> bit) & 1 == 1)
        fr = jnp.where(on, r[..., None], 1.0)
        fi = jnp.where(on, i[..., None], 0.0)
        pr, pi = pr * fr - pi * fi, pr * fi + pi * fr
    ar1, ai1 = sq[0][0] - 1.0, sq[0][1]
    den = lr * lr + li * li
    gr, gi = (ar1 * lr + ai1 * li) / den, (ai1 * lr - ar1 * li) / den
    br_, bi_ = b_re.astype(F32), b_im.astype(F32)
    bbr = gr[..., None] * br_ - gi[..., None] * bi_
    bbi = gr[..., None] * bi_ + gi[..., None] * br_
    pad_k = POW_ROWS - n_pow
    prt, pit = jnp.swapaxes(pr, 2, 3), jnp.swapaxes(pi, 2, 3)
    pw = jnp.pad(jnp.concatenate([prt, prt, pit, pit], axis=-1),
                 ((0, 0), (0, 0), (0, pad_k), (0, 0)))
    brt, bit = jnp.swapaxes(bbr, 2, 3), jnp.swapaxes(bbi, 2, 3)
    bb = jnp.concatenate([brt, bit, -bit, brt, bit, brt, brt, -bit], axis=-1)
    cr, ci = c_re.astype(F32), c_im.astype(F32)
    cm = jnp.stack([jnp.concatenate([cr, -ci], axis=-1),
                    jnp.concatenate([-ci, -cr], axis=-1)], axis=2)
    dv = jnp.tile(d_skip.astype(F32).reshape(SSM_GROUPS, 1, SSM_GROUP), (1, 1, t_len))
    return _s5_operator_call(pw, bb, cm, dv)


def _operator_constants():
    t_len = CHUNK
    s_of_row = np.arange(CHUNK_W) // SSM_GROUP
    k_ar = np.arange(POW_ROWS)
    oh_f = (k_ar[None, :] == (t_len - 1 - s_of_row)[:, None]).astype(np.float32)
    oh_b = (k_ar[None, :] == s_of_row[:, None]).astype(np.float32)
    t_of_lane = np.arange(CHUNK_W) // SSM_GROUP
    expo = [s_of_row + 1, t_len - s_of_row, s_of_row, t_len - 1 - s_of_row]
    sel = np.stack([(k_ar[None, :] == e[:, None]) for e in expo]).astype(np.float32)
    h_of_lane = np.arange(CHUNK_W) % SSM_GROUP
    dmask = ((s_of_row[:, None] == t_of_lane[None, :])
             & ((np.arange(CHUNK_W) % SSM_GROUP)[:, None] == h_of_lane[None, :])).astype(np.float32)
    return oh_f, oh_b, sel, dmask


def _s5_op_kernel(*refs):
    for gg in range(OP_GROUP_BATCH):
        _s5_op_group(gg, *refs)


def _s5_op_group(gg, pw_ref, bb_ref, cm_ref, dv_ref, ohf_ref, ohb_ref, sel_ref, dmask_ref,
                 m_ref, ws_ref, wyt_ref, cst_ref):
    t_len = CHUNK

    def split(a):
        hi = a.astype(BF16)
        return hi, (a - hi.astype(F32)).astype(BF16)

    def pick_rows(onehot, table):
        hi, lo = split(table)
        return (jnp.dot(onehot, hi, preferred_element_type=F32)
                + jnp.dot(onehot, lo, preferred_element_type=F32))

    def tile_rows(a):
        return jnp.concatenate([a] * t_len, axis=0)

    main, swapped = [], []
    for d, oh_ref in ((0, ohf_ref), (1, ohb_ref)):
        pp = pick_rows(oh_ref[...], pw_ref[d, gg])
        p_re, p_im = pp[:, :STATE_W], pp[:, STATE_W:]
        b0, b1, b2, b3 = [tile_rows(bb_ref[d, gg, :, STATE_W * i:STATE_W * (i + 1)])
                          for i in range(4)]
        main.append(p_re * b0 + p_im * b1)
        swapped.append(p_re * b2 + p_im * b3)
    ws_ref[gg] = jnp.concatenate(main + swapped, axis=1).astype(BF16)

    c_tiled = [[tile_rows(cm_ref[d, gg, i]) for i in range(2)] for d in range(2)]

    def block_t(d, pat):
        pp = pick_rows(sel_ref[pat], pw_ref[d, gg])
        return pp[:, :STATE_W] * c_tiled[d][0] + pp[:, STATE_W:] * c_tiled[d][1]

    wyt_ref[gg] = jnp.concatenate([block_t(0, 0), block_t(1, 1)], axis=1).astype(BF16)

    nt = (((1,), (1,)), ((), ()))
    kt_f = lax.dot_general(bb_ref[0, gg, :, :STATE_W], block_t(0, 2), nt,
                           preferred_element_type=F32, precision=lax.Precision.HIGHEST)
    kt_b = lax.dot_general(bb_ref[1, gg, :, :STATE_W], block_t(1, 3), nt,
                           preferred_element_type=F32, precision=lax.Precision.HIGHEST)
    lane = lax.broadcasted_iota(jnp.int32, (SSM_GROUP, CHUNK_W), 1)
    skip = dv_ref[gg]
    for s in range(t_len):
        fwd = kt_f if s == 0 else pltpu.roll(kt_f, SSM_GROUP * s, 1)
        back = t_len - 1 - s
        bwd = kt_b if back == 0 else pltpu.roll(kt_b, CHUNK_W - SSM_GROUP * back, 1)
        rows = (jnp.where(lane >= SSM_GROUP * s, fwd, 0.0)
                + jnp.where(lane < SSM_GROUP * (s + 1), bwd, 0.0)
                + dmask_ref[SSM_GROUP * s:SSM_GROUP * (s + 1), :] * skip)
        m_ref[gg, SSM_GROUP * s:SSM_GROUP * (s + 1), :] = rows.astype(BF16)

    even = lax.broadcasted_iota(jnp.int32, (SUBLANES, STATE_W), 0) % 2 == 0
    sign = jnp.where(lax.broadcasted_iota(jnp.int32, (1, STATE_W), 1) < SSM_STATE, -1.0, 1.0)

    def w12(d, k):
        row = pw_ref[d, gg, k:k + 1, :]
        return row[:, :STATE_W], row[:, STATE_W:] * sign

    zero = (jnp.zeros((1, STATE_W), F32),) * 2
    pairs = [(w12(0, t_len), w12(0, 2 * t_len)), (zero, w12(0, t_len)),
             (w12(1, 2 * t_len), w12(1, t_len)), (w12(1, t_len), zero)]
    idx = 0
    for top, bot in pairs:
        for part in range(2):
            cst_ref[gg, idx] = jnp.where(even, jnp.broadcast_to(top[part], (SUBLANES, STATE_W)),
                                         jnp.broadcast_to(bot[part], (SUBLANES, STATE_W)))
            idx += 1


def _s5_operator_call(pw, bb, cm, dv):
    oh_f, oh_b, sel, dmask = _operator_constants()
    consts = [jnp.asarray(oh_f, BF16), jnp.asarray(oh_b, BF16), jnp.asarray(sel, BF16),
              jnp.asarray(dmask)]

    def per_group(*tail):
        n = len(tail)
        return pl.BlockSpec((2, OP_GROUP_BATCH) + tail, lambda g: (0, g) + (0,) * n)

    def whole(a):
        return pl.BlockSpec(a.shape, lambda g: (0,) * a.ndim)

    def out(*tail):
        return pl.BlockSpec((OP_GROUP_BATCH,) + tail, lambda g: (g,) + (0,) * len(tail))

    return pl.pallas_call(
        _s5_op_kernel,
        grid=(SSM_GROUPS // OP_GROUP_BATCH,),
        in_specs=[per_group(POW_ROWS, 2 * STATE_W), per_group(SSM_GROUP, 4 * STATE_W),
                  per_group(2, SSM_GROUP, STATE_W),
                  pl.BlockSpec((OP_GROUP_BATCH, 1, CHUNK_W), lambda g: (g, 0, 0))]
                 + [whole(a) for a in consts],
        out_specs=[out(CHUNK_W, CHUNK_W), out(CHUNK_W, 4 * STATE_W), out(CHUNK_W, 2 * STATE_W),
                   out(8, SUBLANES, STATE_W)],
        out_shape=[jax.ShapeDtypeStruct((SSM_GROUPS, CHUNK_W, CHUNK_W), BF16),
                   jax.ShapeDtypeStruct((SSM_GROUPS, CHUNK_W, 4 * STATE_W), BF16),
                   jax.ShapeDtypeStruct((SSM_GROUPS, CHUNK_W, 2 * STATE_W), BF16),
                   jax.ShapeDtypeStruct((SSM_GROUPS, 8, SUBLANES, STATE_W), F32)],
        compiler_params=_params(("arbitrary",), 32),
        name="s5_operators",
    )(pw, bb, cm, dv, *consts)


def kernel(x, c, ctx, c_ctx, norm_w, w_ada, b_ada, attn_w_in, attn_sink, attn_w_out,
           ssm_w_in, ssm_lam_re, ssm_lam_im, ssm_log_dt, ssm_b_re, ssm_b_im, ssm_c_re, ssm_c_im,
           ssm_d, ssm_w_glu, ssm_w_out, final_norm_w):
    mod0, mod1 = _modulation(c, c_ctx, w_ada, b_ada)

    w_in0, w_out0, sink_tab = _attn_weights(attn_w_in[0], attn_w_out[0], attn_sink[0])
    cos_tab, sin_tab = _rope_tables()
    q, z0, kbd, vbd = _proj0(x, ctx, mod0, norm_w[0], cos_tab, sin_tab, w_in0)
    xc1 = _attention(q, z0, kbd, vbd, sink_tab, x, ctx, mod0, w_out0)

    u_g, z1 = _proj1(xc1, mod1, norm_w[1], ssm_w_in[0].astype(BF16))
    m_mat, ws_mat, wy_mat, consts = _s5_operators(
        ssm_lam_re[0], ssm_lam_im[0], ssm_log_dt[0], ssm_b_re[0], ssm_b_im[0],
        ssm_c_re[0], ssm_c_im[0], ssm_d[0])
    y_g = _s5_core(u_g, m_mat, ws_mat, wy_mat, consts)
    return _out1(y_g, z1, xc1, mod1, ssm_w_glu[0].astype(BF16), ssm_w_out[0].astype(BF16),
                 final_norm_w)
```

```python
import functools
import math

import jax
import jax.numpy as jnp
import numpy as np
from jax import lax
from jax.experimental import pallas as pl
from jax.experimental.pallas import tpu as pltpu

F32 = jnp.float32
BF16 = jnp.bfloat16

D_MODEL = 1024
BATCH = 4
SEQ = 4096
GRID_W = 64
CTX_LEN = 256
TOTAL = CTX_LEN + SEQ
HEAD_DIM = 64
N_HEADS = 16
N_KV_HEADS = 4
ATTN_WIDTH = N_HEADS * HEAD_DIM
KV_WIDTH = N_KV_HEADS * HEAD_DIM
BLOCK = 128
N_BLOCKS = TOTAL // BLOCK
N_CTX_BLOCKS = CTX_LEN // BLOCK
ROPE_BASE = 10000.0
ROPE_FREQS = HEAD_DIM // 4
SSM_GROUP = 16
SSM_GROUPS = D_MODEL // SSM_GROUP
SSM_STATE = 64
NORM_EPS = 1e-6
NEG_INF = -1e30

LANES = 128
SUBLANES = 8
N_SLOTS = ATTN_WIDTH // LANES
N_KV_PAIRS = KV_WIDTH // LANES
SLOTS_PER_M = N_SLOTS // N_KV_PAIRS
ATTN_QBLOCKS = 2
assert N_CTX_BLOCKS % ATTN_QBLOCKS == 0 and N_BLOCKS % ATTN_QBLOCKS == 0
UNIT_SLOTS = 2
LOG2E = math.log2(math.e)
Q_SCALE = HEAD_DIM ** -0.5 * LOG2E
ROW_TILE = 256
N_ROW_TILES = TOTAL // ROW_TILE
CHUNK = 16
N_CHUNKS = TOTAL // CHUNK
N_CTX_CHUNKS = CTX_LEN // CHUNK
CHUNK_W = CHUNK * SSM_GROUP
CHUNKS_PER_TILE = ROW_TILE // CHUNK
PAIRS_PER_TILE = CHUNKS_PER_TILE // 2
SLOT_PITCH = 24
OUT_PHASES = 4
assert SLOT_PITCH >= CHUNK and SLOT_PITCH % SUBLANES == 0
STATE_W = 2 * SSM_STATE
SCAN_ROWS = N_CHUNKS * BATCH
CTX_SCAN_ROWS = N_CTX_CHUNKS * BATCH
N_SCAN_BLOCKS = SCAN_ROWS // SUBLANES
N_CTX_SCAN_BLOCKS = CTX_SCAN_ROWS // SUBLANES
GROUP_BATCH = 4
OP_GROUP_BATCH = 4
POW_ROWS = 48

assert BATCH * 2 == SUBLANES


def _params(semantics, vmem_mb):
    return pltpu.CompilerParams(dimension_semantics=semantics,
                                vmem_limit_bytes=vmem_mb * 1024 * 1024)


def _mod_kernel(c_ref, w_ref, b_ref, o_ref):
    c = c_ref[...]
    a = c * jax.nn.sigmoid(c)
    o_ref[0] = jnp.dot(a, w_ref[0], preferred_element_type=F32,
                       precision=lax.Precision.HIGHEST) + b_ref[0]


def _modulation(c, c_ctx, w_ada, b_ada):
    depth = w_ada.shape[0]
    rows = jnp.zeros((SUBLANES, D_MODEL), F32).at[:BATCH].set(c).at[BATCH].set(c_ctx)
    n_col = 3
    out = pl.pallas_call(
        _mod_kernel,
        grid=(depth, n_col),
        in_specs=[
            pl.BlockSpec((SUBLANES, D_MODEL), lambda l, j: (0, 0)),
            pl.BlockSpec((1, D_MODEL, D_MODEL), lambda l, j: (l, 0, j)),
            pl.BlockSpec((1, 1, D_MODEL), lambda l, j: (l, 0, j)),
        ],
        out_specs=pl.BlockSpec((1, SUBLANES, D_MODEL), lambda l, j: (l, 0, j)),
        out_shape=jax.ShapeDtypeStruct((depth, SUBLANES, 3 * D_MODEL), F32),
        compiler_params=_params(("arbitrary", "arbitrary"), 32),
        name="adaln_modulation",
    )(rows, w_ada, b_ada.reshape(depth, 1, 3 * D_MODEL))
    tabs = []
    for l in range(depth):
        lat = out[l, :BATCH].reshape(BATCH, 3, D_MODEL)
        cx = jnp.broadcast_to(out[l, BATCH].reshape(1, 3, D_MODEL), (BATCH, 3, D_MODEL))
        tab = jnp.stack([cx, lat], axis=1)
        tabs.append(jnp.pad(tab, ((0, 0), (0, 0), (0, SUBLANES - 3), (0, 0))))
    return tabs


def _modulated_norm(xt, nw, mod_ref, b=0):
    ms = jnp.mean(xt * xt, axis=-1, keepdims=True)
    y = xt * lax.rsqrt(ms + NORM_EPS) * nw
    return y * (1.0 + mod_ref[b, 0, 1:2, :]) + mod_ref[b, 0, 0:1, :]


def _lane_block_transpose(vs):
    n = len(vs)
    width = LANES // n
    blk = lax.broadcasted_iota(jnp.int32, vs[0].shape, 1) // width
    x = list(vs)
    d = n // 2
    while d >= 1:
        clear = (blk & d) == 0
        y = list(x)
        for i in range(n):
            if i & d == 0:
                a, b = x[i], x[i + d]
                y[i] = jnp.where(clear, a, pltpu.roll(b, width * d, 1))
                y[i + d] = jnp.where(clear, pltpu.roll(a, LANES - width * d, 1), b)
        x = y
        d //= 2
    return x


def _chunk_slot(b, cc):
    return (cc // 2) * (2 * BATCH) + 2 * b + (cc % 2)


def _proj0_kernel(x_ref, c_ref, mod_ref, nw_ref, cos_ref, sin_ref, w_ref,
                  q_ref, z_ref, kbd_ref, vt_ref):
    is_ctx = pl.program_id(0) == 0
    h = jnp.concatenate(
        [_modulated_norm(jnp.where(is_ctx, c_ref[b], x_ref[b]), nw_ref[...], mod_ref, b).astype(BF16)
         for b in range(BATCH)], axis=0)
    cos = cos_ref[...]
    sin = sin_ref[...]
    lane = lax.broadcasted_iota(jnp.int32, (ROW_TILE, LANES), 1)
    first_half = (lane % HEAD_DIM) < (HEAD_DIM // 2)
    low = lax.broadcasted_iota(jnp.int32, (BLOCK, LANES), 1) < HEAD_DIM

    def rope(t):
        partner = jnp.where(first_half, pltpu.roll(t, LANES - HEAD_DIM // 2, 1),
                            pltpu.roll(t, HEAD_DIM // 2, 1))
        return t * cos + partner * sin

    q = jnp.dot(h, w_ref[:, :ATTN_WIDTH], preferred_element_type=F32)
    k = jnp.dot(h, w_ref[:, ATTN_WIDTH:ATTN_WIDTH + KV_WIDTH], preferred_element_type=F32)
    v = jnp.dot(h, w_ref[:, ATTN_WIDTH + KV_WIDTH:ATTN_WIDTH + 2 * KV_WIDTH],
                preferred_element_type=F32)
    z = jnp.dot(h, w_ref[:, ATTN_WIDTH + 2 * KV_WIDTH:], preferred_element_type=F32)
    for b in range(BATCH):
        tile = slice(ROW_TILE * b, ROW_TILE * (b + 1))
        z_ref[b] = z[tile].astype(BF16)
        for j in range(N_SLOTS):
            m, gi = divmod(j, SLOTS_PER_M)
            qj = (rope(q[tile, LANES * j:LANES * (j + 1)]) * Q_SCALE).astype(BF16)
            for blk in range(ROW_TILE // BLOCK):
                q_ref[b, blk, m, BLOCK * gi:BLOCK * (gi + 1), :] = qj[BLOCK * blk:BLOCK * (blk + 1)]
        for m in range(N_KV_PAIRS):
            sl = slice(LANES * m, LANES * (m + 1))
            kr = rope(k[tile, sl])
            vm = v[tile, sl]
            for blk in range(ROW_TILE // BLOCK):
                rows = slice(BLOCK * blk, BLOCK * (blk + 1))
                kbd_ref[b, blk, m, :BLOCK, :] = jnp.where(low, kr[rows], 0.0).astype(BF16)
                kbd_ref[b, blk, m, BLOCK:, :] = jnp.where(low, 0.0, kr[rows]).astype(BF16)
                vt_ref[b, blk, LANES * m:LANES * (m + 1), :] = vm[rows].T.astype(BF16)


def _proj0(x, ctx, mod, norm_w, cos_tab, sin_tab, w_in):
    n_col = w_in.shape[1]
    blocks_per_tile = ROW_TILE // BLOCK
    kv_shape = jax.ShapeDtypeStruct((BATCH, N_BLOCKS, N_KV_PAIRS, 2 * BLOCK, LANES), BF16)
    kv_spec = pl.BlockSpec((BATCH, blocks_per_tile, N_KV_PAIRS, 2 * BLOCK, LANES),
                           lambda i: (0, i, 0, 0, 0))
    row_spec = pl.BlockSpec((BATCH, ROW_TILE, D_MODEL), lambda i: (0, i, 0))
    once = pl.Buffered(1)
    return pl.pallas_call(
        _proj0_kernel,
        grid=(N_ROW_TILES,),
        in_specs=[
            pl.BlockSpec((BATCH, ROW_TILE, D_MODEL), lambda i: (0, jnp.maximum(i - 1, 0), 0)),
            pl.BlockSpec((BATCH, ROW_TILE, D_MODEL), lambda i: (0, 0, 0), pipeline_mode=once),
            pl.BlockSpec((BATCH, 1, SUBLANES, D_MODEL), lambda i: (0, jnp.minimum(i, 1), 0, 0)),
            pl.BlockSpec((1, D_MODEL), lambda i: (0, 0)),
            pl.BlockSpec((ROW_TILE, LANES), lambda i: (i, 0)),
            pl.BlockSpec((ROW_TILE, LANES), lambda i: (i, 0)),
            pl.BlockSpec((D_MODEL, n_col), lambda i: (0, 0), pipeline_mode=once),
        ],
        out_specs=[
            pl.BlockSpec((BATCH, blocks_per_tile, N_KV_PAIRS, SLOTS_PER_M * BLOCK, LANES),
                         lambda i: (0, i, 0, 0, 0)),
            row_spec, kv_spec,
            pl.BlockSpec((BATCH, blocks_per_tile, KV_WIDTH, BLOCK), lambda i: (0, i, 0, 0))],
        out_shape=[
            jax.ShapeDtypeStruct((BATCH, N_BLOCKS, N_KV_PAIRS, SLOTS_PER_M * BLOCK, LANES), BF16),
            jax.ShapeDtypeStruct((BATCH, TOTAL, ATTN_WIDTH), BF16),
            kv_shape,
            jax.ShapeDtypeStruct((BATCH, N_BLOCKS, KV_WIDTH, BLOCK), BF16),
        ],
        compiler_params=_params(("arbitrary",), 56),
        name="attn_projection",
    )(x, ctx, mod, norm_w.reshape(1, D_MODEL), cos_tab, sin_tab, w_in)


def _attn_kernel(q_ref, z_ref, k0_ref, k1_ref, k2_ref, k3_ref, kx_ref,
                 v0_ref, v1_ref, v2_ref, v3_ref, vx_ref,
                 sink_ref, x_ref, c_ref, mod_ref, wo_ref, eye_ref, tri_ref, o_ref):
    step = pl.program_id(1)
    is_lat = step >= N_CTX_BLOCKS // ATTN_QBLOCKS
    n_first = ATTN_QBLOCKS * step - N_CTX_BLOCKS
    q_rows = UNIT_SLOTS * BLOCK
    blocked = tri_ref[2]
    bias = []
    for qb in range(ATTN_QBLOCKS):
        n = n_first + qb
        bias.append([jnp.where(jnp.logical_and(is_lat, n >= 1), tri_ref[0], blocked),
                     jnp.where(is_lat, tri_ref[3], blocked),
                     jnp.where(jnp.logical_and(is_lat, n <= SEQ // BLOCK - 2), tri_ref[1], blocked),
                     None, None])
    k_win = (k0_ref, k1_ref, k2_ref, k3_ref)
    v_win = (v0_ref, v1_ref, v2_ref, v3_ref)

    def kpiece(qb, p, m):
        if p < 3:
            return k_win[qb + p][0, 0, m], v_win[qb + p][0, 0]
        return kx_ref[0, p - 3, m], vx_ref[0, p - 3]

    n_piece = 3 + N_CTX_BLOCKS
    units = [(qb, m, h) for qb in range(ATTN_QBLOCKS) for m in range(N_KV_PAIRS)
             for h in range(SLOTS_PER_M // UNIT_SLOTS)]

    nt = (((1,), (1,)), ((), ()))

    half_rows = slice(0, BLOCK), slice(BLOCK, 2 * BLOCK)

    def scores(qb, m, h):
        qu = q_ref[0, qb, m, q_rows * h:q_rows * (h + 1), :]
        qu_masked = jnp.concatenate([qu, eye_ref[...]], axis=1)
        s_list = []
        for p in range(n_piece):
            kbd, _ = kpiece(qb, p, m)
            if bias[qb][p] is None:
                s = lax.dot_general(kbd, qu, nt, preferred_element_type=F32)
            else:
                s = lax.dot_general(jnp.concatenate([kbd, bias[qb][p]], axis=1), qu_masked, nt,
                                    preferred_element_type=F32)
            s_list.append(s)
        return s_list

    def finish(qb, m, h, s_list, y):
        slot0 = SLOTS_PER_M * m + UNIT_SLOTS * h
        halves = []
        for hs in range(2):
            sink = jnp.concatenate(
                [sink_ref[2 * (slot0 + gi) + hs:2 * (slot0 + gi) + hs + 1, :]
                 for gi in range(UNIT_SLOTS)], axis=1)
            mx = sink
            for s in s_list:
                mx = jnp.maximum(mx, jnp.max(s[half_rows[hs]], axis=0, keepdims=True))
            probs = [jnp.exp2(s[half_rows[hs]] - mx) for s in s_list]
            denom = jnp.exp2(sink - mx)
            for e in probs:
                denom = denom + jnp.sum(e, axis=0, keepdims=True)
            probs = [e.astype(BF16) for e in probs]
            kv_head = 2 * m + hs
            acc = None
            for p0 in range(0, n_piece, 2):
                group = list(range(p0, min(p0 + 2, n_piece)))
                vt = jnp.concatenate(
                    [kpiece(qb, p, m)[1][HEAD_DIM * kv_head:HEAD_DIM * (kv_head + 1), :]
                     for p in group], axis=1)
                pt = jnp.concatenate([probs[p] for p in group], axis=0)
                part = jnp.dot(vt, pt, preferred_element_type=F32)
                acc = part if acc is None else acc + part
            halves.append(acc * (1.0 / denom))
        o_t = jnp.concatenate(halves, axis=0)
        outs = []
        for gi in range(UNIT_SLOTS):
            j = slot0 + gi
            o = o_t[:, BLOCK * gi:BLOCK * (gi + 1)].T
            zj = z_ref[0, BLOCK * qb:BLOCK * (qb + 1), LANES * j:LANES * (j + 1)].astype(F32)
            outs.append((o * (zj * jax.nn.sigmoid(zj))).astype(BF16))
        g = jnp.concatenate(outs, axis=1)
        part = jnp.dot(g, wo_ref[LANES * slot0:LANES * (slot0 + UNIT_SLOTS), :],
                       preferred_element_type=F32)
        return part if y is None else y + part

    y = [None] * ATTN_QBLOCKS
    pending = scores(*units[0])
    for u, (qb, m, h) in enumerate(units):
        nxt = scores(*units[u + 1]) if u + 1 < len(units) else None
        y[qb] = finish(qb, m, h, pending, y[qb])
        pending = nxt
    resid = jnp.where(is_lat, x_ref[0], c_ref[0])
    gate = mod_ref[0, 0, 2:3, :]
    for qb in range(ATTN_QBLOCKS):
        rows = slice(BLOCK * qb, BLOCK * (qb + 1))
        o_ref[0, rows, :] = resid[rows] + gate * y[qb]


def _attention(q, z, kbd, vt, sink_tab, x, ctx, mod, w_out):
    last = N_BLOCKS - 1
    n_m = KV_WIDTH // LANES
    nq = ATTN_QBLOCKS
    ctx_steps = N_CTX_BLOCKS // nq
    row_spec = pl.BlockSpec((1, nq * BLOCK, ATTN_WIDTH), lambda b, i: (b, i, 0))

    def kv_spec(off):
        return pl.BlockSpec((1, 1, n_m, 2 * BLOCK, LANES),
                            lambda b, i: (b, jnp.clip(nq * i + off, 0, last), 0, 0, 0))

    ctx_kv_spec = pl.BlockSpec((1, N_CTX_BLOCKS, n_m, 2 * BLOCK, LANES),
                               lambda b, i: (b, 0, 0, 0, 0))

    def vt_spec(off):
        return pl.BlockSpec((1, 1, KV_WIDTH, BLOCK),
                            lambda b, i: (b, jnp.clip(nq * i + off, 0, last), 0, 0))

    ctx_vt_spec = pl.BlockSpec((1, N_CTX_BLOCKS, KV_WIDTH, BLOCK), lambda b, i: (b, 0, 0, 0))
    off = np.arange(BLOCK)
    eye = np.tile(np.eye(BLOCK, dtype=np.float32), (UNIT_SLOTS, 1))
    key_ge = np.where(off[:, None] >= off[None, :], 0.0, NEG_INF)
    key_le = np.where(off[:, None] <= off[None, :], 0.0, NEG_INF)
    tri = np.stack([np.tile(t, (2, 1)) for t in
                    (key_ge, key_le, np.full((BLOCK, BLOCK), NEG_INF), np.zeros((BLOCK, BLOCK)))])
    consts = [jnp.asarray(a, BF16) for a in (eye, tri)]
    return pl.pallas_call(
        _attn_kernel,
        grid=(BATCH, N_BLOCKS // nq),
        in_specs=[
            pl.BlockSpec((1, nq, n_m, SLOTS_PER_M * BLOCK, LANES), lambda b, i: (b, i, 0, 0, 0)),
            row_spec,
            kv_spec(-1), kv_spec(0), kv_spec(1), kv_spec(2), ctx_kv_spec,
            vt_spec(-1), vt_spec(0), vt_spec(1), vt_spec(2), ctx_vt_spec,
            pl.BlockSpec((2 * N_SLOTS, LANES), lambda b, i: (0, 0)),
            pl.BlockSpec((1, nq * BLOCK, D_MODEL),
                         lambda b, i: (b, jnp.maximum(i - ctx_steps, 0), 0)),
            pl.BlockSpec((1, nq * BLOCK, D_MODEL),
                         lambda b, i: (b, jnp.minimum(i, ctx_steps - 1), 0)),
            pl.BlockSpec((1, 1, SUBLANES, D_MODEL),
                         lambda b, i: (b, jnp.minimum(i // ctx_steps, 1), 0, 0)),
            pl.BlockSpec((ATTN_WIDTH, D_MODEL), lambda b, i: (0, 0)),
        ] + [pl.BlockSpec(a.shape, lambda b, i, nd=a.ndim: (0,) * nd) for a in consts],
        out_specs=pl.BlockSpec((1, nq * BLOCK, D_MODEL), lambda b, i: (b, i, 0)),
        out_shape=jax.ShapeDtypeStruct((BATCH, TOTAL, D_MODEL), F32),
        compiler_params=_params(("arbitrary", "arbitrary"), 48),
        name="window_attention",
    )(q, z, kbd, kbd, kbd, kbd, kbd, vt, vt, vt, vt, vt, sink_tab, x, ctx, mod, w_out,
      *consts)


def _proj1_kernel(x_ref, mod_ref, nw_ref, w_ref, u_ref, z_ref, scr_ref):
    h = jnp.concatenate(
        [_modulated_norm(x_ref[b], nw_ref[...], mod_ref, b).astype(BF16) for b in range(BATCH)],
        axis=0)
    u = jnp.dot(h, w_ref[:, :D_MODEL], preferred_element_type=F32)
    z = jnp.dot(h, w_ref[:, D_MODEL:], preferred_element_type=F32)
    for b in range(BATCH):
        z_ref[b] = z[ROW_TILE * b:ROW_TILE * (b + 1)].astype(BF16)
    n_slab = D_MODEL // LANES
    for b in range(BATCH):
        for cc in range(CHUNKS_PER_TILE):
            r0 = ROW_TILE * b + CHUNK * cc
            q0 = SLOT_PITCH * _chunk_slot(b, cc)
            for k in range(n_slab):
                scr_ref[k, q0:q0 + CHUNK, :] = u[r0:r0 + CHUNK, LANES * k:LANES * (k + 1)]
    groups_per_slab = LANES // SSM_GROUP
    for k in range(n_slab):
        for pp in range(PAIRS_PER_TILE // 2):
            parts = []
            for p in (2 * pp, 2 * pp + 1):
                rows = [scr_ref[k, pl.ds(SLOT_PITCH * SUBLANES * p + s, SUBLANES,
                                         stride=SLOT_PITCH), :] for s in range(CHUNK)]
                parts.append([_lane_block_transpose(rows[SUBLANES * m2:SUBLANES * (m2 + 1)])
                              for m2 in range(CHUNK // SUBLANES)])
            for m2 in range(CHUNK // SUBLANES):
                for gl in range(groups_per_slab):
                    val = jnp.concatenate([parts[0][m2][gl], parts[1][m2][gl]], axis=0)
                    u_ref[groups_per_slab * k + gl, 2 * SUBLANES * pp:2 * SUBLANES * (pp + 1),
                          LANES * m2:LANES * (m2 + 1)] = val.astype(BF16)


def _proj1(xc, mod, norm_w, w_in):
    row_spec = pl.BlockSpec((BATCH, ROW_TILE, D_MODEL), lambda i: (0, i, 0))
    tile_rows = PAIRS_PER_TILE * SUBLANES
    return pl.pallas_call(
        _proj1_kernel,
        grid=(N_ROW_TILES,),
        in_specs=[
            row_spec,
            pl.BlockSpec((BATCH, 1, SUBLANES, D_MODEL), lambda i: (0, jnp.minimum(i, 1), 0, 0)),
            pl.BlockSpec((1, D_MODEL), lambda i: (0, 0)),
            pl.BlockSpec((D_MODEL, 2 * D_MODEL), lambda i: (0, 0)),
        ],
        out_specs=[pl.BlockSpec((SSM_GROUPS, tile_rows, CHUNK_W), lambda i: (0, i, 0)), row_spec],
        out_shape=[jax.ShapeDtypeStruct((SSM_GROUPS, SCAN_ROWS, CHUNK_W), BF16),
                   jax.ShapeDtypeStruct((BATCH, TOTAL, D_MODEL), BF16)],
        scratch_shapes=[pltpu.VMEM((D_MODEL // LANES, BATCH * CHUNKS_PER_TILE * SLOT_PITCH, LANES),
                                   F32)],
        compiler_params=_params(("arbitrary",), 56),
        name="ssm_projection",
    )(xc, mod, norm_w.reshape(1, D_MODEL), w_in)


def _s5_kernel(u_ref, m_ref, ws_ref, wy_ref, cst_ref, y_ref, s4_ref, xp_ref):
    for g in range(GROUP_BATCH):
        s4_ref[g] = jnp.dot(u_ref[g], ws_ref[g], preferred_element_type=F32)
    even = lax.broadcasted_iota(jnp.int32, (SUBLANES, STATE_W), 0) % 2 == 0
    down = 1
    up = SUBLANES - 1
    fwd = slice(0, STATE_W)
    bwd = slice(STATE_W, 2 * STATE_W)
    fwd_sw = slice(2 * STATE_W, 3 * STATE_W)
    bwd_sw = slice(3 * STATE_W, 4 * STATE_W)

    def step(j, carry):
        jb = jnp.where(j < N_CTX_SCAN_BLOCKS, N_CTX_SCAN_BLOCKS - 1 - j,
                       N_SCAN_BLOCKS - 1 + N_CTX_SCAN_BLOCKS - j)
        rf = pl.ds(pl.multiple_of(j * SUBLANES, SUBLANES), SUBLANES)
        rb = pl.ds(pl.multiple_of(jb * SUBLANES, SUBLANES), SUBLANES)
        new = []
        for g in range(GROUP_BATCH):
            cf, cfs, cb, cbs = carry[4 * g:4 * g + 4]
            p1f, p2f, q1f, q2f = cst_ref[g, 0], cst_ref[g, 1], cst_ref[g, 2], cst_ref[g, 3]
            p1b, p2b, q1b, q2b = cst_ref[g, 4], cst_ref[g, 5], cst_ref[g, 6], cst_ref[g, 7]
            zf = s4_ref[g, rf, fwd]
            zfs = s4_ref[g, rf, fwd_sw]
            rzf = pltpu.roll(zf, down, 0)
            rzfs = pltpu.roll(zfs, down, 0)
            xf = p1f * cf + p2f * cfs + (zf + q1f * rzf + q2f * rzfs)
            xfs = p1f * cfs - p2f * cf + (zfs + q1f * rzfs - q2f * rzf)
            xp_ref[g, rf, fwd] = jnp.where(even, cf, pltpu.roll(xf, down, 0))
            new += [jnp.where(even, pltpu.roll(xf, up, 0), xf),
                    jnp.where(even, pltpu.roll(xfs, up, 0), xfs)]
            zb = s4_ref[g, rb, bwd]
            zbs = s4_ref[g, rb, bwd_sw]
            rzb = pltpu.roll(zb, up, 0)
            rzbs = pltpu.roll(zbs, up, 0)
            xb = p1b * cb + p2b * cbs + (zb + q1b * rzb + q2b * rzbs)
            xbs = p1b * cbs - p2b * cb + (zbs + q1b * rzbs - q2b * rzb)
            xp_ref[g, rb, bwd] = jnp.where(even, pltpu.roll(xb, up, 0), cb)
            new += [jnp.where(even, xb, pltpu.roll(xb, down, 0)),
                    jnp.where(even, xbs, pltpu.roll(xbs, down, 0))]
        return tuple(new)

    zero = jnp.zeros((SUBLANES, STATE_W), F32)
    lax.fori_loop(0, N_SCAN_BLOCKS, step, (zero,) * (4 * GROUP_BATCH))
    for g in range(GROUP_BATCH):
        y_ref[g] = (
            jnp.dot(u_ref[g, CTX_SCAN_ROWS:, :], m_ref[g], preferred_element_type=F32)
            + lax.dot_general(xp_ref[g, CTX_SCAN_ROWS:, :].astype(BF16), wy_ref[g],
                              (((1,), (1,)), ((), ())), preferred_element_type=F32))


def _s5_core(u_g, m_mat, ws_mat, wy_mat, consts):
    lat_rows = SCAN_ROWS - CTX_SCAN_ROWS

    def gspec(*tail):
        return pl.BlockSpec((GROUP_BATCH,) + tail, lambda i: (i,) + (0,) * len(tail))

    return pl.pallas_call(
        _s5_kernel,
        grid=(SSM_GROUPS // GROUP_BATCH,),
        in_specs=[
            gspec(SCAN_ROWS, CHUNK_W),
            gspec(CHUNK_W, CHUNK_W),
            gspec(CHUNK_W, 4 * STATE_W),
            gspec(CHUNK_W, 2 * STATE_W),
            gspec(8, SUBLANES, STATE_W),
        ],
        out_specs=gspec(lat_rows, CHUNK_W),
        out_shape=jax.ShapeDtypeStruct((SSM_GROUPS, lat_rows, CHUNK_W), F32),
        scratch_shapes=[
            pltpu.VMEM((GROUP_BATCH, SCAN_ROWS, 4 * STATE_W), F32),
            pltpu.VMEM((GROUP_BATCH, SCAN_ROWS, 2 * STATE_W), F32),
        ],
        compiler_params=_params(("arbitrary",), 48),
        name="s5_scan",
    )(u_g, m_mat, ws_mat, wy_mat, consts)


def _out1_kernel(y_ref, z_ref, x_ref, mod_ref, wg_ref, wo_ref, fnw_ref, o_ref, scr_ref):
    n_slab = D_MODEL // LANES
    groups_per_slab = LANES // SSM_GROUP
    pairs_per_phase = PAIRS_PER_TILE // OUT_PHASES
    rows_per_phase = ROW_TILE // OUT_PHASES

    def relayout(ph):
        for k in range(n_slab):
            for p in range(pairs_per_phase * ph, pairs_per_phase * (ph + 1)):
                for m2 in range(CHUNK // SUBLANES):
                    vals = [y_ref[groups_per_slab * k + gl, SUBLANES * p:SUBLANES * (p + 1),
                                  LANES * m2:LANES * (m2 + 1)] for gl in range(groups_per_slab)]
                    steps = _lane_block_transpose(vals)
                    for s2 in range(SUBLANES):
                        t_idx = SUBLANES * m2 + s2
                        scr_ref[k, pl.ds(SLOT_PITCH * SUBLANES * p + t_idx, SUBLANES,
                                         stride=SLOT_PITCH), :] = steps[s2]

    def compute(ph):
        chunks = range(2 * pairs_per_phase * ph, 2 * pairs_per_phase * (ph + 1))
        y = jnp.concatenate(
            [jnp.concatenate(
                [scr_ref[k, SLOT_PITCH * _chunk_slot(b, cc):SLOT_PITCH * _chunk_slot(b, cc) + CHUNK, :]
                 for k in range(n_slab)], axis=1)
             for b in range(BATCH) for cc in chunks], axis=0)
        g = (0.5 * y * (1.0 + lax.erf(y * (2.0 ** -0.5)))).astype(BF16)
        t = jnp.dot(g, wg_ref[...], preferred_element_type=F32)
        rows = slice(rows_per_phase * ph, rows_per_phase * (ph + 1))
        z = jnp.concatenate([z_ref[b, rows, :] for b in range(BATCH)], axis=0).astype(F32)
        r = (t[:, :D_MODEL] * jax.nn.sigmoid(t[:, D_MODEL:]) * (z * jax.nn.sigmoid(z))).astype(BF16)
        o = jnp.dot(r, wo_ref[...], preferred_element_type=F32)
        for b in range(BATCH):
            x2 = (x_ref[b, rows, :]
                  + mod_ref[b, 0, 2:3, :] * o[rows_per_phase * b:rows_per_phase * (b + 1)])
            ms = jnp.mean(x2 * x2, axis=-1, keepdims=True)
            o_ref[b, rows, :] = x2 * lax.rsqrt(ms + NORM_EPS) * fnw_ref[...]

    relayout(0)
    for ph in range(OUT_PHASES):
        if ph + 1 < OUT_PHASES:
            relayout(ph + 1)
        compute(ph)


def _out1(y_g, z, xc, mod, w_glu, w_out, final_norm_w):
    ctx_tiles = CTX_LEN // ROW_TILE
    tile_rows = PAIRS_PER_TILE * SUBLANES
    lat_spec = pl.BlockSpec((BATCH, ROW_TILE, D_MODEL), lambda i: (0, i, 0))
    all_spec = pl.BlockSpec((BATCH, ROW_TILE, D_MODEL), lambda i: (0, i + ctx_tiles, 0))
    return pl.pallas_call(
        _out1_kernel,
        grid=(SEQ // ROW_TILE,),
        in_specs=[
            pl.BlockSpec((SSM_GROUPS, tile_rows, CHUNK_W), lambda i: (0, i, 0)),
            all_spec, all_spec,
            pl.BlockSpec((BATCH, 1, SUBLANES, D_MODEL), lambda i: (0, 1, 0, 0)),
            pl.BlockSpec((D_MODEL, 2 * D_MODEL), lambda i: (0, 0)),
            pl.BlockSpec((D_MODEL, D_MODEL), lambda i: (0, 0)),
            pl.BlockSpec((1, D_MODEL), lambda i: (0, 0)),
        ],
        out_specs=lat_spec,
        out_shape=jax.ShapeDtypeStruct((BATCH, SEQ, D_MODEL), F32),
        scratch_shapes=[pltpu.VMEM((D_MODEL // LANES, BATCH * CHUNKS_PER_TILE * SLOT_PITCH, LANES),
                                   F32)],
        compiler_params=_params(("arbitrary",), 56),
        name="ssm_output",
    )(y_g, z, xc, mod, w_glu, w_out, final_norm_w.reshape(1, D_MODEL))


def _slot_order(t, lead):
    n_m = KV_WIDTH // LANES
    gq = N_HEADS // N_KV_HEADS
    shape = t.shape
    t = t.reshape(shape[:lead] + (n_m, 2, gq) + shape[lead + 1:])
    perm = tuple(range(lead)) + (lead, lead + 2, lead + 1) + tuple(range(lead + 3, t.ndim))
    return jnp.transpose(t, perm)


def _rope_order(t):
    shape = t.shape
    t = t.reshape(shape[:-1] + (2, 2, ROPE_FREQS))
    return jnp.swapaxes(t, -3, -2).reshape(shape)


def _attn_weights(w_in, w_out, sink):
    wq = w_in[:, :ATTN_WIDTH].reshape(D_MODEL, N_HEADS, HEAD_DIM)
    wq = _slot_order(_rope_order(wq), 1).reshape(D_MODEL, ATTN_WIDTH)
    wk = w_in[:, ATTN_WIDTH:ATTN_WIDTH + KV_WIDTH].reshape(D_MODEL, N_KV_HEADS, HEAD_DIM)
    wk = _rope_order(wk).reshape(D_MODEL, KV_WIDTH)
    wv = w_in[:, ATTN_WIDTH + KV_WIDTH:ATTN_WIDTH + 2 * KV_WIDTH]
    wz = w_in[:, ATTN_WIDTH + 2 * KV_WIDTH:].reshape(D_MODEL, N_HEADS, HEAD_DIM)
    wz = _slot_order(wz, 1).reshape(D_MODEL, ATTN_WIDTH)
    w_in_p = jnp.concatenate([wq, wk, wv, wz], axis=1).astype(BF16)
    wo = _slot_order(w_out.reshape(N_HEADS, HEAD_DIM, D_MODEL), 0).reshape(ATTN_WIDTH, D_MODEL)
    sink_p = _slot_order(sink.astype(F32).reshape(N_HEADS), 0).reshape(2 * N_SLOTS)
    sink_tab = jnp.broadcast_to((sink_p * LOG2E)[:, None], (2 * N_SLOTS, LANES))
    return w_in_p, wo.astype(BF16), sink_tab


def _rope_tables():
    inv = ROPE_BASE ** (-np.arange(ROPE_FREQS, dtype=np.float64) / ROPE_FREQS)
    pos = np.arange(SEQ)
    row = (pos // GRID_W)[:, None] * inv
    col = (pos % GRID_W)[:, None] * inv
    w = np.arange(LANES) % HEAD_DIM
    half, axis, f = w // 32, (w % 32) // 16, w % 16
    ang = np.where((axis == 0)[None, :], row[:, f], col[:, f])
    sign = np.where(half == 0, -1.0, 1.0)[None, :]
    cos = np.concatenate([np.ones((CTX_LEN, LANES)), np.cos(ang)], axis=0)
    sin = np.concatenate([np.zeros((CTX_LEN, LANES)), np.sin(ang) * sign], axis=0)
    return jnp.asarray(cos, F32), jnp.asarray(sin, F32)


def _s5_operators(lam_re, lam_im, log_dt, b_re, b_im, c_re, c_im, d_skip):
    t_len = CHUNK
    n_pow = 2 * t_len + 1
    lr, li = lam_re.astype(F32), lam_im.astype(F32)
    dt = jnp.exp(log_dt.astype(F32))[..., None]
    mag = jnp.exp(lr * dt)
    sq = [(mag * jnp.cos(li * dt), mag * jnp.sin(li * dt))]
    while 2 ** len(sq) < n_pow:
        r, i = sq[-1]
        sq.append((r * r - i * i, 2.0 * r * i))
    ks = np.arange(n_pow)
    pr = jnp.ones(lr.shape + (n_pow,), F32)
    pi = jnp.zeros(lr.shape + (n_pow,), F32)
    for bit, (r, i) in enumerate(sq):
        on = jnp.asarray((ks >> bit) & 1 == 1)
        fr = jnp.where(on, r[..., None], 1.0)
        fi = jnp.where(on, i[..., None], 0.0)
        pr, pi = pr * fr - pi * fi, pr * fi + pi * fr
    ar1, ai1 = sq[0][0] - 1.0, sq[0][1]
    den = lr * lr + li * li
    gr, gi = (ar1 * lr + ai1 * li) / den, (ai1 * lr - ar1 * li) / den
    br_, bi_ = b_re.astype(F32), b_im.astype(F32)
    bbr = gr[..., None] * br_ - gi[..., None] * bi_
    bbi = gr[..., None] * bi_ + gi[..., None] * br_
    pad_k = POW_ROWS - n_pow
    prt, pit = jnp.swapaxes(pr, 2, 3), jnp.swapaxes(pi, 2, 3)
    pw = jnp.pad(jnp.concatenate([prt, prt, pit, pit], axis=-1),
                 ((0, 0), (0, 0), (0, pad_k), (0, 0)))
    brt, bit = jnp.swapaxes(bbr, 2, 3), jnp.swapaxes(bbi, 2, 3)
    bb = jnp.concatenate([brt, bit, -bit, brt, bit, brt, brt, -bit], axis=-1)
    cr, ci = c_re.astype(F32), c_im.astype(F32)
    cm = jnp.stack([jnp.concatenate([cr, -ci], axis=-1),
                    jnp.concatenate([-ci, -cr], axis=-1)], axis=2)
    dv = jnp.tile(d_skip.astype(F32).reshape(SSM_GROUPS, 1, SSM_GROUP), (1, 1, t_len))
    return _s5_operator_call(pw, bb, cm, dv)


def _operator_constants():
    t_len = CHUNK
    s_of_row = np.arange(CHUNK_W) // SSM_GROUP
    k_ar = np.arange(POW_ROWS)
    oh_f = (k_ar[None, :] == (t_len - 1 - s_of_row)[:, None]).astype(np.float32)
    oh_b = (k_ar[None, :] == s_of_row[:, None]).astype(np.float32)
    t_of_lane = np.arange(CHUNK_W) // SSM_GROUP
    expo = [s_of_row + 1, t_len - s_of_row, s_of_row, t_len - 1 - s_of_row]
    sel = np.stack([(k_ar[None, :] == e[:, None]) for e in expo]).astype(np.float32)
    h_of_lane = np.arange(CHUNK_W) % SSM_GROUP
    dmask = ((s_of_row[:, None] == t_of_lane[None, :])
             & ((np.arange(CHUNK_W) % SSM_GROUP)[:, None] == h_of_lane[None, :])).astype(np.float32)
    return oh_f, oh_b, sel, dmask


def _s5_op_kernel(*refs):
    for gg in range(OP_GROUP_BATCH):
        _s5_op_group(gg, *refs)


def _s5_op_group(gg, pw_ref, bb_ref, cm_ref, dv_ref, ohf_ref, ohb_ref, sel_ref, dmask_ref,
                 m_ref, ws_ref, wyt_ref, cst_ref):
    t_len = CHUNK

    def split(a):
        hi = a.astype(BF16)
        return hi, (a - hi.astype(F32)).astype(BF16)

    def pick_rows(onehot, table):
        hi, lo = split(table)
        return (jnp.dot(onehot, hi, preferred_element_type=F32)
                + jnp.dot(onehot, lo, preferred_element_type=F32))

    def tile_rows(a):
        return jnp.concatenate([a] * t_len, axis=0)

    main, swapped = [], []
    for d, oh_ref in ((0, ohf_ref), (1, ohb_ref)):
        pp = pick_rows(oh_ref[...], pw_ref[d, gg])
        p_re, p_im = pp[:, :STATE_W], pp[:, STATE_W:]
        b0, b1, b2, b3 = [tile_rows(bb_ref[d, gg, :, STATE_W * i:STATE_W * (i + 1)])
                          for i in range(4)]
        main.append(p_re * b0 + p_im * b1)
        swapped.append(p_re * b2 + p_im * b3)
    ws_ref[gg] = jnp.concatenate(main + swapped, axis=1).astype(BF16)

    c_tiled = [[tile_rows(cm_ref[d, gg, i]) for i in range(2)] for d in range(2)]

    def block_t(d, pat):
        pp = pick_rows(sel_ref[pat], pw_ref[d, gg])
        return pp[:, :STATE_W] * c_tiled[d][0] + pp[:, STATE_W:] * c_tiled[d][1]

    wyt_ref[gg] = jnp.concatenate([block_t(0, 0), block_t(1, 1)], axis=1).astype(BF16)

    nt = (((1,), (1,)), ((), ()))
    kt_f = lax.dot_general(bb_ref[0, gg, :, :STATE_W], block_t(0, 2), nt,
                           preferred_element_type=F32, precision=lax.Precision.HIGHEST)
    kt_b = lax.dot_general(bb_ref[1, gg, :, :STATE_W], block_t(1, 3), nt,
                           preferred_element_type=F32, precision=lax.Precision.HIGHEST)
    lane = lax.broadcasted_iota(jnp.int32, (SSM_GROUP, CHUNK_W), 1)
    skip = dv_ref[gg]
    for s in range(t_len):
        fwd = kt_f if s == 0 else pltpu.roll(kt_f, SSM_GROUP * s, 1)
        back = t_len - 1 - s
        bwd = kt_b if back == 0 else pltpu.roll(kt_b, CHUNK_W - SSM_GROUP * back, 1)
        rows = (jnp.where(lane >= SSM_GROUP * s, fwd, 0.0)
                + jnp.where(lane < SSM_GROUP * (s + 1), bwd, 0.0)
                + dmask_ref[SSM_GROUP * s:SSM_GROUP * (s + 1), :] * skip)
        m_ref[gg, SSM_GROUP * s:SSM_GROUP * (s + 1), :] = rows.astype(BF16)

    even = lax.broadcasted_iota(jnp.int32, (SUBLANES, STATE_W), 0) % 2 == 0
    sign = jnp.where(lax.broadcasted_iota(jnp.int32, (1, STATE_W), 1) < SSM_STATE, -1.0, 1.0)

    def w12(d, k):
        row = pw_ref[d, gg, k:k + 1, :]
        return row[:, :STATE_W], row[:, STATE_W:] * sign

    zero = (jnp.zeros((1, STATE_W), F32),) * 2
    pairs = [(w12(0, t_len), w12(0, 2 * t_len)), (zero, w12(0, t_len)),
             (w12(1, 2 * t_len), w12(1, t_len)), (w12(1, t_len), zero)]
    idx = 0
    for top, bot in pairs:
        for part in range(2):
            cst_ref[gg, idx] = jnp.where(even, jnp.broadcast_to(top[part], (SUBLANES, STATE_W)),
                                         jnp.broadcast_to(bot[part], (SUBLANES, STATE_W)))
            idx += 1


def _s5_operator_call(pw, bb, cm, dv):
    oh_f, oh_b, sel, dmask = _operator_constants()
    consts = [jnp.asarray(oh_f, BF16), jnp.asarray(oh_b, BF16), jnp.asarray(sel, BF16),
              jnp.asarray(dmask)]

    def per_group(*tail):
        n = len(tail)
        return pl.BlockSpec((2, OP_GROUP_BATCH) + tail, lambda g: (0, g) + (0,) * n)

    def whole(a):
        return pl.BlockSpec(a.shape, lambda g: (0,) * a.ndim)

    def out(*tail):
        return pl.BlockSpec((OP_GROUP_BATCH,) + tail, lambda g: (g,) + (0,) * len(tail))

    return pl.pallas_call(
        _s5_op_kernel,
        grid=(SSM_GROUPS // OP_GROUP_BATCH,),
        in_specs=[per_group(POW_ROWS, 2 * STATE_W), per_group(SSM_GROUP, 4 * STATE_W),
                  per_group(2, SSM_GROUP, STATE_W),
                  pl.BlockSpec((OP_GROUP_BATCH, 1, CHUNK_W), lambda g: (g, 0, 0))]
                 + [whole(a) for a in consts],
        out_specs=[out(CHUNK_W, CHUNK_W), out(CHUNK_W, 4 * STATE_W), out(CHUNK_W, 2 * STATE_W),
                   out(8, SUBLANES, STATE_W)],
        out_shape=[jax.ShapeDtypeStruct((SSM_GROUPS, CHUNK_W, CHUNK_W), BF16),
                   jax.ShapeDtypeStruct((SSM_GROUPS, CHUNK_W, 4 * STATE_W), BF16),
                   jax.ShapeDtypeStruct((SSM_GROUPS, CHUNK_W, 2 * STATE_W), BF16),
                   jax.ShapeDtypeStruct((SSM_GROUPS, 8, SUBLANES, STATE_W), F32)],
        compiler_params=_params(("arbitrary",), 32),
        name="s5_operators",
    )(pw, bb, cm, dv, *consts)


def kernel(x, c, ctx, c_ctx, norm_w, w_ada, b_ada, attn_w_in, attn_sink, attn_w_out,
           ssm_w_in, ssm_lam_re, ssm_lam_im, ssm_log_dt, ssm_b_re, ssm_b_im, ssm_c_re, ssm_c_im,
           ssm_d, ssm_w_glu, ssm_w_out, final_norm_w):
    mod0, mod1 = _modulation(c, c_ctx, w_ada, b_ada)

    w_in0, w_out0, sink_tab = _attn_weights(attn_w_in[0], attn_w_out[0], attn_sink[0])
    cos_tab, sin_tab = _rope_tables()
    q, z0, kbd, vt = _proj0(x, ctx, mod0, norm_w[0], cos_tab, sin_tab, w_in0)
    xc1 = _attention(q, z0, kbd, vt, sink_tab, x, ctx, mod0, w_out0)

    u_g, z1 = _proj1(xc1, mod1, norm_w[1], ssm_w_in[0].astype(BF16))
    m_mat, ws_mat, wy_mat, consts = _s5_operators(
        ssm_lam_re[0], ssm_lam_im[0], ssm_log_dt[0], ssm_b_re[0], ssm_b_im[0],
        ssm_c_re[0], ssm_c_im[0], ssm_d[0])
    y_g = _s5_core(u_g, m_mat, ws_mat, wy_mat, consts)
    return _out1(y_g, z1, xc1, mod1, ssm_w_glu[0].astype(BF16), ssm_w_out[0].astype(BF16),
                 final_norm_w)
```

```python
import functools
import math

import jax
import jax.numpy as jnp
import numpy as np
from jax import lax
from jax.experimental import pallas as pl
from jax.experimental.pallas import tpu as pltpu

F32 = jnp.float32
BF16 = jnp.bfloat16

D_MODEL = 1024
BATCH = 4
SEQ = 4096
GRID_W = 64
CTX_LEN = 256
TOTAL = CTX_LEN + SEQ
HEAD_DIM = 64
N_HEADS = 16
N_KV_HEADS = 4
ATTN_WIDTH = N_HEADS * HEAD_DIM
KV_WIDTH = N_KV_HEADS * HEAD_DIM
BLOCK = 128
N_BLOCKS = TOTAL // BLOCK
N_CTX_BLOCKS = CTX_LEN // BLOCK
ROPE_BASE = 10000.0
ROPE_FREQS = HEAD_DIM // 4
SSM_GROUP = 16
SSM_GROUPS = D_MODEL // SSM_GROUP
SSM_STATE = 64
NORM_EPS = 1e-6
NEG_INF = -1e30

LANES = 128
SUBLANES = 8
N_SLOTS = ATTN_WIDTH // LANES
N_KV_PAIRS = KV_WIDTH // LANES
SLOTS_PER_M = N_SLOTS // N_KV_PAIRS
ATTN_QBLOCKS = 2
assert N_CTX_BLOCKS % ATTN_QBLOCKS == 0 and N_BLOCKS % ATTN_QBLOCKS == 0
UNIT_SLOTS = 4
LOG2E = math.log2(math.e)
Q_SCALE = HEAD_DIM ** -0.5 * LOG2E
ROW_TILE = 256
N_ROW_TILES = TOTAL // ROW_TILE
CHUNK = 16
N_CHUNKS = TOTAL // CHUNK
N_CTX_CHUNKS = CTX_LEN // CHUNK
CHUNK_W = CHUNK * SSM_GROUP
CHUNKS_PER_TILE = ROW_TILE // CHUNK
PAIRS_PER_TILE = CHUNKS_PER_TILE // 2
SLOT_PITCH = 24
OUT_PHASES = 4
assert SLOT_PITCH >= CHUNK and SLOT_PITCH % SUBLANES == 0
STATE_W = 2 * SSM_STATE
SCAN_ROWS = N_CHUNKS * BATCH
CTX_SCAN_ROWS = N_CTX_CHUNKS * BATCH
N_SCAN_BLOCKS = SCAN_ROWS // SUBLANES
N_CTX_SCAN_BLOCKS = CTX_SCAN_ROWS // SUBLANES
GROUP_BATCH = 4
OP_GROUP_BATCH = 4
POW_ROWS = 48

assert BATCH * 2 == SUBLANES


def _params(semantics, vmem_mb):
    return pltpu.CompilerParams(dimension_semantics=semantics,
                                vmem_limit_bytes=vmem_mb * 1024 * 1024)


def _mod_kernel(c_ref, w_ref, b_ref, o_ref):
    c = c_ref[...]
    a = c * jax.nn.sigmoid(c)
    o_ref[0] = jnp.dot(a, w_ref[0], preferred_element_type=F32,
                       precision=lax.Precision.HIGHEST) + b_ref[0]


def _modulation(c, c_ctx, w_ada, b_ada):
    depth = w_ada.shape[0]
    rows = jnp.zeros((SUBLANES, D_MODEL), F32).at[:BATCH].set(c).at[BATCH].set(c_ctx)
    n_col = 3
    out = pl.pallas_call(
        _mod_kernel,
        grid=(depth, n_col),
        in_specs=[
            pl.BlockSpec((SUBLANES, D_MODEL), lambda l, j: (0, 0)),
            pl.BlockSpec((1, D_MODEL, D_MODEL), lambda l, j: (l, 0, j)),
            pl.BlockSpec((1, 1, D_MODEL), lambda l, j: (l, 0, j)),
        ],
        out_specs=pl.BlockSpec((1, SUBLANES, D_MODEL), lambda l, j: (l, 0, j)),
        out_shape=jax.ShapeDtypeStruct((depth, SUBLANES, 3 * D_MODEL), F32),
        compiler_params=_params(("arbitrary", "arbitrary"), 32),
        name="adaln_modulation",
    )(rows, w_ada, b_ada.reshape(depth, 1, 3 * D_MODEL))
    tabs = []
    for l in range(depth):
        lat = out[l, :BATCH].reshape(BATCH, 3, D_MODEL)
        cx = jnp.broadcast_to(out[l, BATCH].reshape(1, 3, D_MODEL), (BATCH, 3, D_MODEL))
        tab = jnp.stack([cx, lat], axis=1)
        tabs.append(jnp.pad(tab, ((0, 0), (0, 0), (0, SUBLANES - 3), (0, 0))))
    return tabs


def _modulated_norm(xt, nw, mod_ref, b=0):
    ms = jnp.mean(xt * xt, axis=-1, keepdims=True)
    y = xt * lax.rsqrt(ms + NORM_EPS) * nw
    return y * (1.0 + mod_ref[b, 0, 1:2, :]) + mod_ref[b, 0, 0:1, :]


def _lane_block_transpose(vs):
    n = len(vs)
    width = LANES // n
    blk = lax.broadcasted_iota(jnp.int32, vs[0].shape, 1) // width
    x = list(vs)
    d = n // 2
    while d >= 1:
        clear = (blk & d) == 0
        y = list(x)
        for i in range(n):
            if i & d == 0:
                a, b = x[i], x[i + d]
                y[i] = jnp.where(clear, a, pltpu.roll(b, width * d, 1))
                y[i + d] = jnp.where(clear, pltpu.roll(a, LANES - width * d, 1), b)
        x = y
        d //= 2
    return x


def _chunk_slot(b, cc):
    return (cc // 2) * (2 * BATCH) + 2 * b + (cc % 2)


def _proj0_kernel(x_ref, c_ref, mod_ref, nw_ref, cos_ref, sin_ref, w_ref,
                  q_ref, z_ref, kbd_ref, vt_ref):
    is_ctx = pl.program_id(0) == 0
    h = jnp.concatenate(
        [_modulated_norm(jnp.where(is_ctx, c_ref[b], x_ref[b]), nw_ref[...], mod_ref, b).astype(BF16)
         for b in range(BATCH)], axis=0)
    cos = cos_ref[...]
    sin = sin_ref[...]
    lane = lax.broadcasted_iota(jnp.int32, (ROW_TILE, LANES), 1)
    first_half = (lane % HEAD_DIM) < (HEAD_DIM // 2)
    low = lax.broadcasted_iota(jnp.int32, (BLOCK, LANES), 1) < HEAD_DIM

    def rope(t):
        partner = jnp.where(first_half, pltpu.roll(t, LANES - HEAD_DIM // 2, 1),
                            pltpu.roll(t, HEAD_DIM // 2, 1))
        return t * cos + partner * sin

    q = jnp.dot(h, w_ref[:, :ATTN_WIDTH], preferred_element_type=F32)
    k = jnp.dot(h, w_ref[:, ATTN_WIDTH:ATTN_WIDTH + KV_WIDTH], preferred_element_type=F32)
    v = jnp.dot(h, w_ref[:, ATTN_WIDTH + KV_WIDTH:ATTN_WIDTH + 2 * KV_WIDTH],
                preferred_element_type=F32)
    z = jnp.dot(h, w_ref[:, ATTN_WIDTH + 2 * KV_WIDTH:], preferred_element_type=F32)
    for b in range(BATCH):
        tile = slice(ROW_TILE * b, ROW_TILE * (b + 1))
        z_ref[b] = z[tile].astype(BF16)
        for j in range(N_SLOTS):
            m, gi = divmod(j, SLOTS_PER_M)
            qj = (rope(q[tile, LANES * j:LANES * (j + 1)]) * Q_SCALE).astype(BF16)
            for blk in range(ROW_TILE // BLOCK):
                q_ref[b, blk, m, BLOCK * gi:BLOCK * (gi + 1), :] = qj[BLOCK * blk:BLOCK * (blk + 1)]
        for m in range(N_KV_PAIRS):
            sl = slice(LANES * m, LANES * (m + 1))
            kr = rope(k[tile, sl])
            vm = v[tile, sl]
            for blk in range(ROW_TILE // BLOCK):
                rows = slice(BLOCK * blk, BLOCK * (blk + 1))
                kbd_ref[b, blk, m, :BLOCK, :] = jnp.where(low, kr[rows], 0.0).astype(BF16)
                kbd_ref[b, blk, m, BLOCK:, :] = jnp.where(low, 0.0, kr[rows]).astype(BF16)
                vt_ref[b, blk, LANES * m:LANES * (m + 1), :] = vm[rows].T.astype(BF16)


def _proj0(x, ctx, mod, norm_w, cos_tab, sin_tab, w_in):
    n_col = w_in.shape[1]
    blocks_per_tile = ROW_TILE // BLOCK
    kv_shape = jax.ShapeDtypeStruct((BATCH, N_BLOCKS, N_KV_PAIRS, 2 * BLOCK, LANES), BF16)
    kv_spec = pl.BlockSpec((BATCH, blocks_per_tile, N_KV_PAIRS, 2 * BLOCK, LANES),
                           lambda i: (0, i, 0, 0, 0))
    row_spec = pl.BlockSpec((BATCH, ROW_TILE, D_MODEL), lambda i: (0, i, 0))
    once = pl.Buffered(1)
    return pl.pallas_call(
        _proj0_kernel,
        grid=(N_ROW_TILES,),
        in_specs=[
            pl.BlockSpec((BATCH, ROW_TILE, D_MODEL), lambda i: (0, jnp.maximum(i - 1, 0), 0)),
            pl.BlockSpec((BATCH, ROW_TILE, D_MODEL), lambda i: (0, 0, 0), pipeline_mode=once),
            pl.BlockSpec((BATCH, 1, SUBLANES, D_MODEL), lambda i: (0, jnp.minimum(i, 1), 0, 0)),
            pl.BlockSpec((1, D_MODEL), lambda i: (0, 0)),
            pl.BlockSpec((ROW_TILE, LANES), lambda i: (i, 0)),
            pl.BlockSpec((ROW_TILE, LANES), lambda i: (i, 0)),
            pl.BlockSpec((D_MODEL, n_col), lambda i: (0, 0), pipeline_mode=once),
        ],
        out_specs=[
            pl.BlockSpec((BATCH, blocks_per_tile, N_KV_PAIRS, SLOTS_PER_M * BLOCK, LANES),
                         lambda i: (0, i, 0, 0, 0)),
            row_spec, kv_spec,
            pl.BlockSpec((BATCH, blocks_per_tile, KV_WIDTH, BLOCK), lambda i: (0, i, 0, 0))],
        out_shape=[
            jax.ShapeDtypeStruct((BATCH, N_BLOCKS, N_KV_PAIRS, SLOTS_PER_M * BLOCK, LANES), BF16),
            jax.ShapeDtypeStruct((BATCH, TOTAL, ATTN_WIDTH), BF16),
            kv_shape,
            jax.ShapeDtypeStruct((BATCH, N_BLOCKS, KV_WIDTH, BLOCK), BF16),
        ],
        compiler_params=_params(("arbitrary",), 56),
        name="attn_projection",
    )(x, ctx, mod, norm_w.reshape(1, D_MODEL), cos_tab, sin_tab, w_in)


def _attn_kernel(q_ref, z_ref, k0_ref, k1_ref, k2_ref, k3_ref, kx_ref,
                 v0_ref, v1_ref, v2_ref, v3_ref, vx_ref,
                 sink_ref, x_ref, c_ref, mod_ref, wo_ref, eye_ref, tri_ref, o_ref):
    step = pl.program_id(1)
    is_lat = step >= N_CTX_BLOCKS // ATTN_QBLOCKS
    n_first = ATTN_QBLOCKS * step - N_CTX_BLOCKS
    q_rows = UNIT_SLOTS * BLOCK
    blocked = tri_ref[2]
    bias = []
    for qb in range(ATTN_QBLOCKS):
        n = n_first + qb
        bias.append([jnp.where(jnp.logical_and(is_lat, n >= 1), tri_ref[0], blocked),
                     jnp.where(is_lat, tri_ref[3], blocked),
                     jnp.where(jnp.logical_and(is_lat, n <= SEQ // BLOCK - 2), tri_ref[1], blocked),
                     None, None])
    k_win = (k0_ref, k1_ref, k2_ref, k3_ref)
    v_win = (v0_ref, v1_ref, v2_ref, v3_ref)

    def kpiece(qb, p, m):
        if p < 3:
            return k_win[qb + p][0, 0, m], v_win[qb + p][0, 0]
        return kx_ref[0, p - 3, m], vx_ref[0, p - 3]

    n_piece = 3 + N_CTX_BLOCKS
    units = [(qb, m, h) for qb in range(ATTN_QBLOCKS) for m in range(N_KV_PAIRS)
             for h in range(SLOTS_PER_M // UNIT_SLOTS)]

    nt = (((1,), (1,)), ((), ()))

    half_rows = slice(0, BLOCK), slice(BLOCK, 2 * BLOCK)

    def scores(qb, m, h):
        qu = q_ref[0, qb, m, q_rows * h:q_rows * (h + 1), :]
        qu_masked = jnp.concatenate([qu, eye_ref[...]], axis=1)
        s_list = []
        for p in range(n_piece):
            kbd, _ = kpiece(qb, p, m)
            if bias[qb][p] is None:
                s = lax.dot_general(kbd, qu, nt, preferred_element_type=F32)
            else:
                s = lax.dot_general(jnp.concatenate([kbd, bias[qb][p]], axis=1), qu_masked, nt,
                                    preferred_element_type=F32)
            s_list.append(s)
        return s_list

    def finish(qb, m, h, s_list, y):
        slot0 = SLOTS_PER_M * m + UNIT_SLOTS * h
        halves = []
        for hs in range(2):
            sink = jnp.concatenate(
                [sink_ref[2 * (slot0 + gi) + hs:2 * (slot0 + gi) + hs + 1, :]
                 for gi in range(UNIT_SLOTS)], axis=1)
            mx = sink
            for s in s_list:
                mx = jnp.maximum(mx, jnp.max(s[half_rows[hs]], axis=0, keepdims=True))
            probs = [jnp.exp2(s[half_rows[hs]] - mx) for s in s_list]
            denom = jnp.exp2(sink - mx)
            for e in probs:
                denom = denom + jnp.sum(e, axis=0, keepdims=True)
            probs = [e.astype(BF16) for e in probs]
            kv_head = 2 * m + hs
            acc = None
            for p0 in range(0, n_piece, 2):
                group = list(range(p0, min(p0 + 2, n_piece)))
                vt = jnp.concatenate(
                    [kpiece(qb, p, m)[1][HEAD_DIM * kv_head:HEAD_DIM * (kv_head + 1), :]
                     for p in group], axis=1)
                pt = jnp.concatenate([probs[p] for p in group], axis=0)
                part = jnp.dot(vt, pt, preferred_element_type=F32)
                acc = part if acc is None else acc + part
            halves.append(acc * (1.0 / denom))
        o_t = jnp.concatenate(halves, axis=0)
        outs = []
        for gi in range(UNIT_SLOTS):
            j = slot0 + gi
            o = o_t[:, BLOCK * gi:BLOCK * (gi + 1)].T
            zj = z_ref[0, BLOCK * qb:BLOCK * (qb + 1), LANES * j:LANES * (j + 1)].astype(F32)
            outs.append((o * (zj * jax.nn.sigmoid(zj))).astype(BF16))
        g = jnp.concatenate(outs, axis=1)
        part = jnp.dot(g, wo_ref[LANES * slot0:LANES * (slot0 + UNIT_SLOTS), :],
                       preferred_element_type=F32)
        return part if y is None else y + part

    y = [None] * ATTN_QBLOCKS
    pending = scores(*units[0])
    for u, (qb, m, h) in enumerate(units):
        nxt = scores(*units[u + 1]) if u + 1 < len(units) else None
        y[qb] = finish(qb, m, h, pending, y[qb])
        pending = nxt
    resid = jnp.where(is_lat, x_ref[0], c_ref[0])
    gate = mod_ref[0, 0, 2:3, :]
    for qb in range(ATTN_QBLOCKS):
        rows = slice(BLOCK * qb, BLOCK * (qb + 1))
        o_ref[0, rows, :] = resid[rows] + gate * y[qb]


def _attention(q, z, kbd, vt, sink_tab, x, ctx, mod, w_out):
    last = N_BLOCKS - 1
    n_m = KV_WIDTH // LANES
    nq = ATTN_QBLOCKS
    ctx_steps = N_CTX_BLOCKS // nq
    row_spec = pl.BlockSpec((1, nq * BLOCK, ATTN_WIDTH), lambda b, i: (b, i, 0))

    def kv_spec(off):
        return pl.BlockSpec((1, 1, n_m, 2 * BLOCK, LANES),
                            lambda b, i: (b, jnp.clip(nq * i + off, 0, last), 0, 0, 0))

    ctx_kv_spec = pl.BlockSpec((1, N_CTX_BLOCKS, n_m, 2 * BLOCK, LANES),
                               lambda b, i: (b, 0, 0, 0, 0))

    def vt_spec(off):
        return pl.BlockSpec((1, 1, KV_WIDTH, BLOCK),
                            lambda b, i: (b, jnp.clip(nq * i + off, 0, last), 0, 0))

    ctx_vt_spec = pl.BlockSpec((1, N_CTX_BLOCKS, KV_WIDTH, BLOCK), lambda b, i: (b, 0, 0, 0))
    off = np.arange(BLOCK)
    eye = np.tile(np.eye(BLOCK, dtype=np.float32), (UNIT_SLOTS, 1))
    key_ge = np.where(off[:, None] >= off[None, :], 0.0, NEG_INF)
    key_le = np.where(off[:, None] <= off[None, :], 0.0, NEG_INF)
    tri = np.stack([np.tile(t, (2, 1)) for t in
                    (key_ge, key_le, np.full((BLOCK, BLOCK), NEG_INF), np.zeros((BLOCK, BLOCK)))])
    consts = [jnp.asarray(a, BF16) for a in (eye, tri)]
    return pl.pallas_call(
        _attn_kernel,
        grid=(BATCH, N_BLOCKS // nq),
        in_specs=[
            pl.BlockSpec((1, nq, n_m, SLOTS_PER_M * BLOCK, LANES), lambda b, i: (b, i, 0, 0, 0)),
            row_spec,
            kv_spec(-1), kv_spec(0), kv_spec(1), kv_spec(2), ctx_kv_spec,
            vt_spec(-1), vt_spec(0), vt_spec(1), vt_spec(2), ctx_vt_spec,
            pl.BlockSpec((2 * N_SLOTS, LANES), lambda b, i: (0, 0)),
            pl.BlockSpec((1, nq * BLOCK, D_MODEL),
                         lambda b, i: (b, jnp.maximum(i - ctx_steps, 0), 0)),
            pl.BlockSpec((1, nq * BLOCK, D_MODEL),
                         lambda b, i: (b, jnp.minimum(i, ctx_steps - 1), 0)),
            pl.BlockSpec((1, 1, SUBLANES, D_MODEL),
                         lambda b, i: (b, jnp.minimum(i // ctx_steps, 1), 0, 0)),
            pl.BlockSpec((ATTN_WIDTH, D_MODEL), lambda b, i: (0, 0)),
        ] + [pl.BlockSpec(a.shape, lambda b, i, nd=a.ndim: (0,) * nd) for a in consts],
        out_specs=pl.BlockSpec((1, nq * BLOCK, D_MODEL), lambda b, i: (b, i, 0)),
        out_shape=jax.ShapeDtypeStruct((BATCH, TOTAL, D_MODEL), F32),
        compiler_params=_params(("arbitrary", "arbitrary"), 48),
        name="window_attention",
    )(q, z, kbd, kbd, kbd, kbd, kbd, vt, vt, vt, vt, vt, sink_tab, x, ctx, mod, w_out,
      *consts)


def _proj1_kernel(x_ref, mod_ref, nw_ref, w_ref, u_ref, z_ref, scr_ref):
    h = jnp.concatenate(
        [_modulated_norm(x_ref[b], nw_ref[...], mod_ref, b).astype(BF16) for b in range(BATCH)],
        axis=0)
    u = jnp.dot(h, w_ref[:, :D_MODEL], preferred_element_type=F32)
    z = jnp.dot(h, w_ref[:, D_MODEL:], preferred_element_type=F32)
    for b in range(BATCH):
        z_ref[b] = z[ROW_TILE * b:ROW_TILE * (b + 1)].astype(BF16)
    n_slab = D_MODEL // LANES
    for b in range(BATCH):
        for cc in range(CHUNKS_PER_TILE):
            r0 = ROW_TILE * b + CHUNK * cc
            q0 = SLOT_PITCH * _chunk_slot(b, cc)
            for k in range(n_slab):
                scr_ref[k, q0:q0 + CHUNK, :] = u[r0:r0 + CHUNK, LANES * k:LANES * (k + 1)]
    groups_per_slab = LANES // SSM_GROUP
    for k in range(n_slab):
        for pp in range(PAIRS_PER_TILE // 2):
            parts = []
            for p in (2 * pp, 2 * pp + 1):
                rows = [scr_ref[k, pl.ds(SLOT_PITCH * SUBLANES * p + s, SUBLANES,
                                         stride=SLOT_PITCH), :] for s in range(CHUNK)]
                parts.append([_lane_block_transpose(rows[SUBLANES * m2:SUBLANES * (m2 + 1)])
                              for m2 in range(CHUNK // SUBLANES)])
            for m2 in range(CHUNK // SUBLANES):
                for gl in range(groups_per_slab):
                    val = jnp.concatenate([parts[0][m2][gl], parts[1][m2][gl]], axis=0)
                    u_ref[groups_per_slab * k + gl, 2 * SUBLANES * pp:2 * SUBLANES * (pp + 1),
                          LANES * m2:LANES * (m2 + 1)] = val.astype(BF16)


def _proj1(xc, mod, norm_w, w_in):
    row_spec = pl.BlockSpec((BATCH, ROW_TILE, D_MODEL), lambda i: (0, i, 0))
    tile_rows = PAIRS_PER_TILE * SUBLANES
    return pl.pallas_call(
        _proj1_kernel,
        grid=(N_ROW_TILES,),
        in_specs=[
            row_spec,
            pl.BlockSpec((BATCH, 1, SUBLANES, D_MODEL), lambda i: (0, jnp.minimum(i, 1), 0, 0)),
            pl.BlockSpec((1, D_MODEL), lambda i: (0, 0)),
            pl.BlockSpec((D_MODEL, 2 * D_MODEL), lambda i: (0, 0)),
        ],
        out_specs=[pl.BlockSpec((SSM_GROUPS, tile_rows, CHUNK_W), lambda i: (0, i, 0)), row_spec],
        out_shape=[jax.ShapeDtypeStruct((SSM_GROUPS, SCAN_ROWS, CHUNK_W), BF16),
                   jax.ShapeDtypeStruct((BATCH, TOTAL, D_MODEL), BF16)],
        scratch_shapes=[pltpu.VMEM((D_MODEL // LANES, BATCH * CHUNKS_PER_TILE * SLOT_PITCH, LANES),
                                   F32)],
        compiler_params=_params(("arbitrary",), 56),
        name="ssm_projection",
    )(xc, mod, norm_w.reshape(1, D_MODEL), w_in)


def _s5_kernel(u_ref, m_ref, ws_ref, wy_ref, cst_ref, y_ref, s4_ref, xp_ref):
    for g in range(GROUP_BATCH):
        s4_ref[g] = jnp.dot(u_ref[g], ws_ref[g], preferred_element_type=F32)
    even = lax.broadcasted_iota(jnp.int32, (SUBLANES, STATE_W), 0) % 2 == 0
    down = 1
    up = SUBLANES - 1
    fwd = slice(0, STATE_W)
    bwd = slice(STATE_W, 2 * STATE_W)
    fwd_sw = slice(2 * STATE_W, 3 * STATE_W)
    bwd_sw = slice(3 * STATE_W, 4 * STATE_W)

    def step(j, carry):
        jb = jnp.where(j < N_CTX_SCAN_BLOCKS, N_CTX_SCAN_BLOCKS - 1 - j,
                       N_SCAN_BLOCKS - 1 + N_CTX_SCAN_BLOCKS - j)
        rf = pl.ds(pl.multiple_of(j * SUBLANES, SUBLANES), SUBLANES)
        rb = pl.ds(pl.multiple_of(jb * SUBLANES, SUBLANES), SUBLANES)
        new = []
        for g in range(GROUP_BATCH):
            cf, cfs, cb, cbs = carry[4 * g:4 * g + 4]
            p1f, p2f, q1f, q2f = cst_ref[g, 0], cst_ref[g, 1], cst_ref[g, 2], cst_ref[g, 3]
            p1b, p2b, q1b, q2b = cst_ref[g, 4], cst_ref[g, 5], cst_ref[g, 6], cst_ref[g, 7]
            zf = s4_ref[g, rf, fwd]
            zfs = s4_ref[g, rf, fwd_sw]
            rzf = pltpu.roll(zf, down, 0)
            rzfs = pltpu.roll(zfs, down, 0)
            xf = p1f * cf + p2f * cfs + (zf + q1f * rzf + q2f * rzfs)
            xfs = p1f * cfs - p2f * cf + (zfs + q1f * rzfs - q2f * rzf)
            xp_ref[g, rf, fwd] = jnp.where(even, cf, pltpu.roll(xf, down, 0))
            new += [jnp.where(even, pltpu.roll(xf, up, 0), xf),
                    jnp.where(even, pltpu.roll(xfs, up, 0), xfs)]
            zb = s4_ref[g, rb, bwd]
            zbs = s4_ref[g, rb, bwd_sw]
            rzb = pltpu.roll(zb, up, 0)
            rzbs = pltpu.roll(zbs, up, 0)
            xb = p1b * cb + p2b * cbs + (zb + q1b * rzb + q2b * rzbs)
            xbs = p1b * cbs - p2b * cb + (zbs + q1b * rzbs - q2b * rzb)
            xp_ref[g, rb, bwd] = jnp.where(even, pltpu.roll(xb, up, 0), cb)
            new += [jnp.where(even, xb, pltpu.roll(xb, down, 0)),
                    jnp.where(even, xbs, pltpu.roll(xbs, down, 0))]
        return tuple(new)

    zero = jnp.zeros((SUBLANES, STATE_W), F32)
    lax.fori_loop(0, N_SCAN_BLOCKS, step, (zero,) * (4 * GROUP_BATCH))
    for g in range(GROUP_BATCH):
        y_ref[g] = (
            jnp.dot(u_ref[g, CTX_SCAN_ROWS:, :], m_ref[g], preferred_element_type=F32)
            + lax.dot_general(xp_ref[g, CTX_SCAN_ROWS:, :].astype(BF16), wy_ref[g],
                              (((1,), (1,)), ((), ())), preferred_element_type=F32))


def _s5_core(u_g, m_mat, ws_mat, wy_mat, consts):
    lat_rows = SCAN_ROWS - CTX_SCAN_ROWS

    def gspec(*tail):
        return pl.BlockSpec((GROUP_BATCH,) + tail, lambda i: (i,) + (0,) * len(tail))

    return pl.pallas_call(
        _s5_kernel,
        grid=(SSM_GROUPS // GROUP_BATCH,),
        in_specs=[
            gspec(SCAN_ROWS, CHUNK_W),
            gspec(CHUNK_W, CHUNK_W),
            gspec(CHUNK_W, 4 * STATE_W),
            gspec(CHUNK_W, 2 * STATE_W),
            gspec(8, SUBLANES, STATE_W),
        ],
        out_specs=gspec(lat_rows, CHUNK_W),
        out_shape=jax.ShapeDtypeStruct((SSM_GROUPS, lat_rows, CHUNK_W), F32),
        scratch_shapes=[
            pltpu.VMEM((GROUP_BATCH, SCAN_ROWS, 4 * STATE_W), F32),
            pltpu.VMEM((GROUP_BATCH, SCAN_ROWS, 2 * STATE_W), F32),
        ],
        compiler_params=_params(("arbitrary",), 48),
        name="s5_scan",
    )(u_g, m_mat, ws_mat, wy_mat, consts)


def _out1_kernel(y_ref, z_ref, x_ref, mod_ref, wg_ref, wo_ref, fnw_ref, o_ref, scr_ref):
    n_slab = D_MODEL // LANES
    groups_per_slab = LANES // SSM_GROUP
    pairs_per_phase = PAIRS_PER_TILE // OUT_PHASES
    rows_per_phase = ROW_TILE // OUT_PHASES

    def relayout(ph):
        for k in range(n_slab):
            for p in range(pairs_per_phase * ph, pairs_per_phase * (ph + 1)):
                for m2 in range(CHUNK // SUBLANES):
                    vals = [y_ref[groups_per_slab * k + gl, SUBLANES * p:SUBLANES * (p + 1),
                                  LANES * m2:LANES * (m2 + 1)] for gl in range(groups_per_slab)]
                    steps = _lane_block_transpose(vals)
                    for s2 in range(SUBLANES):
                        t_idx = SUBLANES * m2 + s2
                        scr_ref[k, pl.ds(SLOT_PITCH * SUBLANES * p + t_idx, SUBLANES,
                                         stride=SLOT_PITCH), :] = steps[s2]

    def compute(ph):
        chunks = range(2 * pairs_per_phase * ph, 2 * pairs_per_phase * (ph + 1))
        y = jnp.concatenate(
            [jnp.concatenate(
                [scr_ref[k, SLOT_PITCH * _chunk_slot(b, cc):SLOT_PITCH * _chunk_slot(b, cc) + CHUNK, :]
                 for k in range(n_slab)], axis=1)
             for b in range(BATCH) for cc in chunks], axis=0)
        g = (0.5 * y * (1.0 + lax.erf(y * (2.0 ** -0.5)))).astype(BF16)
        t = jnp.dot(g, wg_ref[...], preferred_element_type=F32)
        rows = slice(rows_per_phase * ph, rows_per_phase * (ph + 1))
        z = jnp.concatenate([z_ref[b, rows, :] for b in range(BATCH)], axis=0).astype(F32)
        r = (t[:, :D_MODEL] * jax.nn.sigmoid(t[:, D_MODEL:]) * (z * jax.nn.sigmoid(z))).astype(BF16)
        o = jnp.dot(r, wo_ref[...], preferred_element_type=F32)
        for b in range(BATCH):
            x2 = (x_ref[b, rows, :]
                  + mod_ref[b, 0, 2:3, :] * o[rows_per_phase * b:rows_per_phase * (b + 1)])
            ms = jnp.mean(x2 * x2, axis=-1, keepdims=True)
            o_ref[b, rows, :] = x2 * lax.rsqrt(ms + NORM_EPS) * fnw_ref[...]

    relayout(0)
    for ph in range(OUT_PHASES):
        if ph + 1 < OUT_PHASES:
            relayout(ph + 1)
        compute(ph)


def _out1(y_g, z, xc, mod, w_glu, w_out, final_norm_w):
    ctx_tiles = CTX_LEN // ROW_TILE
    tile_rows = PAIRS_PER_TILE * SUBLANES
    lat_spec = pl.BlockSpec((BATCH, ROW_TILE, D_MODEL), lambda i: (0, i, 0))
    all_spec = pl.BlockSpec((BATCH, ROW_TILE, D_MODEL), lambda i: (0, i + ctx_tiles, 0))
    return pl.pallas_call(
        _out1_kernel,
        grid=(SEQ // ROW_TILE,),
        in_specs=[
            pl.BlockSpec((SSM_GROUPS, tile_rows, CHUNK_W), lambda i: (0, i, 0)),
            all_spec, all_spec,
            pl.BlockSpec((BATCH, 1, SUBLANES, D_MODEL), lambda i: (0, 1, 0, 0)),
            pl.BlockSpec((D_MODEL, 2 * D_MODEL), lambda i: (0, 0)),
            pl.BlockSpec((D_MODEL, D_MODEL), lambda i: (0, 0)),
            pl.BlockSpec((1, D_MODEL), lambda i: (0, 0)),
        ],
        out_specs=lat_spec,
        out_shape=jax.ShapeDtypeStruct((BATCH, SEQ, D_MODEL), F32),
        scratch_shapes=[pltpu.VMEM((D_MODEL // LANES, BATCH * CHUNKS_PER_TILE * SLOT_PITCH, LANES),
                                   F32)],
        compiler_params=_params(("arbitrary",), 56),
        name="ssm_output",
    )(y_g, z, xc, mod, w_glu, w_out, final_norm_w.reshape(1, D_MODEL))


def _slot_order(t, lead):
    n_m = KV_WIDTH // LANES
    gq = N_HEADS // N_KV_HEADS
    shape = t.shape
    t = t.reshape(shape[:lead] + (n_m, 2, gq) + shape[lead + 1:])
    perm = tuple(range(lead)) + (lead, lead + 2, lead + 1) + tuple(range(lead + 3, t.ndim))
    return jnp.transpose(t, perm)


def _rope_order(t):
    shape = t.shape
    t = t.reshape(shape[:-1] + (2, 2, ROPE_FREQS))
    return jnp.swapaxes(t, -3, -2).reshape(shape)


def _attn_weights(w_in, w_out, sink):
    wq = w_in[:, :ATTN_WIDTH].reshape(D_MODEL, N_HEADS, HEAD_DIM)
    wq = _slot_order(_rope_order(wq), 1).reshape(D_MODEL, ATTN_WIDTH)
    wk = w_in[:, ATTN_WIDTH:ATTN_WIDTH + KV_WIDTH].reshape(D_MODEL, N_KV_HEADS, HEAD_DIM)
    wk = _rope_order(wk).reshape(D_MODEL, KV_WIDTH)
    wv = w_in[:, ATTN_WIDTH + KV_WIDTH:ATTN_WIDTH + 2 * KV_WIDTH]
    wz = w_in[:, ATTN_WIDTH + 2 * KV_WIDTH:].reshape(D_MODEL, N_HEADS, HEAD_DIM)
    wz = _slot_order(wz, 1).reshape(D_MODEL, ATTN_WIDTH)
    w_in_p = jnp.concatenate([wq, wk, wv, wz], axis=1).astype(BF16)
    wo = _slot_order(w_out.reshape(N_HEADS, HEAD_DIM, D_MODEL), 0).reshape(ATTN_WIDTH, D_MODEL)
    sink_p = _slot_order(sink.astype(F32).reshape(N_HEADS), 0).reshape(2 * N_SLOTS)
    sink_tab = jnp.broadcast_to((sink_p * LOG2E)[:, None], (2 * N_SLOTS, LANES))
    return w_in_p, wo.astype(BF16), sink_tab


def _rope_tables():
    inv = ROPE_BASE ** (-np.arange(ROPE_FREQS, dtype=np.float64) / ROPE_FREQS)
    pos = np.arange(SEQ)
    row = (pos // GRID_W)[:, None] * inv
    col = (pos % GRID_W)[:, None] * inv
    w = np.arange(LANES) % HEAD_DIM
    half, axis, f = w // 32, (w % 32) // 16, w % 16
    ang = np.where((axis == 0)[None, :], row[:, f], col[:, f])
    sign = np.where(half == 0, -1.0, 1.0)[None, :]
    cos = np.concatenate([np.ones((CTX_LEN, LANES)), np.cos(ang)], axis=0)
    sin = np.concatenate([np.zeros((CTX_LEN, LANES)), np.sin(ang) * sign], axis=0)
    return jnp.asarray(cos, F32), jnp.asarray(sin, F32)


def _s5_operators(lam_re, lam_im, log_dt, b_re, b_im, c_re, c_im, d_skip):
    t_len = CHUNK
    n_pow = 2 * t_len + 1
    lr, li = lam_re.astype(F32), lam_im.astype(F32)
    dt = jnp.exp(log_dt.astype(F32))[..., None]
    mag = jnp.exp(lr * dt)
    sq = [(mag * jnp.cos(li * dt), mag * jnp.sin(li * dt))]
    while 2 ** len(sq) < n_pow:
        r, i = sq[-1]
        sq.append((r * r - i * i, 2.0 * r * i))
    ks = np.arange(n_pow)
    pr = jnp.ones(lr.shape + (n_pow,), F32)
    pi = jnp.zeros(lr.shape + (n_pow,), F32)
    for bit, (r, i) in enumerate(sq):
        on = jnp.asarray((ks >> bit) & 1 == 1)
        fr = jnp.where(on, r[..., None], 1.0)
        fi = jnp.where(on, i[..., None], 0.0)
        pr, pi = pr * fr - pi * fi, pr * fi + pi * fr
    ar1, ai1 = sq[0][0] - 1.0, sq[0][1]
    den = lr * lr + li * li
    gr, gi = (ar1 * lr + ai1 * li) / den, (ai1 * lr - ar1 * li) / den
    br_, bi_ = b_re.astype(F32), b_im.astype(F32)
    bbr = gr[..., None] * br_ - gi[..., None] * bi_
    bbi = gr[..., None] * bi_ + gi[..., None] * br_
    pad_k = POW_ROWS - n_pow
    prt, pit = jnp.swapaxes(pr, 2, 3), jnp.swapaxes(pi, 2, 3)
    pw = jnp.pad(jnp.concatenate([prt, prt, pit, pit], axis=-1),
                 ((0, 0), (0, 0), (0, pad_k), (0, 0)))
    brt, bit = jnp.swapaxes(bbr, 2, 3), jnp.swapaxes(bbi, 2, 3)
    bb = jnp.concatenate([brt, bit, -bit, brt, bit, brt, brt, -bit], axis=-1)
    cr, ci = c_re.astype(F32), c_im.astype(F32)
    cm = jnp.stack([jnp.concatenate([cr, -ci], axis=-1),
                    jnp.concatenate([-ci, -cr], axis=-1)], axis=2)
    dv = jnp.tile(d_skip.astype(F32).reshape(SSM_GROUPS, 1, SSM_GROUP), (1, 1, t_len))
    return _s5_operator_call(pw, bb, cm, dv)


def _operator_constants():
    t_len = CHUNK
    s_of_row = np.arange(CHUNK_W) // SSM_GROUP
    k_ar = np.arange(POW_ROWS)
    oh_f = (k_ar[None, :] == (t_len - 1 - s_of_row)[:, None]).astype(np.float32)
    oh_b = (k_ar[None, :] == s_of_row[:, None]).astype(np.float32)
    t_of_lane = np.arange(CHUNK_W) // SSM_GROUP
    expo = [s_of_row + 1, t_len - s_of_row, s_of_row, t_len - 1 - s_of_row]
    sel = np.stack([(k_ar[None, :] == e[:, None]) for e in expo]).astype(np.float32)
    h_of_lane = np.arange(CHUNK_W) % SSM_GROUP
    dmask = ((s_of_row[:, None] == t_of_lane[None, :])
             & ((np.arange(CHUNK_W) % SSM_GROUP)[:, None] == h_of_lane[None, :])).astype(np.float32)
    return oh_f, oh_b, sel, dmask


def _s5_op_kernel(*refs):
    for gg in range(OP_GROUP_BATCH):
        _s5_op_group(gg, *refs)


def _s5_op_group(gg, pw_ref, bb_ref, cm_ref, dv_ref, ohf_ref, ohb_ref, sel_ref, dmask_ref,
                 m_ref, ws_ref, wyt_ref, cst_ref):
    t_len = CHUNK

    def split(a):
        hi = a.astype(BF16)
        return hi, (a - hi.astype(F32)).astype(BF16)

    def pick_rows(onehot, table):
        hi, lo = split(table)
        return (jnp.dot(onehot, hi, preferred_element_type=F32)
                + jnp.dot(onehot, lo, preferred_element_type=F32))

    def tile_rows(a):
        return jnp.concatenate([a] * t_len, axis=0)

    main, swapped = [], []
    for d, oh_ref in ((0, ohf_ref), (1, ohb_ref)):
        pp = pick_rows(oh_ref[...], pw_ref[d, gg])
        p_re, p_im = pp[:, :STATE_W], pp[:, STATE_W:]
        b0, b1, b2, b3 = [tile_rows(bb_ref[d, gg, :, STATE_W * i:STATE_W * (i + 1)])
                          for i in range(4)]
        main.append(p_re * b0 + p_im * b1)
        swapped.append(p_re * b2 + p_im * b3)
    ws_ref[gg] = jnp.concatenate(main + swapped, axis=1).astype(BF16)

    c_tiled = [[tile_rows(cm_ref[d, gg, i]) for i in range(2)] for d in range(2)]

    def block_t(d, pat):
        pp = pick_rows(sel_ref[pat], pw_ref[d, gg])
        return pp[:, :STATE_W] * c_tiled[d][0] + pp[:, STATE_W:] * c_tiled[d][1]

    wyt_ref[gg] = jnp.concatenate([block_t(0, 0), block_t(1, 1)], axis=1).astype(BF16)

    nt = (((1,), (1,)), ((), ()))
    kt_f = lax.dot_general(bb_ref[0, gg, :, :STATE_W], block_t(0, 2), nt,
                           preferred_element_type=F32, precision=lax.Precision.HIGHEST)
    kt_b = lax.dot_general(bb_ref[1, gg, :, :STATE_W], block_t(1, 3), nt,
                           preferred_element_type=F32, precision=lax.Precision.HIGHEST)
    lane = lax.broadcasted_iota(jnp.int32, (SSM_GROUP, CHUNK_W), 1)
    skip = dv_ref[gg]
    for s in range(t_len):
        fwd = kt_f if s == 0 else pltpu.roll(kt_f, SSM_GROUP * s, 1)
        back = t_len - 1 - s
        bwd = kt_b if back == 0 else pltpu.roll(kt_b, CHUNK_W - SSM_GROUP * back, 1)
        rows = (jnp.where(lane >= SSM_GROUP * s, fwd, 0.0)
                + jnp.where(lane < SSM_GROUP * (s + 1), bwd, 0.0)
                + dmask_ref[SSM_GROUP * s:SSM_GROUP * (s + 1), :] * skip)
        m_ref[gg, SSM_GROUP * s:SSM_GROUP * (s + 1), :] = rows.astype(BF16)

    even = lax.broadcasted_iota(jnp.int32, (SUBLANES, STATE_W), 0) % 2 == 0
    sign = jnp.where(lax.broadcasted_iota(jnp.int32, (1, STATE_W), 1) < SSM_STATE, -1.0, 1.0)

    def w12(d, k):
        row = pw_ref[d, gg, k:k + 1, :]
        return row[:, :STATE_W], row[:, STATE_W:] * sign

    zero = (jnp.zeros((1, STATE_W), F32),) * 2
    pairs = [(w12(0, t_len), w12(0, 2 * t_len)), (zero, w12(0, t_len)),
             (w12(1, 2 * t_len), w12(1, t_len)), (w12(1, t_len), zero)]
    idx = 0
    for top, bot in pairs:
        for part in range(2):
            cst_ref[gg, idx] = jnp.where(even, jnp.broadcast_to(top[part], (SUBLANES, STATE_W)),
                                         jnp.broadcast_to(bot[part], (SUBLANES, STATE_W)))
            idx += 1


def _s5_operator_call(pw, bb, cm, dv):
    oh_f, oh_b, sel, dmask = _operator_constants()
    consts = [jnp.asarray(oh_f, BF16), jnp.asarray(oh_b, BF16), jnp.asarray(sel, BF16),
              jnp.asarray(dmask)]

    def per_group(*tail):
        n = len(tail)
        return pl.BlockSpec((2, OP_GROUP_BATCH) + tail, lambda g: (0, g) + (0,) * n)

    def whole(a):
        return pl.BlockSpec(a.shape, lambda g: (0,) * a.ndim)

    def out(*tail):
        return pl.BlockSpec((OP_GROUP_BATCH,) + tail, lambda g: (g,) + (0,) * len(tail))

    return pl.pallas_call(
        _s5_op_kernel,
        grid=(SSM_GROUPS // OP_GROUP_BATCH,),
        in_specs=[per_group(POW_ROWS, 2 * STATE_W), per_group(SSM_GROUP, 4 * STATE_W),
                  per_group(2, SSM_GROUP, STATE_W),
                  pl.BlockSpec((OP_GROUP_BATCH, 1, CHUNK_W), lambda g: (g, 0, 0))]
                 + [whole(a) for a in consts],
        out_specs=[out(CHUNK_W, CHUNK_W), out(CHUNK_W, 4 * STATE_W), out(CHUNK_W, 2 * STATE_W),
                   out(8, SUBLANES, STATE_W)],
        out_shape=[jax.ShapeDtypeStruct((SSM_GROUPS, CHUNK_W, CHUNK_W), BF16),
                   jax.ShapeDtypeStruct((SSM_GROUPS, CHUNK_W, 4 * STATE_W), BF16),
                   jax.ShapeDtypeStruct((SSM_GROUPS, CHUNK_W, 2 * STATE_W), BF16),
                   jax.ShapeDtypeStruct((SSM_GROUPS, 8, SUBLANES, STATE_W), F32)],
        compiler_params=_params(("arbitrary",), 32),
        name="s5_operators",
    )(pw, bb, cm, dv, *consts)


def kernel(x, c, ctx, c_ctx, norm_w, w_ada, b_ada, attn_w_in, attn_sink, attn_w_out,
           ssm_w_in, ssm_lam_re, ssm_lam_im, ssm_log_dt, ssm_b_re, ssm_b_im, ssm_c_re, ssm_c_im,
           ssm_d, ssm_w_glu, ssm_w_out, final_norm_w):
    mod0, mod1 = _modulation(c, c_ctx, w_ada, b_ada)

    w_in0, w_out0, sink_tab = _attn_weights(attn_w_in[0], attn_w_out[0], attn_sink[0])
    cos_tab, sin_tab = _rope_tables()
    q, z0, kbd, vt = _proj0(x, ctx, mod0, norm_w[0], cos_tab, sin_tab, w_in0)
    xc1 = _attention(q, z0, kbd, vt, sink_tab, x, ctx, mod0, w_out0)

    u_g, z1 = _proj1(xc1, mod1, norm_w[1], ssm_w_in[0].astype(BF16))
    m_mat, ws_mat, wy_mat, consts = _s5_operators(
        ssm_lam_re[0], ssm_lam_im[0], ssm_log_dt[0], ssm_b_re[0], ssm_b_im[0],
        ssm_c_re[0], ssm_c_im[0], ssm_d[0])
    y_g = _s5_core(u_g, m_mat, ws_mat, wy_mat, consts)
    return _out1(y_g, z1, xc1, mod1, ssm_w_glu[0].astype(BF16), ssm_w_out[0].astype(BF16),
                 final_norm_w)
```

```python
import functools
import math

import jax
import jax.numpy as jnp
import numpy as np
from jax import lax
from jax.experimental import pallas as pl
from jax.experimental.pallas import tpu as pltpu

F32 = jnp.float32
BF16 = jnp.bfloat16

D_MODEL = 1024
BATCH = 4
SEQ = 4096
GRID_W = 64
CTX_LEN = 256
TOTAL = CTX_LEN + SEQ
HEAD_DIM = 64
N_HEADS = 16
N_KV_HEADS = 4
ATTN_WIDTH = N_HEADS * HEAD_DIM
KV_WIDTH = N_KV_HEADS * HEAD_DIM
BLOCK = 128
N_BLOCKS = TOTAL // BLOCK
N_CTX_BLOCKS = CTX_LEN // BLOCK
ROPE_BASE = 10000.0
ROPE_FREQS = HEAD_DIM // 4
SSM_GROUP = 16
SSM_GROUPS = D_MODEL // SSM_GROUP
SSM_STATE = 64
NORM_EPS = 1e-6
NEG_INF = -1e30

LANES = 128
SUBLANES = 8
N_SLOTS = ATTN_WIDTH // LANES
N_KV_PAIRS = KV_WIDTH // LANES
SLOTS_PER_M = N_SLOTS // N_KV_PAIRS
ATTN_QBLOCKS = 2
assert N_CTX_BLOCKS % ATTN_QBLOCKS == 0 and N_BLOCKS % ATTN_QBLOCKS == 0
UNIT_SLOTS = 4
LOG2E = math.log2(math.e)
Q_SCALE = HEAD_DIM ** -0.5 * LOG2E
ROW_TILE = 256
N_ROW_TILES = TOTAL // ROW_TILE
CHUNK = 16
N_CHUNKS = TOTAL // CHUNK
N_CTX_CHUNKS = CTX_LEN // CHUNK
CHUNK_W = CHUNK * SSM_GROUP
CHUNKS_PER_TILE = ROW_TILE // CHUNK
PAIRS_PER_TILE = CHUNKS_PER_TILE // 2
SLOT_PITCH = 24
OUT_PHASES = 4
assert SLOT_PITCH >= CHUNK and SLOT_PITCH % SUBLANES == 0
STATE_W = 2 * SSM_STATE
SCAN_ROWS = N_CHUNKS * BATCH
CTX_SCAN_ROWS = N_CTX_CHUNKS * BATCH
N_SCAN_BLOCKS = SCAN_ROWS // SUBLANES
N_CTX_SCAN_BLOCKS = CTX_SCAN_ROWS // SUBLANES
GROUP_BATCH = 4
OP_GROUP_BATCH = 4
POW_ROWS = 48

assert BATCH * 2 == SUBLANES


def _params(semantics, vmem_mb):
    return pltpu.CompilerParams(dimension_semantics=semantics,
                                vmem_limit_bytes=vmem_mb * 1024 * 1024)


def _mod_kernel(c_ref, w_ref, b_ref, o_ref):
    c = c_ref[...]
    a = c * jax.nn.sigmoid(c)
    o_ref[0] = jnp.dot(a, w_ref[0], preferred_element_type=F32,
                       precision=lax.Precision.HIGHEST) + b_ref[0]


def _modulation(c, c_ctx, w_ada, b_ada):
    depth = w_ada.shape[0]
    rows = jnp.zeros((SUBLANES, D_MODEL), F32).at[:BATCH].set(c).at[BATCH].set(c_ctx)
    n_col = 3
    out = pl.pallas_call(
        _mod_kernel,
        grid=(depth, n_col),
        in_specs=[
            pl.BlockSpec((SUBLANES, D_MODEL), lambda l, j: (0, 0)),
            pl.BlockSpec((1, D_MODEL, D_MODEL), lambda l, j: (l, 0, j)),
            pl.BlockSpec((1, 1, D_MODEL), lambda l, j: (l, 0, j)),
        ],
        out_specs=pl.BlockSpec((1, SUBLANES, D_MODEL), lambda l, j: (l, 0, j)),
        out_shape=jax.ShapeDtypeStruct((depth, SUBLANES, 3 * D_MODEL), F32),
        compiler_params=_params(("arbitrary", "arbitrary"), 32),
        name="adaln_modulation",
    )(rows, w_ada, b_ada.reshape(depth, 1, 3 * D_MODEL))
    tabs = []
    for l in range(depth):
        lat = out[l, :BATCH].reshape(BATCH, 3, D_MODEL)
        cx = jnp.broadcast_to(out[l, BATCH].reshape(1, 3, D_MODEL), (BATCH, 3, D_MODEL))
        tab = jnp.stack([cx, lat], axis=1)
        tabs.append(jnp.pad(tab, ((0, 0), (0, 0), (0, SUBLANES - 3), (0, 0))))
    return tabs


def _modulated_norm(xt, nw, mod_ref, b=0):
    ms = jnp.mean(xt * xt, axis=-1, keepdims=True)
    y = xt * lax.rsqrt(ms + NORM_EPS) * nw
    return y * (1.0 + mod_ref[b, 0, 1:2, :]) + mod_ref[b, 0, 0:1, :]


def _lane_block_transpose(vs):
    n = len(vs)
    width = LANES // n
    blk = lax.broadcasted_iota(jnp.int32, vs[0].shape, 1) // width
    x = list(vs)
    d = n // 2
    while d >= 1:
        clear = (blk & d) == 0
        y = list(x)
        for i in range(n):
            if i & d == 0:
                a, b = x[i], x[i + d]
                y[i] = jnp.where(clear, a, pltpu.roll(b, width * d, 1))
                y[i + d] = jnp.where(clear, pltpu.roll(a, LANES - width * d, 1), b)
        x = y
        d //= 2
    return x


def _chunk_slot(b, cc):
    return (cc // 2) * (2 * BATCH) + 2 * b + (cc % 2)


def _proj0_kernel(x_ref, c_ref, mod_ref, nw_ref, cos_ref, sin_ref, w_ref,
                  q_ref, z_ref, kbd_ref, vt_ref):
    is_ctx = pl.program_id(0) == 0
    h = jnp.concatenate(
        [_modulated_norm(jnp.where(is_ctx, c_ref[b], x_ref[b]), nw_ref[...], mod_ref, b).astype(BF16)
         for b in range(BATCH)], axis=0)
    cos = cos_ref[...]
    sin = sin_ref[...]
    lane = lax.broadcasted_iota(jnp.int32, (ROW_TILE, LANES), 1)
    first_half = (lane % HEAD_DIM) < (HEAD_DIM // 2)
    low = lax.broadcasted_iota(jnp.int32, (BLOCK, LANES), 1) < HEAD_DIM

    def rope(t):
        partner = jnp.where(first_half, pltpu.roll(t, LANES - HEAD_DIM // 2, 1),
                            pltpu.roll(t, HEAD_DIM // 2, 1))
        return t * cos + partner * sin

    q = jnp.dot(h, w_ref[:, :ATTN_WIDTH], preferred_element_type=F32)
    k = jnp.dot(h, w_ref[:, ATTN_WIDTH:ATTN_WIDTH + KV_WIDTH], preferred_element_type=F32)
    v = jnp.dot(h, w_ref[:, ATTN_WIDTH + KV_WIDTH:ATTN_WIDTH + 2 * KV_WIDTH],
                preferred_element_type=F32)
    z = jnp.dot(h, w_ref[:, ATTN_WIDTH + 2 * KV_WIDTH:], preferred_element_type=F32)
    for b in range(BATCH):
        tile = slice(ROW_TILE * b, ROW_TILE * (b + 1))
        z_ref[b] = z[tile].astype(BF16)
        for j in range(N_SLOTS):
            m, gi = divmod(j, SLOTS_PER_M)
            qj = (rope(q[tile, LANES * j:LANES * (j + 1)]) * Q_SCALE).astype(BF16)
            for blk in range(ROW_TILE // BLOCK):
                q_ref[b, blk, m, BLOCK * gi:BLOCK * (gi + 1), :] = qj[BLOCK * blk:BLOCK * (blk + 1)]
        for m in range(N_KV_PAIRS):
            sl = slice(LANES * m, LANES * (m + 1))
            kr = rope(k[tile, sl])
            vm = v[tile, sl]
            for blk in range(ROW_TILE // BLOCK):
                rows = slice(BLOCK * blk, BLOCK * (blk + 1))
                kbd_ref[b, blk, m, :BLOCK, :] = jnp.where(low, kr[rows], 0.0).astype(BF16)
                kbd_ref[b, blk, m, BLOCK:, :] = jnp.where(low, 0.0, kr[rows]).astype(BF16)
                vt_ref[b, blk, LANES * m:LANES * (m + 1), :] = vm[rows].T.astype(BF16)


def _proj0(x, ctx, mod, norm_w, cos_tab, sin_tab, w_in):
    n_col = w_in.shape[1]
    blocks_per_tile = ROW_TILE // BLOCK
    kv_shape = jax.ShapeDtypeStruct((BATCH, N_BLOCKS, N_KV_PAIRS, 2 * BLOCK, LANES), BF16)
    kv_spec = pl.BlockSpec((BATCH, blocks_per_tile, N_KV_PAIRS, 2 * BLOCK, LANES),
                           lambda i: (0, i, 0, 0, 0))
    row_spec = pl.BlockSpec((BATCH, ROW_TILE, D_MODEL), lambda i: (0, i, 0))
    once = pl.Buffered(1)
    return pl.pallas_call(
        _proj0_kernel,
        grid=(N_ROW_TILES,),
        in_specs=[
            pl.BlockSpec((BATCH, ROW_TILE, D_MODEL), lambda i: (0, jnp.maximum(i - 1, 0), 0)),
            pl.BlockSpec((BATCH, ROW_TILE, D_MODEL), lambda i: (0, 0, 0), pipeline_mode=once),
            pl.BlockSpec((BATCH, 1, SUBLANES, D_MODEL), lambda i: (0, jnp.minimum(i, 1), 0, 0)),
            pl.BlockSpec((1, D_MODEL), lambda i: (0, 0)),
            pl.BlockSpec((ROW_TILE, LANES), lambda i: (i, 0)),
            pl.BlockSpec((ROW_TILE, LANES), lambda i: (i, 0)),
            pl.BlockSpec((D_MODEL, n_col), lambda i: (0, 0), pipeline_mode=once),
        ],
        out_specs=[
            pl.BlockSpec((BATCH, blocks_per_tile, N_KV_PAIRS, SLOTS_PER_M * BLOCK, LANES),
                         lambda i: (0, i, 0, 0, 0)),
            row_spec, kv_spec,
            pl.BlockSpec((BATCH, blocks_per_tile, KV_WIDTH, BLOCK), lambda i: (0, i, 0, 0))],
        out_shape=[
            jax.ShapeDtypeStruct((BATCH, N_BLOCKS, N_KV_PAIRS, SLOTS_PER_M * BLOCK, LANES), BF16),
            jax.ShapeDtypeStruct((BATCH, TOTAL, ATTN_WIDTH), BF16),
            kv_shape,
            jax.ShapeDtypeStruct((BATCH, N_BLOCKS, KV_WIDTH, BLOCK), BF16),
        ],
        compiler_params=_params(("arbitrary",), 56),
        name="attn_projection",
    )(x, ctx, mod, norm_w.reshape(1, D_MODEL), cos_tab, sin_tab, w_in)


def _attn_kernel(q_ref, z_ref, k0_ref, k1_ref, k2_ref, k3_ref, kx_ref,
                 v0_ref, v1_ref, v2_ref, v3_ref, vx_ref,
                 sink_ref, x_ref, c_ref, mod_ref, wo_ref, eye_ref, tri_ref, o_ref):
    step = pl.program_id(1)
    is_lat = step >= N_CTX_BLOCKS // ATTN_QBLOCKS
    n_first = ATTN_QBLOCKS * step - N_CTX_BLOCKS
    q_rows = UNIT_SLOTS * BLOCK
    blocked = tri_ref[2]
    bias = []
    for qb in range(ATTN_QBLOCKS):
        n = n_first + qb
        bias.append([jnp.where(jnp.logical_and(is_lat, n >= 1), tri_ref[0], blocked),
                     jnp.where(is_lat, tri_ref[3], blocked),
                     jnp.where(jnp.logical_and(is_lat, n <= SEQ // BLOCK - 2), tri_ref[1], blocked),
                     None, None])
    k_win = (k0_ref, k1_ref, k2_ref, k3_ref)
    v_win = (v0_ref, v1_ref, v2_ref, v3_ref)

    def kpiece(qb, p, m):
        if p < 3:
            return k_win[qb + p][0, 0, m], v_win[qb + p][0, 0]
        return kx_ref[0, p - 3, m], vx_ref[0, p - 3]

    n_piece = 3 + N_CTX_BLOCKS
    units = [(qb, m, h) for qb in range(ATTN_QBLOCKS) for m in range(N_KV_PAIRS)
             for h in range(SLOTS_PER_M // UNIT_SLOTS)]

    nt = (((1,), (1,)), ((), ()))

    half_rows = slice(0, BLOCK), slice(BLOCK, 2 * BLOCK)

    def scores(qb, m, h):
        qu = q_ref[0, qb, m, q_rows * h:q_rows * (h + 1), :]
        qu_masked = jnp.concatenate([qu, eye_ref[...]], axis=1)
        s_list = []
        for p in range(n_piece):
            kbd, _ = kpiece(qb, p, m)
            if bias[qb][p] is None:
                s = lax.dot_general(kbd, qu, nt, preferred_element_type=F32)
            else:
                s = lax.dot_general(jnp.concatenate([kbd, bias[qb][p]], axis=1), qu_masked, nt,
                                    preferred_element_type=F32)
            s_list.append(s)
        return s_list

    def finish(qb, m, h, s_list, y):
        slot0 = SLOTS_PER_M * m + UNIT_SLOTS * h
        halves = []
        for hs in range(2):
            sink = jnp.concatenate(
                [sink_ref[2 * (slot0 + gi) + hs:2 * (slot0 + gi) + hs + 1, :]
                 for gi in range(UNIT_SLOTS)], axis=1)
            mx = sink
            for s in s_list:
                mx = jnp.maximum(mx, jnp.max(s[half_rows[hs]], axis=0, keepdims=True))
            probs = [jnp.exp2(s[half_rows[hs]] - mx) for s in s_list]
            denom = jnp.exp2(sink - mx)
            for e in probs:
                denom = denom + jnp.sum(e, axis=0, keepdims=True)
            probs = [e.astype(BF16) for e in probs]
            kv_head = 2 * m + hs
            acc = None
            for p0 in range(0, n_piece, 2):
                group = list(range(p0, min(p0 + 2, n_piece)))
                vt = jnp.concatenate(
                    [kpiece(qb, p, m)[1][HEAD_DIM * kv_head:HEAD_DIM * (kv_head + 1), :]
                     for p in group], axis=1)
                pt = jnp.concatenate([probs[p] for p in group], axis=0)
                part = jnp.dot(vt, pt, preferred_element_type=F32)
                acc = part if acc is None else acc + part
            halves.append(acc * (1.0 / denom))
        o_t = jnp.concatenate(halves, axis=0)
        outs = []
        for gi in range(UNIT_SLOTS):
            j = slot0 + gi
            o = o_t[:, BLOCK * gi:BLOCK * (gi + 1)].T
            zj = z_ref[0, BLOCK * qb:BLOCK * (qb + 1), LANES * j:LANES * (j + 1)].astype(F32)
            outs.append((o * (zj * jax.nn.sigmoid(zj))).astype(BF16))
        g = jnp.concatenate(outs, axis=1)
        part = jnp.dot(g, wo_ref[LANES * slot0:LANES * (slot0 + UNIT_SLOTS), :],
                       preferred_element_type=F32)
        return part if y is None else y + part

    y = [None] * ATTN_QBLOCKS
    pending = scores(*units[0])
    for u, (qb, m, h) in enumerate(units):
        nxt = scores(*units[u + 1]) if u + 1 < len(units) else None
        y[qb] = finish(qb, m, h, pending, y[qb])
        pending = nxt
    resid = jnp.where(is_lat, x_ref[0], c_ref[0])
    gate = mod_ref[0, 0, 2:3, :]
    for qb in range(ATTN_QBLOCKS):
        rows = slice(BLOCK * qb, BLOCK * (qb + 1))
        o_ref[0, rows, :] = resid[rows] + gate * y[qb]


def _attention(q, z, kbd, vt, sink_tab, x, ctx, mod, w_out):
    last = N_BLOCKS - 1
    n_m = KV_WIDTH // LANES
    nq = ATTN_QBLOCKS
    ctx_steps = N_CTX_BLOCKS // nq
    row_spec = pl.BlockSpec((1, nq * BLOCK, ATTN_WIDTH), lambda b, i: (b, i, 0))

    def kv_spec(off):
        return pl.BlockSpec((1, 1, n_m, 2 * BLOCK, LANES),
                            lambda b, i: (b, jnp.clip(nq * i + off, 0, last), 0, 0, 0))

    ctx_kv_spec = pl.BlockSpec((1, N_CTX_BLOCKS, n_m, 2 * BLOCK, LANES),
                               lambda b, i: (b, 0, 0, 0, 0))

    def vt_spec(off):
        return pl.BlockSpec((1, 1, KV_WIDTH, BLOCK),
                            lambda b, i: (b, jnp.clip(nq * i + off, 0, last), 0, 0))

    ctx_vt_spec = pl.BlockSpec((1, N_CTX_BLOCKS, KV_WIDTH, BLOCK), lambda b, i: (b, 0, 0, 0))
    off = np.arange(BLOCK)
    eye = np.tile(np.eye(BLOCK, dtype=np.float32), (UNIT_SLOTS, 1))
    key_ge = np.where(off[:, None] >= off[None, :], 0.0, NEG_INF)
    key_le = np.where(off[:, None] <= off[None, :], 0.0, NEG_INF)
    tri = np.stack([np.tile(t, (2, 1)) for t in
                    (key_ge, key_le, np.full((BLOCK, BLOCK), NEG_INF), np.zeros((BLOCK, BLOCK)))])
    consts = [jnp.asarray(a, BF16) for a in (eye, tri)]
    return pl.pallas_call(
        _attn_kernel,
        grid=(BATCH, N_BLOCKS // nq),
        in_specs=[
            pl.BlockSpec((1, nq, n_m, SLOTS_PER_M * BLOCK, LANES), lambda b, i: (b, i, 0, 0, 0)),
            row_spec,
            kv_spec(-1), kv_spec(0), kv_spec(1), kv_spec(2), ctx_kv_spec,
            vt_spec(-1), vt_spec(0), vt_spec(1), vt_spec(2), ctx_vt_spec,
            pl.BlockSpec((2 * N_SLOTS, LANES), lambda b, i: (0, 0)),
            pl.BlockSpec((1, nq * BLOCK, D_MODEL),
                         lambda b, i: (b, jnp.maximum(i - ctx_steps, 0), 0)),
            pl.BlockSpec((1, nq * BLOCK, D_MODEL),
                         lambda b, i: (b, jnp.minimum(i, ctx_steps - 1), 0)),
            pl.BlockSpec((1, 1, SUBLANES, D_MODEL),
                         lambda b, i: (b, jnp.minimum(i // ctx_steps, 1), 0, 0)),
            pl.BlockSpec((ATTN_WIDTH, D_MODEL), lambda b, i: (0, 0)),
        ] + [pl.BlockSpec(a.shape, lambda b, i, nd=a.ndim: (0,) * nd) for a in consts],
        out_specs=pl.BlockSpec((1, nq * BLOCK, D_MODEL), lambda b, i: (b, i, 0)),
        out_shape=jax.ShapeDtypeStruct((BATCH, TOTAL, D_MODEL), F32),
        compiler_params=_params(("arbitrary", "arbitrary"), 48),
        name="window_attention",
    )(q, z, kbd, kbd, kbd, kbd, kbd, vt, vt, vt, vt, vt, sink_tab, x, ctx, mod, w_out,
      *consts)


def _proj1_kernel(x_ref, mod_ref, nw_ref, w_ref, u_ref, z_ref, scr_ref):
    h = jnp.concatenate(
        [_modulated_norm(x_ref[b], nw_ref[...], mod_ref, b).astype(BF16) for b in range(BATCH)],
        axis=0)
    u = jnp.dot(h, w_ref[:, :D_MODEL], preferred_element_type=F32)
    z = jnp.dot(h, w_ref[:, D_MODEL:], preferred_element_type=F32)
    for b in range(BATCH):
        z_ref[b] = z[ROW_TILE * b:ROW_TILE * (b + 1)].astype(BF16)
    n_slab = D_MODEL // LANES
    for b in range(BATCH):
        for cc in range(CHUNKS_PER_TILE):
            r0 = ROW_TILE * b + CHUNK * cc
            q0 = SLOT_PITCH * _chunk_slot(b, cc)
            for k in range(n_slab):
                scr_ref[k, q0:q0 + CHUNK, :] = u[r0:r0 + CHUNK, LANES * k:LANES * (k + 1)]
    groups_per_slab = LANES // SSM_GROUP
    for k in range(n_slab):
        for pp in range(PAIRS_PER_TILE // 2):
            parts = []
            for p in (2 * pp, 2 * pp + 1):
                rows = [scr_ref[k, pl.ds(SLOT_PITCH * SUBLANES * p + s, SUBLANES,
                                         stride=SLOT_PITCH), :] for s in range(CHUNK)]
                parts.append([_lane_block_transpose(rows[SUBLANES * m2:SUBLANES * (m2 + 1)])
                              for m2 in range(CHUNK // SUBLANES)])
            for m2 in range(CHUNK // SUBLANES):
                for gl in range(groups_per_slab):
                    val = jnp.concatenate([parts[0][m2][gl], parts[1][m2][gl]], axis=0)
                    u_ref[groups_per_slab * k + gl, 2 * SUBLANES * pp:2 * SUBLANES * (pp + 1),
                          LANES * m2:LANES * (m2 + 1)] = val.astype(BF16)


def _proj1(xc, mod, norm_w, w_in):
    row_spec = pl.BlockSpec((BATCH, ROW_TILE, D_MODEL), lambda i: (0, i, 0))
    tile_rows = PAIRS_PER_TILE * SUBLANES
    return pl.pallas_call(
        _proj1_kernel,
        grid=(N_ROW_TILES,),
        in_specs=[
            row_spec,
            pl.BlockSpec((BATCH, 1, SUBLANES, D_MODEL), lambda i: (0, jnp.minimum(i, 1), 0, 0)),
            pl.BlockSpec((1, D_MODEL), lambda i: (0, 0)),
            pl.BlockSpec((D_MODEL, 2 * D_MODEL), lambda i: (0, 0)),
        ],
        out_specs=[pl.BlockSpec((SSM_GROUPS, tile_rows, CHUNK_W), lambda i: (0, i, 0)), row_spec],
        out_shape=[jax.ShapeDtypeStruct((SSM_GROUPS, SCAN_ROWS, CHUNK_W), BF16),
                   jax.ShapeDtypeStruct((BATCH, TOTAL, D_MODEL), BF16)],
        scratch_shapes=[pltpu.VMEM((D_MODEL // LANES, BATCH * CHUNKS_PER_TILE * SLOT_PITCH, LANES),
                                   F32)],
        compiler_params=_params(("arbitrary",), 56),
        name="ssm_projection",
    )(xc, mod, norm_w.reshape(1, D_MODEL), w_in)


def _s5_kernel(u_ref, m_ref, ws_ref, wy_ref, cst_ref, y_ref, s4_ref, xp_ref):
    even = lax.broadcasted_iota(jnp.int32, (SUBLANES, STATE_W), 0) % 2 == 0
    down = 1
    up = SUBLANES - 1
    fwd = slice(0, STATE_W)
    bwd = slice(STATE_W, 2 * STATE_W)
    fwd_sw = slice(2 * STATE_W, 3 * STATE_W)
    bwd_sw = slice(3 * STATE_W, 4 * STATE_W)
    for g in range(GROUP_BATCH):
        s = jnp.dot(u_ref[g], ws_ref[g], preferred_element_type=F32)
        s = s.reshape(N_SCAN_BLOCKS, SUBLANES, 4 * STATE_W)
        q1f, q2f, q1b, q2b = cst_ref[g, 2], cst_ref[g, 3], cst_ref[g, 6], cst_ref[g, 7]
        zf, zb, zfs, zbs = s[..., fwd], s[..., bwd], s[..., fwd_sw], s[..., bwd_sw]
        rzf, rzfs = pltpu.roll(zf, down, 1), pltpu.roll(zfs, down, 1)
        rzb, rzbs = pltpu.roll(zb, up, 1), pltpu.roll(zbs, up, 1)
        t = jnp.concatenate([zf + q1f * rzf + q2f * rzfs, zb + q1b * rzb + q2b * rzbs,
                             zfs + q1f * rzfs - q2f * rzf, zbs + q1b * rzbs - q2b * rzb], axis=-1)
        s4_ref[g] = t.reshape(SCAN_ROWS, 4 * STATE_W)

    def step(j, carry):
        jb = jnp.where(j < N_CTX_SCAN_BLOCKS, N_CTX_SCAN_BLOCKS - 1 - j,
                       N_SCAN_BLOCKS - 1 + N_CTX_SCAN_BLOCKS - j)
        rf = pl.ds(pl.multiple_of(j * SUBLANES, SUBLANES), SUBLANES)
        rb = pl.ds(pl.multiple_of(jb * SUBLANES, SUBLANES), SUBLANES)
        new = []
        for g in range(GROUP_BATCH):
            cf, cfs, cb, cbs = carry[4 * g:4 * g + 4]
            p1f, p2f, p1b, p2b = cst_ref[g, 0], cst_ref[g, 1], cst_ref[g, 4], cst_ref[g, 5]
            xf = p1f * cf + p2f * cfs + s4_ref[g, rf, fwd]
            xfs = p1f * cfs - p2f * cf + s4_ref[g, rf, fwd_sw]
            xp_ref[g, rf, fwd] = jnp.where(even, cf, pltpu.roll(xf, down, 0))
            new += [jnp.where(even, pltpu.roll(xf, up, 0), xf),
                    jnp.where(even, pltpu.roll(xfs, up, 0), xfs)]
            xb = p1b * cb + p2b * cbs + s4_ref[g, rb, bwd]
            xbs = p1b * cbs - p2b * cb + s4_ref[g, rb, bwd_sw]
            xp_ref[g, rb, bwd] = jnp.where(even, pltpu.roll(xb, up, 0), cb)
            new += [jnp.where(even, xb, pltpu.roll(xb, down, 0)),
                    jnp.where(even, xbs, pltpu.roll(xbs, down, 0))]
        return tuple(new)

    zero = jnp.zeros((SUBLANES, STATE_W), F32)
    lax.fori_loop(0, N_SCAN_BLOCKS, step, (zero,) * (4 * GROUP_BATCH))
    for g in range(GROUP_BATCH):
        y_ref[g] = (
            jnp.dot(u_ref[g, CTX_SCAN_ROWS:, :], m_ref[g], preferred_element_type=F32)
            + lax.dot_general(xp_ref[g, CTX_SCAN_ROWS:, :].astype(BF16), wy_ref[g],
                              (((1,), (1,)), ((), ())), preferred_element_type=F32))


def _s5_core(u_g, m_mat, ws_mat, wy_mat, consts):
    lat_rows = SCAN_ROWS - CTX_SCAN_ROWS

    def gspec(*tail):
        return pl.BlockSpec((GROUP_BATCH,) + tail, lambda i: (i,) + (0,) * len(tail))

    return pl.pallas_call(
        _s5_kernel,
        grid=(SSM_GROUPS // GROUP_BATCH,),
        in_specs=[
            gspec(SCAN_ROWS, CHUNK_W),
            gspec(CHUNK_W, CHUNK_W),
            gspec(CHUNK_W, 4 * STATE_W),
            gspec(CHUNK_W, 2 * STATE_W),
            gspec(8, SUBLANES, STATE_W),
        ],
        out_specs=gspec(lat_rows, CHUNK_W),
        out_shape=jax.ShapeDtypeStruct((SSM_GROUPS, lat_rows, CHUNK_W), F32),
        scratch_shapes=[
            pltpu.VMEM((GROUP_BATCH, SCAN_ROWS, 4 * STATE_W), F32),
            pltpu.VMEM((GROUP_BATCH, SCAN_ROWS, 2 * STATE_W), F32),
        ],
        compiler_params=_params(("arbitrary",), 48),
        name="s5_scan",
    )(u_g, m_mat, ws_mat, wy_mat, consts)


def _out1_kernel(y_ref, z_ref, x_ref, mod_ref, wg_ref, wo_ref, fnw_ref, o_ref, scr_ref):
    n_slab = D_MODEL // LANES
    groups_per_slab = LANES // SSM_GROUP
    pairs_per_phase = PAIRS_PER_TILE // OUT_PHASES
    rows_per_phase = ROW_TILE // OUT_PHASES

    def relayout(ph):
        for k in range(n_slab):
            for p in range(pairs_per_phase * ph, pairs_per_phase * (ph + 1)):
                for m2 in range(CHUNK // SUBLANES):
                    vals = [y_ref[groups_per_slab * k + gl, SUBLANES * p:SUBLANES * (p + 1),
                                  LANES * m2:LANES * (m2 + 1)] for gl in range(groups_per_slab)]
                    steps = _lane_block_transpose(vals)
                    for s2 in range(SUBLANES):
                        t_idx = SUBLANES * m2 + s2
                        scr_ref[k, pl.ds(SLOT_PITCH * SUBLANES * p + t_idx, SUBLANES,
                                         stride=SLOT_PITCH), :] = steps[s2]

    def compute(ph):
        chunks = range(2 * pairs_per_phase * ph, 2 * pairs_per_phase * (ph + 1))
        y = jnp.concatenate(
            [jnp.concatenate(
                [scr_ref[k, SLOT_PITCH * _chunk_slot(b, cc):SLOT_PITCH * _chunk_slot(b, cc) + CHUNK, :]
                 for k in range(n_slab)], axis=1)
             for b in range(BATCH) for cc in chunks], axis=0)
        g = (0.5 * y * (1.0 + lax.erf(y * (2.0 ** -0.5)))).astype(BF16)
        t = jnp.dot(g, wg_ref[...], preferred_element_type=F32)
        rows = slice(rows_per_phase * ph, rows_per_phase * (ph + 1))
        z = jnp.concatenate([z_ref[b, rows, :] for b in range(BATCH)], axis=0).astype(F32)
        r = (t[:, :D_MODEL] * jax.nn.sigmoid(t[:, D_MODEL:]) * (z * jax.nn.sigmoid(z))).astype(BF16)
        o = jnp.dot(r, wo_ref[...], preferred_element_type=F32)
        for b in range(BATCH):
            x2 = (x_ref[b, rows, :]
                  + mod_ref[b, 0, 2:3, :] * o[rows_per_phase * b:rows_per_phase * (b + 1)])
            ms = jnp.mean(x2 * x2, axis=-1, keepdims=True)
            o_ref[b, rows, :] = x2 * lax.rsqrt(ms + NORM_EPS) * fnw_ref[...]

    relayout(0)
    for ph in range(OUT_PHASES):
        if ph + 1 < OUT_PHASES:
            relayout(ph + 1)
        compute(ph)


def _out1(y_g, z, xc, mod, w_glu, w_out, final_norm_w):
    ctx_tiles = CTX_LEN // ROW_TILE
    tile_rows = PAIRS_PER_TILE * SUBLANES
    lat_spec = pl.BlockSpec((BATCH, ROW_TILE, D_MODEL), lambda i: (0, i, 0))
    all_spec = pl.BlockSpec((BATCH, ROW_TILE, D_MODEL), lambda i: (0, i + ctx_tiles, 0))
    return pl.pallas_call(
        _out1_kernel,
        grid=(SEQ // ROW_TILE,),
        in_specs=[
            pl.BlockSpec((SSM_GROUPS, tile_rows, CHUNK_W), lambda i: (0, i, 0)),
            all_spec, all_spec,
            pl.BlockSpec((BATCH, 1, SUBLANES, D_MODEL), lambda i: (0, 1, 0, 0)),
            pl.BlockSpec((D_MODEL, 2 * D_MODEL), lambda i: (0, 0)),
            pl.BlockSpec((D_MODEL, D_MODEL), lambda i: (0, 0)),
            pl.BlockSpec((1, D_MODEL), lambda i: (0, 0)),
        ],
        out_specs=lat_spec,
        out_shape=jax.ShapeDtypeStruct((BATCH, SEQ, D_MODEL), F32),
        scratch_shapes=[pltpu.VMEM((D_MODEL // LANES, BATCH * CHUNKS_PER_TILE * SLOT_PITCH, LANES),
                                   F32)],
        compiler_params=_params(("arbitrary",), 56),
        name="ssm_output",
    )(y_g, z, xc, mod, w_glu, w_out, final_norm_w.reshape(1, D_MODEL))


def _slot_order(t, lead):
    n_m = KV_WIDTH // LANES
    gq = N_HEADS // N_KV_HEADS
    shape = t.shape
    t = t.reshape(shape[:lead] + (n_m, 2, gq) + shape[lead + 1:])
    perm = tuple(range(lead)) + (lead, lead + 2, lead + 1) + tuple(range(lead + 3, t.ndim))
    return jnp.transpose(t, perm)


def _rope_order(t):
    shape = t.shape
    t = t.reshape(shape[:-1] + (2, 2, ROPE_FREQS))
    return jnp.swapaxes(t, -3, -2).reshape(shape)


def _attn_weights(w_in, w_out, sink):
    wq = w_in[:, :ATTN_WIDTH].reshape(D_MODEL, N_HEADS, HEAD_DIM)
    wq = _slot_order(_rope_order(wq), 1).reshape(D_MODEL, ATTN_WIDTH)
    wk = w_in[:, ATTN_WIDTH:ATTN_WIDTH + KV_WIDTH].reshape(D_MODEL, N_KV_HEADS, HEAD_DIM)
    wk = _rope_order(wk).reshape(D_MODEL, KV_WIDTH)
    wv = w_in[:, ATTN_WIDTH + KV_WIDTH:ATTN_WIDTH + 2 * KV_WIDTH]
    wz = w_in[:, ATTN_WIDTH + 2 * KV_WIDTH:].reshape(D_MODEL, N_HEADS, HEAD_DIM)
    wz = _slot_order(wz, 1).reshape(D_MODEL, ATTN_WIDTH)
    w_in_p = jnp.concatenate([wq, wk, wv, wz], axis=1).astype(BF16)
    wo = _slot_order(w_out.reshape(N_HEADS, HEAD_DIM, D_MODEL), 0).reshape(ATTN_WIDTH, D_MODEL)
    sink_p = _slot_order(sink.astype(F32).reshape(N_HEADS), 0).reshape(2 * N_SLOTS)
    sink_tab = jnp.broadcast_to((sink_p * LOG2E)[:, None], (2 * N_SLOTS, LANES))
    return w_in_p, wo.astype(BF16), sink_tab


def _rope_tables():
    inv = ROPE_BASE ** (-np.arange(ROPE_FREQS, dtype=np.float64) / ROPE_FREQS)
    pos = np.arange(SEQ)
    row = (pos // GRID_W)[:, None] * inv
    col = (pos % GRID_W)[:, None] * inv
    w = np.arange(LANES) % HEAD_DIM
    half, axis, f = w // 32, (w % 32) // 16, w % 16
    ang = np.where((axis == 0)[None, :], row[:, f], col[:, f])
    sign = np.where(half == 0, -1.0, 1.0)[None, :]
    cos = np.concatenate([np.ones((CTX_LEN, LANES)), np.cos(ang)], axis=0)
    sin = np.concatenate([np.zeros((CTX_LEN, LANES)), np.sin(ang) * sign], axis=0)
    return jnp.asarray(cos, F32), jnp.asarray(sin, F32)


def _s5_operators(lam_re, lam_im, log_dt, b_re, b_im, c_re, c_im, d_skip):
    t_len = CHUNK
    n_pow = 2 * t_len + 1
    lr, li = lam_re.astype(F32), lam_im.astype(F32)
    dt = jnp.exp(log_dt.astype(F32))[..., None]
    mag = jnp.exp(lr * dt)
    sq = [(mag * jnp.cos(li * dt), mag * jnp.sin(li * dt))]
    while 2 ** len(sq) < n_pow:
        r, i = sq[-1]
        sq.append((r * r - i * i, 2.0 * r * i))
    ks = np.arange(n_pow)
    pr = jnp.ones(lr.shape + (n_pow,), F32)
    pi = jnp.zeros(lr.shape + (n_pow,), F32)
    for bit, (r, i) in enumerate(sq):
        on = jnp.asarray((ks >> bit) & 1 == 1)
        fr = jnp.where(on, r[..., None], 1.0)
        fi = jnp.where(on, i[..., None], 0.0)
        pr, pi = pr * fr - pi * fi, pr * fi + pi * fr
    ar1, ai1 = sq[0][0] - 1.0, sq[0][1]
    den = lr * lr + li * li
    gr, gi = (ar1 * lr + ai1 * li) / den, (ai1 * lr - ar1 * li) / den
    br_, bi_ = b_re.astype(F32), b_im.astype(F32)
    bbr = gr[..., None] * br_ - gi[..., None] * bi_
    bbi = gr[..., None] * bi_ + gi[..., None] * br_
    pad_k = POW_ROWS - n_pow
    prt, pit = jnp.swapaxes(pr, 2, 3), jnp.swapaxes(pi, 2, 3)
    pw = jnp.pad(jnp.concatenate([prt, prt, pit, pit], axis=-1),
                 ((0, 0), (0, 0), (0, pad_k), (0, 0)))
    brt, bit = jnp.swapaxes(bbr, 2, 3), jnp.swapaxes(bbi, 2, 3)
    bb = jnp.concatenate([brt, bit, -bit, brt, bit, brt, brt, -bit], axis=-1)
    cr, ci = c_re.astype(F32), c_im.astype(F32)
    cm = jnp.stack([jnp.concatenate([cr, -ci], axis=-1),
                    jnp.concatenate([-ci, -cr], axis=-1)], axis=2)
    dv = jnp.tile(d_skip.astype(F32).reshape(SSM_GROUPS, 1, SSM_GROUP), (1, 1, t_len))
    return _s5_operator_call(pw, bb, cm, dv)


def _operator_constants():
    t_len = CHUNK
    s_of_row = np.arange(CHUNK_W) // SSM_GROUP
    k_ar = np.arange(POW_ROWS)
    oh_f = (k_ar[None, :] == (t_len - 1 - s_of_row)[:, None]).astype(np.float32)
    oh_b = (k_ar[None, :] == s_of_row[:, None]).astype(np.float32)
    t_of_lane = np.arange(CHUNK_W) // SSM_GROUP
    expo = [s_of_row + 1, t_len - s_of_row, s_of_row, t_len - 1 - s_of_row]
    sel = np.stack([(k_ar[None, :] == e[:, None]) for e in expo]).astype(np.float32)
    h_of_lane = np.arange(CHUNK_W) % SSM_GROUP
    dmask = ((s_of_row[:, None] == t_of_lane[None, :])
             & ((np.arange(CHUNK_W) % SSM_GROUP)[:, None] == h_of_lane[None, :])).astype(np.float32)
    return oh_f, oh_b, sel, dmask


def _s5_op_kernel(*refs):
    for gg in range(OP_GROUP_BATCH):
        _s5_op_group(gg, *refs)


def _s5_op_group(gg, pw_ref, bb_ref, cm_ref, dv_ref, ohf_ref, ohb_ref, sel_ref, dmask_ref,
                 m_ref, ws_ref, wyt_ref, cst_ref):
    t_len = CHUNK

    def split(a):
        hi = a.astype(BF16)
        return hi, (a - hi.astype(F32)).astype(BF16)

    def pick_rows(onehot, table):
        hi, lo = split(table)
        return (jnp.dot(onehot, hi, preferred_element_type=F32)
                + jnp.dot(onehot, lo, preferred_element_type=F32))

    def tile_rows(a):
        return jnp.concatenate([a] * t_len, axis=0)

    main, swapped = [], []
    for d, oh_ref in ((0, ohf_ref), (1, ohb_ref)):
        pp = pick_rows(oh_ref[...], pw_ref[d, gg])
        p_re, p_im = pp[:, :STATE_W], pp[:, STATE_W:]
        b0, b1, b2, b3 = [tile_rows(bb_ref[d, gg, :, STATE_W * i:STATE_W * (i + 1)])
                          for i in range(4)]
        main.append(p_re * b0 + p_im * b1)
        swapped.append(p_re * b2 + p_im * b3)
    ws_ref[gg] = jnp.concatenate(main + swapped, axis=1).astype(BF16)

    c_tiled = [[tile_rows(cm_ref[d, gg, i]) for i in range(2)] for d in range(2)]

    def block_t(d, pat):
        pp = pick_rows(sel_ref[pat], pw_ref[d, gg])
        return pp[:, :STATE_W] * c_tiled[d][0] + pp[:, STATE_W:] * c_tiled[d][1]

    wyt_ref[gg] = jnp.concatenate([block_t(0, 0), block_t(1, 1)], axis=1).astype(BF16)

    nt = (((1,), (1,)), ((), ()))
    kt_f = lax.dot_general(bb_ref[0, gg, :, :STATE_W], block_t(0, 2), nt,
                           preferred_element_type=F32, precision=lax.Precision.HIGHEST)
    kt_b = lax.dot_general(bb_ref[1, gg, :, :STATE_W], block_t(1, 3), nt,
                           preferred_element_type=F32, precision=lax.Precision.HIGHEST)
    lane = lax.broadcasted_iota(jnp.int32, (SSM_GROUP, CHUNK_W), 1)
    skip = dv_ref[gg]
    for s in range(t_len):
        fwd = kt_f if s == 0 else pltpu.roll(kt_f, SSM_GROUP * s, 1)
        back = t_len - 1 - s
        bwd = kt_b if back == 0 else pltpu.roll(kt_b, CHUNK_W - SSM_GROUP * back, 1)
        rows = (jnp.where(lane >= SSM_GROUP * s, fwd, 0.0)
                + jnp.where(lane < SSM_GROUP * (s + 1), bwd, 0.0)
                + dmask_ref[SSM_GROUP * s:SSM_GROUP * (s + 1), :] * skip)
        m_ref[gg, SSM_GROUP * s:SSM_GROUP * (s + 1), :] = rows.astype(BF16)

    even = lax.broadcasted_iota(jnp.int32, (SUBLANES, STATE_W), 0) % 2 == 0
    sign = jnp.where(lax.broadcasted_iota(jnp.int32, (1, STATE_W), 1) < SSM_STATE, -1.0, 1.0)

    def w12(d, k):
        row = pw_ref[d, gg, k:k + 1, :]
        return row[:, :STATE_W], row[:, STATE_W:] * sign

    zero = (jnp.zeros((1, STATE_W), F32),) * 2
    pairs = [(w12(0, t_len), w12(0, 2 * t_len)), (zero, w12(0, t_len)),
             (w12(1, 2 * t_len), w12(1, t_len)), (w12(1, t_len), zero)]
    idx = 0
    for top, bot in pairs:
        for part in range(2):
            cst_ref[gg, idx] = jnp.where(even, jnp.broadcast_to(top[part], (SUBLANES, STATE_W)),
                                         jnp.broadcast_to(bot[part], (SUBLANES, STATE_W)))
            idx += 1


def _s5_operator_call(pw, bb, cm, dv):
    oh_f, oh_b, sel, dmask = _operator_constants()
    consts = [jnp.asarray(oh_f, BF16), jnp.asarray(oh_b, BF16), jnp.asarray(sel, BF16),
              jnp.asarray(dmask)]

    def per_group(*tail):
        n = len(tail)
        return pl.BlockSpec((2, OP_GROUP_BATCH) + tail, lambda g: (0, g) + (0,) * n)

    def whole(a):
        return pl.BlockSpec(a.shape, lambda g: (0,) * a.ndim)

    def out(*tail):
        return pl.BlockSpec((OP_GROUP_BATCH,) + tail, lambda g: (g,) + (0,) * len(tail))

    return pl.pallas_call(
        _s5_op_kernel,
        grid=(SSM_GROUPS // OP_GROUP_BATCH,),
        in_specs=[per_group(POW_ROWS, 2 * STATE_W), per_group(SSM_GROUP, 4 * STATE_W),
                  per_group(2, SSM_GROUP, STATE_W),
                  pl.BlockSpec((OP_GROUP_BATCH, 1, CHUNK_W), lambda g: (g, 0, 0))]
                 + [whole(a) for a in consts],
        out_specs=[out(CHUNK_W, CHUNK_W), out(CHUNK_W, 4 * STATE_W), out(CHUNK_W, 2 * STATE_W),
                   out(8, SUBLANES, STATE_W)],
        out_shape=[jax.ShapeDtypeStruct((SSM_GROUPS, CHUNK_W, CHUNK_W), BF16),
                   jax.ShapeDtypeStruct((SSM_GROUPS, CHUNK_W, 4 * STATE_W), BF16),
                   jax.ShapeDtypeStruct((SSM_GROUPS, CHUNK_W, 2 * STATE_W), BF16),
                   jax.ShapeDtypeStruct((SSM_GROUPS, 8, SUBLANES, STATE_W), F32)],
        compiler_params=_params(("arbitrary",), 32),
        name="s5_operators",
    )(pw, bb, cm, dv, *consts)


def kernel(x, c, ctx, c_ctx, norm_w, w_ada, b_ada, attn_w_in, attn_sink, attn_w_out,
           ssm_w_in, ssm_lam_re, ssm_lam_im, ssm_log_dt, ssm_b_re, ssm_b_im, ssm_c_re, ssm_c_im,
           ssm_d, ssm_w_glu, ssm_w_out, final_norm_w):
    mod0, mod1 = _modulation(c, c_ctx, w_ada, b_ada)

    w_in0, w_out0, sink_tab = _attn_weights(attn_w_in[0], attn_w_out[0], attn_sink[0])
    cos_tab, sin_tab = _rope_tables()
    q, z0, kbd, vt = _proj0(x, ctx, mod0, norm_w[0], cos_tab, sin_tab, w_in0)
    xc1 = _attention(q, z0, kbd, vt, sink_tab, x, ctx, mod0, w_out0)

    u_g, z1 = _proj1(xc1, mod1, norm_w[1], ssm_w_in[0].astype(BF16))
    m_mat, ws_mat, wy_mat, consts = _s5_operators(
        ssm_lam_re[0], ssm_lam_im[0], ssm_log_dt[0], ssm_b_re[0], ssm_b_im[0],
        ssm_c_re[0], ssm_c_im[0], ssm_d[0])
    y_g = _s5_core(u_g, m_mat, ws_mat, wy_mat, consts)
    return _out1(y_g, z1, xc1, mod1, ssm_w_glu[0].astype(BF16), ssm_w_out[0].astype(BF16),
                 final_norm_w)
```

```python
import functools
import math

import jax
import jax.numpy as jnp
import numpy as np
from jax import lax
from jax.experimental import pallas as pl
from jax.experimental.pallas import tpu as pltpu

F32 = jnp.float32
BF16 = jnp.bfloat16

D_MODEL = 1024
BATCH = 4
SEQ = 4096
GRID_W = 64
CTX_LEN = 256
TOTAL = CTX_LEN + SEQ
HEAD_DIM = 64
N_HEADS = 16
N_KV_HEADS = 4
ATTN_WIDTH = N_HEADS * HEAD_DIM
KV_WIDTH = N_KV_HEADS * HEAD_DIM
BLOCK = 128
N_BLOCKS = TOTAL // BLOCK
N_CTX_BLOCKS = CTX_LEN // BLOCK
ROPE_BASE = 10000.0
ROPE_FREQS = HEAD_DIM // 4
SSM_GROUP = 16
SSM_GROUPS = D_MODEL // SSM_GROUP
SSM_STATE = 64
NORM_EPS = 1e-6
NEG_INF = -1e30

LANES = 128
SUBLANES = 8
N_SLOTS = ATTN_WIDTH // LANES
N_KV_PAIRS = KV_WIDTH // LANES
SLOTS_PER_M = N_SLOTS // N_KV_PAIRS
ATTN_QBLOCKS = 2
assert N_CTX_BLOCKS % ATTN_QBLOCKS == 0 and N_BLOCKS % ATTN_QBLOCKS == 0
UNIT_SLOTS = 4
LOG2E = math.log2(math.e)
Q_SCALE = HEAD_DIM ** -0.5 * LOG2E
ROW_TILE = 256
N_ROW_TILES = TOTAL // ROW_TILE
CHUNK = 16
N_CHUNKS = TOTAL // CHUNK
N_CTX_CHUNKS = CTX_LEN // CHUNK
CHUNK_W = CHUNK * SSM_GROUP
CHUNKS_PER_TILE = ROW_TILE // CHUNK
PAIRS_PER_TILE = CHUNKS_PER_TILE // 2
SLOT_PITCH = 24
OUT_PHASES = 4
assert SLOT_PITCH >= CHUNK and SLOT_PITCH % SUBLANES == 0
STATE_W = 2 * SSM_STATE
SCAN_ROWS = N_CHUNKS * BATCH
CTX_SCAN_ROWS = N_CTX_CHUNKS * BATCH
N_SCAN_BLOCKS = SCAN_ROWS // SUBLANES
N_CTX_SCAN_BLOCKS = CTX_SCAN_ROWS // SUBLANES
GROUP_BATCH = 4
OP_GROUP_BATCH = 4
POW_ROWS = 48

assert BATCH * 2 == SUBLANES


def _params(semantics, vmem_mb):
    return pltpu.CompilerParams(dimension_semantics=semantics,
                                vmem_limit_bytes=vmem_mb * 1024 * 1024)


def _mod_kernel(c_ref, w_ref, b_ref, o_ref):
    c = c_ref[...]
    a = c * jax.nn.sigmoid(c)
    o_ref[0] = jnp.dot(a, w_ref[0], preferred_element_type=F32,
                       precision=lax.Precision.HIGHEST) + b_ref[0]


def _modulation(c, c_ctx, w_ada, b_ada):
    depth = w_ada.shape[0]
    rows = jnp.zeros((SUBLANES, D_MODEL), F32).at[:BATCH].set(c).at[BATCH].set(c_ctx)
    n_col = 3
    out = pl.pallas_call(
        _mod_kernel,
        grid=(depth, n_col),
        in_specs=[
            pl.BlockSpec((SUBLANES, D_MODEL), lambda l, j: (0, 0)),
            pl.BlockSpec((1, D_MODEL, D_MODEL), lambda l, j: (l, 0, j)),
            pl.BlockSpec((1, 1, D_MODEL), lambda l, j: (l, 0, j)),
        ],
        out_specs=pl.BlockSpec((1, SUBLANES, D_MODEL), lambda l, j: (l, 0, j)),
        out_shape=jax.ShapeDtypeStruct((depth, SUBLANES, 3 * D_MODEL), F32),
        compiler_params=_params(("arbitrary", "arbitrary"), 32),
        name="adaln_modulation",
    )(rows, w_ada, b_ada.reshape(depth, 1, 3 * D_MODEL))
    tabs = []
    for l in range(depth):
        lat = out[l, :BATCH].reshape(BATCH, 3, D_MODEL)
        cx = jnp.broadcast_to(out[l, BATCH].reshape(1, 3, D_MODEL), (BATCH, 3, D_MODEL))
        tab = jnp.stack([cx, lat], axis=1)
        tabs.append(jnp.pad(tab, ((0, 0), (0, 0), (0, SUBLANES - 3), (0, 0))))
    return tabs


def _modulated_norm(xt, nw, mod_ref, b=0):
    ms = jnp.mean(xt * xt, axis=-1, keepdims=True)
    y = xt * lax.rsqrt(ms + NORM_EPS) * nw
    return y * (1.0 + mod_ref[b, 0, 1:2, :]) + mod_ref[b, 0, 0:1, :]


def _lane_block_transpose(vs):
    n = len(vs)
    width = LANES // n
    blk = lax.broadcasted_iota(jnp.int32, vs[0].shape, 1) // width
    x = list(vs)
    d = n // 2
    while d >= 1:
        clear = (blk & d) == 0
        y = list(x)
        for i in range(n):
            if i & d == 0:
                a, b = x[i], x[i + d]
                y[i] = jnp.where(clear, a, pltpu.roll(b, width * d, 1))
                y[i + d] = jnp.where(clear, pltpu.roll(a, LANES - width * d, 1), b)
        x = y
        d //= 2
    return x


def _chunk_slot(b, cc):
    return (cc // 2) * (2 * BATCH) + 2 * b + (cc % 2)


def _proj0_kernel(x_ref, c_ref, mod_ref, nw_ref, cos_ref, sin_ref, w_ref,
                  q_ref, z_ref, kbd_ref, vt_ref):
    is_ctx = pl.program_id(0) == 0
    h = jnp.concatenate(
        [_modulated_norm(jnp.where(is_ctx, c_ref[b], x_ref[b]), nw_ref[...], mod_ref, b).astype(BF16)
         for b in range(BATCH)], axis=0)
    cos = cos_ref[...]
    sin = sin_ref[...]
    lane = lax.broadcasted_iota(jnp.int32, (ROW_TILE, LANES), 1)
    first_half = (lane % HEAD_DIM) < (HEAD_DIM // 2)
    low = lax.broadcasted_iota(jnp.int32, (BLOCK, LANES), 1) < HEAD_DIM

    def rope(t):
        partner = jnp.where(first_half, pltpu.roll(t, LANES - HEAD_DIM // 2, 1),
                            pltpu.roll(t, HEAD_DIM // 2, 1))
        return t * cos + partner * sin

    q = jnp.dot(h, w_ref[:, :ATTN_WIDTH], preferred_element_type=F32)
    k = jnp.dot(h, w_ref[:, ATTN_WIDTH:ATTN_WIDTH + KV_WIDTH], preferred_element_type=F32)
    v = jnp.dot(h, w_ref[:, ATTN_WIDTH + KV_WIDTH:ATTN_WIDTH + 2 * KV_WIDTH],
                preferred_element_type=F32)
    z = jnp.dot(h, w_ref[:, ATTN_WIDTH + 2 * KV_WIDTH:], preferred_element_type=F32)
    for b in range(BATCH):
        tile = slice(ROW_TILE * b, ROW_TILE * (b + 1))
        z_ref[b] = z[tile].astype(BF16)
        for j in range(N_SLOTS):
            m, gi = divmod(j, SLOTS_PER_M)
            qj = (rope(q[tile, LANES * j:LANES * (j + 1)]) * Q_SCALE).astype(BF16)
            for blk in range(ROW_TILE // BLOCK):
                q_ref[b, blk, m, BLOCK * gi:BLOCK * (gi + 1), :] = qj[BLOCK * blk:BLOCK * (blk + 1)]
        for m in range(N_KV_PAIRS):
            sl = slice(LANES * m, LANES * (m + 1))
            kr = rope(k[tile, sl])
            vm = v[tile, sl]
            for blk in range(ROW_TILE // BLOCK):
                rows = slice(BLOCK * blk, BLOCK * (blk + 1))
                kbd_ref[b, blk, m, :BLOCK, :] = jnp.where(low, kr[rows], 0.0).astype(BF16)
                kbd_ref[b, blk, m, BLOCK:, :] = jnp.where(low, 0.0, kr[rows]).astype(BF16)
                vt_ref[b, blk, LANES * m:LANES * (m + 1), :] = vm[rows].T.astype(BF16)


def _proj0(x, ctx, mod, norm_w, cos_tab, sin_tab, w_in):
    n_col = w_in.shape[1]
    blocks_per_tile = ROW_TILE // BLOCK
    kv_shape = jax.ShapeDtypeStruct((BATCH, N_BLOCKS, N_KV_PAIRS, 2 * BLOCK, LANES), BF16)
    kv_spec = pl.BlockSpec((BATCH, blocks_per_tile, N_KV_PAIRS, 2 * BLOCK, LANES),
                           lambda i: (0, i, 0, 0, 0))
    row_spec = pl.BlockSpec((BATCH, ROW_TILE, D_MODEL), lambda i: (0, i, 0))
    once = pl.Buffered(1)
    return pl.pallas_call(
        _proj0_kernel,
        grid=(N_ROW_TILES,),
        in_specs=[
            pl.BlockSpec((BATCH, ROW_TILE, D_MODEL), lambda i: (0, jnp.maximum(i - 1, 0), 0)),
            pl.BlockSpec((BATCH, ROW_TILE, D_MODEL), lambda i: (0, 0, 0), pipeline_mode=once),
            pl.BlockSpec((BATCH, 1, SUBLANES, D_MODEL), lambda i: (0, jnp.minimum(i, 1), 0, 0)),
            pl.BlockSpec((1, D_MODEL), lambda i: (0, 0)),
            pl.BlockSpec((ROW_TILE, LANES), lambda i: (i, 0)),
            pl.BlockSpec((ROW_TILE, LANES), lambda i: (i, 0)),
            pl.BlockSpec((D_MODEL, n_col), lambda i: (0, 0), pipeline_mode=once),
        ],
        out_specs=[
            pl.BlockSpec((BATCH, blocks_per_tile, N_KV_PAIRS, SLOTS_PER_M * BLOCK, LANES),
                         lambda i: (0, i, 0, 0, 0)),
            row_spec, kv_spec,
            pl.BlockSpec((BATCH, blocks_per_tile, KV_WIDTH, BLOCK), lambda i: (0, i, 0, 0))],
        out_shape=[
            jax.ShapeDtypeStruct((BATCH, N_BLOCKS, N_KV_PAIRS, SLOTS_PER_M * BLOCK, LANES), BF16),
            jax.ShapeDtypeStruct((BATCH, TOTAL, ATTN_WIDTH), BF16),
            kv_shape,
            jax.ShapeDtypeStruct((BATCH, N_BLOCKS, KV_WIDTH, BLOCK), BF16),
        ],
        compiler_params=_params(("arbitrary",), 56),
        name="attn_projection",
    )(x, ctx, mod, norm_w.reshape(1, D_MODEL), cos_tab, sin_tab, w_in)


def _attn_kernel(q_ref, z_ref, k0_ref, k1_ref, k2_ref, k3_ref, kx_ref,
                 v0_ref, v1_ref, v2_ref, v3_ref, vx_ref,
                 sink_ref, x_ref, c_ref, mod_ref, wo_ref, eye_ref, tri_ref, o_ref):
    step = pl.program_id(1)
    is_lat = step >= N_CTX_BLOCKS // ATTN_QBLOCKS
    n_first = ATTN_QBLOCKS * step - N_CTX_BLOCKS
    q_rows = UNIT_SLOTS * BLOCK
    blocked = tri_ref[2]
    bias = []
    for qb in range(ATTN_QBLOCKS):
        n = n_first + qb
        bias.append([jnp.where(jnp.logical_and(is_lat, n >= 1), tri_ref[0], blocked),
                     jnp.where(is_lat, tri_ref[3], blocked),
                     jnp.where(jnp.logical_and(is_lat, n <= SEQ // BLOCK - 2), tri_ref[1], blocked),
                     None, None])
    k_win = (k0_ref, k1_ref, k2_ref, k3_ref)
    v_win = (v0_ref, v1_ref, v2_ref, v3_ref)

    def kpiece(qb, p, m):
        if p < 3:
            return k_win[qb + p][0, 0, m], v_win[qb + p][0, 0]
        return kx_ref[0, p - 3, m], vx_ref[0, p - 3]

    n_piece = 3 + N_CTX_BLOCKS
    nt = (((1,), (1,)), ((), ()))

    half_rows = slice(0, BLOCK), slice(BLOCK, 2 * BLOCK)

    def scores(qb, m, h):
        qu = q_ref[0, qb, m, q_rows * h:q_rows * (h + 1), :]
        qu_masked = jnp.concatenate([qu, eye_ref[...]], axis=1)
        s_list = []
        for p in range(n_piece):
            kbd, _ = kpiece(qb, p, m)
            if bias[qb][p] is None:
                s = lax.dot_general(kbd, qu, nt, preferred_element_type=F32)
            else:
                s = lax.dot_general(jnp.concatenate([kbd, bias[qb][p]], axis=1), qu_masked, nt,
                                    preferred_element_type=F32)
            s_list.append(s)
        return s_list

    def finish(qb, m, h, s_list):
        slot0 = SLOTS_PER_M * m + UNIT_SLOTS * h
        halves = []
        for hs in range(2):
            sink = jnp.concatenate(
                [sink_ref[2 * (slot0 + gi) + hs:2 * (slot0 + gi) + hs + 1, :]
                 for gi in range(UNIT_SLOTS)], axis=1)
            mx = sink
            for s in s_list:
                mx = jnp.maximum(mx, jnp.max(s[half_rows[hs]], axis=0, keepdims=True))
            probs = [jnp.exp2(s[half_rows[hs]] - mx) for s in s_list]
            denom = jnp.exp2(sink - mx)
            for e in probs:
                denom = denom + jnp.sum(e, axis=0, keepdims=True)
            probs = [e.astype(BF16) for e in probs]
            kv_head = 2 * m + hs
            acc = None
            for p0 in range(0, n_piece, 2):
                group = list(range(p0, min(p0 + 2, n_piece)))
                vt = jnp.concatenate(
                    [kpiece(qb, p, m)[1][HEAD_DIM * kv_head:HEAD_DIM * (kv_head + 1), :]
                     for p in group], axis=1)
                pt = jnp.concatenate([probs[p] for p in group], axis=0)
                part = jnp.dot(vt, pt, preferred_element_type=F32)
                acc = part if acc is None else acc + part
            halves.append(acc * (1.0 / denom))
        o_t = jnp.concatenate(halves, axis=0)
        outs = []
        for gi in range(UNIT_SLOTS):
            j = slot0 + gi
            o = o_t[:, BLOCK * gi:BLOCK * (gi + 1)].T
            zj = z_ref[0, BLOCK * qb:BLOCK * (qb + 1), LANES * j:LANES * (j + 1)].astype(F32)
            outs.append((o * (zj * jax.nn.sigmoid(zj))).astype(BF16))
        return jnp.concatenate(outs, axis=1)

    units = [(qb, m, h) for m in range(N_KV_PAIRS) for h in range(SLOTS_PER_M // UNIT_SLOTS)
             for qb in range(ATTN_QBLOCKS)]
    y = None
    gated = []
    pending = scores(*units[0])
    for u, (qb, m, h) in enumerate(units):
        nxt = scores(*units[u + 1]) if u + 1 < len(units) else None
        gated.append(finish(qb, m, h, pending))
        pending = nxt
        if qb == ATTN_QBLOCKS - 1:
            slot0 = SLOTS_PER_M * m + UNIT_SLOTS * h
            part = jnp.dot(jnp.concatenate(gated, axis=0),
                           wo_ref[LANES * slot0:LANES * (slot0 + UNIT_SLOTS), :],
                           preferred_element_type=F32)
            y = part if y is None else y + part
            gated = []
    resid = jnp.where(is_lat, x_ref[0], c_ref[0])
    o_ref[0] = resid + mod_ref[0, 0, 2:3, :] * y


def _attention(q, z, kbd, vt, sink_tab, x, ctx, mod, w_out):
    last = N_BLOCKS - 1
    n_m = KV_WIDTH // LANES
    nq = ATTN_QBLOCKS
    ctx_steps = N_CTX_BLOCKS // nq
    row_spec = pl.BlockSpec((1, nq * BLOCK, ATTN_WIDTH), lambda b, i: (b, i, 0))

    def kv_spec(off):
        return pl.BlockSpec((1, 1, n_m, 2 * BLOCK, LANES),
                            lambda b, i: (b, jnp.clip(nq * i + off, 0, last), 0, 0, 0))

    ctx_kv_spec = pl.BlockSpec((1, N_CTX_BLOCKS, n_m, 2 * BLOCK, LANES),
                               lambda b, i: (b, 0, 0, 0, 0))

    def vt_spec(off):
        return pl.BlockSpec((1, 1, KV_WIDTH, BLOCK),
                            lambda b, i: (b, jnp.clip(nq * i + off, 0, last), 0, 0))

    ctx_vt_spec = pl.BlockSpec((1, N_CTX_BLOCKS, KV_WIDTH, BLOCK), lambda b, i: (b, 0, 0, 0))
    off = np.arange(BLOCK)
    eye = np.tile(np.eye(BLOCK, dtype=np.float32), (UNIT_SLOTS, 1))
    key_ge = np.where(off[:, None] >= off[None, :], 0.0, NEG_INF)
    key_le = np.where(off[:, None] <= off[None, :], 0.0, NEG_INF)
    tri = np.stack([np.tile(t, (2, 1)) for t in
                    (key_ge, key_le, np.full((BLOCK, BLOCK), NEG_INF), np.zeros((BLOCK, BLOCK)))])
    consts = [jnp.asarray(a, BF16) for a in (eye, tri)]
    return pl.pallas_call(
        _attn_kernel,
        grid=(BATCH, N_BLOCKS // nq),
        in_specs=[
            pl.BlockSpec((1, nq, n_m, SLOTS_PER_M * BLOCK, LANES), lambda b, i: (b, i, 0, 0, 0)),
            row_spec,
            kv_spec(-1), kv_spec(0), kv_spec(1), kv_spec(2), ctx_kv_spec,
            vt_spec(-1), vt_spec(0), vt_spec(1), vt_spec(2), ctx_vt_spec,
            pl.BlockSpec((2 * N_SLOTS, LANES), lambda b, i: (0, 0)),
            pl.BlockSpec((1, nq * BLOCK, D_MODEL),
                         lambda b, i: (b, jnp.maximum(i - ctx_steps, 0), 0)),
            pl.BlockSpec((1, nq * BLOCK, D_MODEL),
                         lambda b, i: (b, jnp.minimum(i, ctx_steps - 1), 0)),
            pl.BlockSpec((1, 1, SUBLANES, D_MODEL),
                         lambda b, i: (b, jnp.minimum(i // ctx_steps, 1), 0, 0)),
            pl.BlockSpec((ATTN_WIDTH, D_MODEL), lambda b, i: (0, 0)),
        ] + [pl.BlockSpec(a.shape, lambda b, i, nd=a.ndim: (0,) * nd) for a in consts],
        out_specs=pl.BlockSpec((1, nq * BLOCK, D_MODEL), lambda b, i: (b, i, 0)),
        out_shape=jax.ShapeDtypeStruct((BATCH, TOTAL, D_MODEL), F32),
        compiler_params=_params(("arbitrary", "arbitrary"), 48),
        name="window_attention",
    )(q, z, kbd, kbd, kbd, kbd, kbd, vt, vt, vt, vt, vt, sink_tab, x, ctx, mod, w_out,
      *consts)


def _proj1_kernel(x_ref, mod_ref, nw_ref, w_ref, u_ref, z_ref, scr_ref):
    h = jnp.concatenate(
        [_modulated_norm(x_ref[b], nw_ref[...], mod_ref, b).astype(BF16) for b in range(BATCH)],
        axis=0)
    u = jnp.dot(h, w_ref[:, :D_MODEL], preferred_element_type=F32)
    z = jnp.dot(h, w_ref[:, D_MODEL:], preferred_element_type=F32)
    for b in range(BATCH):
        z_ref[b] = z[ROW_TILE * b:ROW_TILE * (b + 1)].astype(BF16)
    n_slab = D_MODEL // LANES
    for b in range(BATCH):
        for cc in range(CHUNKS_PER_TILE):
            r0 = ROW_TILE * b + CHUNK * cc
            q0 = SLOT_PITCH * _chunk_slot(b, cc)
            for k in range(n_slab):
                scr_ref[k, q0:q0 + CHUNK, :] = u[r0:r0 + CHUNK, LANES * k:LANES * (k + 1)]
    groups_per_slab = LANES // SSM_GROUP
    for k in range(n_slab):
        for pp in range(PAIRS_PER_TILE // 2):
            parts = []
            for p in (2 * pp, 2 * pp + 1):
                rows = [scr_ref[k, pl.ds(SLOT_PITCH * SUBLANES * p + s, SUBLANES,
                                         stride=SLOT_PITCH), :] for s in range(CHUNK)]
                parts.append([_lane_block_transpose(rows[SUBLANES * m2:SUBLANES * (m2 + 1)])
                              for m2 in range(CHUNK // SUBLANES)])
            for m2 in range(CHUNK // SUBLANES):
                for gl in range(groups_per_slab):
                    val = jnp.concatenate([parts[0][m2][gl], parts[1][m2][gl]], axis=0)
                    u_ref[groups_per_slab * k + gl, 2 * SUBLANES * pp:2 * SUBLANES * (pp + 1),
                          LANES * m2:LANES * (m2 + 1)] = val.astype(BF16)


def _proj1(xc, mod, norm_w, w_in):
    row_spec = pl.BlockSpec((BATCH, ROW_TILE, D_MODEL), lambda i: (0, i, 0))
    tile_rows = PAIRS_PER_TILE * SUBLANES
    return pl.pallas_call(
        _proj1_kernel,
        grid=(N_ROW_TILES,),
        in_specs=[
            row_spec,
            pl.BlockSpec((BATCH, 1, SUBLANES, D_MODEL), lambda i: (0, jnp.minimum(i, 1), 0, 0)),
            pl.BlockSpec((1, D_MODEL), lambda i: (0, 0)),
            pl.BlockSpec((D_MODEL, 2 * D_MODEL), lambda i: (0, 0)),
        ],
        out_specs=[pl.BlockSpec((SSM_GROUPS, tile_rows, CHUNK_W), lambda i: (0, i, 0)), row_spec],
        out_shape=[jax.ShapeDtypeStruct((SSM_GROUPS, SCAN_ROWS, CHUNK_W), BF16),
                   jax.ShapeDtypeStruct((BATCH, TOTAL, D_MODEL), BF16)],
        scratch_shapes=[pltpu.VMEM((D_MODEL // LANES, BATCH * CHUNKS_PER_TILE * SLOT_PITCH, LANES),
                                   F32)],
        compiler_params=_params(("arbitrary",), 56),
        name="ssm_projection",
    )(xc, mod, norm_w.reshape(1, D_MODEL), w_in)


def _s5_kernel(u_ref, m_ref, ws_ref, wy_ref, cst_ref, y_ref, s4_ref, xp_ref):
    even = lax.broadcasted_iota(jnp.int32, (SUBLANES, STATE_W), 0) % 2 == 0
    down = 1
    up = SUBLANES - 1
    fwd = slice(0, STATE_W)
    bwd = slice(STATE_W, 2 * STATE_W)
    fwd_sw = slice(2 * STATE_W, 3 * STATE_W)
    bwd_sw = slice(3 * STATE_W, 4 * STATE_W)
    for g in range(GROUP_BATCH):
        s = jnp.dot(u_ref[g], ws_ref[g], preferred_element_type=F32)
        s = s.reshape(N_SCAN_BLOCKS, SUBLANES, 4 * STATE_W)
        q1f, q2f, q1b, q2b = cst_ref[g, 2], cst_ref[g, 3], cst_ref[g, 6], cst_ref[g, 7]
        zf, zb, zfs, zbs = s[..., fwd], s[..., bwd], s[..., fwd_sw], s[..., bwd_sw]
        rzf, rzfs = pltpu.roll(zf, down, 1), pltpu.roll(zfs, down, 1)
        rzb, rzbs = pltpu.roll(zb, up, 1), pltpu.roll(zbs, up, 1)
        t = jnp.concatenate([zf + q1f * rzf + q2f * rzfs, zb + q1b * rzb + q2b * rzbs,
                             zfs + q1f * rzfs - q2f * rzf, zbs + q1b * rzbs - q2b * rzb], axis=-1)
        s4_ref[g] = t.reshape(SCAN_ROWS, 4 * STATE_W)

    def step(j, carry):
        jb = jnp.where(j < N_CTX_SCAN_BLOCKS, N_CTX_SCAN_BLOCKS - 1 - j,
                       N_SCAN_BLOCKS - 1 + N_CTX_SCAN_BLOCKS - j)
        rf = pl.ds(pl.multiple_of(j * SUBLANES, SUBLANES), SUBLANES)
        rb = pl.ds(pl.multiple_of(jb * SUBLANES, SUBLANES), SUBLANES)
        new = []
        for g in range(GROUP_BATCH):
            cf, cfs, cb, cbs = carry[4 * g:4 * g + 4]
            p1f, p2f, p1b, p2b = cst_ref[g, 0], cst_ref[g, 1], cst_ref[g, 4], cst_ref[g, 5]
            xf = p1f * cf + p2f * cfs + s4_ref[g, rf, fwd]
            xfs = p1f * cfs - p2f * cf + s4_ref[g, rf, fwd_sw]
            xp_ref[g, rf, fwd] = jnp.where(even, cf, pltpu.roll(xf, down, 0))
            new += [jnp.where(even, pltpu.roll(xf, up, 0), xf),
                    jnp.where(even, pltpu.roll(xfs, up, 0), xfs)]
            xb = p1b * cb + p2b * cbs + s4_ref[g, rb, bwd]
            xbs = p1b * cbs - p2b * cb + s4_ref[g, rb, bwd_sw]
            xp_ref[g, rb, bwd] = jnp.where(even, pltpu.roll(xb, up, 0), cb)
            new += [jnp.where(even, xb, pltpu.roll(xb, down, 0)),
                    jnp.where(even, xbs, pltpu.roll(xbs, down, 0))]
        return tuple(new)

    zero = jnp.zeros((SUBLANES, STATE_W), F32)
    lax.fori_loop(0, N_SCAN_BLOCKS, step, (zero,) * (4 * GROUP_BATCH))
    for g in range(GROUP_BATCH):
        y_ref[g] = (
            jnp.dot(u_ref[g, CTX_SCAN_ROWS:, :], m_ref[g], preferred_element_type=F32)
            + lax.dot_general(xp_ref[g, CTX_SCAN_ROWS:, :].astype(BF16), wy_ref[g],
                              (((1,), (1,)), ((), ())), preferred_element_type=F32))


def _s5_core(u_g, m_mat, ws_mat, wy_mat, consts):
    lat_rows = SCAN_ROWS - CTX_SCAN_ROWS

    def gspec(*tail):
        return pl.BlockSpec((GROUP_BATCH,) + tail, lambda i: (i,) + (0,) * len(tail))

    return pl.pallas_call(
        _s5_kernel,
        grid=(SSM_GROUPS // GROUP_BATCH,),
        in_specs=[
            gspec(SCAN_ROWS, CHUNK_W),
            gspec(CHUNK_W, CHUNK_W),
            gspec(CHUNK_W, 4 * STATE_W),
            gspec(CHUNK_W, 2 * STATE_W),
            gspec(8, SUBLANES, STATE_W),
        ],
        out_specs=gspec(lat_rows, CHUNK_W),
        out_shape=jax.ShapeDtypeStruct((SSM_GROUPS, lat_rows, CHUNK_W), F32),
        scratch_shapes=[
            pltpu.VMEM((GROUP_BATCH, SCAN_ROWS, 4 * STATE_W), F32),
            pltpu.VMEM((GROUP_BATCH, SCAN_ROWS, 2 * STATE_W), F32),
        ],
        compiler_params=_params(("arbitrary",), 48),
        name="s5_scan",
    )(u_g, m_mat, ws_mat, wy_mat, consts)


def _out1_kernel(y_ref, z_ref, x_ref, mod_ref, wg_ref, wo_ref, fnw_ref, o_ref, scr_ref):
    n_slab = D_MODEL // LANES
    groups_per_slab = LANES // SSM_GROUP
    pairs_per_phase = PAIRS_PER_TILE // OUT_PHASES
    rows_per_phase = ROW_TILE // OUT_PHASES

    def relayout(ph):
        for k in range(n_slab):
            for p in range(pairs_per_phase * ph, pairs_per_phase * (ph + 1)):
                for m2 in range(CHUNK // SUBLANES):
                    vals = [y_ref[groups_per_slab * k + gl, SUBLANES * p:SUBLANES * (p + 1),
                                  LANES * m2:LANES * (m2 + 1)] for gl in range(groups_per_slab)]
                    steps = _lane_block_transpose(vals)
                    for s2 in range(SUBLANES):
                        t_idx = SUBLANES * m2 + s2
                        scr_ref[k, pl.ds(SLOT_PITCH * SUBLANES * p + t_idx, SUBLANES,
                                         stride=SLOT_PITCH), :] = steps[s2]

    def compute(ph):
        chunks = range(2 * pairs_per_phase * ph, 2 * pairs_per_phase * (ph + 1))
        y = jnp.concatenate(
            [jnp.concatenate(
                [scr_ref[k, SLOT_PITCH * _chunk_slot(b, cc):SLOT_PITCH * _chunk_slot(b, cc) + CHUNK, :]
                 for k in range(n_slab)], axis=1)
             for b in range(BATCH) for cc in chunks], axis=0)
        g = (0.5 * y * (1.0 + lax.erf(y * (2.0 ** -0.5)))).astype(BF16)
        t = jnp.dot(g, wg_ref[...], preferred_element_type=F32)
        rows = slice(rows_per_phase * ph, rows_per_phase * (ph + 1))
        z = jnp.concatenate([z_ref[b, rows, :] for b in range(BATCH)], axis=0).astype(F32)
        r = (t[:, :D_MODEL] * jax.nn.sigmoid(t[:, D_MODEL:]) * (z * jax.nn.sigmoid(z))).astype(BF16)
        o = jnp.dot(r, wo_ref[...], preferred_element_type=F32)
        for b in range(BATCH):
            x2 = (x_ref[b, rows, :]
                  + mod_ref[b, 0, 2:3, :] * o[rows_per_phase * b:rows_per_phase * (b + 1)])
            ms = jnp.mean(x2 * x2, axis=-1, keepdims=True)
            o_ref[b, rows, :] = x2 * lax.rsqrt(ms + NORM_EPS) * fnw_ref[...]

    relayout(0)
    for ph in range(OUT_PHASES):
        if ph + 1 < OUT_PHASES:
            relayout(ph + 1)
        compute(ph)


def _out1(y_g, z, xc, mod, w_glu, w_out, final_norm_w):
    ctx_tiles = CTX_LEN // ROW_TILE
    tile_rows = PAIRS_PER_TILE * SUBLANES
    lat_spec = pl.BlockSpec((BATCH, ROW_TILE, D_MODEL), lambda i: (0, i, 0))
    all_spec = pl.BlockSpec((BATCH, ROW_TILE, D_MODEL), lambda i: (0, i + ctx_tiles, 0))
    return pl.pallas_call(
        _out1_kernel,
        grid=(SEQ // ROW_TILE,),
        in_specs=[
            pl.BlockSpec((SSM_GROUPS, tile_rows, CHUNK_W), lambda i: (0, i, 0)),
            all_spec, all_spec,
            pl.BlockSpec((BATCH, 1, SUBLANES, D_MODEL), lambda i: (0, 1, 0, 0)),
            pl.BlockSpec((D_MODEL, 2 * D_MODEL), lambda i: (0, 0)),
            pl.BlockSpec((D_MODEL, D_MODEL), lambda i: (0, 0)),
            pl.BlockSpec((1, D_MODEL), lambda i: (0, 0)),
        ],
        out_specs=lat_spec,
        out_shape=jax.ShapeDtypeStruct((BATCH, SEQ, D_MODEL), F32),
        scratch_shapes=[pltpu.VMEM((D_MODEL // LANES, BATCH * CHUNKS_PER_TILE * SLOT_PITCH, LANES),
                                   F32)],
        compiler_params=_params(("arbitrary",), 56),
        name="ssm_output",
    )(y_g, z, xc, mod, w_glu, w_out, final_norm_w.reshape(1, D_MODEL))


def _slot_order(t, lead):
    n_m = KV_WIDTH // LANES
    gq = N_HEADS // N_KV_HEADS
    shape = t.shape
    t = t.reshape(shape[:lead] + (n_m, 2, gq) + shape[lead + 1:])
    perm = tuple(range(lead)) + (lead, lead + 2, lead + 1) + tuple(range(lead + 3, t.ndim))
    return jnp.transpose(t, perm)


def _rope_order(t):
    shape = t.shape
    t = t.reshape(shape[:-1] + (2, 2, ROPE_FREQS))
    return jnp.swapaxes(t, -3, -2).reshape(shape)


def _attn_weights(w_in, w_out, sink):
    wq = w_in[:, :ATTN_WIDTH].reshape(D_MODEL, N_HEADS, HEAD_DIM)
    wq = _slot_order(_rope_order(wq), 1).reshape(D_MODEL, ATTN_WIDTH)
    wk = w_in[:, ATTN_WIDTH:ATTN_WIDTH + KV_WIDTH].reshape(D_MODEL, N_KV_HEADS, HEAD_DIM)
    wk = _rope_order(wk).reshape(D_MODEL, KV_WIDTH)
    wv = w_in[:, ATTN_WIDTH + KV_WIDTH:ATTN_WIDTH + 2 * KV_WIDTH]
    wz = w_in[:, ATTN_WIDTH + 2 * KV_WIDTH:].reshape(D_MODEL, N_HEADS, HEAD_DIM)
    wz = _slot_order(wz, 1).reshape(D_MODEL, ATTN_WIDTH)
    w_in_p = jnp.concatenate([wq, wk, wv, wz], axis=1).astype(BF16)
    wo = _slot_order(w_out.reshape(N_HEADS, HEAD_DIM, D_MODEL), 0).reshape(ATTN_WIDTH, D_MODEL)
    sink_p = _slot_order(sink.astype(F32).reshape(N_HEADS), 0).reshape(2 * N_SLOTS)
    sink_tab = jnp.broadcast_to((sink_p * LOG2E)[:, None], (2 * N_SLOTS, LANES))
    return w_in_p, wo.astype(BF16), sink_tab


def _rope_tables():
    inv = ROPE_BASE ** (-np.arange(ROPE_FREQS, dtype=np.float64) / ROPE_FREQS)
    pos = np.arange(SEQ)
    row = (pos // GRID_W)[:, None] * inv
    col = (pos % GRID_W)[:, None] * inv
    w = np.arange(LANES) % HEAD_DIM
    half, axis, f = w // 32, (w % 32) // 16, w % 16
    ang = np.where((axis == 0)[None, :], row[:, f], col[:, f])
    sign = np.where(half == 0, -1.0, 1.0)[None, :]
    cos = np.concatenate([np.ones((CTX_LEN, LANES)), np.cos(ang)], axis=0)
    sin = np.concatenate([np.zeros((CTX_LEN, LANES)), np.sin(ang) * sign], axis=0)
    return jnp.asarray(cos, F32), jnp.asarray(sin, F32)


def _s5_operators(lam_re, lam_im, log_dt, b_re, b_im, c_re, c_im, d_skip):
    t_len = CHUNK
    n_pow = 2 * t_len + 1
    lr, li = lam_re.astype(F32), lam_im.astype(F32)
    dt = jnp.exp(log_dt.astype(F32))[..., None]
    mag = jnp.exp(lr * dt)
    sq = [(mag * jnp.cos(li * dt), mag * jnp.sin(li * dt))]
    while 2 ** len(sq) < n_pow:
        r, i = sq[-1]
        sq.append((r * r - i * i, 2.0 * r * i))
    ks = np.arange(n_pow)
    pr = jnp.ones(lr.shape + (n_pow,), F32)
    pi = jnp.zeros(lr.shape + (n_pow,), F32)
    for bit, (r, i) in enumerate(sq):
        on = jnp.asarray((ks >> bit) & 1 == 1)
        fr = jnp.where(on, r[..., None], 1.0)
        fi = jnp.where(on, i[..., None], 0.0)
        pr, pi = pr * fr - pi * fi, pr * fi + pi * fr
    ar1, ai1 = sq[0][0] - 1.0, sq[0][1]
    den = lr * lr + li * li
    gr, gi = (ar1 * lr + ai1 * li) / den, (ai1 * lr - ar1 * li) / den
    br_, bi_ = b_re.astype(F32), b_im.astype(F32)
    bbr = gr[..., None] * br_ - gi[..., None] * bi_
    bbi = gr[..., None] * bi_ + gi[..., None] * br_
    pad_k = POW_ROWS - n_pow
    prt, pit = jnp.swapaxes(pr, 2, 3), jnp.swapaxes(pi, 2, 3)
    pw = jnp.pad(jnp.concatenate([prt, prt, pit, pit], axis=-1),
                 ((0, 0), (0, 0), (0, pad_k), (0, 0)))
    brt, bit = jnp.swapaxes(bbr, 2, 3), jnp.swapaxes(bbi, 2, 3)
    bb = jnp.concatenate([brt, bit, -bit, brt, bit, brt, brt, -bit], axis=-1)
    cr, ci = c_re.astype(F32), c_im.astype(F32)
    cm = jnp.stack([jnp.concatenate([cr, -ci], axis=-1),
                    jnp.concatenate([-ci, -cr], axis=-1)], axis=2)
    dv = jnp.tile(d_skip.astype(F32).reshape(SSM_GROUPS, 1, SSM_GROUP), (1, 1, t_len))
    return _s5_operator_call(pw, bb, cm, dv)


def _operator_constants():
    t_len = CHUNK
    s_of_row = np.arange(CHUNK_W) // SSM_GROUP
    k_ar = np.arange(POW_ROWS)
    oh_f = (k_ar[None, :] == (t_len - 1 - s_of_row)[:, None]).astype(np.float32)
    oh_b = (k_ar[None, :] == s_of_row[:, None]).astype(np.float32)
    t_of_lane = np.arange(CHUNK_W) // SSM_GROUP
    expo = [s_of_row + 1, t_len - s_of_row, s_of_row, t_len - 1 - s_of_row]
    sel = np.stack([(k_ar[None, :] == e[:, None]) for e in expo]).astype(np.float32)
    h_of_lane = np.arange(CHUNK_W) % SSM_GROUP
    dmask = ((s_of_row[:, None] == t_of_lane[None, :])
             & ((np.arange(CHUNK_W) % SSM_GROUP)[:, None] == h_of_lane[None, :])).astype(np.float32)
    return oh_f, oh_b, sel, dmask


def _s5_op_kernel(*refs):
    for gg in range(OP_GROUP_BATCH):
        _s5_op_group(gg, *refs)


def _s5_op_group(gg, pw_ref, bb_ref, cm_ref, dv_ref, ohf_ref, ohb_ref, sel_ref, dmask_ref,
                 m_ref, ws_ref, wyt_ref, cst_ref):
    t_len = CHUNK

    def split(a):
        hi = a.astype(BF16)
        return hi, (a - hi.astype(F32)).astype(BF16)

    def pick_rows(onehot, table):
        hi, lo = split(table)
        return (jnp.dot(onehot, hi, preferred_element_type=F32)
                + jnp.dot(onehot, lo, preferred_element_type=F32))

    def tile_rows(a):
        return jnp.concatenate([a] * t_len, axis=0)

    main, swapped = [], []
    for d, oh_ref in ((0, ohf_ref), (1, ohb_ref)):
        pp = pick_rows(oh_ref[...], pw_ref[d, gg])
        p_re, p_im = pp[:, :STATE_W], pp[:, STATE_W:]
        b0, b1, b2, b3 = [tile_rows(bb_ref[d, gg, :, STATE_W * i:STATE_W * (i + 1)])
                          for i in range(4)]
        main.append(p_re * b0 + p_im * b1)
        swapped.append(p_re * b2 + p_im * b3)
    ws_ref[gg] = jnp.concatenate(main + swapped, axis=1).astype(BF16)

    c_tiled = [[tile_rows(cm_ref[d, gg, i]) for i in range(2)] for d in range(2)]

    def block_t(d, pat):
        pp = pick_rows(sel_ref[pat], pw_ref[d, gg])
        return pp[:, :STATE_W] * c_tiled[d][0] + pp[:, STATE_W:] * c_tiled[d][1]

    wyt_ref[gg] = jnp.concatenate([block_t(0, 0), block_t(1, 1)], axis=1).astype(BF16)

    nt = (((1,), (1,)), ((), ()))
    kt_f = lax.dot_general(bb_ref[0, gg, :, :STATE_W], block_t(0, 2), nt,
                           preferred_element_type=F32, precision=lax.Precision.HIGHEST)
    kt_b = lax.dot_general(bb_ref[1, gg, :, :STATE_W], block_t(1, 3), nt,
                           preferred_element_type=F32, precision=lax.Precision.HIGHEST)
    lane = lax.broadcasted_iota(jnp.int32, (SSM_GROUP, CHUNK_W), 1)
    skip = dv_ref[gg]
    for s in range(t_len):
        fwd = kt_f if s == 0 else pltpu.roll(kt_f, SSM_GROUP * s, 1)
        back = t_len - 1 - s
        bwd = kt_b if back == 0 else pltpu.roll(kt_b, CHUNK_W - SSM_GROUP * back, 1)
        rows = (jnp.where(lane >= SSM_GROUP * s, fwd, 0.0)
                + jnp.where(lane < SSM_GROUP * (s + 1), bwd, 0.0)
                + dmask_ref[SSM_GROUP * s:SSM_GROUP * (s + 1), :] * skip)
        m_ref[gg, SSM_GROUP * s:SSM_GROUP * (s + 1), :] = rows.astype(BF16)

    even = lax.broadcasted_iota(jnp.int32, (SUBLANES, STATE_W), 0) % 2 == 0
    sign = jnp.where(lax.broadcasted_iota(jnp.int32, (1, STATE_W), 1) < SSM_STATE, -1.0, 1.0)

    def w12(d, k):
        row = pw_ref[d, gg, k:k + 1, :]
        return row[:, :STATE_W], row[:, STATE_W:] * sign

    zero = (jnp.zeros((1, STATE_W), F32),) * 2
    pairs = [(w12(0, t_len), w12(0, 2 * t_len)), (zero, w12(0, t_len)),
             (w12(1, 2 * t_len), w12(1, t_len)), (w12(1, t_len), zero)]
    idx = 0
    for top, bot in pairs:
        for part in range(2):
            cst_ref[gg, idx] = jnp.where(even, jnp.broadcast_to(top[part], (SUBLANES, STATE_W)),
                                         jnp.broadcast_to(bot[part], (SUBLANES, STATE_W)))
            idx += 1


def _s5_operator_call(pw, bb, cm, dv):
    oh_f, oh_b, sel, dmask = _operator_constants()
    consts = [jnp.asarray(oh_f, BF16), jnp.asarray(oh_b, BF16), jnp.asarray(sel, BF16),
              jnp.asarray(dmask)]

    def per_group(*tail):
        n = len(tail)
        return pl.BlockSpec((2, OP_GROUP_BATCH) + tail, lambda g: (0, g) + (0,) * n)

    def whole(a):
        return pl.BlockSpec(a.shape, lambda g: (0,) * a.ndim)

    def out(*tail):
        return pl.BlockSpec((OP_GROUP_BATCH,) + tail, lambda g: (g,) + (0,) * len(tail))

    return pl.pallas_call(
        _s5_op_kernel,
        grid=(SSM_GROUPS // OP_GROUP_BATCH,),
        in_specs=[per_group(POW_ROWS, 2 * STATE_W), per_group(SSM_GROUP, 4 * STATE_W),
                  per_group(2, SSM_GROUP, STATE_W),
                  pl.BlockSpec((OP_GROUP_BATCH, 1, CHUNK_W), lambda g: (g, 0, 0))]
                 + [whole(a) for a in consts],
        out_specs=[out(CHUNK_W, CHUNK_W), out(CHUNK_W, 4 * STATE_W), out(CHUNK_W, 2 * STATE_W),
                   out(8, SUBLANES, STATE_W)],
        out_shape=[jax.ShapeDtypeStruct((SSM_GROUPS, CHUNK_W, CHUNK_W), BF16),
                   jax.ShapeDtypeStruct((SSM_GROUPS, CHUNK_W, 4 * STATE_W), BF16),
                   jax.ShapeDtypeStruct((SSM_GROUPS, CHUNK_W, 2 * STATE_W), BF16),
                   jax.ShapeDtypeStruct((SSM_GROUPS, 8, SUBLANES, STATE_W), F32)],
        compiler_params=_params(("arbitrary",), 32),
        name="s5_operators",
    )(pw, bb, cm, dv, *consts)


def kernel(x, c, ctx, c_ctx, norm_w, w_ada, b_ada, attn_w_in, attn_sink, attn_w_out,
           ssm_w_in, ssm_lam_re, ssm_lam_im, ssm_log_dt, ssm_b_re, ssm_b_im, ssm_c_re, ssm_c_im,
           ssm_d, ssm_w_glu, ssm_w_out, final_norm_w):
    mod0, mod1 = _modulation(c, c_ctx, w_ada, b_ada)

    w_in0, w_out0, sink_tab = _attn_weights(attn_w_in[0], attn_w_out[0], attn_sink[0])
    cos_tab, sin_tab = _rope_tables()
    q, z0, kbd, vt = _proj0(x, ctx, mod0, norm_w[0], cos_tab, sin_tab, w_in0)
    xc1 = _attention(q, z0, kbd, vt, sink_tab, x, ctx, mod0, w_out0)

    u_g, z1 = _proj1(xc1, mod1, norm_w[1], ssm_w_in[0].astype(BF16))
    m_mat, ws_mat, wy_mat, consts = _s5_operators(
        ssm_lam_re[0], ssm_lam_im[0], ssm_log_dt[0], ssm_b_re[0], ssm_b_im[0],
        ssm_c_re[0], ssm_c_im[0], ssm_d[0])
    y_g = _s5_core(u_g, m_mat, ws_mat, wy_mat, consts)
    return _out1(y_g, z1, xc1, mod1, ssm_w_glu[0].astype(BF16), ssm_w_out[0].astype(BF16),
                 final_norm_w)
```

```python
import functools
import math

import jax
import jax.numpy as jnp
import numpy as np
from jax import lax
from jax.experimental import pallas as pl
from jax.experimental.pallas import tpu as pltpu

F32 = jnp.float32
BF16 = jnp.bfloat16

D_MODEL = 1024
BATCH = 4
SEQ = 4096
GRID_W = 64
CTX_LEN = 256
TOTAL = CTX_LEN + SEQ
HEAD_DIM = 64
N_HEADS = 16
N_KV_HEADS = 4
ATTN_WIDTH = N_HEADS * HEAD_DIM
KV_WIDTH = N_KV_HEADS * HEAD_DIM
BLOCK = 128
N_BLOCKS = TOTAL // BLOCK
N_CTX_BLOCKS = CTX_LEN // BLOCK
ROPE_BASE = 10000.0
ROPE_FREQS = HEAD_DIM // 4
SSM_GROUP = 16
SSM_GROUPS = D_MODEL // SSM_GROUP
SSM_STATE = 64
NORM_EPS = 1e-6
NEG_INF = -1e30

LANES = 128
SUBLANES = 8
N_SLOTS = ATTN_WIDTH // LANES
N_KV_PAIRS = KV_WIDTH // LANES
SLOTS_PER_M = N_SLOTS // N_KV_PAIRS
ATTN_QBLOCKS = 2
assert N_CTX_BLOCKS % ATTN_QBLOCKS == 0 and N_BLOCKS % ATTN_QBLOCKS == 0
UNIT_SLOTS = 4
LOG2E = math.log2(math.e)
Q_SCALE = HEAD_DIM ** -0.5 * LOG2E
ROW_TILE = 256
N_ROW_TILES = TOTAL // ROW_TILE
CHUNK = 16
N_CHUNKS = TOTAL // CHUNK
N_CTX_CHUNKS = CTX_LEN // CHUNK
CHUNK_W = CHUNK * SSM_GROUP
CHUNKS_PER_TILE = ROW_TILE // CHUNK
PAIRS_PER_TILE = CHUNKS_PER_TILE // 2
SLOT_PITCH = 24
OUT_PHASES = 4
assert SLOT_PITCH >= CHUNK and SLOT_PITCH % SUBLANES == 0
STATE_W = 2 * SSM_STATE
SCAN_ROWS = N_CHUNKS * BATCH
CTX_SCAN_ROWS = N_CTX_CHUNKS * BATCH
N_SCAN_BLOCKS = SCAN_ROWS // SUBLANES
N_CTX_SCAN_BLOCKS = CTX_SCAN_ROWS // SUBLANES
GROUP_BATCH = 4
OP_GROUP_BATCH = 4
POW_ROWS = 48

assert BATCH * 2 == SUBLANES


def _params(semantics, vmem_mb):
    return pltpu.CompilerParams(dimension_semantics=semantics,
                                vmem_limit_bytes=vmem_mb * 1024 * 1024)


def _mod_kernel(c_ref, w_ref, b_ref, o_ref):
    c = c_ref[...]
    a = c * jax.nn.sigmoid(c)
    o_ref[0] = jnp.dot(a, w_ref[0], preferred_element_type=F32,
                       precision=lax.Precision.HIGHEST) + b_ref[0]


def _modulation(c, c_ctx, w_ada, b_ada):
    depth = w_ada.shape[0]
    rows = jnp.zeros((SUBLANES, D_MODEL), F32).at[:BATCH].set(c).at[BATCH].set(c_ctx)
    n_col = 3
    out = pl.pallas_call(
        _mod_kernel,
        grid=(depth, n_col),
        in_specs=[
            pl.BlockSpec((SUBLANES, D_MODEL), lambda l, j: (0, 0)),
            pl.BlockSpec((1, D_MODEL, D_MODEL), lambda l, j: (l, 0, j)),
            pl.BlockSpec((1, 1, D_MODEL), lambda l, j: (l, 0, j)),
        ],
        out_specs=pl.BlockSpec((1, SUBLANES, D_MODEL), lambda l, j: (l, 0, j)),
        out_shape=jax.ShapeDtypeStruct((depth, SUBLANES, 3 * D_MODEL), F32),
        compiler_params=_params(("arbitrary", "arbitrary"), 32),
        name="adaln_modulation",
    )(rows, w_ada, b_ada.reshape(depth, 1, 3 * D_MODEL))
    tabs = []
    for l in range(depth):
        lat = out[l, :BATCH].reshape(BATCH, 3, D_MODEL)
        cx = jnp.broadcast_to(out[l, BATCH].reshape(1, 3, D_MODEL), (BATCH, 3, D_MODEL))
        tab = jnp.stack([cx, lat], axis=1)
        tabs.append(jnp.pad(tab, ((0, 0), (0, 0), (0, SUBLANES - 3), (0, 0))))
    return tabs


def _modulated_norm(xt, nw, mod_ref, b=0):
    ms = jnp.mean(xt * xt, axis=-1, keepdims=True)
    y = xt * lax.rsqrt(ms + NORM_EPS) * nw
    return y * (1.0 + mod_ref[b, 0, 1:2, :]) + mod_ref[b, 0, 0:1, :]


def _lane_block_transpose(vs):
    n = len(vs)
    width = LANES // n
    blk = lax.broadcasted_iota(jnp.int32, vs[0].shape, 1) // width
    x = list(vs)
    d = n // 2
    while d >= 1:
        clear = (blk & d) == 0
        y = list(x)
        for i in range(n):
            if i & d == 0:
                a, b = x[i], x[i + d]
                y[i] = jnp.where(clear, a, pltpu.roll(b, width * d, 1))
                y[i + d] = jnp.where(clear, pltpu.roll(a, LANES - width * d, 1), b)
        x = y
        d //= 2
    return x


def _chunk_slot(b, cc):
    return (cc // 2) * (2 * BATCH) + 2 * b + (cc % 2)


def _proj0_kernel(x_ref, c_ref, mod_ref, nw_ref, cos_ref, sin_ref, w_ref,
                  q_ref, z_ref, kbd_ref, vt_ref):
    is_ctx = pl.program_id(0) == 0
    h = jnp.concatenate(
        [_modulated_norm(jnp.where(is_ctx, c_ref[b], x_ref[b]), nw_ref[...], mod_ref, b).astype(BF16)
         for b in range(BATCH)], axis=0)
    cos = cos_ref[...]
    sin = sin_ref[...]
    lane = lax.broadcasted_iota(jnp.int32, (ROW_TILE, LANES), 1)
    first_half = (lane % HEAD_DIM) < (HEAD_DIM // 2)
    low = lax.broadcasted_iota(jnp.int32, (BLOCK, LANES), 1) < HEAD_DIM

    def rope(t):
        partner = jnp.where(first_half, pltpu.roll(t, LANES - HEAD_DIM // 2, 1),
                            pltpu.roll(t, HEAD_DIM // 2, 1))
        return t * cos + partner * sin

    q = jnp.dot(h, w_ref[:, :ATTN_WIDTH], preferred_element_type=F32)
    k = jnp.dot(h, w_ref[:, ATTN_WIDTH:ATTN_WIDTH + KV_WIDTH], preferred_element_type=F32)
    v = jnp.dot(h, w_ref[:, ATTN_WIDTH + KV_WIDTH:ATTN_WIDTH + 2 * KV_WIDTH],
                preferred_element_type=F32)
    z = jnp.dot(h, w_ref[:, ATTN_WIDTH + 2 * KV_WIDTH:], preferred_element_type=F32)
    for b in range(BATCH):
        tile = slice(ROW_TILE * b, ROW_TILE * (b + 1))
        z_ref[b] = z[tile].astype(BF16)
        for j in range(N_SLOTS):
            m, gi = divmod(j, SLOTS_PER_M)
            qj = (rope(q[tile, LANES * j:LANES * (j + 1)]) * Q_SCALE).astype(BF16)
            for blk in range(ROW_TILE // BLOCK):
                q_ref[b, blk, m, BLOCK * gi:BLOCK * (gi + 1), :] = qj[BLOCK * blk:BLOCK * (blk + 1)]
        for m in range(N_KV_PAIRS):
            sl = slice(LANES * m, LANES * (m + 1))
            kr = rope(k[tile, sl])
            vm = v[tile, sl]
            for blk in range(ROW_TILE // BLOCK):
                rows = slice(BLOCK * blk, BLOCK * (blk + 1))
                kbd_ref[b, blk, m, :BLOCK, :] = jnp.where(low, kr[rows], 0.0).astype(BF16)
                kbd_ref[b, blk, m, BLOCK:, :] = jnp.where(low, 0.0, kr[rows]).astype(BF16)
                vt_ref[b, blk, LANES * m:LANES * (m + 1), :] = vm[rows].T.astype(BF16)


def _proj0(x, ctx, mod, norm_w, cos_tab, sin_tab, w_in):
    n_col = w_in.shape[1]
    blocks_per_tile = ROW_TILE // BLOCK
    kv_shape = jax.ShapeDtypeStruct((BATCH, N_BLOCKS, N_KV_PAIRS, 2 * BLOCK, LANES), BF16)
    kv_spec = pl.BlockSpec((BATCH, blocks_per_tile, N_KV_PAIRS, 2 * BLOCK, LANES),
                           lambda i: (0, i, 0, 0, 0))
    row_spec = pl.BlockSpec((BATCH, ROW_TILE, D_MODEL), lambda i: (0, i, 0))
    once = pl.Buffered(1)
    return pl.pallas_call(
        _proj0_kernel,
        grid=(N_ROW_TILES,),
        in_specs=[
            pl.BlockSpec((BATCH, ROW_TILE, D_MODEL), lambda i: (0, jnp.maximum(i - 1, 0), 0)),
            pl.BlockSpec((BATCH, ROW_TILE, D_MODEL), lambda i: (0, 0, 0), pipeline_mode=once),
            pl.BlockSpec((BATCH, 1, SUBLANES, D_MODEL), lambda i: (0, jnp.minimum(i, 1), 0, 0)),
            pl.BlockSpec((1, D_MODEL), lambda i: (0, 0)),
            pl.BlockSpec((ROW_TILE, LANES), lambda i: (i, 0)),
            pl.BlockSpec((ROW_TILE, LANES), lambda i: (i, 0)),
            pl.BlockSpec((D_MODEL, n_col), lambda i: (0, 0), pipeline_mode=once),
        ],
        out_specs=[
            pl.BlockSpec((BATCH, blocks_per_tile, N_KV_PAIRS, SLOTS_PER_M * BLOCK, LANES),
                         lambda i: (0, i, 0, 0, 0)),
            row_spec, kv_spec,
            pl.BlockSpec((BATCH, blocks_per_tile, KV_WIDTH, BLOCK), lambda i: (0, i, 0, 0))],
        out_shape=[
            jax.ShapeDtypeStruct((BATCH, N_BLOCKS, N_KV_PAIRS, SLOTS_PER_M * BLOCK, LANES), BF16),
            jax.ShapeDtypeStruct((BATCH, TOTAL, ATTN_WIDTH), BF16),
            kv_shape,
            jax.ShapeDtypeStruct((BATCH, N_BLOCKS, KV_WIDTH, BLOCK), BF16),
        ],
        compiler_params=_params(("arbitrary",), 56),
        name="attn_projection",
    )(x, ctx, mod, norm_w.reshape(1, D_MODEL), cos_tab, sin_tab, w_in)


def _attn_kernel(q_ref, z_ref, k0_ref, k1_ref, k2_ref, k3_ref, kx_ref,
                 v0_ref, v1_ref, v2_ref, v3_ref, vx_ref,
                 sink_ref, x_ref, c_ref, mod_ref, wo_ref, eye_ref, tri_ref, o_ref):
    step = pl.program_id(1)
    is_lat = step >= N_CTX_BLOCKS // ATTN_QBLOCKS
    n_first = ATTN_QBLOCKS * step - N_CTX_BLOCKS
    q_rows = UNIT_SLOTS * BLOCK
    blocked = tri_ref[2]
    bias = []
    for qb in range(ATTN_QBLOCKS):
        n = n_first + qb
        bias.append([jnp.where(jnp.logical_and(is_lat, n >= 1), tri_ref[0], blocked),
                     jnp.where(is_lat, tri_ref[3], blocked),
                     jnp.where(jnp.logical_and(is_lat, n <= SEQ // BLOCK - 2), tri_ref[1], blocked),
                     None, None])
    k_win = (k0_ref, k1_ref, k2_ref, k3_ref)
    v_win = (v0_ref, v1_ref, v2_ref, v3_ref)

    def kpiece(qb, p, m):
        if p < 3:
            return k_win[qb + p][0, 0, m], v_win[qb + p][0, 0]
        return kx_ref[0, p - 3, m], vx_ref[0, p - 3]

    n_piece = 3 + N_CTX_BLOCKS
    nt = (((1,), (1,)), ((), ()))

    half_rows = slice(0, BLOCK), slice(BLOCK, 2 * BLOCK)

    def scores(qb, m, h):
        qu = q_ref[0, qb, m, q_rows * h:q_rows * (h + 1), :]
        qu_masked = jnp.concatenate([qu, eye_ref[...]], axis=1)
        s_list = []
        for p in range(n_piece):
            kbd, _ = kpiece(qb, p, m)
            if bias[qb][p] is None:
                s = lax.dot_general(kbd, qu, nt, preferred_element_type=F32)
            else:
                s = lax.dot_general(jnp.concatenate([kbd, bias[qb][p]], axis=1), qu_masked, nt,
                                    preferred_element_type=F32)
            s_list.append(s)
        return s_list

    def finish(qb, m, h, s_list):
        slot0 = SLOTS_PER_M * m + UNIT_SLOTS * h
        halves = []
        for hs in range(2):
            sink = jnp.concatenate(
                [sink_ref[2 * (slot0 + gi) + hs:2 * (slot0 + gi) + hs + 1, :]
                 for gi in range(UNIT_SLOTS)], axis=1)
            mx = sink
            for s in s_list:
                mx = jnp.maximum(mx, jnp.max(s[half_rows[hs]], axis=0, keepdims=True))
            probs = [jnp.exp2(s[half_rows[hs]] - mx) for s in s_list]
            denom = jnp.exp2(sink - mx)
            for e in probs:
                denom = denom + jnp.sum(e, axis=0, keepdims=True)
            probs = [e.astype(BF16) for e in probs]
            kv_head = 2 * m + hs
            acc = None
            for p0 in range(0, n_piece, 2):
                group = list(range(p0, min(p0 + 2, n_piece)))
                vt = jnp.concatenate(
                    [kpiece(qb, p, m)[1][HEAD_DIM * kv_head:HEAD_DIM * (kv_head + 1), :]
                     for p in group], axis=1)
                pt = jnp.concatenate([probs[p] for p in group], axis=0)
                part = jnp.dot(vt, pt, preferred_element_type=F32)
                acc = part if acc is None else acc + part
            halves.append(acc * (1.0 / denom))
        o_t = jnp.concatenate(halves, axis=0)
        outs = []
        for gi in range(UNIT_SLOTS):
            j = slot0 + gi
            o = o_t[:, BLOCK * gi:BLOCK * (gi + 1)].T
            zj = z_ref[0, BLOCK * qb:BLOCK * (qb + 1), LANES * j:LANES * (j + 1)].astype(F32)
            outs.append((o * (zj * jax.nn.sigmoid(zj))).astype(BF16))
        return jnp.concatenate(outs, axis=1)

    units = [(qb, m, h) for m in range(N_KV_PAIRS) for h in range(SLOTS_PER_M // UNIT_SLOTS)
             for qb in range(ATTN_QBLOCKS)]
    y = None
    gated = []
    pending = scores(*units[0])
    for u, (qb, m, h) in enumerate(units):
        nxt = scores(*units[u + 1]) if u + 1 < len(units) else None
        gated.append(finish(qb, m, h, pending))
        pending = nxt
        if qb == ATTN_QBLOCKS - 1:
            slot0 = SLOTS_PER_M * m + UNIT_SLOTS * h
            part = jnp.dot(jnp.concatenate(gated, axis=0),
                           wo_ref[LANES * slot0:LANES * (slot0 + UNIT_SLOTS), :],
                           preferred_element_type=F32)
            y = part if y is None else y + part
            gated = []
    resid = jnp.where(is_lat, x_ref[0], c_ref[0])
    o_ref[0] = resid + mod_ref[0, 0, 2:3, :] * y


def _attention(q, z, kbd, vt, sink_tab, x, ctx, mod, w_out):
    last = N_BLOCKS - 1
    n_m = KV_WIDTH // LANES
    nq = ATTN_QBLOCKS
    ctx_steps = N_CTX_BLOCKS // nq
    row_spec = pl.BlockSpec((1, nq * BLOCK, ATTN_WIDTH), lambda b, i: (b, i, 0))

    def kv_spec(off):
        return pl.BlockSpec((1, 1, n_m, 2 * BLOCK, LANES),
                            lambda b, i: (b, jnp.clip(nq * i + off, 0, last), 0, 0, 0))

    ctx_kv_spec = pl.BlockSpec((1, N_CTX_BLOCKS, n_m, 2 * BLOCK, LANES),
                               lambda b, i: (b, 0, 0, 0, 0))

    def vt_spec(off):
        return pl.BlockSpec((1, 1, KV_WIDTH, BLOCK),
                            lambda b, i: (b, jnp.clip(nq * i + off, 0, last), 0, 0))

    ctx_vt_spec = pl.BlockSpec((1, N_CTX_BLOCKS, KV_WIDTH, BLOCK), lambda b, i: (b, 0, 0, 0))
    off = np.arange(BLOCK)
    eye = np.tile(np.eye(BLOCK, dtype=np.float32), (UNIT_SLOTS, 1))
    key_ge = np.where(off[:, None] >= off[None, :], 0.0, NEG_INF)
    key_le = np.where(off[:, None] <= off[None, :], 0.0, NEG_INF)
    tri = np.stack([np.tile(t, (2, 1)) for t in
                    (key_ge, key_le, np.full((BLOCK, BLOCK), NEG_INF), np.zeros((BLOCK, BLOCK)))])
    consts = [jnp.asarray(a, BF16) for a in (eye, tri)]
    return pl.pallas_call(
        _attn_kernel,
        grid=(BATCH, N_BLOCKS // nq),
        in_specs=[
            pl.BlockSpec((1, nq, n_m, SLOTS_PER_M * BLOCK, LANES), lambda b, i: (b, i, 0, 0, 0)),
            row_spec,
            kv_spec(-1), kv_spec(0), kv_spec(1), kv_spec(2), ctx_kv_spec,
            vt_spec(-1), vt_spec(0), vt_spec(1), vt_spec(2), ctx_vt_spec,
            pl.BlockSpec((2 * N_SLOTS, LANES), lambda b, i: (0, 0)),
            pl.BlockSpec((1, nq * BLOCK, D_MODEL),
                         lambda b, i: (b, jnp.maximum(i - ctx_steps, 0), 0)),
            pl.BlockSpec((1, nq * BLOCK, D_MODEL),
                         lambda b, i: (b, jnp.minimum(i, ctx_steps - 1), 0)),
            pl.BlockSpec((1, 1, SUBLANES, D_MODEL),
                         lambda b, i: (b, jnp.minimum(i // ctx_steps, 1), 0, 0)),
            pl.BlockSpec((ATTN_WIDTH, D_MODEL), lambda b, i: (0, 0)),
        ] + [pl.BlockSpec(a.shape, lambda b, i, nd=a.ndim: (0,) * nd) for a in consts],
        out_specs=pl.BlockSpec((1, nq * BLOCK, D_MODEL), lambda b, i: (b, i, 0)),
        out_shape=jax.ShapeDtypeStruct((BATCH, TOTAL, D_MODEL), F32),
        compiler_params=_params(("arbitrary", "arbitrary"), 48),
        name="window_attention",
    )(q, z, kbd, kbd, kbd, kbd, kbd, vt, vt, vt, vt, vt, sink_tab, x, ctx, mod, w_out,
      *consts)


def _proj1_kernel(x_ref, mod_ref, nw_ref, w_ref, u_ref, z_ref, scr_ref):
    h = jnp.concatenate(
        [_modulated_norm(x_ref[b], nw_ref[...], mod_ref, b).astype(BF16) for b in range(BATCH)],
        axis=0)
    u = jnp.dot(h, w_ref[:, :D_MODEL], preferred_element_type=F32)
    z = jnp.dot(h, w_ref[:, D_MODEL:], preferred_element_type=F32)
    for b in range(BATCH):
        z_ref[b] = z[ROW_TILE * b:ROW_TILE * (b + 1)].astype(BF16)
    n_slab = D_MODEL // LANES
    for b in range(BATCH):
        for cc in range(CHUNKS_PER_TILE):
            r0 = ROW_TILE * b + CHUNK * cc
            q0 = SLOT_PITCH * _chunk_slot(b, cc)
            for k in range(n_slab):
                scr_ref[k, q0:q0 + CHUNK, :] = u[r0:r0 + CHUNK, LANES * k:LANES * (k + 1)]
    groups_per_slab = LANES // SSM_GROUP
    for k in range(n_slab):
        for pp in range(PAIRS_PER_TILE // 2):
            parts = []
            for p in (2 * pp, 2 * pp + 1):
                rows = [scr_ref[k, pl.ds(SLOT_PITCH * SUBLANES * p + s, SUBLANES,
                                         stride=SLOT_PITCH), :] for s in range(CHUNK)]
                parts.append([_lane_block_transpose(rows[SUBLANES * m2:SUBLANES * (m2 + 1)])
                              for m2 in range(CHUNK // SUBLANES)])
            for m2 in range(CHUNK // SUBLANES):
                for gl in range(groups_per_slab):
                    val = jnp.concatenate([parts[0][m2][gl], parts[1][m2][gl]], axis=0)
                    u_ref[groups_per_slab * k + gl, 2 * SUBLANES * pp:2 * SUBLANES * (pp + 1),
                          LANES * m2:LANES * (m2 + 1)] = val.astype(BF16)


def _proj1(xc, mod, norm_w, w_in):
    row_spec = pl.BlockSpec((BATCH, ROW_TILE, D_MODEL), lambda i: (0, i, 0))
    tile_rows = PAIRS_PER_TILE * SUBLANES
    return pl.pallas_call(
        _proj1_kernel,
        grid=(N_ROW_TILES,),
        in_specs=[
            row_spec,
            pl.BlockSpec((BATCH, 1, SUBLANES, D_MODEL), lambda i: (0, jnp.minimum(i, 1), 0, 0)),
            pl.BlockSpec((1, D_MODEL), lambda i: (0, 0)),
            pl.BlockSpec((D_MODEL, 2 * D_MODEL), lambda i: (0, 0)),
        ],
        out_specs=[pl.BlockSpec((SSM_GROUPS, tile_rows, CHUNK_W), lambda i: (0, i, 0)), row_spec],
        out_shape=[jax.ShapeDtypeStruct((SSM_GROUPS, SCAN_ROWS, CHUNK_W), BF16),
                   jax.ShapeDtypeStruct((BATCH, TOTAL, D_MODEL), BF16)],
        scratch_shapes=[pltpu.VMEM((D_MODEL // LANES, BATCH * CHUNKS_PER_TILE * SLOT_PITCH, LANES),
                                   F32)],
        compiler_params=_params(("arbitrary",), 56),
        name="ssm_projection",
    )(xc, mod, norm_w.reshape(1, D_MODEL), w_in)


def _s5_kernel(u_ref, m_ref, ws_ref, wy_ref, cst_ref, y_ref, s4_ref, xp_ref):
    even = lax.broadcasted_iota(jnp.int32, (SUBLANES, STATE_W), 0) % 2 == 0
    down = 1
    up = SUBLANES - 1
    fwd = slice(0, STATE_W)
    bwd = slice(STATE_W, 2 * STATE_W)
    fwd_sw = slice(2 * STATE_W, 3 * STATE_W)
    bwd_sw = slice(3 * STATE_W, 4 * STATE_W)
    for g in range(GROUP_BATCH):
        s = jnp.dot(u_ref[g], ws_ref[g], preferred_element_type=F32)
        s = s.reshape(N_SCAN_BLOCKS, SUBLANES, 4 * STATE_W)
        q1f, q2f, q1b, q2b = cst_ref[g, 2], cst_ref[g, 3], cst_ref[g, 6], cst_ref[g, 7]
        zf, zb, zfs, zbs = s[..., fwd], s[..., bwd], s[..., fwd_sw], s[..., bwd_sw]
        rzf, rzfs = pltpu.roll(zf, down, 1), pltpu.roll(zfs, down, 1)
        rzb, rzbs = pltpu.roll(zb, up, 1), pltpu.roll(zbs, up, 1)
        t = jnp.concatenate([zf + q1f * rzf + q2f * rzfs, zb + q1b * rzb + q2b * rzbs,
                             zfs + q1f * rzfs - q2f * rzf, zbs + q1b * rzbs - q2b * rzb], axis=-1)
        s4_ref[g] = t.reshape(SCAN_ROWS, 4 * STATE_W)

    def step(j, carry):
        jb = jnp.where(j < N_CTX_SCAN_BLOCKS, N_CTX_SCAN_BLOCKS - 1 - j,
                       N_SCAN_BLOCKS - 1 + N_CTX_SCAN_BLOCKS - j)
        rf = pl.ds(pl.multiple_of(j * SUBLANES, SUBLANES), SUBLANES)
        rb = pl.ds(pl.multiple_of(jb * SUBLANES, SUBLANES), SUBLANES)
        new = []
        for g in range(GROUP_BATCH):
            cf, cfs, cb, cbs = carry[4 * g:4 * g + 4]
            p1f, p2f, p1b, p2b = cst_ref[g, 0], cst_ref[g, 1], cst_ref[g, 4], cst_ref[g, 5]
            xf = p1f * cf + p2f * cfs + s4_ref[g, rf, fwd]
            xfs = p1f * cfs - p2f * cf + s4_ref[g, rf, fwd_sw]
            xp_ref[g, rf, fwd] = jnp.where(even, cf, pltpu.roll(xf, down, 0))
            new += [jnp.where(even, pltpu.roll(xf, up, 0), xf),
                    jnp.where(even, pltpu.roll(xfs, up, 0), xfs)]
            xb = p1b * cb + p2b * cbs + s4_ref[g, rb, bwd]
            xbs = p1b * cbs - p2b * cb + s4_ref[g, rb, bwd_sw]
            xp_ref[g, rb, bwd] = jnp.where(even, pltpu.roll(xb, up, 0), cb)
            new += [jnp.where(even, xb, pltpu.roll(xb, down, 0)),
                    jnp.where(even, xbs, pltpu.roll(xbs, down, 0))]
        return tuple(new)

    zero = jnp.zeros((SUBLANES, STATE_W), F32)
    lax.fori_loop(0, N_SCAN_BLOCKS, step, (zero,) * (4 * GROUP_BATCH))
    for g in range(GROUP_BATCH):
        y_ref[g] = (
            jnp.dot(u_ref[g, CTX_SCAN_ROWS:, :], m_ref[g], preferred_element_type=F32)
            + lax.dot_general(xp_ref[g, CTX_SCAN_ROWS:, :].astype(BF16), wy_ref[g],
                              (((1,), (1,)), ((), ())), preferred_element_type=F32))


def _s5_core(u_g, m_mat, ws_mat, wy_mat, consts):
    lat_rows = SCAN_ROWS - CTX_SCAN_ROWS

    def gspec(*tail):
        return pl.BlockSpec((GROUP_BATCH,) + tail, lambda i: (i,) + (0,) * len(tail))

    return pl.pallas_call(
        _s5_kernel,
        grid=(SSM_GROUPS // GROUP_BATCH,),
        in_specs=[
            gspec(SCAN_ROWS, CHUNK_W),
            gspec(CHUNK_W, CHUNK_W),
            gspec(CHUNK_W, 4 * STATE_W),
            gspec(CHUNK_W, 2 * STATE_W),
            gspec(8, SUBLANES, STATE_W),
        ],
        out_specs=gspec(lat_rows, CHUNK_W),
        out_shape=jax.ShapeDtypeStruct((SSM_GROUPS, lat_rows, CHUNK_W), F32),
        scratch_shapes=[
            pltpu.VMEM((GROUP_BATCH, SCAN_ROWS, 4 * STATE_W), F32),
            pltpu.VMEM((GROUP_BATCH, SCAN_ROWS, 2 * STATE_W), F32),
        ],
        compiler_params=_params(("arbitrary",), 48),
        name="s5_scan",
    )(u_g, m_mat, ws_mat, wy_mat, consts)


def _out1_kernel(y_ref, z_ref, x_ref, mod_ref, wg_ref, wo_ref, fnw_ref, o_ref, scr_ref):
    n_slab = D_MODEL // LANES
    groups_per_slab = LANES // SSM_GROUP
    pairs_per_phase = PAIRS_PER_TILE // OUT_PHASES
    rows_per_phase = ROW_TILE // OUT_PHASES

    def relayout(ph):
        for k in range(n_slab):
            for p in range(pairs_per_phase * ph, pairs_per_phase * (ph + 1)):
                for m2 in range(CHUNK // SUBLANES):
                    vals = [y_ref[groups_per_slab * k + gl, SUBLANES * p:SUBLANES * (p + 1),
                                  LANES * m2:LANES * (m2 + 1)] for gl in range(groups_per_slab)]
                    steps = _lane_block_transpose(vals)
                    for s2 in range(SUBLANES):
                        t_idx = SUBLANES * m2 + s2
                        scr_ref[k, pl.ds(SLOT_PITCH * SUBLANES * p + t_idx, SUBLANES,
                                         stride=SLOT_PITCH), :] = steps[s2]

    def compute(ph):
        chunks = range(2 * pairs_per_phase * ph, 2 * pairs_per_phase * (ph + 1))
        y = jnp.concatenate(
            [jnp.concatenate(
                [scr_ref[k, SLOT_PITCH * _chunk_slot(b, cc):SLOT_PITCH * _chunk_slot(b, cc) + CHUNK, :]
                 for k in range(n_slab)], axis=1)
             for b in range(BATCH) for cc in chunks], axis=0)
        g = (0.5 * y * (1.0 + lax.erf(y * (2.0 ** -0.5)))).astype(BF16)
        t = jnp.dot(g, wg_ref[...], preferred_element_type=F32)
        rows = slice(rows_per_phase * ph, rows_per_phase * (ph + 1))
        z = jnp.concatenate([z_ref[b, rows, :] for b in range(BATCH)], axis=0).astype(F32)
        r = (t[:, :D_MODEL] * jax.nn.sigmoid(t[:, D_MODEL:]) * (z * jax.nn.sigmoid(z))).astype(BF16)
        o = jnp.dot(r, wo_ref[...], preferred_element_type=F32)
        for b in range(BATCH):
            x2 = (x_ref[b, rows, :]
                  + mod_ref[b, 0, 2:3, :] * o[rows_per_phase * b:rows_per_phase * (b + 1)])
            ms = jnp.mean(x2 * x2, axis=-1, keepdims=True)
            o_ref[b, rows, :] = x2 * lax.rsqrt(ms + NORM_EPS) * fnw_ref[...]

    relayout(0)
    for ph in range(OUT_PHASES):
        if ph + 1 < OUT_PHASES:
            relayout(ph + 1)
        compute(ph)


def _out1(y_g, z, xc, mod, w_glu, w_out, final_norm_w):
    ctx_tiles = CTX_LEN // ROW_TILE
    tile_rows = PAIRS_PER_TILE * SUBLANES
    lat_spec = pl.BlockSpec((BATCH, ROW_TILE, D_MODEL), lambda i: (0, i, 0))
    all_spec = pl.BlockSpec((BATCH, ROW_TILE, D_MODEL), lambda i: (0, i + ctx_tiles, 0))
    return pl.pallas_call(
        _out1_kernel,
        grid=(SEQ // ROW_TILE,),
        in_specs=[
            pl.BlockSpec((SSM_GROUPS, tile_rows, CHUNK_W), lambda i: (0, i, 0)),
            all_spec, all_spec,
            pl.BlockSpec((BATCH, 1, SUBLANES, D_MODEL), lambda i: (0, 1, 0, 0)),
            pl.BlockSpec((D_MODEL, 2 * D_MODEL), lambda i: (0, 0)),
            pl.BlockSpec((D_MODEL, D_MODEL), lambda i: (0, 0)),
            pl.BlockSpec((1, D_MODEL), lambda i: (0, 0)),
        ],
        out_specs=lat_spec,
        out_shape=jax.ShapeDtypeStruct((BATCH, SEQ, D_MODEL), F32),
        scratch_shapes=[pltpu.VMEM((D_MODEL // LANES, BATCH * CHUNKS_PER_TILE * SLOT_PITCH, LANES),
                                   F32)],
        compiler_params=_params(("arbitrary",), 56),
        name="ssm_output",
    )(y_g, z, xc, mod, w_glu, w_out, final_norm_w.reshape(1, D_MODEL))


def _slot_order(t, lead):
    n_m = KV_WIDTH // LANES
    gq = N_HEADS // N_KV_HEADS
    shape = t.shape
    t = t.reshape(shape[:lead] + (n_m, 2, gq) + shape[lead + 1:])
    perm = tuple(range(lead)) + (lead, lead + 2, lead + 1) + tuple(range(lead + 3, t.ndim))
    return jnp.transpose(t, perm)


def _rope_order(t):
    shape = t.shape
    t = t.reshape(shape[:-1] + (2, 2, ROPE_FREQS))
    return jnp.swapaxes(t, -3, -2).reshape(shape)


def _attn_weights(w_in, w_out, sink):
    wq = w_in[:, :ATTN_WIDTH].reshape(D_MODEL, N_HEADS, HEAD_DIM)
    wq = _slot_order(_rope_order(wq), 1).reshape(D_MODEL, ATTN_WIDTH)
    wk = w_in[:, ATTN_WIDTH:ATTN_WIDTH + KV_WIDTH].reshape(D_MODEL, N_KV_HEADS, HEAD_DIM)
    wk = _rope_order(wk).reshape(D_MODEL, KV_WIDTH)
    wv = w_in[:, ATTN_WIDTH + KV_WIDTH:ATTN_WIDTH + 2 * KV_WIDTH]
    wz = w_in[:, ATTN_WIDTH + 2 * KV_WIDTH:].reshape(D_MODEL, N_HEADS, HEAD_DIM)
    wz = _slot_order(wz, 1).reshape(D_MODEL, ATTN_WIDTH)
    w_in_p = jnp.concatenate([wq, wk, wv, wz], axis=1).astype(BF16)
    wo = _slot_order(w_out.reshape(N_HEADS, HEAD_DIM, D_MODEL), 0).reshape(ATTN_WIDTH, D_MODEL)
    sink_p = _slot_order(sink.astype(F32).reshape(N_HEADS), 0).reshape(2 * N_SLOTS)
    sink_tab = jnp.broadcast_to((sink_p * LOG2E)[:, None], (2 * N_SLOTS, LANES))
    return w_in_p, wo.astype(BF16), sink_tab


def _rope_tables():
    inv = ROPE_BASE ** (-np.arange(ROPE_FREQS, dtype=np.float64) / ROPE_FREQS)
    pos = np.arange(SEQ)
    row = (pos // GRID_W)[:, None] * inv
    col = (pos % GRID_W)[:, None] * inv
    w = np.arange(LANES) % HEAD_DIM
    half, axis, f = w // 32, (w % 32) // 16, w % 16
    ang = np.where((axis == 0)[None, :], row[:, f], col[:, f])
    sign = np.where(half == 0, -1.0, 1.0)[None, :]
    cos = np.concatenate([np.ones((CTX_LEN, LANES)), np.cos(ang)], axis=0)
    sin = np.concatenate([np.zeros((CTX_LEN, LANES)), np.sin(ang) * sign], axis=0)
    return jnp.asarray(cos, F32), jnp.asarray(sin, F32)


def _s5_operators(lam_re, lam_im, log_dt, b_re, b_im, c_re, c_im, d_skip):
    t_len = CHUNK
    n_pow = 2 * t_len + 1
    lr, li = lam_re.astype(F32), lam_im.astype(F32)
    dt = jnp.exp(log_dt.astype(F32))[..., None]
    mag = jnp.exp(lr * dt)
    sq = [(mag * jnp.cos(li * dt), mag * jnp.sin(li * dt))]
    while 2 ** len(sq) < n_pow:
        r, i = sq[-1]
        sq.append((r * r - i * i, 2.0 * r * i))
    ks = np.arange(n_pow)
    pr = jnp.ones(lr.shape + (n_pow,), F32)
    pi = jnp.zeros(lr.shape + (n_pow,), F32)
    for bit, (r, i) in enumerate(sq):
        on = jnp.asarray((ks >> bit) & 1 == 1)
        fr = jnp.where(on, r[..., None], 1.0)
        fi = jnp.where(on, i[..., None], 0.0)
        pr, pi = pr * fr - pi * fi, pr * fi + pi * fr
    ar1, ai1 = sq[0][0] - 1.0, sq[0][1]
    den = lr * lr + li * li
    gr, gi = (ar1 * lr + ai1 * li) / den, (ai1 * lr - ar1 * li) / den
    br_, bi_ = b_re.astype(F32), b_im.astype(F32)
    bbr = gr[..., None] * br_ - gi[..., None] * bi_
    bbi = gr[..., None] * bi_ + gi[..., None] * br_
    pad_k = POW_ROWS - n_pow
    prt, pit = jnp.swapaxes(pr, 2, 3), jnp.swapaxes(pi, 2, 3)
    pw = jnp.pad(jnp.concatenate([prt, prt, pit, pit], axis=-1),
                 ((0, 0), (0, 0), (0, pad_k), (0, 0)))
    brt, bit = jnp.swapaxes(bbr, 2, 3), jnp.swapaxes(bbi, 2, 3)
    bb = jnp.concatenate([brt, bit, -bit, brt, bit, brt, brt, -bit], axis=-1)
    cr, ci = c_re.astype(F32), c_im.astype(F32)
    cm = jnp.stack([jnp.concatenate([cr, -ci], axis=-1),
                    jnp.concatenate([-ci, -cr], axis=-1)], axis=2)
    dv = jnp.tile(d_skip.astype(F32).reshape(SSM_GROUPS, 1, SSM_GROUP), (1, 1, t_len))
    return _s5_operator_call(pw, bb, cm, dv)


def _operator_constants():
    t_len = CHUNK
    s_of_row = np.arange(CHUNK_W) // SSM_GROUP
    k_ar = np.arange(POW_ROWS)
    oh_f = (k_ar[None, :] == (t_len - 1 - s_of_row)[:, None]).astype(np.float32)
    oh_b = (k_ar[None, :] == s_of_row[:, None]).astype(np.float32)
    t_of_lane = np.arange(CHUNK_W) // SSM_GROUP
    expo = [s_of_row + 1, t_len - s_of_row, s_of_row, t_len - 1 - s_of_row]
    sel = np.stack([(k_ar[None, :] == e[:, None]) for e in expo]).astype(np.float32)
    h_of_lane = np.arange(CHUNK_W) % SSM_GROUP
    dmask = ((s_of_row[:, None] == t_of_lane[None, :])
             & ((np.arange(CHUNK_W) % SSM_GROUP)[:, None] == h_of_lane[None, :])).astype(np.float32)
    return oh_f, oh_b, sel, dmask


def _s5_op_kernel(*refs):
    for gg in range(OP_GROUP_BATCH):
        _s5_op_group(gg, *refs)


def _s5_op_group(gg, pw_ref, bb_ref, cm_ref, dv_ref, ohf_ref, ohb_ref, sel_ref, dmask_ref,
                 m_ref, ws_ref, wyt_ref, cst_ref):
    t_len = CHUNK

    def split(a):
        hi = a.astype(BF16)
        return hi, (a - hi.astype(F32)).astype(BF16)

    def pick_rows(onehot2, table):
        hi, lo = split(table)
        return jnp.dot(onehot2, jnp.concatenate([hi, lo], axis=0), preferred_element_type=F32)

    def tile_rows(a):
        return jnp.concatenate([a] * t_len, axis=0)

    main, swapped = [], []
    for d, oh_ref in ((0, ohf_ref), (1, ohb_ref)):
        pp = pick_rows(oh_ref[...], pw_ref[d, gg])
        p_re, p_im = pp[:, :STATE_W], pp[:, STATE_W:]
        b0, b1, b2, b3 = [tile_rows(bb_ref[d, gg, :, STATE_W * i:STATE_W * (i + 1)])
                          for i in range(4)]
        main.append(p_re * b0 + p_im * b1)
        swapped.append(p_re * b2 + p_im * b3)
    ws_ref[gg] = jnp.concatenate(main + swapped, axis=1).astype(BF16)

    c_tiled = [[tile_rows(cm_ref[d, gg, i]) for i in range(2)] for d in range(2)]

    def block_t(d, pat):
        pp = pick_rows(sel_ref[pat], pw_ref[d, gg])
        return pp[:, :STATE_W] * c_tiled[d][0] + pp[:, STATE_W:] * c_tiled[d][1]

    wyt_ref[gg] = jnp.concatenate([block_t(0, 0), block_t(1, 1)], axis=1).astype(BF16)

    nt = (((1,), (1,)), ((), ()))
    kt_f = lax.dot_general(bb_ref[0, gg, :, :STATE_W], block_t(0, 2), nt,
                           preferred_element_type=F32, precision=lax.Precision.HIGHEST)
    kt_b = lax.dot_general(bb_ref[1, gg, :, :STATE_W], block_t(1, 3), nt,
                           preferred_element_type=F32, precision=lax.Precision.HIGHEST)
    lane = lax.broadcasted_iota(jnp.int32, (SSM_GROUP, CHUNK_W), 1)
    skip = dv_ref[gg]
    for s in range(t_len):
        fwd = kt_f if s == 0 else pltpu.roll(kt_f, SSM_GROUP * s, 1)
        back = t_len - 1 - s
        bwd = kt_b if back == 0 else pltpu.roll(kt_b, CHUNK_W - SSM_GROUP * back, 1)
        rows = (jnp.where(lane >= SSM_GROUP * s, fwd, 0.0)
                + jnp.where(lane < SSM_GROUP * (s + 1), bwd, 0.0)
                + dmask_ref[SSM_GROUP * s:SSM_GROUP * (s + 1), :] * skip)
        m_ref[gg, SSM_GROUP * s:SSM_GROUP * (s + 1), :] = rows.astype(BF16)

    even = lax.broadcasted_iota(jnp.int32, (SUBLANES, STATE_W), 0) % 2 == 0
    sign = jnp.where(lax.broadcasted_iota(jnp.int32, (1, STATE_W), 1) < SSM_STATE, -1.0, 1.0)

    def w12(d, k):
        row = pw_ref[d, gg, k:k + 1, :]
        return row[:, :STATE_W], row[:, STATE_W:] * sign

    zero = (jnp.zeros((1, STATE_W), F32),) * 2
    pairs = [(w12(0, t_len), w12(0, 2 * t_len)), (zero, w12(0, t_len)),
             (w12(1, 2 * t_len), w12(1, t_len)), (w12(1, t_len), zero)]
    idx = 0
    for top, bot in pairs:
        for part in range(2):
            cst_ref[gg, idx] = jnp.where(even, jnp.broadcast_to(top[part], (SUBLANES, STATE_W)),
                                         jnp.broadcast_to(bot[part], (SUBLANES, STATE_W)))
            idx += 1


def _s5_operator_call(pw, bb, cm, dv):
    oh_f, oh_b, sel, dmask = _operator_constants()
    consts = [jnp.asarray(np.concatenate([a, a], axis=-1), BF16) for a in (oh_f, oh_b, sel)]
    consts.append(jnp.asarray(dmask))

    def per_group(*tail):
        n = len(tail)
        return pl.BlockSpec((2, OP_GROUP_BATCH) + tail, lambda g: (0, g) + (0,) * n)

    def whole(a):
        return pl.BlockSpec(a.shape, lambda g: (0,) * a.ndim)

    def out(*tail):
        return pl.BlockSpec((OP_GROUP_BATCH,) + tail, lambda g: (g,) + (0,) * len(tail))

    return pl.pallas_call(
        _s5_op_kernel,
        grid=(SSM_GROUPS // OP_GROUP_BATCH,),
        in_specs=[per_group(POW_ROWS, 2 * STATE_W), per_group(SSM_GROUP, 4 * STATE_W),
                  per_group(2, SSM_GROUP, STATE_W),
                  pl.BlockSpec((OP_GROUP_BATCH, 1, CHUNK_W), lambda g: (g, 0, 0))]
                 + [whole(a) for a in consts],
        out_specs=[out(CHUNK_W, CHUNK_W), out(CHUNK_W, 4 * STATE_W), out(CHUNK_W, 2 * STATE_W),
                   out(8, SUBLANES, STATE_W)],
        out_shape=[jax.ShapeDtypeStruct((SSM_GROUPS, CHUNK_W, CHUNK_W), BF16),
                   jax.ShapeDtypeStruct((SSM_GROUPS, CHUNK_W, 4 * STATE_W), BF16),
                   jax.ShapeDtypeStruct((SSM_GROUPS, CHUNK_W, 2 * STATE_W), BF16),
                   jax.ShapeDtypeStruct((SSM_GROUPS, 8, SUBLANES, STATE_W), F32)],
        compiler_params=_params(("arbitrary",), 32),
        name="s5_operators",
    )(pw, bb, cm, dv, *consts)


def kernel(x, c, ctx, c_ctx, norm_w, w_ada, b_ada, attn_w_in, attn_sink, attn_w_out,
           ssm_w_in, ssm_lam_re, ssm_lam_im, ssm_log_dt, ssm_b_re, ssm_b_im, ssm_c_re, ssm_c_im,
           ssm_d, ssm_w_glu, ssm_w_out, final_norm_w):
    mod0, mod1 = _modulation(c, c_ctx, w_ada, b_ada)

    w_in0, w_out0, sink_tab = _attn_weights(attn_w_in[0], attn_w_out[0], attn_sink[0])
    cos_tab, sin_tab = _rope_tables()
    q, z0, kbd, vt = _proj0(x, ctx, mod0, norm_w[0], cos_tab, sin_tab, w_in0)
    xc1 = _attention(q, z0, kbd, vt, sink_tab, x, ctx, mod0, w_out0)

    u_g, z1 = _proj1(xc1, mod1, norm_w[1], ssm_w_in[0].astype(BF16))
    m_mat, ws_mat, wy_mat, consts = _s5_operators(
        ssm_lam_re[0], ssm_lam_im[0], ssm_log_dt[0], ssm_b_re[0], ssm_b_im[0],
        ssm_c_re[0], ssm_c_im[0], ssm_d[0])
    y_g = _s5_core(u_g, m_mat, ws_mat, wy_mat, consts)
    return _out1(y_g, z1, xc1, mod1, ssm_w_glu[0].astype(BF16), ssm_w_out[0].astype(BF16),
                 final_norm_w)
```

```python
import math

import jax
import jax.numpy as jnp
import numpy as np
from jax import lax
from jax.experimental import pallas as pl
from jax.experimental.pallas import tpu as pltpu

F32 = jnp.float32
BF16 = jnp.bfloat16

D_MODEL = 1024
BATCH = 4
SEQ = 4096
GRID_W = 64
CTX_LEN = 256
TOTAL = CTX_LEN + SEQ
HEAD_DIM = 64
N_HEADS = 16
N_KV_HEADS = 4
ATTN_WIDTH = N_HEADS * HEAD_DIM
KV_WIDTH = N_KV_HEADS * HEAD_DIM
BLOCK = 128
N_BLOCKS = TOTAL // BLOCK
N_CTX_BLOCKS = CTX_LEN // BLOCK
ROPE_BASE = 10000.0
ROPE_FREQS = HEAD_DIM // 4
SSM_GROUP = 16
SSM_GROUPS = D_MODEL // SSM_GROUP
SSM_STATE = 64
NORM_EPS = 1e-6
NEG_INF = -1e30

LANES = 128
SUBLANES = 8
N_SLOTS = ATTN_WIDTH // LANES
N_KV_PAIRS = KV_WIDTH // LANES
SLOTS_PER_M = N_SLOTS // N_KV_PAIRS
ATTN_QBLOCKS = 2
assert N_CTX_BLOCKS % ATTN_QBLOCKS == 0 and N_BLOCKS % ATTN_QBLOCKS == 0
assert ATTN_QBLOCKS == 2
UNIT_SLOTS = 4
LOG2E = math.log2(math.e)
Q_SCALE = HEAD_DIM ** -0.5 * LOG2E
ROW_TILE = 256
N_ROW_TILES = TOTAL // ROW_TILE
CHUNK = 16
N_CHUNKS = TOTAL // CHUNK
N_CTX_CHUNKS = CTX_LEN // CHUNK
CHUNK_W = CHUNK * SSM_GROUP
CHUNKS_PER_TILE = ROW_TILE // CHUNK
PAIRS_PER_TILE = CHUNKS_PER_TILE // 2
SLOT_PITCH = 24
OUT_PHASES = 4
assert SLOT_PITCH >= CHUNK and SLOT_PITCH % SUBLANES == 0
STATE_W = 2 * SSM_STATE
SCAN_ROWS = N_CHUNKS * BATCH
CTX_SCAN_ROWS = N_CTX_CHUNKS * BATCH
N_SCAN_BLOCKS = SCAN_ROWS // SUBLANES
N_CTX_SCAN_BLOCKS = CTX_SCAN_ROWS // SUBLANES
GROUP_BATCH = 4
OP_GROUP_BATCH = 4
POW_ROWS = 48

assert BATCH * 2 == SUBLANES


def _params(semantics, vmem_mb):
    return pltpu.CompilerParams(dimension_semantics=semantics,
                                vmem_limit_bytes=vmem_mb * 1024 * 1024)


def _mod_kernel(c_ref, w_ref, b_ref, o_ref):
    c = c_ref[...]
    a = c * jax.nn.sigmoid(c)
    o_ref[0] = jnp.dot(a, w_ref[0], preferred_element_type=F32,
                       precision=lax.Precision.HIGHEST) + b_ref[0]


def _modulation(c, c_ctx, w_ada, b_ada):
    depth = w_ada.shape[0]
    rows = jnp.zeros((SUBLANES, D_MODEL), F32).at[:BATCH].set(c).at[BATCH].set(c_ctx)
    n_col = 3
    out = pl.pallas_call(
        _mod_kernel,
        grid=(depth, n_col),
        in_specs=[
            pl.BlockSpec((SUBLANES, D_MODEL), lambda l, j: (0, 0)),
            pl.BlockSpec((1, D_MODEL, D_MODEL), lambda l, j: (l, 0, j)),
            pl.BlockSpec((1, 1, D_MODEL), lambda l, j: (l, 0, j)),
        ],
        out_specs=pl.BlockSpec((1, SUBLANES, D_MODEL), lambda l, j: (l, 0, j)),
        out_shape=jax.ShapeDtypeStruct((depth, SUBLANES, 3 * D_MODEL), F32),
        compiler_params=_params(("arbitrary", "arbitrary"), 32),
        name="adaln_modulation",
    )(rows, w_ada, b_ada.reshape(depth, 1, 3 * D_MODEL))
    tabs = []
    for l in range(depth):
        lat = out[l, :BATCH].reshape(BATCH, 3, D_MODEL)
        cx = jnp.broadcast_to(out[l, BATCH].reshape(1, 3, D_MODEL), (BATCH, 3, D_MODEL))
        tab = jnp.stack([cx, lat], axis=1)
        tabs.append(jnp.pad(tab, ((0, 0), (0, 0), (0, SUBLANES - 3), (0, 0))))
    return tabs


def _modulated_norm(xt, nw, mod_ref, b=0):
    ms = jnp.mean(xt * xt, axis=-1, keepdims=True)
    y = xt * lax.rsqrt(ms + NORM_EPS) * nw
    return y * (1.0 + mod_ref[b, 0, 1:2, :]) + mod_ref[b, 0, 0:1, :]


def _lane_block_transpose(vs):
    n = len(vs)
    width = LANES // n
    blk = lax.broadcasted_iota(jnp.int32, vs[0].shape, 1) // width
    x = list(vs)
    d = n // 2
    while d >= 1:
        clear = (blk & d) == 0
        y = list(x)
        for i in range(n):
            if i & d == 0:
                a, b = x[i], x[i + d]
                y[i] = jnp.where(clear, a, pltpu.roll(b, width * d, 1))
                y[i + d] = jnp.where(clear, pltpu.roll(a, LANES - width * d, 1), b)
        x = y
        d //= 2
    return x


def _chunk_slot(b, cc):
    return (cc // 2) * (2 * BATCH) + 2 * b + (cc % 2)


def _proj0_kernel(x_ref, c_ref, mod_ref, nw_ref, cos_ref, sin_ref, w_ref,
                  q_ref, z_ref, kbd_ref, vt_ref):
    is_ctx = pl.program_id(0) == 0
    h = jnp.concatenate(
        [_modulated_norm(jnp.where(is_ctx, c_ref[b], x_ref[b]), nw_ref[...], mod_ref, b).astype(BF16)
         for b in range(BATCH)], axis=0)
    cos = cos_ref[...]
    sin = sin_ref[...]
    lane = lax.broadcasted_iota(jnp.int32, (ROW_TILE, LANES), 1)
    first_half = (lane % HEAD_DIM) < (HEAD_DIM // 2)
    low = lax.broadcasted_iota(jnp.int32, (BLOCK, LANES), 1) < HEAD_DIM

    def rope(t):
        partner = jnp.where(first_half, pltpu.roll(t, LANES - HEAD_DIM // 2, 1),
                            pltpu.roll(t, HEAD_DIM // 2, 1))
        return t * cos + partner * sin

    q = jnp.dot(h, w_ref[:, :ATTN_WIDTH], preferred_element_type=F32)
    k = jnp.dot(h, w_ref[:, ATTN_WIDTH:ATTN_WIDTH + KV_WIDTH], preferred_element_type=F32)
    v = jnp.dot(h, w_ref[:, ATTN_WIDTH + KV_WIDTH:ATTN_WIDTH + 2 * KV_WIDTH],
                preferred_element_type=F32)
    z = jnp.dot(h, w_ref[:, ATTN_WIDTH + 2 * KV_WIDTH:], preferred_element_type=F32)
    for b in range(BATCH):
        tile = slice(ROW_TILE * b, ROW_TILE * (b + 1))
        z_ref[b] = z[tile].astype(BF16)
        for j in range(N_SLOTS):
            m, gi = divmod(j, SLOTS_PER_M)
            qj = (rope(q[tile, LANES * j:LANES * (j + 1)]) * Q_SCALE).astype(BF16)
            for blk in range(ROW_TILE // BLOCK):
                q_ref[b, blk, m, BLOCK * gi:BLOCK * (gi + 1), :] = qj[BLOCK * blk:BLOCK * (blk + 1)]
        for m in range(N_KV_PAIRS):
            sl = slice(LANES * m, LANES * (m + 1))
            kr = rope(k[tile, sl])
            vm = v[tile, sl]
            for blk in range(ROW_TILE // BLOCK):
                rows = slice(BLOCK * blk, BLOCK * (blk + 1))
                kbd_ref[b, blk, m, :BLOCK, :] = jnp.where(low, kr[rows], 0.0).astype(BF16)
                kbd_ref[b, blk, m, BLOCK:, :] = jnp.where(low, 0.0, kr[rows]).astype(BF16)
                vt_ref[b, blk, LANES * m:LANES * (m + 1), :] = vm[rows].T.astype(BF16)


def _proj0(x, ctx, mod, norm_w, cos_tab, sin_tab, w_in):
    n_col = w_in.shape[1]
    blocks_per_tile = ROW_TILE // BLOCK
    kv_shape = jax.ShapeDtypeStruct((BATCH, N_BLOCKS, N_KV_PAIRS, 2 * BLOCK, LANES), BF16)
    kv_spec = pl.BlockSpec((BATCH, blocks_per_tile, N_KV_PAIRS, 2 * BLOCK, LANES),
                           lambda i: (0, i, 0, 0, 0))
    row_spec = pl.BlockSpec((BATCH, ROW_TILE, D_MODEL), lambda i: (0, i, 0))
    once = pl.Buffered(1)
    return pl.pallas_call(
        _proj0_kernel,
        grid=(N_ROW_TILES,),
        in_specs=[
            pl.BlockSpec((BATCH, ROW_TILE, D_MODEL), lambda i: (0, jnp.maximum(i - 1, 0), 0)),
            pl.BlockSpec((BATCH, ROW_TILE, D_MODEL), lambda i: (0, 0, 0), pipeline_mode=once),
            pl.BlockSpec((BATCH, 1, SUBLANES, D_MODEL), lambda i: (0, jnp.minimum(i, 1), 0, 0)),
            pl.BlockSpec((1, D_MODEL), lambda i: (0, 0)),
            pl.BlockSpec((ROW_TILE, LANES), lambda i: (i, 0)),
            pl.BlockSpec((ROW_TILE, LANES), lambda i: (i, 0)),
            pl.BlockSpec((D_MODEL, n_col), lambda i: (0, 0), pipeline_mode=once),
        ],
        out_specs=[
            pl.BlockSpec((BATCH, blocks_per_tile, N_KV_PAIRS, SLOTS_PER_M * BLOCK, LANES),
                         lambda i: (0, i, 0, 0, 0)),
            row_spec, kv_spec,
            pl.BlockSpec((BATCH, blocks_per_tile, KV_WIDTH, BLOCK), lambda i: (0, i, 0, 0))],
        out_shape=[
            jax.ShapeDtypeStruct((BATCH, N_BLOCKS, N_KV_PAIRS, SLOTS_PER_M * BLOCK, LANES), BF16),
            jax.ShapeDtypeStruct((BATCH, TOTAL, ATTN_WIDTH), BF16),
            kv_shape,
            jax.ShapeDtypeStruct((BATCH, N_BLOCKS, KV_WIDTH, BLOCK), BF16),
        ],
        compiler_params=_params(("arbitrary",), 56),
        name="attn_projection",
    )(x, ctx, mod, norm_w.reshape(1, D_MODEL), cos_tab, sin_tab, w_in)


def _attn_kernel(q_ref, z_ref, k0_ref, k1_ref, k2_ref, k3_ref, kx_ref,
                 v0_ref, v1_ref, v2_ref, v3_ref, vx_ref,
                 sink_ref, x_ref, c_ref, mod_ref, wo_ref, eye_ref, tri_ref, o_ref):
    step = pl.program_id(1)
    is_lat = step >= N_CTX_BLOCKS // ATTN_QBLOCKS
    n_first = ATTN_QBLOCKS * step - N_CTX_BLOCKS
    q_rows = UNIT_SLOTS * BLOCK
    blocked = tri_ref[2]
    bias = []
    for qb in range(ATTN_QBLOCKS):
        n = n_first + qb
        bias.append([jnp.where(jnp.logical_and(is_lat, n >= 1), tri_ref[0], blocked),
                     jnp.where(is_lat, tri_ref[3], blocked),
                     jnp.where(jnp.logical_and(is_lat, n <= SEQ // BLOCK - 2), tri_ref[1], blocked),
                     None, None])
    k_win = (k0_ref, k1_ref, k2_ref, k3_ref)
    v_win = (v0_ref, v1_ref, v2_ref, v3_ref)

    def kpiece(qb, p, m):
        if p < 3:
            return k_win[qb + p][0, 0, m], v_win[qb + p][0, 0]
        return kx_ref[0, p - 3, m], vx_ref[0, p - 3]

    n_piece = 3 + N_CTX_BLOCKS
    nt = (((1,), (1,)), ((), ()))

    half_rows = slice(0, BLOCK), slice(BLOCK, 2 * BLOCK)

    def scores(qb, m, h):
        qu = q_ref[0, qb, m, q_rows * h:q_rows * (h + 1), :]
        qu_masked = jnp.concatenate([qu, eye_ref[...]], axis=1)
        s_list = []
        for p in range(n_piece):
            kbd, _ = kpiece(qb, p, m)
            if bias[qb][p] is None:
                s = lax.dot_general(kbd, qu, nt, preferred_element_type=F32)
            else:
                s = lax.dot_general(jnp.concatenate([kbd, bias[qb][p]], axis=1), qu_masked, nt,
                                    preferred_element_type=F32)
            s_list.append(s)
        return s_list

    def finish(qb, m, h, s_list):
        slot0 = SLOTS_PER_M * m + UNIT_SLOTS * h
        halves = []
        for hs in range(2):
            sink = jnp.concatenate(
                [sink_ref[2 * (slot0 + gi) + hs:2 * (slot0 + gi) + hs + 1, :]
                 for gi in range(UNIT_SLOTS)], axis=1)
            mx = sink
            for s in s_list:
                mx = jnp.maximum(mx, jnp.max(s[half_rows[hs]], axis=0, keepdims=True))
            probs = [jnp.exp2(s[half_rows[hs]] - mx) for s in s_list]
            denom = jnp.exp2(sink - mx)
            for e in probs:
                denom = denom + jnp.sum(e, axis=0, keepdims=True)
            probs = [e.astype(BF16) for e in probs]
            kv_head = 2 * m + hs
            acc = None
            for p0 in range(0, n_piece, 2):
                group = list(range(p0, min(p0 + 2, n_piece)))
                vt = jnp.concatenate(
                    [kpiece(qb, p, m)[1][HEAD_DIM * kv_head:HEAD_DIM * (kv_head + 1), :]
                     for p in group], axis=1)
                pt = jnp.concatenate([probs[p] for p in group], axis=0)
                part = jnp.dot(vt, pt, preferred_element_type=F32)
                acc = part if acc is None else acc + part
            halves.append(acc * (1.0 / denom))
        o_t = jnp.concatenate(halves, axis=0)
        outs = []
        for gi in range(UNIT_SLOTS):
            j = slot0 + gi
            o = o_t[:, BLOCK * gi:BLOCK * (gi + 1)].T
            zj = z_ref[0, BLOCK * qb:BLOCK * (qb + 1), LANES * j:LANES * (j + 1)].astype(F32)
            outs.append((o * (zj * jax.nn.sigmoid(zj))).astype(BF16))
        return jnp.concatenate(outs, axis=1)

    units = [(qb, m, h) for m in range(N_KV_PAIRS) for h in range(SLOTS_PER_M // UNIT_SLOTS)
             for qb in range(ATTN_QBLOCKS)]
    y = None
    gated = []
    pending = scores(*units[0])
    for u, (qb, m, h) in enumerate(units):
        nxt = scores(*units[u + 1]) if u + 1 < len(units) else None
        gated.append(finish(qb, m, h, pending))
        pending = nxt
        if qb == ATTN_QBLOCKS - 1:
            slot0 = SLOTS_PER_M * m + UNIT_SLOTS * h
            part = jnp.dot(jnp.concatenate(gated, axis=0),
                           wo_ref[LANES * slot0:LANES * (slot0 + UNIT_SLOTS), :],
                           preferred_element_type=F32)
            y = part if y is None else y + part
            gated = []
    resid = jnp.where(is_lat, x_ref[0], c_ref[0])
    o_ref[0] = resid + mod_ref[0, 0, 2:3, :] * y


def _attention(q, z, kbd, vt, sink_tab, x, ctx, mod, w_out):
    last = N_BLOCKS - 1
    n_m = KV_WIDTH // LANES
    nq = ATTN_QBLOCKS
    ctx_steps = N_CTX_BLOCKS // nq
    row_spec = pl.BlockSpec((1, nq * BLOCK, ATTN_WIDTH), lambda b, i: (b, i, 0))

    def kv_spec(off):
        return pl.BlockSpec((1, 1, n_m, 2 * BLOCK, LANES),
                            lambda b, i: (b, jnp.clip(nq * i + off, 0, last), 0, 0, 0))

    ctx_kv_spec = pl.BlockSpec((1, N_CTX_BLOCKS, n_m, 2 * BLOCK, LANES),
                               lambda b, i: (b, 0, 0, 0, 0))

    def vt_spec(off):
        return pl.BlockSpec((1, 1, KV_WIDTH, BLOCK),
                            lambda b, i: (b, jnp.clip(nq * i + off, 0, last), 0, 0))

    ctx_vt_spec = pl.BlockSpec((1, N_CTX_BLOCKS, KV_WIDTH, BLOCK), lambda b, i: (b, 0, 0, 0))
    off = np.arange(BLOCK)
    eye = np.tile(np.eye(BLOCK, dtype=np.float32), (UNIT_SLOTS, 1))
    key_ge = np.where(off[:, None] >= off[None, :], 0.0, NEG_INF)
    key_le = np.where(off[:, None] <= off[None, :], 0.0, NEG_INF)
    tri = np.stack([np.tile(t, (2, 1)) for t in
                    (key_ge, key_le, np.full((BLOCK, BLOCK), NEG_INF), np.zeros((BLOCK, BLOCK)))])
    consts = [jnp.asarray(a, BF16) for a in (eye, tri)]
    return pl.pallas_call(
        _attn_kernel,
        grid=(BATCH, N_BLOCKS // nq),
        in_specs=[
            pl.BlockSpec((1, nq, n_m, SLOTS_PER_M * BLOCK, LANES), lambda b, i: (b, i, 0, 0, 0)),
            row_spec,
            kv_spec(-1), kv_spec(0), kv_spec(1), kv_spec(2), ctx_kv_spec,
            vt_spec(-1), vt_spec(0), vt_spec(1), vt_spec(2), ctx_vt_spec,
            pl.BlockSpec((2 * N_SLOTS, LANES), lambda b, i: (0, 0)),
            pl.BlockSpec((1, nq * BLOCK, D_MODEL),
                         lambda b, i: (b, jnp.maximum(i - ctx_steps, 0), 0)),
            pl.BlockSpec((1, nq * BLOCK, D_MODEL),
                         lambda b, i: (b, jnp.minimum(i, ctx_steps - 1), 0)),
            pl.BlockSpec((1, 1, SUBLANES, D_MODEL),
                         lambda b, i: (b, jnp.minimum(i // ctx_steps, 1), 0, 0)),
            pl.BlockSpec((ATTN_WIDTH, D_MODEL), lambda b, i: (0, 0)),
        ] + [pl.BlockSpec(a.shape, lambda b, i, nd=a.ndim: (0,) * nd) for a in consts],
        out_specs=pl.BlockSpec((1, nq * BLOCK, D_MODEL), lambda b, i: (b, i, 0)),
        out_shape=jax.ShapeDtypeStruct((BATCH, TOTAL, D_MODEL), F32),
        compiler_params=_params(("arbitrary", "arbitrary"), 48),
        name="window_attention",
    )(q, z, kbd, kbd, kbd, kbd, kbd, vt, vt, vt, vt, vt, sink_tab, x, ctx, mod, w_out,
      *consts)


def _proj1_kernel(x_ref, mod_ref, nw_ref, w_ref, u_ref, z_ref, scr_ref):
    h = jnp.concatenate(
        [_modulated_norm(x_ref[b], nw_ref[...], mod_ref, b).astype(BF16) for b in range(BATCH)],
        axis=0)
    u = jnp.dot(h, w_ref[:, :D_MODEL], preferred_element_type=F32)
    z = jnp.dot(h, w_ref[:, D_MODEL:], preferred_element_type=F32)
    for b in range(BATCH):
        z_ref[b] = z[ROW_TILE * b:ROW_TILE * (b + 1)].astype(BF16)
    n_slab = D_MODEL // LANES
    for b in range(BATCH):
        for cc in range(CHUNKS_PER_TILE):
            r0 = ROW_TILE * b + CHUNK * cc
            q0 = SLOT_PITCH * _chunk_slot(b, cc)
            for k in range(n_slab):
                scr_ref[k, q0:q0 + CHUNK, :] = u[r0:r0 + CHUNK, LANES * k:LANES * (k + 1)]
    groups_per_slab = LANES // SSM_GROUP
    for k in range(n_slab):
        for pp in range(PAIRS_PER_TILE // 2):
            parts = []
            for p in (2 * pp, 2 * pp + 1):
                rows = [scr_ref[k, pl.ds(SLOT_PITCH * SUBLANES * p + s, SUBLANES,
                                         stride=SLOT_PITCH), :] for s in range(CHUNK)]
                parts.append([_lane_block_transpose(rows[SUBLANES * m2:SUBLANES * (m2 + 1)])
                              for m2 in range(CHUNK // SUBLANES)])
            for m2 in range(CHUNK // SUBLANES):
                for gl in range(groups_per_slab):
                    val = jnp.concatenate([parts[0][m2][gl], parts[1][m2][gl]], axis=0)
                    u_ref[groups_per_slab * k + gl, 2 * SUBLANES * pp:2 * SUBLANES * (pp + 1),
                          LANES * m2:LANES * (m2 + 1)] = val.astype(BF16)


def _proj1(xc, mod, norm_w, w_in):
    row_spec = pl.BlockSpec((BATCH, ROW_TILE, D_MODEL), lambda i: (0, i, 0))
    tile_rows = PAIRS_PER_TILE * SUBLANES
    return pl.pallas_call(
        _proj1_kernel,
        grid=(N_ROW_TILES,),
        in_specs=[
            row_spec,
            pl.BlockSpec((BATCH, 1, SUBLANES, D_MODEL), lambda i: (0, jnp.minimum(i, 1), 0, 0)),
            pl.BlockSpec((1, D_MODEL), lambda i: (0, 0)),
            pl.BlockSpec((D_MODEL, 2 * D_MODEL), lambda i: (0, 0)),
        ],
        out_specs=[pl.BlockSpec((SSM_GROUPS, tile_rows, CHUNK_W), lambda i: (0, i, 0)), row_spec],
        out_shape=[jax.ShapeDtypeStruct((SSM_GROUPS, SCAN_ROWS, CHUNK_W), BF16),
                   jax.ShapeDtypeStruct((BATCH, TOTAL, D_MODEL), BF16)],
        scratch_shapes=[pltpu.VMEM((D_MODEL // LANES, BATCH * CHUNKS_PER_TILE * SLOT_PITCH, LANES),
                                   F32)],
        compiler_params=_params(("arbitrary",), 56),
        name="ssm_projection",
    )(xc, mod, norm_w.reshape(1, D_MODEL), w_in)


def _s5_kernel(u_ref, m_ref, ws_ref, wy_ref, cst_ref, y_ref, s4_ref, xp_ref):
    even = lax.broadcasted_iota(jnp.int32, (SUBLANES, STATE_W), 0) % 2 == 0
    down = 1
    up = SUBLANES - 1
    fwd = slice(0, STATE_W)
    bwd = slice(STATE_W, 2 * STATE_W)
    fwd_sw = slice(2 * STATE_W, 3 * STATE_W)
    bwd_sw = slice(3 * STATE_W, 4 * STATE_W)
    for g in range(GROUP_BATCH):
        s = jnp.dot(u_ref[g], ws_ref[g], preferred_element_type=F32)
        s = s.reshape(N_SCAN_BLOCKS, SUBLANES, 4 * STATE_W)
        q1f, q2f, q1b, q2b = cst_ref[g, 2], cst_ref[g, 3], cst_ref[g, 6], cst_ref[g, 7]
        zf, zb, zfs, zbs = s[..., fwd], s[..., bwd], s[..., fwd_sw], s[..., bwd_sw]
        rzf, rzfs = pltpu.roll(zf, down, 1), pltpu.roll(zfs, down, 1)
        rzb, rzbs = pltpu.roll(zb, up, 1), pltpu.roll(zbs, up, 1)
        t = jnp.concatenate([zf + q1f * rzf + q2f * rzfs, zb + q1b * rzb + q2b * rzbs,
                             zfs + q1f * rzfs - q2f * rzf, zbs + q1b * rzbs - q2b * rzb], axis=-1)
        s4_ref[g] = t.reshape(SCAN_ROWS, 4 * STATE_W)

    def step(j, carry):
        jb = jnp.where(j < N_CTX_SCAN_BLOCKS, N_CTX_SCAN_BLOCKS - 1 - j,
                       N_SCAN_BLOCKS - 1 + N_CTX_SCAN_BLOCKS - j)
        rf = pl.ds(pl.multiple_of(j * SUBLANES, SUBLANES), SUBLANES)
        rb = pl.ds(pl.multiple_of(jb * SUBLANES, SUBLANES), SUBLANES)
        new = []
        for g in range(GROUP_BATCH):
            cf, cfs, cb, cbs = carry[4 * g:4 * g + 4]
            p1f, p2f, p1b, p2b = cst_ref[g, 0], cst_ref[g, 1], cst_ref[g, 4], cst_ref[g, 5]
            xf = p1f * cf + p2f * cfs + s4_ref[g, rf, fwd]
            xfs = p1f * cfs - p2f * cf + s4_ref[g, rf, fwd_sw]
            xp_ref[g, rf, fwd] = jnp.where(even, cf, pltpu.roll(xf, down, 0))
            new += [jnp.where(even, pltpu.roll(xf, up, 0), xf),
                    jnp.where(even, pltpu.roll(xfs, up, 0), xfs)]
            xb = p1b * cb + p2b * cbs + s4_ref[g, rb, bwd]
            xbs = p1b * cbs - p2b * cb + s4_ref[g, rb, bwd_sw]
            xp_ref[g, rb, bwd] = jnp.where(even, pltpu.roll(xb, up, 0), cb)
            new += [jnp.where(even, xb, pltpu.roll(xb, down, 0)),
                    jnp.where(even, xbs, pltpu.roll(xbs, down, 0))]
        return tuple(new)

    zero = jnp.zeros((SUBLANES, STATE_W), F32)
    lax.fori_loop(0, N_SCAN_BLOCKS, step, (zero,) * (4 * GROUP_BATCH))
    for g in range(GROUP_BATCH):
        y_ref[g] = (
            jnp.dot(u_ref[g, CTX_SCAN_ROWS:, :], m_ref[g], preferred_element_type=F32)
            + lax.dot_general(xp_ref[g, CTX_SCAN_ROWS:, :].astype(BF16), wy_ref[g],
                              (((1,), (1,)), ((), ())), preferred_element_type=F32))


def _s5_core(u_g, m_mat, ws_mat, wy_mat, consts):
    lat_rows = SCAN_ROWS - CTX_SCAN_ROWS

    def gspec(*tail):
        return pl.BlockSpec((GROUP_BATCH,) + tail, lambda i: (i,) + (0,) * len(tail))

    return pl.pallas_call(
        _s5_kernel,
        grid=(SSM_GROUPS // GROUP_BATCH,),
        in_specs=[
            gspec(SCAN_ROWS, CHUNK_W),
            gspec(CHUNK_W, CHUNK_W),
            gspec(CHUNK_W, 4 * STATE_W),
            gspec(CHUNK_W, 2 * STATE_W),
            gspec(8, SUBLANES, STATE_W),
        ],
        out_specs=gspec(lat_rows, CHUNK_W),
        out_shape=jax.ShapeDtypeStruct((SSM_GROUPS, lat_rows, CHUNK_W), F32),
        scratch_shapes=[
            pltpu.VMEM((GROUP_BATCH, SCAN_ROWS, 4 * STATE_W), F32),
            pltpu.VMEM((GROUP_BATCH, SCAN_ROWS, 2 * STATE_W), F32),
        ],
        compiler_params=_params(("arbitrary",), 48),
        name="s5_scan",
    )(u_g, m_mat, ws_mat, wy_mat, consts)


def _out1_kernel(y_ref, z_ref, x_ref, mod_ref, wg_ref, wo_ref, fnw_ref, o_ref, scr_ref):
    n_slab = D_MODEL // LANES
    groups_per_slab = LANES // SSM_GROUP
    pairs_per_phase = PAIRS_PER_TILE // OUT_PHASES
    rows_per_phase = ROW_TILE // OUT_PHASES

    def relayout(ph):
        for k in range(n_slab):
            for p in range(pairs_per_phase * ph, pairs_per_phase * (ph + 1)):
                for m2 in range(CHUNK // SUBLANES):
                    vals = [y_ref[groups_per_slab * k + gl, SUBLANES * p:SUBLANES * (p + 1),
                                  LANES * m2:LANES * (m2 + 1)] for gl in range(groups_per_slab)]
                    steps = _lane_block_transpose(vals)
                    for s2 in range(SUBLANES):
                        t_idx = SUBLANES * m2 + s2
                        scr_ref[k, pl.ds(SLOT_PITCH * SUBLANES * p + t_idx, SUBLANES,
                                         stride=SLOT_PITCH), :] = steps[s2]

    def compute(ph):
        chunks = range(2 * pairs_per_phase * ph, 2 * pairs_per_phase * (ph + 1))
        y = jnp.concatenate(
            [jnp.concatenate(
                [scr_ref[k, SLOT_PITCH * _chunk_slot(b, cc):SLOT_PITCH * _chunk_slot(b, cc) + CHUNK, :]
                 for k in range(n_slab)], axis=1)
             for b in range(BATCH) for cc in chunks], axis=0)
        g = (0.5 * y * (1.0 + lax.erf(y * (2.0 ** -0.5)))).astype(BF16)
        t = jnp.dot(g, wg_ref[...], preferred_element_type=F32)
        rows = slice(rows_per_phase * ph, rows_per_phase * (ph + 1))
        z = jnp.concatenate([z_ref[b, rows, :] for b in range(BATCH)], axis=0).astype(F32)
        r = (t[:, :D_MODEL] * jax.nn.sigmoid(t[:, D_MODEL:]) * (z * jax.nn.sigmoid(z))).astype(BF16)
        o = jnp.dot(r, wo_ref[...], preferred_element_type=F32)
        for b in range(BATCH):
            x2 = (x_ref[b, rows, :]
                  + mod_ref[b, 0, 2:3, :] * o[rows_per_phase * b:rows_per_phase * (b + 1)])
            ms = jnp.mean(x2 * x2, axis=-1, keepdims=True)
            o_ref[b, rows, :] = x2 * lax.rsqrt(ms + NORM_EPS) * fnw_ref[...]

    relayout(0)
    for ph in range(OUT_PHASES):
        if ph + 1 < OUT_PHASES:
            relayout(ph + 1)
        compute(ph)


def _out1(y_g, z, xc, mod, w_glu, w_out, final_norm_w):
    ctx_tiles = CTX_LEN // ROW_TILE
    tile_rows = PAIRS_PER_TILE * SUBLANES
    lat_spec = pl.BlockSpec((BATCH, ROW_TILE, D_MODEL), lambda i: (0, i, 0))
    all_spec = pl.BlockSpec((BATCH, ROW_TILE, D_MODEL), lambda i: (0, i + ctx_tiles, 0))
    return pl.pallas_call(
        _out1_kernel,
        grid=(SEQ // ROW_TILE,),
        in_specs=[
            pl.BlockSpec((SSM_GROUPS, tile_rows, CHUNK_W), lambda i: (0, i, 0)),
            all_spec, all_spec,
            pl.BlockSpec((BATCH, 1, SUBLANES, D_MODEL), lambda i: (0, 1, 0, 0)),
            pl.BlockSpec((D_MODEL, 2 * D_MODEL), lambda i: (0, 0)),
            pl.BlockSpec((D_MODEL, D_MODEL), lambda i: (0, 0)),
            pl.BlockSpec((1, D_MODEL), lambda i: (0, 0)),
        ],
        out_specs=lat_spec,
        out_shape=jax.ShapeDtypeStruct((BATCH, SEQ, D_MODEL), F32),
        scratch_shapes=[pltpu.VMEM((D_MODEL // LANES, BATCH * CHUNKS_PER_TILE * SLOT_PITCH, LANES),
                                   F32)],
        compiler_params=_params(("arbitrary",), 56),
        name="ssm_output",
    )(y_g, z, xc, mod, w_glu, w_out, final_norm_w.reshape(1, D_MODEL))


def _slot_order(t, lead):
    n_m = KV_WIDTH // LANES
    gq = N_HEADS // N_KV_HEADS
    shape = t.shape
    t = t.reshape(shape[:lead] + (n_m, 2, gq) + shape[lead + 1:])
    perm = tuple(range(lead)) + (lead, lead + 2, lead + 1) + tuple(range(lead + 3, t.ndim))
    return jnp.transpose(t, perm)


def _rope_order(t):
    shape = t.shape
    t = t.reshape(shape[:-1] + (2, 2, ROPE_FREQS))
    return jnp.swapaxes(t, -3, -2).reshape(shape)


def _attn_weights(w_in, w_out, sink):
    wq = w_in[:, :ATTN_WIDTH].reshape(D_MODEL, N_HEADS, HEAD_DIM)
    wq = _slot_order(_rope_order(wq), 1).reshape(D_MODEL, ATTN_WIDTH)
    wk = w_in[:, ATTN_WIDTH:ATTN_WIDTH + KV_WIDTH].reshape(D_MODEL, N_KV_HEADS, HEAD_DIM)
    wk = _rope_order(wk).reshape(D_MODEL, KV_WIDTH)
    wv = w_in[:, ATTN_WIDTH + KV_WIDTH:ATTN_WIDTH + 2 * KV_WIDTH]
    wz = w_in[:, ATTN_WIDTH + 2 * KV_WIDTH:].reshape(D_MODEL, N_HEADS, HEAD_DIM)
    wz = _slot_order(wz, 1).reshape(D_MODEL, ATTN_WIDTH)
    w_in_p = jnp.concatenate([wq, wk, wv, wz], axis=1).astype(BF16)
    wo = _slot_order(w_out.reshape(N_HEADS, HEAD_DIM, D_MODEL), 0).reshape(ATTN_WIDTH, D_MODEL)
    sink_p = _slot_order(sink.astype(F32).reshape(N_HEADS), 0).reshape(2 * N_SLOTS)
    sink_tab = jnp.broadcast_to((sink_p * LOG2E)[:, None], (2 * N_SLOTS, LANES))
    return w_in_p, wo.astype(BF16), sink_tab


def _rope_tables():
    inv = ROPE_BASE ** (-np.arange(ROPE_FREQS, dtype=np.float64) / ROPE_FREQS)
    pos = np.arange(SEQ)
    row = (pos // GRID_W)[:, None] * inv
    col = (pos % GRID_W)[:, None] * inv
    w = np.arange(LANES) % HEAD_DIM
    half_w = HEAD_DIM // 2
    half, axis, f = w // half_w, (w % half_w) // ROPE_FREQS, w % ROPE_FREQS
    ang = np.where((axis == 0)[None, :], row[:, f], col[:, f])
    sign = np.where(half == 0, -1.0, 1.0)[None, :]
    cos = np.concatenate([np.ones((CTX_LEN, LANES)), np.cos(ang)], axis=0)
    sin = np.concatenate([np.zeros((CTX_LEN, LANES)), np.sin(ang) * sign], axis=0)
    return jnp.asarray(cos, F32), jnp.asarray(sin, F32)


def _s5_operators(lam_re, lam_im, log_dt, b_re, b_im, c_re, c_im, d_skip):
    t_len = CHUNK
    n_pow = 2 * t_len + 1
    lr, li = lam_re.astype(F32), lam_im.astype(F32)
    dt = jnp.exp(log_dt.astype(F32))[..., None]
    mag = jnp.exp(lr * dt)
    sq = [(mag * jnp.cos(li * dt), mag * jnp.sin(li * dt))]
    while 2 ** len(sq) < n_pow:
        r, i = sq[-1]
        sq.append((r * r - i * i, 2.0 * r * i))
    ks = np.arange(n_pow)
    pr = jnp.ones(lr.shape + (n_pow,), F32)
    pi = jnp.zeros(lr.shape + (n_pow,), F32)
    for bit, (r, i) in enumerate(sq):
        on = jnp.asarray((ks >> bit) & 1 == 1)
        fr = jnp.where(on, r[..., None], 1.0)
        fi = jnp.where(on, i[..., None], 0.0)
        pr, pi = pr * fr - pi * fi, pr * fi + pi * fr
    ar1, ai1 = sq[0][0] - 1.0, sq[0][1]
    den = lr * lr + li * li
    gr, gi = (ar1 * lr + ai1 * li) / den, (ai1 * lr - ar1 * li) / den
    br_, bi_ = b_re.astype(F32), b_im.astype(F32)
    bbr = gr[..., None] * br_ - gi[..., None] * bi_
    bbi = gr[..., None] * bi_ + gi[..., None] * br_
    pad_k = POW_ROWS - n_pow
    prt, pit = jnp.swapaxes(pr, 2, 3), jnp.swapaxes(pi, 2, 3)
    pw = jnp.pad(jnp.concatenate([prt, prt, pit, pit], axis=-1),
                 ((0, 0), (0, 0), (0, pad_k), (0, 0)))
    brt, bit = jnp.swapaxes(bbr, 2, 3), jnp.swapaxes(bbi, 2, 3)
    bb = jnp.concatenate([brt, bit, -bit, brt, bit, brt, brt, -bit], axis=-1)
    cr, ci = c_re.astype(F32), c_im.astype(F32)
    cm = jnp.stack([jnp.concatenate([cr, -ci], axis=-1),
                    jnp.concatenate([-ci, -cr], axis=-1)], axis=2)
    dv = jnp.tile(d_skip.astype(F32).reshape(SSM_GROUPS, 1, SSM_GROUP), (1, 1, t_len))
    return _s5_operator_call(pw, bb, cm, dv)


def _operator_constants():
    t_len = CHUNK
    s_of_row = np.arange(CHUNK_W) // SSM_GROUP
    k_ar = np.arange(POW_ROWS)
    oh_f = (k_ar[None, :] == (t_len - 1 - s_of_row)[:, None]).astype(np.float32)
    oh_b = (k_ar[None, :] == s_of_row[:, None]).astype(np.float32)
    t_of_lane = np.arange(CHUNK_W) // SSM_GROUP
    expo = [s_of_row + 1, t_len - s_of_row, s_of_row, t_len - 1 - s_of_row]
    sel = np.stack([(k_ar[None, :] == e[:, None]) for e in expo]).astype(np.float32)
    h_of_lane = np.arange(CHUNK_W) % SSM_GROUP
    dmask = ((s_of_row[:, None] == t_of_lane[None, :])
             & ((np.arange(CHUNK_W) % SSM_GROUP)[:, None] == h_of_lane[None, :])).astype(np.float32)
    return oh_f, oh_b, sel, dmask


def _s5_op_kernel(*refs):
    for gg in range(OP_GROUP_BATCH):
        _s5_op_group(gg, *refs)


def _s5_op_group(gg, pw_ref, bb_ref, cm_ref, dv_ref, ohf_ref, ohb_ref, sel_ref, dmask_ref,
                 m_ref, ws_ref, wyt_ref, cst_ref):
    t_len = CHUNK

    def split(a):
        hi = a.astype(BF16)
        return hi, (a - hi.astype(F32)).astype(BF16)

    def pick_rows(onehot2, table):
        hi, lo = split(table)
        return jnp.dot(onehot2, jnp.concatenate([hi, lo], axis=0), preferred_element_type=F32)

    def tile_rows(a):
        return jnp.concatenate([a] * t_len, axis=0)

    main, swapped = [], []
    for d, oh_ref in ((0, ohf_ref), (1, ohb_ref)):
        pp = pick_rows(oh_ref[...], pw_ref[d, gg])
        p_re, p_im = pp[:, :STATE_W], pp[:, STATE_W:]
        b0, b1, b2, b3 = [tile_rows(bb_ref[d, gg, :, STATE_W * i:STATE_W * (i + 1)])
                          for i in range(4)]
        main.append(p_re * b0 + p_im * b1)
        swapped.append(p_re * b2 + p_im * b3)
    ws_ref[gg] = jnp.concatenate(main + swapped, axis=1).astype(BF16)

    c_tiled = [[tile_rows(cm_ref[d, gg, i]) for i in range(2)] for d in range(2)]

    def block_t(d, pat):
        pp = pick_rows(sel_ref[pat], pw_ref[d, gg])
        return pp[:, :STATE_W] * c_tiled[d][0] + pp[:, STATE_W:] * c_tiled[d][1]

    wyt_ref[gg] = jnp.concatenate([block_t(0, 0), block_t(1, 1)], axis=1).astype(BF16)

    nt = (((1,), (1,)), ((), ()))
    kt_f = lax.dot_general(bb_ref[0, gg, :, :STATE_W], block_t(0, 2), nt,
                           preferred_element_type=F32, precision=lax.Precision.HIGHEST)
    kt_b = lax.dot_general(bb_ref[1, gg, :, :STATE_W], block_t(1, 3), nt,
                           preferred_element_type=F32, precision=lax.Precision.HIGHEST)
    lane = lax.broadcasted_iota(jnp.int32, (SSM_GROUP, CHUNK_W), 1)
    skip = dv_ref[gg]
    for s in range(t_len):
        fwd = kt_f if s == 0 else pltpu.roll(kt_f, SSM_GROUP * s, 1)
        back = t_len - 1 - s
        bwd = kt_b if back == 0 else pltpu.roll(kt_b, CHUNK_W - SSM_GROUP * back, 1)
        rows = (jnp.where(lane >= SSM_GROUP * s, fwd, 0.0)
                + jnp.where(lane < SSM_GROUP * (s + 1), bwd, 0.0)
                + dmask_ref[SSM_GROUP * s:SSM_GROUP * (s + 1), :] * skip)
        m_ref[gg, SSM_GROUP * s:SSM_GROUP * (s + 1), :] = rows.astype(BF16)

    even = lax.broadcasted_iota(jnp.int32, (SUBLANES, STATE_W), 0) % 2 == 0
    sign = jnp.where(lax.broadcasted_iota(jnp.int32, (1, STATE_W), 1) < SSM_STATE, -1.0, 1.0)

    def w12(d, k):
        row = pw_ref[d, gg, k:k + 1, :]
        return row[:, :STATE_W], row[:, STATE_W:] * sign

    zero = (jnp.zeros((1, STATE_W), F32),) * 2
    pairs = [(w12(0, t_len), w12(0, 2 * t_len)), (zero, w12(0, t_len)),
             (w12(1, 2 * t_len), w12(1, t_len)), (w12(1, t_len), zero)]
    idx = 0
    for top, bot in pairs:
        for part in range(2):
            cst_ref[gg, idx] = jnp.where(even, jnp.broadcast_to(top[part], (SUBLANES, STATE_W)),
                                         jnp.broadcast_to(bot[part], (SUBLANES, STATE_W)))
            idx += 1


def _s5_operator_call(pw, bb, cm, dv):
    oh_f, oh_b, sel, dmask = _operator_constants()
    consts = [jnp.asarray(np.concatenate([a, a], axis=-1), BF16) for a in (oh_f, oh_b, sel)]
    consts.append(jnp.asarray(dmask))

    def per_group(*tail):
        n = len(tail)
        return pl.BlockSpec((2, OP_GROUP_BATCH) + tail, lambda g: (0, g) + (0,) * n)

    def whole(a):
        return pl.BlockSpec(a.shape, lambda g: (0,) * a.ndim)

    def out(*tail):
        return pl.BlockSpec((OP_GROUP_BATCH,) + tail, lambda g: (g,) + (0,) * len(tail))

    return pl.pallas_call(
        _s5_op_kernel,
        grid=(SSM_GROUPS // OP_GROUP_BATCH,),
        in_specs=[per_group(POW_ROWS, 2 * STATE_W), per_group(SSM_GROUP, 4 * STATE_W),
                  per_group(2, SSM_GROUP, STATE_W),
                  pl.BlockSpec((OP_GROUP_BATCH, 1, CHUNK_W), lambda g: (g, 0, 0))]
                 + [whole(a) for a in consts],
        out_specs=[out(CHUNK_W, CHUNK_W), out(CHUNK_W, 4 * STATE_W), out(CHUNK_W, 2 * STATE_W),
                   out(8, SUBLANES, STATE_W)],
        out_shape=[jax.ShapeDtypeStruct((SSM_GROUPS, CHUNK_W, CHUNK_W), BF16),
                   jax.ShapeDtypeStruct((SSM_GROUPS, CHUNK_W, 4 * STATE_W), BF16),
                   jax.ShapeDtypeStruct((SSM_GROUPS, CHUNK_W, 2 * STATE_W), BF16),
                   jax.ShapeDtypeStruct((SSM_GROUPS, 8, SUBLANES, STATE_W), F32)],
        compiler_params=_params(("arbitrary",), 32),
        name="s5_operators",
    )(pw, bb, cm, dv, *consts)


def kernel(x, c, ctx, c_ctx, norm_w, w_ada, b_ada, attn_w_in, attn_sink, attn_w_out,
           ssm_w_in, ssm_lam_re, ssm_lam_im, ssm_log_dt, ssm_b_re, ssm_b_im, ssm_c_re, ssm_c_im,
           ssm_d, ssm_w_glu, ssm_w_out, final_norm_w):
    mod0, mod1 = _modulation(c, c_ctx, w_ada, b_ada)

    w_in0, w_out0, sink_tab = _attn_weights(attn_w_in[0], attn_w_out[0], attn_sink[0])
    cos_tab, sin_tab = _rope_tables()
    q, z0, kbd, vt = _proj0(x, ctx, mod0, norm_w[0], cos_tab, sin_tab, w_in0)
    xc1 = _attention(q, z0, kbd, vt, sink_tab, x, ctx, mod0, w_out0)

    u_g, z1 = _proj1(xc1, mod1, norm_w[1], ssm_w_in[0].astype(BF16))
    m_mat, ws_mat, wy_mat, consts = _s5_operators(
        ssm_lam_re[0], ssm_lam_im[0], ssm_log_dt[0], ssm_b_re[0], ssm_b_im[0],
        ssm_c_re[0], ssm_c_im[0], ssm_d[0])
    y_g = _s5_core(u_g, m_mat, ws_mat, wy_mat, consts)
    return _out1(y_g, z1, xc1, mod1, ssm_w_glu[0].astype(BF16), ssm_w_out[0].astype(BF16),
                 final_norm_w)
```

```python
import math

import jax
import jax.numpy as jnp
import numpy as np
from jax import lax
from jax.experimental import pallas as pl
from jax.experimental.pallas import tpu as pltpu

F32 = jnp.float32
BF16 = jnp.bfloat16

D_MODEL = 1024
BATCH = 4
SEQ = 4096
GRID_W = 64
CTX_LEN = 256
TOTAL = CTX_LEN + SEQ
HEAD_DIM = 64
N_HEADS = 16
N_KV_HEADS = 4
ATTN_WIDTH = N_HEADS * HEAD_DIM
KV_WIDTH = N_KV_HEADS * HEAD_DIM
BLOCK = 128
N_BLOCKS = TOTAL // BLOCK
N_CTX_BLOCKS = CTX_LEN // BLOCK
ROPE_BASE = 10000.0
ROPE_FREQS = HEAD_DIM // 4
SSM_GROUP = 16
SSM_GROUPS = D_MODEL // SSM_GROUP
SSM_STATE = 64
NORM_EPS = 1e-6
NEG_INF = -1e30

LANES = 128
SUBLANES = 8
N_SLOTS = ATTN_WIDTH // LANES
N_KV_PAIRS = KV_WIDTH // LANES
SLOTS_PER_M = N_SLOTS // N_KV_PAIRS
ATTN_QBLOCKS = 2
assert N_CTX_BLOCKS % ATTN_QBLOCKS == 0 and N_BLOCKS % ATTN_QBLOCKS == 0
assert ATTN_QBLOCKS == 2
UNIT_SLOTS = 4
LOG2E = math.log2(math.e)
Q_SCALE = HEAD_DIM ** -0.5 * LOG2E
ROW_TILE = 256
N_ROW_TILES = TOTAL // ROW_TILE
CHUNK = 16
N_CHUNKS = TOTAL // CHUNK
N_CTX_CHUNKS = CTX_LEN // CHUNK
CHUNK_W = CHUNK * SSM_GROUP
CHUNKS_PER_TILE = ROW_TILE // CHUNK
PAIRS_PER_TILE = CHUNKS_PER_TILE // 2
SLOT_PITCH = 17
OUT_PHASES = 4
assert SLOT_PITCH >= CHUNK and (BATCH * CHUNKS_PER_TILE * SLOT_PITCH) % SUBLANES == 0
STATE_W = 2 * SSM_STATE
SCAN_ROWS = N_CHUNKS * BATCH
CTX_SCAN_ROWS = N_CTX_CHUNKS * BATCH
N_SCAN_BLOCKS = SCAN_ROWS // SUBLANES
N_CTX_SCAN_BLOCKS = CTX_SCAN_ROWS // SUBLANES
GROUP_BATCH = 4
OP_GROUP_BATCH = 4
POW_ROWS = 48

assert BATCH * 2 == SUBLANES


def _params(semantics, vmem_mb):
    return pltpu.CompilerParams(dimension_semantics=semantics,
                                vmem_limit_bytes=vmem_mb * 1024 * 1024)


def _mod_kernel(c_ref, w_ref, b_ref, o_ref):
    c = c_ref[...]
    a = c * jax.nn.sigmoid(c)
    o_ref[0] = jnp.dot(a, w_ref[0], preferred_element_type=F32,
                       precision=lax.Precision.HIGHEST) + b_ref[0]


def _modulation(c, c_ctx, w_ada, b_ada):
    depth = w_ada.shape[0]
    rows = jnp.zeros((SUBLANES, D_MODEL), F32).at[:BATCH].set(c).at[BATCH].set(c_ctx)
    n_col = 3
    out = pl.pallas_call(
        _mod_kernel,
        grid=(depth, n_col),
        in_specs=[
            pl.BlockSpec((SUBLANES, D_MODEL), lambda l, j: (0, 0)),
            pl.BlockSpec((1, D_MODEL, D_MODEL), lambda l, j: (l, 0, j)),
            pl.BlockSpec((1, 1, D_MODEL), lambda l, j: (l, 0, j)),
        ],
        out_specs=pl.BlockSpec((1, SUBLANES, D_MODEL), lambda l, j: (l, 0, j)),
        out_shape=jax.ShapeDtypeStruct((depth, SUBLANES, 3 * D_MODEL), F32),
        compiler_params=_params(("arbitrary", "arbitrary"), 32),
        name="adaln_modulation",
    )(rows, w_ada, b_ada.reshape(depth, 1, 3 * D_MODEL))
    tabs = []
    for l in range(depth):
        lat = out[l, :BATCH].reshape(BATCH, 3, D_MODEL)
        cx = jnp.broadcast_to(out[l, BATCH].reshape(1, 3, D_MODEL), (BATCH, 3, D_MODEL))
        tab = jnp.stack([cx, lat], axis=1)
        tabs.append(jnp.pad(tab, ((0, 0), (0, 0), (0, SUBLANES - 3), (0, 0))))
    return tabs


def _modulated_norm(xt, nw, mod_ref, b=0):
    ms = jnp.mean(xt * xt, axis=-1, keepdims=True)
    y = xt * lax.rsqrt(ms + NORM_EPS) * nw
    return y * (1.0 + mod_ref[b, 0, 1:2, :]) + mod_ref[b, 0, 0:1, :]


def _lane_block_transpose(vs):
    n = len(vs)
    width = LANES // n
    blk = lax.broadcasted_iota(jnp.int32, vs[0].shape, 1) // width
    x = list(vs)
    d = n // 2
    while d >= 1:
        clear = (blk & d) == 0
        y = list(x)
        for i in range(n):
            if i & d == 0:
                a, b = x[i], x[i + d]
                y[i] = jnp.where(clear, a, pltpu.roll(b, width * d, 1))
                y[i + d] = jnp.where(clear, pltpu.roll(a, LANES - width * d, 1), b)
        x = y
        d //= 2
    return x


def _chunk_slot(b, cc):
    return (cc // 2) * (2 * BATCH) + 2 * b + (cc % 2)


def _proj0_kernel(x_ref, c_ref, mod_ref, nw_ref, cos_ref, sin_ref, w_ref,
                  q_ref, z_ref, kbd_ref, vt_ref):
    is_ctx = pl.program_id(0) == 0
    h = jnp.concatenate(
        [_modulated_norm(jnp.where(is_ctx, c_ref[b], x_ref[b]), nw_ref[...], mod_ref, b).astype(BF16)
         for b in range(BATCH)], axis=0)
    cos = cos_ref[...]
    sin = sin_ref[...]
    lane = lax.broadcasted_iota(jnp.int32, (ROW_TILE, LANES), 1)
    first_half = (lane % HEAD_DIM) < (HEAD_DIM // 2)
    low = lax.broadcasted_iota(jnp.int32, (BLOCK, LANES), 1) < HEAD_DIM

    def rope(t):
        partner = jnp.where(first_half, pltpu.roll(t, LANES - HEAD_DIM // 2, 1),
                            pltpu.roll(t, HEAD_DIM // 2, 1))
        return t * cos + partner * sin

    q = jnp.dot(h, w_ref[:, :ATTN_WIDTH], preferred_element_type=F32)
    k = jnp.dot(h, w_ref[:, ATTN_WIDTH:ATTN_WIDTH + KV_WIDTH], preferred_element_type=F32)
    v = jnp.dot(h, w_ref[:, ATTN_WIDTH + KV_WIDTH:ATTN_WIDTH + 2 * KV_WIDTH],
                preferred_element_type=F32)
    z = jnp.dot(h, w_ref[:, ATTN_WIDTH + 2 * KV_WIDTH:], preferred_element_type=F32)
    for b in range(BATCH):
        tile = slice(ROW_TILE * b, ROW_TILE * (b + 1))
        z_ref[b] = z[tile].astype(BF16)
        for j in range(N_SLOTS):
            m, gi = divmod(j, SLOTS_PER_M)
            qj = (rope(q[tile, LANES * j:LANES * (j + 1)]) * Q_SCALE).astype(BF16)
            for blk in range(ROW_TILE // BLOCK):
                q_ref[b, blk, m, BLOCK * gi:BLOCK * (gi + 1), :] = qj[BLOCK * blk:BLOCK * (blk + 1)]
        for m in range(N_KV_PAIRS):
            sl = slice(LANES * m, LANES * (m + 1))
            kr = rope(k[tile, sl])
            vm = v[tile, sl]
            for blk in range(ROW_TILE // BLOCK):
                rows = slice(BLOCK * blk, BLOCK * (blk + 1))
                kbd_ref[b, blk, m, :BLOCK, :] = jnp.where(low, kr[rows], 0.0).astype(BF16)
                kbd_ref[b, blk, m, BLOCK:, :] = jnp.where(low, 0.0, kr[rows]).astype(BF16)
                vt_ref[b, blk, LANES * m:LANES * (m + 1), :] = vm[rows].T.astype(BF16)


def _proj0(x, ctx, mod, norm_w, cos_tab, sin_tab, w_in):
    n_col = w_in.shape[1]
    blocks_per_tile = ROW_TILE // BLOCK
    kv_shape = jax.ShapeDtypeStruct((BATCH, N_BLOCKS, N_KV_PAIRS, 2 * BLOCK, LANES), BF16)
    kv_spec = pl.BlockSpec((BATCH, blocks_per_tile, N_KV_PAIRS, 2 * BLOCK, LANES),
                           lambda i: (0, i, 0, 0, 0))
    row_spec = pl.BlockSpec((BATCH, ROW_TILE, D_MODEL), lambda i: (0, i, 0))
    once = pl.Buffered(1)
    return pl.pallas_call(
        _proj0_kernel,
        grid=(N_ROW_TILES,),
        in_specs=[
            pl.BlockSpec((BATCH, ROW_TILE, D_MODEL), lambda i: (0, jnp.maximum(i - 1, 0), 0)),
            pl.BlockSpec((BATCH, ROW_TILE, D_MODEL), lambda i: (0, 0, 0), pipeline_mode=once),
            pl.BlockSpec((BATCH, 1, SUBLANES, D_MODEL), lambda i: (0, jnp.minimum(i, 1), 0, 0)),
            pl.BlockSpec((1, D_MODEL), lambda i: (0, 0)),
            pl.BlockSpec((ROW_TILE, LANES), lambda i: (i, 0)),
            pl.BlockSpec((ROW_TILE, LANES), lambda i: (i, 0)),
            pl.BlockSpec((D_MODEL, n_col), lambda i: (0, 0), pipeline_mode=once),
        ],
        out_specs=[
            pl.BlockSpec((BATCH, blocks_per_tile, N_KV_PAIRS, SLOTS_PER_M * BLOCK, LANES),
                         lambda i: (0, i, 0, 0, 0)),
            row_spec, kv_spec,
            pl.BlockSpec((BATCH, blocks_per_tile, KV_WIDTH, BLOCK), lambda i: (0, i, 0, 0))],
        out_shape=[
            jax.ShapeDtypeStruct((BATCH, N_BLOCKS, N_KV_PAIRS, SLOTS_PER_M * BLOCK, LANES), BF16),
            jax.ShapeDtypeStruct((BATCH, TOTAL, ATTN_WIDTH), BF16),
            kv_shape,
            jax.ShapeDtypeStruct((BATCH, N_BLOCKS, KV_WIDTH, BLOCK), BF16),
        ],
        compiler_params=_params(("arbitrary",), 56),
        name="attn_projection",
    )(x, ctx, mod, norm_w.reshape(1, D_MODEL), cos_tab, sin_tab, w_in)


def _attn_kernel(q_ref, z_ref, k0_ref, k1_ref, k2_ref, k3_ref, kx_ref,
                 v0_ref, v1_ref, v2_ref, v3_ref, vx_ref,
                 sink_ref, x_ref, c_ref, mod_ref, wo_ref, eye_ref, tri_ref, o_ref):
    step = pl.program_id(1)
    is_lat = step >= N_CTX_BLOCKS // ATTN_QBLOCKS
    n_first = ATTN_QBLOCKS * step - N_CTX_BLOCKS
    q_rows = UNIT_SLOTS * BLOCK
    blocked = tri_ref[2]
    bias = []
    for qb in range(ATTN_QBLOCKS):
        n = n_first + qb
        bias.append([jnp.where(jnp.logical_and(is_lat, n >= 1), tri_ref[0], blocked),
                     jnp.where(is_lat, tri_ref[3], blocked),
                     jnp.where(jnp.logical_and(is_lat, n <= SEQ // BLOCK - 2), tri_ref[1], blocked),
                     None, None])
    k_win = (k0_ref, k1_ref, k2_ref, k3_ref)
    v_win = (v0_ref, v1_ref, v2_ref, v3_ref)

    def kpiece(qb, p, m):
        if p < 3:
            return k_win[qb + p][0, 0, m], v_win[qb + p][0, 0]
        return kx_ref[0, p - 3, m], vx_ref[0, p - 3]

    n_piece = 3 + N_CTX_BLOCKS
    nt = (((1,), (1,)), ((), ()))

    half_rows = slice(0, BLOCK), slice(BLOCK, 2 * BLOCK)

    def scores(qb, m, h):
        qu = q_ref[0, qb, m, q_rows * h:q_rows * (h + 1), :]
        qu_masked = jnp.concatenate([qu, eye_ref[...]], axis=1)
        s_list = []
        for p in range(n_piece):
            kbd, _ = kpiece(qb, p, m)
            if bias[qb][p] is None:
                s = lax.dot_general(kbd, qu, nt, preferred_element_type=F32)
            else:
                s = lax.dot_general(jnp.concatenate([kbd, bias[qb][p]], axis=1), qu_masked, nt,
                                    preferred_element_type=F32)
            s_list.append(s)
        return s_list

    def finish(qb, m, h, s_list):
        slot0 = SLOTS_PER_M * m + UNIT_SLOTS * h
        halves = []
        for hs in range(2):
            sink = jnp.concatenate(
                [sink_ref[2 * (slot0 + gi) + hs:2 * (slot0 + gi) + hs + 1, :]
                 for gi in range(UNIT_SLOTS)], axis=1)
            mx = sink
            for s in s_list:
                mx = jnp.maximum(mx, jnp.max(s[half_rows[hs]], axis=0, keepdims=True))
            probs = [jnp.exp2(s[half_rows[hs]] - mx) for s in s_list]
            denom = jnp.exp2(sink - mx)
            for e in probs:
                denom = denom + jnp.sum(e, axis=0, keepdims=True)
            probs = [e.astype(BF16) for e in probs]
            kv_head = 2 * m + hs
            acc = None
            for p0 in range(0, n_piece, 2):
                group = list(range(p0, min(p0 + 2, n_piece)))
                vt = jnp.concatenate(
                    [kpiece(qb, p, m)[1][HEAD_DIM * kv_head:HEAD_DIM * (kv_head + 1), :]
                     for p in group], axis=1)
                pt = jnp.concatenate([probs[p] for p in group], axis=0)
                part = jnp.dot(vt, pt, preferred_element_type=F32)
                acc = part if acc is None else acc + part
            halves.append(acc * (1.0 / denom))
        o_t = jnp.concatenate(halves, axis=0)
        outs = []
        for gi in range(UNIT_SLOTS):
            j = slot0 + gi
            o = o_t[:, BLOCK * gi:BLOCK * (gi + 1)].T
            zj = z_ref[0, BLOCK * qb:BLOCK * (qb + 1), LANES * j:LANES * (j + 1)].astype(F32)
            outs.append((o * (zj * jax.nn.sigmoid(zj))).astype(BF16))
        return jnp.concatenate(outs, axis=1)

    units = [(qb, m, h) for m in range(N_KV_PAIRS) for h in range(SLOTS_PER_M // UNIT_SLOTS)
             for qb in range(ATTN_QBLOCKS)]
    y = None
    gated = []
    pending = scores(*units[0])
    for u, (qb, m, h) in enumerate(units):
        nxt = scores(*units[u + 1]) if u + 1 < len(units) else None
        gated.append(finish(qb, m, h, pending))
        pending = nxt
        if qb == ATTN_QBLOCKS - 1:
            slot0 = SLOTS_PER_M * m + UNIT_SLOTS * h
            part = jnp.dot(jnp.concatenate(gated, axis=0),
                           wo_ref[LANES * slot0:LANES * (slot0 + UNIT_SLOTS), :],
                           preferred_element_type=F32)
            y = part if y is None else y + part
            gated = []
    resid = jnp.where(is_lat, x_ref[0], c_ref[0])
    o_ref[0] = resid + mod_ref[0, 0, 2:3, :] * y


def _attention(q, z, kbd, vt, sink_tab, x, ctx, mod, w_out):
    last = N_BLOCKS - 1
    n_m = KV_WIDTH // LANES
    nq = ATTN_QBLOCKS
    ctx_steps = N_CTX_BLOCKS // nq
    row_spec = pl.BlockSpec((1, nq * BLOCK, ATTN_WIDTH), lambda b, i: (b, i, 0))

    def kv_spec(off):
        return pl.BlockSpec((1, 1, n_m, 2 * BLOCK, LANES),
                            lambda b, i: (b, jnp.clip(nq * i + off, 0, last), 0, 0, 0))

    ctx_kv_spec = pl.BlockSpec((1, N_CTX_BLOCKS, n_m, 2 * BLOCK, LANES),
                               lambda b, i: (b, 0, 0, 0, 0))

    def vt_spec(off):
        return pl.BlockSpec((1, 1, KV_WIDTH, BLOCK),
                            lambda b, i: (b, jnp.clip(nq * i + off, 0, last), 0, 0))

    ctx_vt_spec = pl.BlockSpec((1, N_CTX_BLOCKS, KV_WIDTH, BLOCK), lambda b, i: (b, 0, 0, 0))
    off = np.arange(BLOCK)
    eye = np.tile(np.eye(BLOCK, dtype=np.float32), (UNIT_SLOTS, 1))
    key_ge = np.where(off[:, None] >= off[None, :], 0.0, NEG_INF)
    key_le = np.where(off[:, None] <= off[None, :], 0.0, NEG_INF)
    tri = np.stack([np.tile(t, (2, 1)) for t in
                    (key_ge, key_le, np.full((BLOCK, BLOCK), NEG_INF), np.zeros((BLOCK, BLOCK)))])
    consts = [jnp.asarray(a, BF16) for a in (eye, tri)]
    return pl.pallas_call(
        _attn_kernel,
        grid=(BATCH, N_BLOCKS // nq),
        in_specs=[
            pl.BlockSpec((1, nq, n_m, SLOTS_PER_M * BLOCK, LANES), lambda b, i: (b, i, 0, 0, 0)),
            row_spec,
            kv_spec(-1), kv_spec(0), kv_spec(1), kv_spec(2), ctx_kv_spec,
            vt_spec(-1), vt_spec(0), vt_spec(1), vt_spec(2), ctx_vt_spec,
            pl.BlockSpec((2 * N_SLOTS, LANES), lambda b, i: (0, 0)),
            pl.BlockSpec((1, nq * BLOCK, D_MODEL),
                         lambda b, i: (b, jnp.maximum(i - ctx_steps, 0), 0)),
            pl.BlockSpec((1, nq * BLOCK, D_MODEL),
                         lambda b, i: (b, jnp.minimum(i, ctx_steps - 1), 0)),
            pl.BlockSpec((1, 1, SUBLANES, D_MODEL),
                         lambda b, i: (b, jnp.minimum(i // ctx_steps, 1), 0, 0)),
            pl.BlockSpec((ATTN_WIDTH, D_MODEL), lambda b, i: (0, 0)),
        ] + [pl.BlockSpec(a.shape, lambda b, i, nd=a.ndim: (0,) * nd) for a in consts],
        out_specs=pl.BlockSpec((1, nq * BLOCK, D_MODEL), lambda b, i: (b, i, 0)),
        out_shape=jax.ShapeDtypeStruct((BATCH, TOTAL, D_MODEL), F32),
        compiler_params=_params(("arbitrary", "arbitrary"), 48),
        name="window_attention",
    )(q, z, kbd, kbd, kbd, kbd, kbd, vt, vt, vt, vt, vt, sink_tab, x, ctx, mod, w_out,
      *consts)


def _proj1_kernel(x_ref, mod_ref, nw_ref, w_ref, u_ref, z_ref, scr_ref):
    h = jnp.concatenate(
        [_modulated_norm(x_ref[b], nw_ref[...], mod_ref, b).astype(BF16) for b in range(BATCH)],
        axis=0)
    u = jnp.dot(h, w_ref[:, :D_MODEL], preferred_element_type=F32)
    z = jnp.dot(h, w_ref[:, D_MODEL:], preferred_element_type=F32)
    for b in range(BATCH):
        z_ref[b] = z[ROW_TILE * b:ROW_TILE * (b + 1)].astype(BF16)
    n_slab = D_MODEL // LANES
    for b in range(BATCH):
        for cc in range(CHUNKS_PER_TILE):
            r0 = ROW_TILE * b + CHUNK * cc
            q0 = SLOT_PITCH * _chunk_slot(b, cc)
            for k in range(n_slab):
                scr_ref[k, q0:q0 + CHUNK, :] = u[r0:r0 + CHUNK, LANES * k:LANES * (k + 1)]
    groups_per_slab = LANES // SSM_GROUP
    for k in range(n_slab):
        for pp in range(PAIRS_PER_TILE // 2):
            parts = []
            for p in (2 * pp, 2 * pp + 1):
                rows = [scr_ref[k, pl.ds(SLOT_PITCH * SUBLANES * p + s, SUBLANES,
                                         stride=SLOT_PITCH), :] for s in range(CHUNK)]
                parts.append([_lane_block_transpose(rows[SUBLANES * m2:SUBLANES * (m2 + 1)])
                              for m2 in range(CHUNK // SUBLANES)])
            for m2 in range(CHUNK // SUBLANES):
                for gl in range(groups_per_slab):
                    val = jnp.concatenate([parts[0][m2][gl], parts[1][m2][gl]], axis=0)
                    u_ref[groups_per_slab * k + gl, 2 * SUBLANES * pp:2 * SUBLANES * (pp + 1),
                          LANES * m2:LANES * (m2 + 1)] = val.astype(BF16)


def _proj1(xc, mod, norm_w, w_in):
    row_spec = pl.BlockSpec((BATCH, ROW_TILE, D_MODEL), lambda i: (0, i, 0))
    tile_rows = PAIRS_PER_TILE * SUBLANES
    return pl.pallas_call(
        _proj1_kernel,
        grid=(N_ROW_TILES,),
        in_specs=[
            row_spec,
            pl.BlockSpec((BATCH, 1, SUBLANES, D_MODEL), lambda i: (0, jnp.minimum(i, 1), 0, 0)),
            pl.BlockSpec((1, D_MODEL), lambda i: (0, 0)),
            pl.BlockSpec((D_MODEL, 2 * D_MODEL), lambda i: (0, 0)),
        ],
        out_specs=[pl.BlockSpec((SSM_GROUPS, tile_rows, CHUNK_W), lambda i: (0, i, 0)), row_spec],
        out_shape=[jax.ShapeDtypeStruct((SSM_GROUPS, SCAN_ROWS, CHUNK_W), BF16),
                   jax.ShapeDtypeStruct((BATCH, TOTAL, D_MODEL), BF16)],
        scratch_shapes=[pltpu.VMEM((D_MODEL // LANES, BATCH * CHUNKS_PER_TILE * SLOT_PITCH, LANES),
                                   F32)],
        compiler_params=_params(("arbitrary",), 56),
        name="ssm_projection",
    )(xc, mod, norm_w.reshape(1, D_MODEL), w_in)


def _s5_kernel(u_ref, m_ref, ws_ref, wy_ref, cst_ref, y_ref, s4_ref, xp_ref):
    even = lax.broadcasted_iota(jnp.int32, (SUBLANES, STATE_W), 0) % 2 == 0
    down = 1
    up = SUBLANES - 1
    fwd = slice(0, STATE_W)
    bwd = slice(STATE_W, 2 * STATE_W)
    fwd_sw = slice(2 * STATE_W, 3 * STATE_W)
    bwd_sw = slice(3 * STATE_W, 4 * STATE_W)
    for g in range(GROUP_BATCH):
        s = jnp.dot(u_ref[g], ws_ref[g], preferred_element_type=F32)
        s = s.reshape(N_SCAN_BLOCKS, SUBLANES, 4 * STATE_W)
        q1f, q2f, q1b, q2b = cst_ref[g, 2], cst_ref[g, 3], cst_ref[g, 6], cst_ref[g, 7]
        zf, zb, zfs, zbs = s[..., fwd], s[..., bwd], s[..., fwd_sw], s[..., bwd_sw]
        rzf, rzfs = pltpu.roll(zf, down, 1), pltpu.roll(zfs, down, 1)
        rzb, rzbs = pltpu.roll(zb, up, 1), pltpu.roll(zbs, up, 1)
        t = jnp.concatenate([zf + q1f * rzf + q2f * rzfs, zb + q1b * rzb + q2b * rzbs,
                             zfs + q1f * rzfs - q2f * rzf, zbs + q1b * rzbs - q2b * rzb], axis=-1)
        s4_ref[g] = t.reshape(SCAN_ROWS, 4 * STATE_W)

    def step(j, carry):
        jb = jnp.where(j < N_CTX_SCAN_BLOCKS, N_CTX_SCAN_BLOCKS - 1 - j,
                       N_SCAN_BLOCKS - 1 + N_CTX_SCAN_BLOCKS - j)
        rf = pl.ds(pl.multiple_of(j * SUBLANES, SUBLANES), SUBLANES)
        rb = pl.ds(pl.multiple_of(jb * SUBLANES, SUBLANES), SUBLANES)
        new = []
        for g in range(GROUP_BATCH):
            cf, cfs, cb, cbs = carry[4 * g:4 * g + 4]
            p1f, p2f, p1b, p2b = cst_ref[g, 0], cst_ref[g, 1], cst_ref[g, 4], cst_ref[g, 5]
            xf = p1f * cf + p2f * cfs + s4_ref[g, rf, fwd]
            xfs = p1f * cfs - p2f * cf + s4_ref[g, rf, fwd_sw]
            xp_ref[g, rf, fwd] = jnp.where(even, cf, pltpu.roll(xf, down, 0))
            new += [jnp.where(even, pltpu.roll(xf, up, 0), xf),
                    jnp.where(even, pltpu.roll(xfs, up, 0), xfs)]
            xb = p1b * cb + p2b * cbs + s4_ref[g, rb, bwd]
            xbs = p1b * cbs - p2b * cb + s4_ref[g, rb, bwd_sw]
            xp_ref[g, rb, bwd] = jnp.where(even, pltpu.roll(xb, up, 0), cb)
            new += [jnp.where(even, xb, pltpu.roll(xb, down, 0)),
                    jnp.where(even, xbs, pltpu.roll(xbs, down, 0))]
        return tuple(new)

    zero = jnp.zeros((SUBLANES, STATE_W), F32)
    lax.fori_loop(0, N_SCAN_BLOCKS, step, (zero,) * (4 * GROUP_BATCH))
    for g in range(GROUP_BATCH):
        y_ref[g] = (
            jnp.dot(u_ref[g, CTX_SCAN_ROWS:, :], m_ref[g], preferred_element_type=F32)
            + lax.dot_general(xp_ref[g, CTX_SCAN_ROWS:, :].astype(BF16), wy_ref[g],
                              (((1,), (1,)), ((), ())), preferred_element_type=F32))


def _s5_core(u_g, m_mat, ws_mat, wy_mat, consts):
    lat_rows = SCAN_ROWS - CTX_SCAN_ROWS

    def gspec(*tail):
        return pl.BlockSpec((GROUP_BATCH,) + tail, lambda i: (i,) + (0,) * len(tail))

    return pl.pallas_call(
        _s5_kernel,
        grid=(SSM_GROUPS // GROUP_BATCH,),
        in_specs=[
            gspec(SCAN_ROWS, CHUNK_W),
            gspec(CHUNK_W, CHUNK_W),
            gspec(CHUNK_W, 4 * STATE_W),
            gspec(CHUNK_W, 2 * STATE_W),
            gspec(8, SUBLANES, STATE_W),
        ],
        out_specs=gspec(lat_rows, CHUNK_W),
        out_shape=jax.ShapeDtypeStruct((SSM_GROUPS, lat_rows, CHUNK_W), F32),
        scratch_shapes=[
            pltpu.VMEM((GROUP_BATCH, SCAN_ROWS, 4 * STATE_W), F32),
            pltpu.VMEM((GROUP_BATCH, SCAN_ROWS, 2 * STATE_W), F32),
        ],
        compiler_params=_params(("arbitrary",), 48),
        name="s5_scan",
    )(u_g, m_mat, ws_mat, wy_mat, consts)


def _out1_kernel(y_ref, z_ref, x_ref, mod_ref, wg_ref, wo_ref, fnw_ref, o_ref, scr_ref):
    n_slab = D_MODEL // LANES
    groups_per_slab = LANES // SSM_GROUP
    pairs_per_phase = PAIRS_PER_TILE // OUT_PHASES
    rows_per_phase = ROW_TILE // OUT_PHASES

    def relayout(ph):
        for k in range(n_slab):
            for p in range(pairs_per_phase * ph, pairs_per_phase * (ph + 1)):
                for m2 in range(CHUNK // SUBLANES):
                    vals = [y_ref[groups_per_slab * k + gl, SUBLANES * p:SUBLANES * (p + 1),
                                  LANES * m2:LANES * (m2 + 1)] for gl in range(groups_per_slab)]
                    steps = _lane_block_transpose(vals)
                    for s2 in range(SUBLANES):
                        t_idx = SUBLANES * m2 + s2
                        scr_ref[k, pl.ds(SLOT_PITCH * SUBLANES * p + t_idx, SUBLANES,
                                         stride=SLOT_PITCH), :] = steps[s2]

    def compute(ph):
        chunks = range(2 * pairs_per_phase * ph, 2 * pairs_per_phase * (ph + 1))
        y = jnp.concatenate(
            [jnp.concatenate(
                [scr_ref[k, SLOT_PITCH * _chunk_slot(b, cc):SLOT_PITCH * _chunk_slot(b, cc) + CHUNK, :]
                 for k in range(n_slab)], axis=1)
             for b in range(BATCH) for cc in chunks], axis=0)
        g = (0.5 * y * (1.0 + lax.erf(y * (2.0 ** -0.5)))).astype(BF16)
        t = jnp.dot(g, wg_ref[...], preferred_element_type=F32)
        rows = slice(rows_per_phase * ph, rows_per_phase * (ph + 1))
        z = jnp.concatenate([z_ref[b, rows, :] for b in range(BATCH)], axis=0).astype(F32)
        r = (t[:, :D_MODEL] * jax.nn.sigmoid(t[:, D_MODEL:]) * (z * jax.nn.sigmoid(z))).astype(BF16)
        o = jnp.dot(r, wo_ref[...], preferred_element_type=F32)
        for b in range(BATCH):
            x2 = (x_ref[b, rows, :]
                  + mod_ref[b, 0, 2:3, :] * o[rows_per_phase * b:rows_per_phase * (b + 1)])
            ms = jnp.mean(x2 * x2, axis=-1, keepdims=True)
            o_ref[b, rows, :] = x2 * lax.rsqrt(ms + NORM_EPS) * fnw_ref[...]

    relayout(0)
    for ph in range(OUT_PHASES):
        if ph + 1 < OUT_PHASES:
            relayout(ph + 1)
        compute(ph)


def _out1(y_g, z, xc, mod, w_glu, w_out, final_norm_w):
    ctx_tiles = CTX_LEN // ROW_TILE
    tile_rows = PAIRS_PER_TILE * SUBLANES
    lat_spec = pl.BlockSpec((BATCH, ROW_TILE, D_MODEL), lambda i: (0, i, 0))
    all_spec = pl.BlockSpec((BATCH, ROW_TILE, D_MODEL), lambda i: (0, i + ctx_tiles, 0))
    return pl.pallas_call(
        _out1_kernel,
        grid=(SEQ // ROW_TILE,),
        in_specs=[
            pl.BlockSpec((SSM_GROUPS, tile_rows, CHUNK_W), lambda i: (0, i, 0)),
            all_spec, all_spec,
            pl.BlockSpec((BATCH, 1, SUBLANES, D_MODEL), lambda i: (0, 1, 0, 0)),
            pl.BlockSpec((D_MODEL, 2 * D_MODEL), lambda i: (0, 0)),
            pl.BlockSpec((D_MODEL, D_MODEL), lambda i: (0, 0)),
            pl.BlockSpec((1, D_MODEL), lambda i: (0, 0)),
        ],
        out_specs=lat_spec,
        out_shape=jax.ShapeDtypeStruct((BATCH, SEQ, D_MODEL), F32),
        scratch_shapes=[pltpu.VMEM((D_MODEL // LANES, BATCH * CHUNKS_PER_TILE * SLOT_PITCH, LANES),
                                   F32)],
        compiler_params=_params(("arbitrary",), 56),
        name="ssm_output",
    )(y_g, z, xc, mod, w_glu, w_out, final_norm_w.reshape(1, D_MODEL))


def _slot_order(t, lead):
    n_m = KV_WIDTH // LANES
    gq = N_HEADS // N_KV_HEADS
    shape = t.shape
    t = t.reshape(shape[:lead] + (n_m, 2, gq) + shape[lead + 1:])
    perm = tuple(range(lead)) + (lead, lead + 2, lead + 1) + tuple(range(lead + 3, t.ndim))
    return jnp.transpose(t, perm)


def _rope_order(t):
    shape = t.shape
    t = t.reshape(shape[:-1] + (2, 2, ROPE_FREQS))
    return jnp.swapaxes(t, -3, -2).reshape(shape)


def _attn_weights(w_in, w_out, sink):
    wq = w_in[:, :ATTN_WIDTH].reshape(D_MODEL, N_HEADS, HEAD_DIM)
    wq = _slot_order(_rope_order(wq), 1).reshape(D_MODEL, ATTN_WIDTH)
    wk = w_in[:, ATTN_WIDTH:ATTN_WIDTH + KV_WIDTH].reshape(D_MODEL, N_KV_HEADS, HEAD_DIM)
    wk = _rope_order(wk).reshape(D_MODEL, KV_WIDTH)
    wv = w_in[:, ATTN_WIDTH + KV_WIDTH:ATTN_WIDTH + 2 * KV_WIDTH]
    wz = w_in[:, ATTN_WIDTH + 2 * KV_WIDTH:].reshape(D_MODEL, N_HEADS, HEAD_DIM)
    wz = _slot_order(wz, 1).reshape(D_MODEL, ATTN_WIDTH)
    w_in_p = jnp.concatenate([wq, wk, wv, wz], axis=1).astype(BF16)
    wo = _slot_order(w_out.reshape(N_HEADS, HEAD_DIM, D_MODEL), 0).reshape(ATTN_WIDTH, D_MODEL)
    sink_p = _slot_order(sink.astype(F32).reshape(N_HEADS), 0).reshape(2 * N_SLOTS)
    sink_tab = jnp.broadcast_to((sink_p * LOG2E)[:, None], (2 * N_SLOTS, LANES))
    return w_in_p, wo.astype(BF16), sink_tab


def _rope_tables():
    inv = ROPE_BASE ** (-np.arange(ROPE_FREQS, dtype=np.float64) / ROPE_FREQS)
    pos = np.arange(SEQ)
    row = (pos // GRID_W)[:, None] * inv
    col = (pos % GRID_W)[:, None] * inv
    w = np.arange(LANES) % HEAD_DIM
    half_w = HEAD_DIM // 2
    half, axis, f = w // half_w, (w % half_w) // ROPE_FREQS, w % ROPE_FREQS
    ang = np.where((axis == 0)[None, :], row[:, f], col[:, f])
    sign = np.where(half == 0, -1.0, 1.0)[None, :]
    cos = np.concatenate([np.ones((CTX_LEN, LANES)), np.cos(ang)], axis=0)
    sin = np.concatenate([np.zeros((CTX_LEN, LANES)), np.sin(ang) * sign], axis=0)
    return jnp.asarray(cos, F32), jnp.asarray(sin, F32)


def _s5_operators(lam_re, lam_im, log_dt, b_re, b_im, c_re, c_im, d_skip):
    t_len = CHUNK
    n_pow = 2 * t_len + 1
    lr, li = lam_re.astype(F32), lam_im.astype(F32)
    dt = jnp.exp(log_dt.astype(F32))[..., None]
    mag = jnp.exp(lr * dt)
    sq = [(mag * jnp.cos(li * dt), mag * jnp.sin(li * dt))]
    while 2 ** len(sq) < n_pow:
        r, i = sq[-1]
        sq.append((r * r - i * i, 2.0 * r * i))
    ks = np.arange(n_pow)
    pr = jnp.ones(lr.shape + (n_pow,), F32)
    pi = jnp.zeros(lr.shape + (n_pow,), F32)
    for bit, (r, i) in enumerate(sq):
        on = jnp.asarray((ks >> bit) & 1 == 1)
        fr = jnp.where(on, r[..., None], 1.0)
        fi = jnp.where(on, i[..., None], 0.0)
        pr, pi = pr * fr - pi * fi, pr * fi + pi * fr
    ar1, ai1 = sq[0][0] - 1.0, sq[0][1]
    den = lr * lr + li * li
    gr, gi = (ar1 * lr + ai1 * li) / den, (ai1 * lr - ar1 * li) / den
    br_, bi_ = b_re.astype(F32), b_im.astype(F32)
    bbr = gr[..., None] * br_ - gi[..., None] * bi_
    bbi = gr[..., None] * bi_ + gi[..., None] * br_
    pad_k = POW_ROWS - n_pow
    prt, pit = jnp.swapaxes(pr, 2, 3), jnp.swapaxes(pi, 2, 3)
    pw = jnp.pad(jnp.concatenate([prt, prt, pit, pit], axis=-1),
                 ((0, 0), (0, 0), (0, pad_k), (0, 0)))
    brt, bit = jnp.swapaxes(bbr, 2, 3), jnp.swapaxes(bbi, 2, 3)
    bb = jnp.concatenate([brt, bit, -bit, brt, bit, brt, brt, -bit], axis=-1)
    cr, ci = c_re.astype(F32), c_im.astype(F32)
    cm = jnp.stack([jnp.concatenate([cr, -ci], axis=-1),
                    jnp.concatenate([-ci, -cr], axis=-1)], axis=2)
    dv = jnp.tile(d_skip.astype(F32).reshape(SSM_GROUPS, 1, SSM_GROUP), (1, 1, t_len))
    return _s5_operator_call(pw, bb, cm, dv)


def _operator_constants():
    t_len = CHUNK
    s_of_row = np.arange(CHUNK_W) // SSM_GROUP
    k_ar = np.arange(POW_ROWS)
    oh_f = (k_ar[None, :] == (t_len - 1 - s_of_row)[:, None]).astype(np.float32)
    oh_b = (k_ar[None, :] == s_of_row[:, None]).astype(np.float32)
    t_of_lane = np.arange(CHUNK_W) // SSM_GROUP
    expo = [s_of_row + 1, t_len - s_of_row, s_of_row, t_len - 1 - s_of_row]
    sel = np.stack([(k_ar[None, :] == e[:, None]) for e in expo]).astype(np.float32)
    h_of_lane = np.arange(CHUNK_W) % SSM_GROUP
    dmask = ((s_of_row[:, None] == t_of_lane[None, :])
             & ((np.arange(CHUNK_W) % SSM_GROUP)[:, None] == h_of_lane[None, :])).astype(np.float32)
    return oh_f, oh_b, sel, dmask


def _s5_op_kernel(*refs):
    for gg in range(OP_GROUP_BATCH):
        _s5_op_group(gg, *refs)


def _s5_op_group(gg, pw_ref, bb_ref, cm_ref, dv_ref, ohf_ref, ohb_ref, sel_ref, dmask_ref,
                 m_ref, ws_ref, wyt_ref, cst_ref):
    t_len = CHUNK

    def split(a):
        hi = a.astype(BF16)
        return hi, (a - hi.astype(F32)).astype(BF16)

    def pick_rows(onehot2, table):
        hi, lo = split(table)
        return jnp.dot(onehot2, jnp.concatenate([hi, lo], axis=0), preferred_element_type=F32)

    def tile_rows(a):
        return jnp.concatenate([a] * t_len, axis=0)

    main, swapped = [], []
    for d, oh_ref in ((0, ohf_ref), (1, ohb_ref)):
        pp = pick_rows(oh_ref[...], pw_ref[d, gg])
        p_re, p_im = pp[:, :STATE_W], pp[:, STATE_W:]
        b0, b1, b2, b3 = [tile_rows(bb_ref[d, gg, :, STATE_W * i:STATE_W * (i + 1)])
                          for i in range(4)]
        main.append(p_re * b0 + p_im * b1)
        swapped.append(p_re * b2 + p_im * b3)
    ws_ref[gg] = jnp.concatenate(main + swapped, axis=1).astype(BF16)

    c_tiled = [[tile_rows(cm_ref[d, gg, i]) for i in range(2)] for d in range(2)]

    def block_t(d, pat):
        pp = pick_rows(sel_ref[pat], pw_ref[d, gg])
        return pp[:, :STATE_W] * c_tiled[d][0] + pp[:, STATE_W:] * c_tiled[d][1]

    wyt_ref[gg] = jnp.concatenate([block_t(0, 0), block_t(1, 1)], axis=1).astype(BF16)

    nt = (((1,), (1,)), ((), ()))
    kt_f = lax.dot_general(bb_ref[0, gg, :, :STATE_W], block_t(0, 2), nt,
                           preferred_element_type=F32, precision=lax.Precision.HIGHEST)
    kt_b = lax.dot_general(bb_ref[1, gg, :, :STATE_W], block_t(1, 3), nt,
                           preferred_element_type=F32, precision=lax.Precision.HIGHEST)
    lane = lax.broadcasted_iota(jnp.int32, (SSM_GROUP, CHUNK_W), 1)
    skip = dv_ref[gg]
    for s in range(t_len):
        fwd = kt_f if s == 0 else pltpu.roll(kt_f, SSM_GROUP * s, 1)
        back = t_len - 1 - s
        bwd = kt_b if back == 0 else pltpu.roll(kt_b, CHUNK_W - SSM_GROUP * back, 1)
        rows = (jnp.where(lane >= SSM_GROUP * s, fwd, 0.0)
                + jnp.where(lane < SSM_GROUP * (s + 1), bwd, 0.0)
                + dmask_ref[SSM_GROUP * s:SSM_GROUP * (s + 1), :] * skip)
        m_ref[gg, SSM_GROUP * s:SSM_GROUP * (s + 1), :] = rows.astype(BF16)

    even = lax.broadcasted_iota(jnp.int32, (SUBLANES, STATE_W), 0) % 2 == 0
    sign = jnp.where(lax.broadcasted_iota(jnp.int32, (1, STATE_W), 1) < SSM_STATE, -1.0, 1.0)

    def w12(d, k):
        row = pw_ref[d, gg, k:k + 1, :]
        return row[:, :STATE_W], row[:, STATE_W:] * sign

    zero = (jnp.zeros((1, STATE_W), F32),) * 2
    pairs = [(w12(0, t_len), w12(0, 2 * t_len)), (zero, w12(0, t_len)),
             (w12(1, 2 * t_len), w12(1, t_len)), (w12(1, t_len), zero)]
    idx = 0
    for top, bot in pairs:
        for part in range(2):
            cst_ref[gg, idx] = jnp.where(even, jnp.broadcast_to(top[part], (SUBLANES, STATE_W)),
                                         jnp.broadcast_to(bot[part], (SUBLANES, STATE_W)))
            idx += 1


def _s5_operator_call(pw, bb, cm, dv):
    oh_f, oh_b, sel, dmask = _operator_constants()
    consts = [jnp.asarray(np.concatenate([a, a], axis=-1), BF16) for a in (oh_f, oh_b, sel)]
    consts.append(jnp.asarray(dmask))

    def per_group(*tail):
        n = len(tail)
        return pl.BlockSpec((2, OP_GROUP_BATCH) + tail, lambda g: (0, g) + (0,) * n)

    def whole(a):
        return pl.BlockSpec(a.shape, lambda g: (0,) * a.ndim)

    def out(*tail):
        return pl.BlockSpec((OP_GROUP_BATCH,) + tail, lambda g: (g,) + (0,) * len(tail))

    return pl.pallas_call(
        _s5_op_kernel,
        grid=(SSM_GROUPS // OP_GROUP_BATCH,),
        in_specs=[per_group(POW_ROWS, 2 * STATE_W), per_group(SSM_GROUP, 4 * STATE_W),
                  per_group(2, SSM_GROUP, STATE_W),
                  pl.BlockSpec((OP_GROUP_BATCH, 1, CHUNK_W), lambda g: (g, 0, 0))]
                 + [whole(a) for a in consts],
        out_specs=[out(CHUNK_W, CHUNK_W), out(CHUNK_W, 4 * STATE_W), out(CHUNK_W, 2 * STATE_W),
                   out(8, SUBLANES, STATE_W)],
        out_shape=[jax.ShapeDtypeStruct((SSM_GROUPS, CHUNK_W, CHUNK_W), BF16),
                   jax.ShapeDtypeStruct((SSM_GROUPS, CHUNK_W, 4 * STATE_W), BF16),
                   jax.ShapeDtypeStruct((SSM_GROUPS, CHUNK_W, 2 * STATE_W), BF16),
                   jax.ShapeDtypeStruct((SSM_GROUPS, 8, SUBLANES, STATE_W), F32)],
        compiler_params=_params(("arbitrary",), 32),
        name="s5_operators",
    )(pw, bb, cm, dv, *consts)


def kernel(x, c, ctx, c_ctx, norm_w, w_ada, b_ada, attn_w_in, attn_sink, attn_w_out,
           ssm_w_in, ssm_lam_re, ssm_lam_im, ssm_log_dt, ssm_b_re, ssm_b_im, ssm_c_re, ssm_c_im,
           ssm_d, ssm_w_glu, ssm_w_out, final_norm_w):
    mod0, mod1 = _modulation(c, c_ctx, w_ada, b_ada)

    w_in0, w_out0, sink_tab = _attn_weights(attn_w_in[0], attn_w_out[0], attn_sink[0])
    cos_tab, sin_tab = _rope_tables()
    q, z0, kbd, vt = _proj0(x, ctx, mod0, norm_w[0], cos_tab, sin_tab, w_in0)
    xc1 = _attention(q, z0, kbd, vt, sink_tab, x, ctx, mod0, w_out0)

    u_g, z1 = _proj1(xc1, mod1, norm_w[1], ssm_w_in[0].astype(BF16))
    m_mat, ws_mat, wy_mat, consts = _s5_operators(
        ssm_lam_re[0], ssm_lam_im[0], ssm_log_dt[0], ssm_b_re[0], ssm_b_im[0],
        ssm_c_re[0], ssm_c_im[0], ssm_d[0])
    y_g = _s5_core(u_g, m_mat, ws_mat, wy_mat, consts)
    return _out1(y_g, z1, xc1, mod1, ssm_w_glu[0].astype(BF16), ssm_w_out[0].astype(BF16),
                 final_norm_w)
```

```python
import math

import jax
import jax.numpy as jnp
import numpy as np
from jax import lax
from jax.experimental import pallas as pl
from jax.experimental.pallas import tpu as pltpu

F32 = jnp.float32
BF16 = jnp.bfloat16

D_MODEL = 1024
BATCH = 4
SEQ = 4096
GRID_W = 64
CTX_LEN = 256
TOTAL = CTX_LEN + SEQ
HEAD_DIM = 64
N_HEADS = 16
N_KV_HEADS = 4
ATTN_WIDTH = N_HEADS * HEAD_DIM
KV_WIDTH = N_KV_HEADS * HEAD_DIM
BLOCK = 128
N_BLOCKS = TOTAL // BLOCK
N_CTX_BLOCKS = CTX_LEN // BLOCK
ROPE_BASE = 10000.0
ROPE_FREQS = HEAD_DIM // 4
SSM_GROUP = 16
SSM_GROUPS = D_MODEL // SSM_GROUP
SSM_STATE = 64
NORM_EPS = 1e-6
NEG_INF = -1e30

LANES = 128
SUBLANES = 8
N_SLOTS = ATTN_WIDTH // LANES
N_KV_PAIRS = KV_WIDTH // LANES
SLOTS_PER_M = N_SLOTS // N_KV_PAIRS
ATTN_QBLOCKS = 2
assert N_CTX_BLOCKS % ATTN_QBLOCKS == 0 and N_BLOCKS % ATTN_QBLOCKS == 0
assert ATTN_QBLOCKS == 2
UNIT_SLOTS = 4
LOG2E = math.log2(math.e)
Q_SCALE = HEAD_DIM ** -0.5 * LOG2E
ROW_TILE = 256
N_ROW_TILES = TOTAL // ROW_TILE
CHUNK = 16
N_CHUNKS = TOTAL // CHUNK
N_CTX_CHUNKS = CTX_LEN // CHUNK
CHUNK_W = CHUNK * SSM_GROUP
CHUNKS_PER_TILE = ROW_TILE // CHUNK
PAIRS_PER_TILE = CHUNKS_PER_TILE // 2
SLOT_PITCH = 17
OUT_PHASES = 4
assert SLOT_PITCH >= CHUNK and (BATCH * CHUNKS_PER_TILE * SLOT_PITCH) % SUBLANES == 0
STATE_W = 2 * SSM_STATE
SCAN_ROWS = N_CHUNKS * BATCH
CTX_SCAN_ROWS = N_CTX_CHUNKS * BATCH
N_SCAN_BLOCKS = SCAN_ROWS // SUBLANES
N_CTX_SCAN_BLOCKS = CTX_SCAN_ROWS // SUBLANES
GROUP_BATCH = 4
OP_GROUP_BATCH = 4
POW_ROWS = 48

assert BATCH * 2 == SUBLANES


def _params(semantics, vmem_mb):
    return pltpu.CompilerParams(dimension_semantics=semantics,
                                vmem_limit_bytes=vmem_mb * 1024 * 1024)


def _mod_kernel(c_ref, w_ref, b_ref, o_ref):
    c = c_ref[...]
    a = c * jax.nn.sigmoid(c)
    o_ref[0] = jnp.dot(a, w_ref[0], preferred_element_type=F32,
                       precision=lax.Precision.HIGHEST) + b_ref[0]


def _modulation(c, c_ctx, w_ada, b_ada):
    depth = w_ada.shape[0]
    rows = jnp.zeros((SUBLANES, D_MODEL), F32).at[:BATCH].set(c).at[BATCH].set(c_ctx)
    n_col = 3
    out = pl.pallas_call(
        _mod_kernel,
        grid=(depth, n_col),
        in_specs=[
            pl.BlockSpec((SUBLANES, D_MODEL), lambda l, j: (0, 0)),
            pl.BlockSpec((1, D_MODEL, D_MODEL), lambda l, j: (l, 0, j)),
            pl.BlockSpec((1, 1, D_MODEL), lambda l, j: (l, 0, j)),
        ],
        out_specs=pl.BlockSpec((1, SUBLANES, D_MODEL), lambda l, j: (l, 0, j)),
        out_shape=jax.ShapeDtypeStruct((depth, SUBLANES, 3 * D_MODEL), F32),
        compiler_params=_params(("arbitrary", "arbitrary"), 32),
        name="adaln_modulation",
    )(rows, w_ada, b_ada.reshape(depth, 1, 3 * D_MODEL))
    tabs = []
    for l in range(depth):
        lat = out[l, :BATCH].reshape(BATCH, 3, D_MODEL)
        cx = jnp.broadcast_to(out[l, BATCH].reshape(1, 3, D_MODEL), (BATCH, 3, D_MODEL))
        tab = jnp.stack([cx, lat], axis=1)
        tabs.append(jnp.pad(tab, ((0, 0), (0, 0), (0, SUBLANES - 3), (0, 0))))
    return tabs


def _modulated_norm(xt, nw, mod_ref, b=0):
    ms = jnp.mean(xt * xt, axis=-1, keepdims=True)
    y = xt * lax.rsqrt(ms + NORM_EPS) * nw
    return y * (1.0 + mod_ref[b, 0, 1:2, :]) + mod_ref[b, 0, 0:1, :]


def _lane_block_transpose(vs):
    n = len(vs)
    width = LANES // n
    blk = lax.broadcasted_iota(jnp.int32, vs[0].shape, 1) // width
    x = list(vs)
    d = n // 2
    while d >= 1:
        clear = (blk & d) == 0
        y = list(x)
        for i in range(n):
            if i & d == 0:
                a, b = x[i], x[i + d]
                y[i] = jnp.where(clear, a, pltpu.roll(b, width * d, 1))
                y[i + d] = jnp.where(clear, pltpu.roll(a, LANES - width * d, 1), b)
        x = y
        d //= 2
    return x


def _chunk_slot(b, cc):
    return (cc // 2) * (2 * BATCH) + 2 * b + (cc % 2)


def _proj0_kernel(x_ref, c_ref, mod_ref, nw_ref, cos_ref, sin_ref, w_ref,
                  q_ref, z_ref, kbd_ref, vt_ref):
    is_ctx = pl.program_id(0) == 0
    h = jnp.concatenate(
        [_modulated_norm(jnp.where(is_ctx, c_ref[b], x_ref[b]), nw_ref[...], mod_ref, b).astype(BF16)
         for b in range(BATCH)], axis=0)
    cos = cos_ref[...]
    sin = sin_ref[...]
    lane = lax.broadcasted_iota(jnp.int32, (ROW_TILE, LANES), 1)
    first_half = (lane % HEAD_DIM) < (HEAD_DIM // 2)
    low = lax.broadcasted_iota(jnp.int32, (BLOCK, LANES), 1) < HEAD_DIM

    def rope(t):
        partner = jnp.where(first_half, pltpu.roll(t, LANES - HEAD_DIM // 2, 1),
                            pltpu.roll(t, HEAD_DIM // 2, 1))
        return t * cos + partner * sin

    q = jnp.dot(h, w_ref[:, :ATTN_WIDTH], preferred_element_type=F32)
    k = jnp.dot(h, w_ref[:, ATTN_WIDTH:ATTN_WIDTH + KV_WIDTH], preferred_element_type=F32)
    v = jnp.dot(h, w_ref[:, ATTN_WIDTH + KV_WIDTH:ATTN_WIDTH + 2 * KV_WIDTH],
                preferred_element_type=F32)
    z = jnp.dot(h, w_ref[:, ATTN_WIDTH + 2 * KV_WIDTH:], preferred_element_type=F32)
    for b in range(BATCH):
        tile = slice(ROW_TILE * b, ROW_TILE * (b + 1))
        z_ref[b] = z[tile].astype(BF16)
        for j in range(N_SLOTS):
            m, gi = divmod(j, SLOTS_PER_M)
            qj = (rope(q[tile, LANES * j:LANES * (j + 1)]) * Q_SCALE).astype(BF16)
            for blk in range(ROW_TILE // BLOCK):
                q_ref[b, blk, m, BLOCK * gi:BLOCK * (gi + 1), :] = qj[BLOCK * blk:BLOCK * (blk + 1)]
        for m in range(N_KV_PAIRS):
            sl = slice(LANES * m, LANES * (m + 1))
            kr = rope(k[tile, sl])
            vm = v[tile, sl]
            for blk in range(ROW_TILE // BLOCK):
                rows = slice(BLOCK * blk, BLOCK * (blk + 1))
                kbd_ref[b, blk, m, :BLOCK, :] = jnp.where(low, kr[rows], 0.0).astype(BF16)
                kbd_ref[b, blk, m, BLOCK:, :] = jnp.where(low, 0.0, kr[rows]).astype(BF16)
                vt_ref[b, blk, LANES * m:LANES * (m + 1), :] = vm[rows].T.astype(BF16)


def _proj0(x, ctx, mod, norm_w, cos_tab, sin_tab, w_in):
    n_col = w_in.shape[1]
    blocks_per_tile = ROW_TILE // BLOCK
    kv_shape = jax.ShapeDtypeStruct((BATCH, N_BLOCKS, N_KV_PAIRS, 2 * BLOCK, LANES), BF16)
    kv_spec = pl.BlockSpec((BATCH, blocks_per_tile, N_KV_PAIRS, 2 * BLOCK, LANES),
                           lambda i: (0, i, 0, 0, 0))
    row_spec = pl.BlockSpec((BATCH, ROW_TILE, D_MODEL), lambda i: (0, i, 0))
    once = pl.Buffered(1)
    return pl.pallas_call(
        _proj0_kernel,
        grid=(N_ROW_TILES,),
        in_specs=[
            pl.BlockSpec((BATCH, ROW_TILE, D_MODEL), lambda i: (0, jnp.maximum(i - 1, 0), 0)),
            pl.BlockSpec((BATCH, ROW_TILE, D_MODEL), lambda i: (0, 0, 0), pipeline_mode=once),
            pl.BlockSpec((BATCH, 1, SUBLANES, D_MODEL), lambda i: (0, jnp.minimum(i, 1), 0, 0)),
            pl.BlockSpec((1, D_MODEL), lambda i: (0, 0)),
            pl.BlockSpec((ROW_TILE, LANES), lambda i: (i, 0)),
            pl.BlockSpec((ROW_TILE, LANES), lambda i: (i, 0)),
            pl.BlockSpec((D_MODEL, n_col), lambda i: (0, 0), pipeline_mode=once),
        ],
        out_specs=[
            pl.BlockSpec((BATCH, blocks_per_tile, N_KV_PAIRS, SLOTS_PER_M * BLOCK, LANES),
                         lambda i: (0, i, 0, 0, 0)),
            row_spec, kv_spec,
            pl.BlockSpec((BATCH, blocks_per_tile, KV_WIDTH, BLOCK), lambda i: (0, i, 0, 0))],
        out_shape=[
            jax.ShapeDtypeStruct((BATCH, N_BLOCKS, N_KV_PAIRS, SLOTS_PER_M * BLOCK, LANES), BF16),
            jax.ShapeDtypeStruct((BATCH, TOTAL, ATTN_WIDTH), BF16),
            kv_shape,
            jax.ShapeDtypeStruct((BATCH, N_BLOCKS, KV_WIDTH, BLOCK), BF16),
        ],
        compiler_params=_params(("arbitrary",), 56),
        name="attn_projection",
    )(x, ctx, mod, norm_w.reshape(1, D_MODEL), cos_tab, sin_tab, w_in)


def _attn_kernel(q_ref, z_ref, k0_ref, k1_ref, k2_ref, k3_ref, kx_ref,
                 v0_ref, v1_ref, v2_ref, v3_ref, vx_ref,
                 sink_ref, x_ref, c_ref, mod_ref, wo_ref, eye_ref, tri_ref, o_ref):
    step = pl.program_id(1)
    is_lat = step >= N_CTX_BLOCKS // ATTN_QBLOCKS
    n_first = ATTN_QBLOCKS * step - N_CTX_BLOCKS
    q_rows = UNIT_SLOTS * BLOCK
    blocked = tri_ref[2]
    bias = []
    for qb in range(ATTN_QBLOCKS):
        n = n_first + qb
        bias.append([jnp.where(jnp.logical_and(is_lat, n >= 1), tri_ref[0], blocked),
                     jnp.where(is_lat, tri_ref[3], blocked),
                     jnp.where(jnp.logical_and(is_lat, n <= SEQ // BLOCK - 2), tri_ref[1], blocked),
                     None, None])
    k_win = (k0_ref, k1_ref, k2_ref, k3_ref)
    v_win = (v0_ref, v1_ref, v2_ref, v3_ref)

    def kpiece(qb, p, m):
        if p < 3:
            return k_win[qb + p][0, 0, m], v_win[qb + p][0, 0]
        return kx_ref[0, p - 3, m], vx_ref[0, p - 3]

    n_piece = 3 + N_CTX_BLOCKS
    nt = (((1,), (1,)), ((), ()))

    half_rows = slice(0, BLOCK), slice(BLOCK, 2 * BLOCK)

    def scores(qb, m, h):
        qu = q_ref[0, qb, m, q_rows * h:q_rows * (h + 1), :]
        qu_masked = jnp.concatenate([qu, eye_ref[...]], axis=1)
        s_list = []
        for p in range(n_piece):
            kbd, _ = kpiece(qb, p, m)
            if bias[qb][p] is None:
                s = lax.dot_general(kbd, qu, nt, preferred_element_type=F32)
            else:
                s = lax.dot_general(jnp.concatenate([kbd, bias[qb][p]], axis=1), qu_masked, nt,
                                    preferred_element_type=F32)
            s_list.append(s)
        return s_list

    def finish(qb, m, h, s_list):
        slot0 = SLOTS_PER_M * m + UNIT_SLOTS * h
        halves = []
        for hs in range(2):
            sink = jnp.concatenate(
                [sink_ref[2 * (slot0 + gi) + hs:2 * (slot0 + gi) + hs + 1, :]
                 for gi in range(UNIT_SLOTS)], axis=1)
            mx = sink
            for s in s_list:
                mx = jnp.maximum(mx, jnp.max(s[half_rows[hs]], axis=0, keepdims=True))
            probs = [jnp.exp2(s[half_rows[hs]] - mx) for s in s_list]
            denom = jnp.exp2(sink - mx)
            for e in probs:
                denom = denom + jnp.sum(e, axis=0, keepdims=True)
            probs = [e.astype(BF16) for e in probs]
            kv_head = 2 * m + hs
            acc = None
            for p0 in range(0, n_piece, 2):
                group = list(range(p0, min(p0 + 2, n_piece)))
                vt = jnp.concatenate(
                    [kpiece(qb, p, m)[1][HEAD_DIM * kv_head:HEAD_DIM * (kv_head + 1), :]
                     for p in group], axis=1)
                pt = jnp.concatenate([probs[p] for p in group], axis=0)
                part = jnp.dot(vt, pt, preferred_element_type=F32)
                acc = part if acc is None else acc + part
            halves.append(acc * (1.0 / denom))
        o_t = jnp.concatenate(halves, axis=0)
        outs = []
        for gi in range(UNIT_SLOTS):
            j = slot0 + gi
            o = o_t[:, BLOCK * gi:BLOCK * (gi + 1)].T
            zj = z_ref[0, BLOCK * qb:BLOCK * (qb + 1), LANES * j:LANES * (j + 1)].astype(F32)
            outs.append((o * (zj * jax.nn.sigmoid(zj))).astype(BF16))
        return jnp.concatenate(outs, axis=1)

    units = [(qb, m, h) for m in range(N_KV_PAIRS) for h in range(SLOTS_PER_M // UNIT_SLOTS)
             for qb in range(ATTN_QBLOCKS)]
    y = None
    gated = []
    pending = scores(*units[0])
    for u, (qb, m, h) in enumerate(units):
        nxt = scores(*units[u + 1]) if u + 1 < len(units) else None
        gated.append(finish(qb, m, h, pending))
        pending = nxt
        if qb == ATTN_QBLOCKS - 1:
            slot0 = SLOTS_PER_M * m + UNIT_SLOTS * h
            part = jnp.dot(jnp.concatenate(gated, axis=0),
                           wo_ref[LANES * slot0:LANES * (slot0 + UNIT_SLOTS), :],
                           preferred_element_type=F32)
            y = part if y is None else y + part
            gated = []
    resid = jnp.where(is_lat, x_ref[0], c_ref[0])
    o_ref[0] = resid + mod_ref[0, 0, 2:3, :] * y


def _attention(q, z, kbd, vt, sink_tab, x, ctx, mod, w_out):
    last = N_BLOCKS - 1
    n_m = KV_WIDTH // LANES
    nq = ATTN_QBLOCKS
    ctx_steps = N_CTX_BLOCKS // nq
    row_spec = pl.BlockSpec((1, nq * BLOCK, ATTN_WIDTH), lambda b, i: (b, i, 0))

    def kv_spec(off):
        return pl.BlockSpec((1, 1, n_m, 2 * BLOCK, LANES),
                            lambda b, i: (b, jnp.clip(nq * i + off, 0, last), 0, 0, 0))

    ctx_kv_spec = pl.BlockSpec((1, N_CTX_BLOCKS, n_m, 2 * BLOCK, LANES),
                               lambda b, i: (b, 0, 0, 0, 0))

    def vt_spec(off):
        return pl.BlockSpec((1, 1, KV_WIDTH, BLOCK),
                            lambda b, i: (b, jnp.clip(nq * i + off, 0, last), 0, 0))

    ctx_vt_spec = pl.BlockSpec((1, N_CTX_BLOCKS, KV_WIDTH, BLOCK), lambda b, i: (b, 0, 0, 0))
    off = np.arange(BLOCK)
    eye = np.tile(np.eye(BLOCK, dtype=np.float32), (UNIT_SLOTS, 1))
    key_ge = np.where(off[:, None] >= off[None, :], 0.0, NEG_INF)
    key_le = np.where(off[:, None] <= off[None, :], 0.0, NEG_INF)
    tri = np.stack([np.tile(t, (2, 1)) for t in
                    (key_ge, key_le, np.full((BLOCK, BLOCK), NEG_INF), np.zeros((BLOCK, BLOCK)))])
    consts = [jnp.asarray(a, BF16) for a in (eye, tri)]
    return pl.pallas_call(
        _attn_kernel,
        grid=(BATCH, N_BLOCKS // nq),
        in_specs=[
            pl.BlockSpec((1, nq, n_m, SLOTS_PER_M * BLOCK, LANES), lambda b, i: (b, i, 0, 0, 0)),
            row_spec,
            kv_spec(-1), kv_spec(0), kv_spec(1), kv_spec(2), ctx_kv_spec,
            vt_spec(-1), vt_spec(0), vt_spec(1), vt_spec(2), ctx_vt_spec,
            pl.BlockSpec((2 * N_SLOTS, LANES), lambda b, i: (0, 0)),
            pl.BlockSpec((1, nq * BLOCK, D_MODEL),
                         lambda b, i: (b, jnp.maximum(i - ctx_steps, 0), 0)),
            pl.BlockSpec((1, nq * BLOCK, D_MODEL),
                         lambda b, i: (b, jnp.minimum(i, ctx_steps - 1), 0)),
            pl.BlockSpec((1, 1, SUBLANES, D_MODEL),
                         lambda b, i: (b, jnp.minimum(i // ctx_steps, 1), 0, 0)),
            pl.BlockSpec((ATTN_WIDTH, D_MODEL), lambda b, i: (0, 0)),
        ] + [pl.BlockSpec(a.shape, lambda b, i, nd=a.ndim: (0,) * nd) for a in consts],
        out_specs=pl.BlockSpec((1, nq * BLOCK, D_MODEL), lambda b, i: (b, i, 0)),
        out_shape=jax.ShapeDtypeStruct((BATCH, TOTAL, D_MODEL), F32),
        compiler_params=_params(("arbitrary", "arbitrary"), 48),
        name="window_attention",
    )(q, z, kbd, kbd, kbd, kbd, kbd, vt, vt, vt, vt, vt, sink_tab, x, ctx, mod, w_out,
      *consts)


def _proj1_kernel(x_ref, mod_ref, nw_ref, w_ref, u_ref, z_ref, scr_ref):
    h = jnp.concatenate(
        [_modulated_norm(x_ref[b], nw_ref[...], mod_ref, b).astype(BF16) for b in range(BATCH)],
        axis=0)
    u = jnp.dot(h, w_ref[:, :D_MODEL], preferred_element_type=F32)
    z = jnp.dot(h, w_ref[:, D_MODEL:], preferred_element_type=F32)
    for b in range(BATCH):
        z_ref[b] = z[ROW_TILE * b:ROW_TILE * (b + 1)].astype(BF16)
    n_slab = D_MODEL // LANES
    for b in range(BATCH):
        for cc in range(CHUNKS_PER_TILE):
            r0 = ROW_TILE * b + CHUNK * cc
            q0 = SLOT_PITCH * _chunk_slot(b, cc)
            for k in range(n_slab):
                scr_ref[k, q0:q0 + CHUNK, :] = u[r0:r0 + CHUNK, LANES * k:LANES * (k + 1)]
    groups_per_slab = LANES // SSM_GROUP
    for k in range(n_slab):
        for pp in range(PAIRS_PER_TILE // 2):
            parts = []
            for p in (2 * pp, 2 * pp + 1):
                rows = [scr_ref[k, pl.ds(SLOT_PITCH * SUBLANES * p + s, SUBLANES,
                                         stride=SLOT_PITCH), :] for s in range(CHUNK)]
                parts.append([_lane_block_transpose(rows[SUBLANES * m2:SUBLANES * (m2 + 1)])
                              for m2 in range(CHUNK // SUBLANES)])
            for m2 in range(CHUNK // SUBLANES):
                for gl in range(groups_per_slab):
                    val = jnp.concatenate([parts[0][m2][gl], parts[1][m2][gl]], axis=0)
                    u_ref[groups_per_slab * k + gl, 2 * SUBLANES * pp:2 * SUBLANES * (pp + 1),
                          LANES * m2:LANES * (m2 + 1)] = val.astype(BF16)


def _proj1(xc, mod, norm_w, w_in):
    row_spec = pl.BlockSpec((BATCH, ROW_TILE, D_MODEL), lambda i: (0, i, 0))
    tile_rows = PAIRS_PER_TILE * SUBLANES
    return pl.pallas_call(
        _proj1_kernel,
        grid=(N_ROW_TILES,),
        in_specs=[
            row_spec,
            pl.BlockSpec((BATCH, 1, SUBLANES, D_MODEL), lambda i: (0, jnp.minimum(i, 1), 0, 0)),
            pl.BlockSpec((1, D_MODEL), lambda i: (0, 0)),
            pl.BlockSpec((D_MODEL, 2 * D_MODEL), lambda i: (0, 0)),
        ],
        out_specs=[pl.BlockSpec((SSM_GROUPS, tile_rows, CHUNK_W), lambda i: (0, i, 0)), row_spec],
        out_shape=[jax.ShapeDtypeStruct((SSM_GROUPS, SCAN_ROWS, CHUNK_W), BF16),
                   jax.ShapeDtypeStruct((BATCH, TOTAL, D_MODEL), BF16)],
        scratch_shapes=[pltpu.VMEM((D_MODEL // LANES, BATCH * CHUNKS_PER_TILE * SLOT_PITCH, LANES),
                                   F32)],
        compiler_params=_params(("arbitrary",), 56),
        name="ssm_projection",
    )(xc, mod, norm_w.reshape(1, D_MODEL), w_in)


def _s5_kernel(u_ref, m_ref, ws_ref, wy_ref, cst_ref, y_ref, s4_ref, xp_ref):
    even = lax.broadcasted_iota(jnp.int32, (SUBLANES, STATE_W), 0) % 2 == 0
    down = 1
    up = SUBLANES - 1
    fwd = slice(0, STATE_W)
    bwd = slice(STATE_W, 2 * STATE_W)
    fwd_sw = slice(2 * STATE_W, 3 * STATE_W)
    bwd_sw = slice(3 * STATE_W, 4 * STATE_W)
    for g in range(GROUP_BATCH):
        s = jnp.dot(u_ref[g], ws_ref[g], preferred_element_type=F32)
        s = s.reshape(N_SCAN_BLOCKS, SUBLANES, 4 * STATE_W)
        q1f, q2f, q1b, q2b = cst_ref[g, 2], cst_ref[g, 3], cst_ref[g, 6], cst_ref[g, 7]
        zf, zb, zfs, zbs = s[..., fwd], s[..., bwd], s[..., fwd_sw], s[..., bwd_sw]
        rzf, rzfs = pltpu.roll(zf, down, 1), pltpu.roll(zfs, down, 1)
        rzb, rzbs = pltpu.roll(zb, up, 1), pltpu.roll(zbs, up, 1)
        t = jnp.concatenate([zf + q1f * rzf + q2f * rzfs, zb + q1b * rzb + q2b * rzbs,
                             zfs + q1f * rzfs - q2f * rzf, zbs + q1b * rzbs - q2b * rzb], axis=-1)
        s4_ref[g] = t.reshape(SCAN_ROWS, 4 * STATE_W)

    def step(j, carry):
        jb = jnp.where(j < N_CTX_SCAN_BLOCKS, N_CTX_SCAN_BLOCKS - 1 - j,
                       N_SCAN_BLOCKS - 1 + N_CTX_SCAN_BLOCKS - j)
        rf = pl.ds(pl.multiple_of(j * SUBLANES, SUBLANES), SUBLANES)
        rb = pl.ds(pl.multiple_of(jb * SUBLANES, SUBLANES), SUBLANES)
        new = []
        for g in range(GROUP_BATCH):
            cf, cfs, cb, cbs = carry[4 * g:4 * g + 4]
            p1f, p2f, p1b, p2b = cst_ref[g, 0], cst_ref[g, 1], cst_ref[g, 4], cst_ref[g, 5]
            xf = p1f * cf + p2f * cfs + s4_ref[g, rf, fwd]
            xfs = p1f * cfs - p2f * cf + s4_ref[g, rf, fwd_sw]
            xp_ref[g, rf, fwd] = jnp.where(even, cf, pltpu.roll(xf, down, 0))
            new += [jnp.where(even, pltpu.roll(xf, up, 0), xf),
                    jnp.where(even, pltpu.roll(xfs, up, 0), xfs)]
            xb = p1b * cb + p2b * cbs + s4_ref[g, rb, bwd]
            xbs = p1b * cbs - p2b * cb + s4_ref[g, rb, bwd_sw]
            xp_ref[g, rb, bwd] = jnp.where(even, pltpu.roll(xb, up, 0), cb)
            new += [jnp.where(even, xb, pltpu.roll(xb, down, 0)),
                    jnp.where(even, xbs, pltpu.roll(xbs, down, 0))]
        return tuple(new)

    zero = jnp.zeros((SUBLANES, STATE_W), F32)
    lax.fori_loop(0, N_SCAN_BLOCKS, step, (zero,) * (4 * GROUP_BATCH))
    for g in range(GROUP_BATCH):
        y_ref[g] = (
            jnp.dot(u_ref[g, CTX_SCAN_ROWS:, :], m_ref[g], preferred_element_type=F32)
            + lax.dot_general(xp_ref[g, CTX_SCAN_ROWS:, :].astype(BF16), wy_ref[g],
                              (((1,), (1,)), ((), ())), preferred_element_type=F32))


def _s5_core(u_g, m_mat, ws_mat, wy_mat, consts):
    lat_rows = SCAN_ROWS - CTX_SCAN_ROWS

    def gspec(*tail):
        return pl.BlockSpec((GROUP_BATCH,) + tail, lambda i: (i,) + (0,) * len(tail))

    return pl.pallas_call(
        _s5_kernel,
        grid=(SSM_GROUPS // GROUP_BATCH,),
        in_specs=[
            gspec(SCAN_ROWS, CHUNK_W),
            gspec(CHUNK_W, CHUNK_W),
            gspec(CHUNK_W, 4 * STATE_W),
            gspec(CHUNK_W, 2 * STATE_W),
            gspec(8, SUBLANES, STATE_W),
        ],
        out_specs=gspec(lat_rows, CHUNK_W),
        out_shape=jax.ShapeDtypeStruct((SSM_GROUPS, lat_rows, CHUNK_W), F32),
        scratch_shapes=[
            pltpu.VMEM((GROUP_BATCH, SCAN_ROWS, 4 * STATE_W), F32),
            pltpu.VMEM((GROUP_BATCH, SCAN_ROWS, 2 * STATE_W), F32),
        ],
        compiler_params=_params(("arbitrary",), 48),
        name="s5_scan",
    )(u_g, m_mat, ws_mat, wy_mat, consts)


def _out1_kernel(y_ref, z_ref, x_ref, mod_ref, wg_ref, wo_ref, fnw_ref, o_ref, scr_ref):
    n_slab = D_MODEL // LANES
    groups_per_slab = LANES // SSM_GROUP
    pairs_per_phase = PAIRS_PER_TILE // OUT_PHASES
    rows_per_phase = ROW_TILE // OUT_PHASES

    def relayout(ph):
        for k in range(n_slab):
            for p in range(pairs_per_phase * ph, pairs_per_phase * (ph + 1)):
                for m2 in range(CHUNK // SUBLANES):
                    vals = [y_ref[groups_per_slab * k + gl, SUBLANES * p:SUBLANES * (p + 1),
                                  LANES * m2:LANES * (m2 + 1)] for gl in range(groups_per_slab)]
                    steps = _lane_block_transpose(vals)
                    for s2 in range(SUBLANES):
                        t_idx = SUBLANES * m2 + s2
                        scr_ref[k, pl.ds(SLOT_PITCH * SUBLANES * p + t_idx, SUBLANES,
                                         stride=SLOT_PITCH), :] = steps[s2]

    def glu_matmul(ph):
        chunks = range(2 * pairs_per_phase * ph, 2 * pairs_per_phase * (ph + 1))
        y = jnp.concatenate(
            [jnp.concatenate(
                [scr_ref[k, SLOT_PITCH * _chunk_slot(b, cc):SLOT_PITCH * _chunk_slot(b, cc) + CHUNK, :]
                 for k in range(n_slab)], axis=1)
             for b in range(BATCH) for cc in chunks], axis=0)
        g = (0.5 * y * (1.0 + lax.erf(y * (2.0 ** -0.5)))).astype(BF16)
        return jnp.dot(g, wg_ref[...], preferred_element_type=F32)

    def finish(ph, t):
        rows = slice(rows_per_phase * ph, rows_per_phase * (ph + 1))
        z = jnp.concatenate([z_ref[b, rows, :] for b in range(BATCH)], axis=0).astype(F32)
        r = (t[:, :D_MODEL] * jax.nn.sigmoid(t[:, D_MODEL:]) * (z * jax.nn.sigmoid(z))).astype(BF16)
        o = jnp.dot(r, wo_ref[...], preferred_element_type=F32)
        for b in range(BATCH):
            x2 = (x_ref[b, rows, :]
                  + mod_ref[b, 0, 2:3, :] * o[rows_per_phase * b:rows_per_phase * (b + 1)])
            ms = jnp.mean(x2 * x2, axis=-1, keepdims=True)
            o_ref[b, rows, :] = x2 * lax.rsqrt(ms + NORM_EPS) * fnw_ref[...]

    relayout(0)
    relayout(1)
    t_next = glu_matmul(0)
    for ph in range(OUT_PHASES):
        t_cur = t_next
        if ph + 2 < OUT_PHASES:
            relayout(ph + 2)
        if ph + 1 < OUT_PHASES:
            t_next = glu_matmul(ph + 1)
        finish(ph, t_cur)


def _out1(y_g, z, xc, mod, w_glu, w_out, final_norm_w):
    ctx_tiles = CTX_LEN // ROW_TILE
    tile_rows = PAIRS_PER_TILE * SUBLANES
    lat_spec = pl.BlockSpec((BATCH, ROW_TILE, D_MODEL), lambda i: (0, i, 0))
    all_spec = pl.BlockSpec((BATCH, ROW_TILE, D_MODEL), lambda i: (0, i + ctx_tiles, 0))
    return pl.pallas_call(
        _out1_kernel,
        grid=(SEQ // ROW_TILE,),
        in_specs=[
            pl.BlockSpec((SSM_GROUPS, tile_rows, CHUNK_W), lambda i: (0, i, 0)),
            all_spec, all_spec,
            pl.BlockSpec((BATCH, 1, SUBLANES, D_MODEL), lambda i: (0, 1, 0, 0)),
            pl.BlockSpec((D_MODEL, 2 * D_MODEL), lambda i: (0, 0)),
            pl.BlockSpec((D_MODEL, D_MODEL), lambda i: (0, 0)),
            pl.BlockSpec((1, D_MODEL), lambda i: (0, 0)),
        ],
        out_specs=lat_spec,
        out_shape=jax.ShapeDtypeStruct((BATCH, SEQ, D_MODEL), F32),
        scratch_shapes=[pltpu.VMEM((D_MODEL // LANES, BATCH * CHUNKS_PER_TILE * SLOT_PITCH, LANES),
                                   F32)],
        compiler_params=_params(("arbitrary",), 56),
        name="ssm_output",
    )(y_g, z, xc, mod, w_glu, w_out, final_norm_w.reshape(1, D_MODEL))


def _slot_order(t, lead):
    n_m = KV_WIDTH // LANES
    gq = N_HEADS // N_KV_HEADS
    shape = t.shape
    t = t.reshape(shape[:lead] + (n_m, 2, gq) + shape[lead + 1:])
    perm = tuple(range(lead)) + (lead, lead + 2, lead + 1) + tuple(range(lead + 3, t.ndim))
    return jnp.transpose(t, perm)


def _rope_order(t):
    shape = t.shape
    t = t.reshape(shape[:-1] + (2, 2, ROPE_FREQS))
    return jnp.swapaxes(t, -3, -2).reshape(shape)


def _attn_weights(w_in, w_out, sink):
    wq = w_in[:, :ATTN_WIDTH].reshape(D_MODEL, N_HEADS, HEAD_DIM)
    wq = _slot_order(_rope_order(wq), 1).reshape(D_MODEL, ATTN_WIDTH)
    wk = w_in[:, ATTN_WIDTH:ATTN_WIDTH + KV_WIDTH].reshape(D_MODEL, N_KV_HEADS, HEAD_DIM)
    wk = _rope_order(wk).reshape(D_MODEL, KV_WIDTH)
    wv = w_in[:, ATTN_WIDTH + KV_WIDTH:ATTN_WIDTH + 2 * KV_WIDTH]
    wz = w_in[:, ATTN_WIDTH + 2 * KV_WIDTH:].reshape(D_MODEL, N_HEADS, HEAD_DIM)
    wz = _slot_order(wz, 1).reshape(D_MODEL, ATTN_WIDTH)
    w_in_p = jnp.concatenate([wq, wk, wv, wz], axis=1).astype(BF16)
    wo = _slot_order(w_out.reshape(N_HEADS, HEAD_DIM, D_MODEL), 0).reshape(ATTN_WIDTH, D_MODEL)
    sink_p = _slot_order(sink.astype(F32).reshape(N_HEADS), 0).reshape(2 * N_SLOTS)
    sink_tab = jnp.broadcast_to((sink_p * LOG2E)[:, None], (2 * N_SLOTS, LANES))
    return w_in_p, wo.astype(BF16), sink_tab


def _rope_tables():
    inv = ROPE_BASE ** (-np.arange(ROPE_FREQS, dtype=np.float64) / ROPE_FREQS)
    pos = np.arange(SEQ)
    row = (pos // GRID_W)[:, None] * inv
    col = (pos % GRID_W)[:, None] * inv
    w = np.arange(LANES) % HEAD_DIM
    half_w = HEAD_DIM // 2
    half, axis, f = w // half_w, (w % half_w) // ROPE_FREQS, w % ROPE_FREQS
    ang = np.where((axis == 0)[None, :], row[:, f], col[:, f])
    sign = np.where(half == 0, -1.0, 1.0)[None, :]
    cos = np.concatenate([np.ones((CTX_LEN, LANES)), np.cos(ang)], axis=0)
    sin = np.concatenate([np.zeros((CTX_LEN, LANES)), np.sin(ang) * sign], axis=0)
    return jnp.asarray(cos, F32), jnp.asarray(sin, F32)


def _s5_operators(lam_re, lam_im, log_dt, b_re, b_im, c_re, c_im, d_skip):
    t_len = CHUNK
    n_pow = 2 * t_len + 1
    lr, li = lam_re.astype(F32), lam_im.astype(F32)
    dt = jnp.exp(log_dt.astype(F32))[..., None]
    mag = jnp.exp(lr * dt)
    sq = [(mag * jnp.cos(li * dt), mag * jnp.sin(li * dt))]
    while 2 ** len(sq) < n_pow:
        r, i = sq[-1]
        sq.append((r * r - i * i, 2.0 * r * i))
    ks = np.arange(n_pow)
    pr = jnp.ones(lr.shape + (n_pow,), F32)
    pi = jnp.zeros(lr.shape + (n_pow,), F32)
    for bit, (r, i) in enumerate(sq):
        on = jnp.asarray((ks >> bit) & 1 == 1)
        fr = jnp.where(on, r[..., None], 1.0)
        fi = jnp.where(on, i[..., None], 0.0)
        pr, pi = pr * fr - pi * fi, pr * fi + pi * fr
    ar1, ai1 = sq[0][0] - 1.0, sq[0][1]
    den = lr * lr + li * li
    gr, gi = (ar1 * lr + ai1 * li) / den, (ai1 * lr - ar1 * li) / den
    br_, bi_ = b_re.astype(F32), b_im.astype(F32)
    bbr = gr[..., None] * br_ - gi[..., None] * bi_
    bbi = gr[..., None] * bi_ + gi[..., None] * br_
    pad_k = POW_ROWS - n_pow
    prt, pit = jnp.swapaxes(pr, 2, 3), jnp.swapaxes(pi, 2, 3)
    pw = jnp.pad(jnp.concatenate([prt, prt, pit, pit], axis=-1),
                 ((0, 0), (0, 0), (0, pad_k), (0, 0)))
    brt, bit = jnp.swapaxes(bbr, 2, 3), jnp.swapaxes(bbi, 2, 3)
    bb = jnp.concatenate([brt, bit, -bit, brt, bit, brt, brt, -bit], axis=-1)
    cr, ci = c_re.astype(F32), c_im.astype(F32)
    cm = jnp.stack([jnp.concatenate([cr, -ci], axis=-1),
                    jnp.concatenate([-ci, -cr], axis=-1)], axis=2)
    dv = jnp.tile(d_skip.astype(F32).reshape(SSM_GROUPS, 1, SSM_GROUP), (1, 1, t_len))
    return _s5_operator_call(pw, bb, cm, dv)


def _operator_constants():
    t_len = CHUNK
    s_of_row = np.arange(CHUNK_W) // SSM_GROUP
    k_ar = np.arange(POW_ROWS)
    oh_f = (k_ar[None, :] == (t_len - 1 - s_of_row)[:, None]).astype(np.float32)
    oh_b = (k_ar[None, :] == s_of_row[:, None]).astype(np.float32)
    t_of_lane = np.arange(CHUNK_W) // SSM_GROUP
    expo = [s_of_row + 1, t_len - s_of_row, s_of_row, t_len - 1 - s_of_row]
    sel = np.stack([(k_ar[None, :] == e[:, None]) for e in expo]).astype(np.float32)
    h_of_lane = np.arange(CHUNK_W) % SSM_GROUP
    dmask = ((s_of_row[:, None] == t_of_lane[None, :])
             & ((np.arange(CHUNK_W) % SSM_GROUP)[:, None] == h_of_lane[None, :])).astype(np.float32)
    return oh_f, oh_b, sel, dmask


def _s5_op_kernel(*refs):
    for gg in range(OP_GROUP_BATCH):
        _s5_op_group(gg, *refs)


def _s5_op_group(gg, pw_ref, bb_ref, cm_ref, dv_ref, ohf_ref, ohb_ref, sel_ref, dmask_ref,
                 m_ref, ws_ref, wyt_ref, cst_ref):
    t_len = CHUNK

    def split(a):
        hi = a.astype(BF16)
        return hi, (a - hi.astype(F32)).astype(BF16)

    def pick_rows(onehot2, table):
        hi, lo = split(table)
        return jnp.dot(onehot2, jnp.concatenate([hi, lo], axis=0), preferred_element_type=F32)

    def tile_rows(a):
        return jnp.concatenate([a] * t_len, axis=0)

    main, swapped = [], []
    for d, oh_ref in ((0, ohf_ref), (1, ohb_ref)):
        pp = pick_rows(oh_ref[...], pw_ref[d, gg])
        p_re, p_im = pp[:, :STATE_W], pp[:, STATE_W:]
        b0, b1, b2, b3 = [tile_rows(bb_ref[d, gg, :, STATE_W * i:STATE_W * (i + 1)])
                          for i in range(4)]
        main.append(p_re * b0 + p_im * b1)
        swapped.append(p_re * b2 + p_im * b3)
    ws_ref[gg] = jnp.concatenate(main + swapped, axis=1).astype(BF16)

    c_tiled = [[tile_rows(cm_ref[d, gg, i]) for i in range(2)] for d in range(2)]

    def block_t(d, pat):
        pp = pick_rows(sel_ref[pat], pw_ref[d, gg])
        return pp[:, :STATE_W] * c_tiled[d][0] + pp[:, STATE_W:] * c_tiled[d][1]

    wyt_ref[gg] = jnp.concatenate([block_t(0, 0), block_t(1, 1)], axis=1).astype(BF16)

    nt = (((1,), (1,)), ((), ()))
    kt_f = lax.dot_general(bb_ref[0, gg, :, :STATE_W], block_t(0, 2), nt,
                           preferred_element_type=F32, precision=lax.Precision.HIGHEST)
    kt_b = lax.dot_general(bb_ref[1, gg, :, :STATE_W], block_t(1, 3), nt,
                           preferred_element_type=F32, precision=lax.Precision.HIGHEST)
    lane = lax.broadcasted_iota(jnp.int32, (SSM_GROUP, CHUNK_W), 1)
    skip = dv_ref[gg]
    for s in range(t_len):
        fwd = kt_f if s == 0 else pltpu.roll(kt_f, SSM_GROUP * s, 1)
        back = t_len - 1 - s
        bwd = kt_b if back == 0 else pltpu.roll(kt_b, CHUNK_W - SSM_GROUP * back, 1)
        rows = (jnp.where(lane >= SSM_GROUP * s, fwd, 0.0)
                + jnp.where(lane < SSM_GROUP * (s + 1), bwd, 0.0)
                + dmask_ref[SSM_GROUP * s:SSM_GROUP * (s + 1), :] * skip)
        m_ref[gg, SSM_GROUP * s:SSM_GROUP * (s + 1), :] = rows.astype(BF16)

    even = lax.broadcasted_iota(jnp.int32, (SUBLANES, STATE_W), 0) % 2 == 0
    sign = jnp.where(lax.broadcasted_iota(jnp.int32, (1, STATE_W), 1) < SSM_STATE, -1.0, 1.0)

    def w12(d, k):
        row = pw_ref[d, gg, k:k + 1, :]
        return row[:, :STATE_W], row[:, STATE_W:] * sign

    zero = (jnp.zeros((1, STATE_W), F32),) * 2
    pairs = [(w12(0, t_len), w12(0, 2 * t_len)), (zero, w12(0, t_len)),
             (w12(1, 2 * t_len), w12(1, t_len)), (w12(1, t_len), zero)]
    idx = 0
    for top, bot in pairs:
        for part in range(2):
            cst_ref[gg, idx] = jnp.where(even, jnp.broadcast_to(top[part], (SUBLANES, STATE_W)),
                                         jnp.broadcast_to(bot[part], (SUBLANES, STATE_W)))
            idx += 1


def _s5_operator_call(pw, bb, cm, dv):
    oh_f, oh_b, sel, dmask = _operator_constants()
    consts = [jnp.asarray(np.concatenate([a, a], axis=-1), BF16) for a in (oh_f, oh_b, sel)]
    consts.append(jnp.asarray(dmask))

    def per_group(*tail):
        n = len(tail)
        return pl.BlockSpec((2, OP_GROUP_BATCH) + tail, lambda g: (0, g) + (0,) * n)

    def whole(a):
        return pl.BlockSpec(a.shape, lambda g: (0,) * a.ndim)

    def out(*tail):
        return pl.BlockSpec((OP_GROUP_BATCH,) + tail, lambda g: (g,) + (0,) * len(tail))

    return pl.pallas_call(
        _s5_op_kernel,
        grid=(SSM_GROUPS // OP_GROUP_BATCH,),
        in_specs=[per_group(POW_ROWS, 2 * STATE_W), per_group(SSM_GROUP, 4 * STATE_W),
                  per_group(2, SSM_GROUP, STATE_W),
                  pl.BlockSpec((OP_GROUP_BATCH, 1, CHUNK_W), lambda g: (g, 0, 0))]
                 + [whole(a) for a in consts],
        out_specs=[out(CHUNK_W, CHUNK_W), out(CHUNK_W, 4 * STATE_W), out(CHUNK_W, 2 * STATE_W),
                   out(8, SUBLANES, STATE_W)],
        out_shape=[jax.ShapeDtypeStruct((SSM_GROUPS, CHUNK_W, CHUNK_W), BF16),
                   jax.ShapeDtypeStruct((SSM_GROUPS, CHUNK_W, 4 * STATE_W), BF16),
                   jax.ShapeDtypeStruct((SSM_GROUPS, CHUNK_W, 2 * STATE_W), BF16),
                   jax.ShapeDtypeStruct((SSM_GROUPS, 8, SUBLANES, STATE_W), F32)],
        compiler_params=_params(("arbitrary",), 32),
        name="s5_operators",
    )(pw, bb, cm, dv, *consts)


def kernel(x, c, ctx, c_ctx, norm_w, w_ada, b_ada, attn_w_in, attn_sink, attn_w_out,
           ssm_w_in, ssm_lam_re, ssm_lam_im, ssm_log_dt, ssm_b_re, ssm_b_im, ssm_c_re, ssm_c_im,
           ssm_d, ssm_w_glu, ssm_w_out, final_norm_w):
    mod0, mod1 = _modulation(c, c_ctx, w_ada, b_ada)

    w_in0, w_out0, sink_tab = _attn_weights(attn_w_in[0], attn_w_out[0], attn_sink[0])
    cos_tab, sin_tab = _rope_tables()
    q, z0, kbd, vt = _proj0(x, ctx, mod0, norm_w[0], cos_tab, sin_tab, w_in0)
    xc1 = _attention(q, z0, kbd, vt, sink_tab, x, ctx, mod0, w_out0)

    u_g, z1 = _proj1(xc1, mod1, norm_w[1], ssm_w_in[0].astype(BF16))
    m_mat, ws_mat, wy_mat, consts = _s5_operators(
        ssm_lam_re[0], ssm_lam_im[0], ssm_log_dt[0], ssm_b_re[0], ssm_b_im[0],
        ssm_c_re[0], ssm_c_im[0], ssm_d[0])
    y_g = _s5_core(u_g, m_mat, ws_mat, wy_mat, consts)
    return _out1(y_g, z1, xc1, mod1, ssm_w_glu[0].astype(BF16), ssm_w_out[0].astype(BF16),
                 final_norm_w)
```

```python
import math

import jax
import jax.numpy as jnp
import numpy as np
from jax import lax
from jax.experimental import pallas as pl
from jax.experimental.pallas import tpu as pltpu

F32 = jnp.float32
BF16 = jnp.bfloat16

D_MODEL = 1024
BATCH = 4
SEQ = 4096
GRID_W = 64
CTX_LEN = 256
TOTAL = CTX_LEN + SEQ
HEAD_DIM = 64
N_HEADS = 16
N_KV_HEADS = 4
ATTN_WIDTH = N_HEADS * HEAD_DIM
KV_WIDTH = N_KV_HEADS * HEAD_DIM
BLOCK = 128
N_BLOCKS = TOTAL // BLOCK
N_CTX_BLOCKS = CTX_LEN // BLOCK
ROPE_BASE = 10000.0
ROPE_FREQS = HEAD_DIM // 4
SSM_GROUP = 16
SSM_GROUPS = D_MODEL // SSM_GROUP
SSM_STATE = 64
NORM_EPS = 1e-6
NEG_INF = -1e30

LANES = 128
SUBLANES = 8
N_SLOTS = ATTN_WIDTH // LANES
N_KV_PAIRS = KV_WIDTH // LANES
SLOTS_PER_M = N_SLOTS // N_KV_PAIRS
ATTN_QBLOCKS = 2
assert N_CTX_BLOCKS % ATTN_QBLOCKS == 0 and N_BLOCKS % ATTN_QBLOCKS == 0
assert ATTN_QBLOCKS == 2
UNIT_SLOTS = 4
LOG2E = math.log2(math.e)
Q_SCALE = HEAD_DIM ** -0.5 * LOG2E
ROW_TILE = 256
N_ROW_TILES = TOTAL // ROW_TILE
CHUNK = 16
N_CHUNKS = TOTAL // CHUNK
N_CTX_CHUNKS = CTX_LEN // CHUNK
CHUNK_W = CHUNK * SSM_GROUP
CHUNKS_PER_TILE = ROW_TILE // CHUNK
PAIRS_PER_TILE = CHUNKS_PER_TILE // 2
SLOT_PITCH = 17
OUT_PHASES = 4
assert SLOT_PITCH >= CHUNK and (BATCH * CHUNKS_PER_TILE * SLOT_PITCH) % SUBLANES == 0
STATE_W = 2 * SSM_STATE
SCAN_ROWS = N_CHUNKS * BATCH
CTX_SCAN_ROWS = N_CTX_CHUNKS * BATCH
N_SCAN_BLOCKS = SCAN_ROWS // SUBLANES
N_CTX_SCAN_BLOCKS = CTX_SCAN_ROWS // SUBLANES
GROUP_BATCH = 4
OP_GROUP_BATCH = 4
POW_ROWS = 48

assert BATCH * 2 == SUBLANES


def _params(semantics, vmem_mb):
    return pltpu.CompilerParams(dimension_semantics=semantics,
                                vmem_limit_bytes=vmem_mb * 1024 * 1024)


def _mod_kernel(c_ref, w_ref, b_ref, o_ref):
    c = c_ref[...]
    a = c * jax.nn.sigmoid(c)
    o_ref[0] = jnp.dot(a, w_ref[0], preferred_element_type=F32,
                       precision=lax.Precision.HIGHEST) + b_ref[0]


def _modulation(c, c_ctx, w_ada, b_ada):
    depth = w_ada.shape[0]
    rows = jnp.zeros((SUBLANES, D_MODEL), F32).at[:BATCH].set(c).at[BATCH].set(c_ctx)
    n_col = 3
    out = pl.pallas_call(
        _mod_kernel,
        grid=(depth, n_col),
        in_specs=[
            pl.BlockSpec((SUBLANES, D_MODEL), lambda l, j: (0, 0)),
            pl.BlockSpec((1, D_MODEL, D_MODEL), lambda l, j: (l, 0, j)),
            pl.BlockSpec((1, 1, D_MODEL), lambda l, j: (l, 0, j)),
        ],
        out_specs=pl.BlockSpec((1, SUBLANES, D_MODEL), lambda l, j: (l, 0, j)),
        out_shape=jax.ShapeDtypeStruct((depth, SUBLANES, 3 * D_MODEL), F32),
        compiler_params=_params(("arbitrary", "arbitrary"), 32),
        name="adaln_modulation",
    )(rows, w_ada, b_ada.reshape(depth, 1, 3 * D_MODEL))
    tabs = []
    for l in range(depth):
        lat = out[l, :BATCH].reshape(BATCH, 3, D_MODEL)
        cx = jnp.broadcast_to(out[l, BATCH].reshape(1, 3, D_MODEL), (BATCH, 3, D_MODEL))
        tab = jnp.stack([cx, lat], axis=1)
        tabs.append(jnp.pad(tab, ((0, 0), (0, 0), (0, SUBLANES - 3), (0, 0))))
    return tabs


def _modulated_norm(xt, nw, mod_ref, b=0):
    ms = jnp.mean(xt * xt, axis=-1, keepdims=True)
    y = xt * lax.rsqrt(ms + NORM_EPS) * nw
    return y * (1.0 + mod_ref[b, 0, 1:2, :]) + mod_ref[b, 0, 0:1, :]


def _lane_block_transpose(vs):
    n = len(vs)
    width = LANES // n
    blk = lax.broadcasted_iota(jnp.int32, vs[0].shape, 1) // width
    x = list(vs)
    d = n // 2
    while d >= 1:
        clear = (blk & d) == 0
        y = list(x)
        for i in range(n):
            if i & d == 0:
                a, b = x[i], x[i + d]
                y[i] = jnp.where(clear, a, pltpu.roll(b, width * d, 1))
                y[i + d] = jnp.where(clear, pltpu.roll(a, LANES - width * d, 1), b)
        x = y
        d //= 2
    return x


def _chunk_slot(b, cc):
    return (cc // 2) * (2 * BATCH) + 2 * b + (cc % 2)


def _proj0_kernel(x_ref, c_ref, mod_ref, nw_ref, cos_ref, sin_ref, w_ref,
                  q_ref, z_ref, kbd_ref, vt_ref):
    is_ctx = pl.program_id(0) == 0
    h = jnp.concatenate(
        [_modulated_norm(jnp.where(is_ctx, c_ref[b], x_ref[b]), nw_ref[...], mod_ref, b).astype(BF16)
         for b in range(BATCH)], axis=0)
    cos = cos_ref[...]
    sin = sin_ref[...]
    lane = lax.broadcasted_iota(jnp.int32, (ROW_TILE, LANES), 1)
    first_half = (lane % HEAD_DIM) < (HEAD_DIM // 2)
    low = lax.broadcasted_iota(jnp.int32, (BLOCK, LANES), 1) < HEAD_DIM

    def rope(t):
        partner = jnp.where(first_half, pltpu.roll(t, LANES - HEAD_DIM // 2, 1),
                            pltpu.roll(t, HEAD_DIM // 2, 1))
        return t * cos + partner * sin

    q = jnp.dot(h, w_ref[:, :ATTN_WIDTH], preferred_element_type=F32)
    k = jnp.dot(h, w_ref[:, ATTN_WIDTH:ATTN_WIDTH + KV_WIDTH], preferred_element_type=F32)
    v = jnp.dot(h, w_ref[:, ATTN_WIDTH + KV_WIDTH:ATTN_WIDTH + 2 * KV_WIDTH],
                preferred_element_type=F32)
    z = jnp.dot(h, w_ref[:, ATTN_WIDTH + 2 * KV_WIDTH:], preferred_element_type=F32)
    for b in range(BATCH):
        tile = slice(ROW_TILE * b, ROW_TILE * (b + 1))
        z_ref[b] = z[tile].astype(BF16)
        for j in range(N_SLOTS):
            m, gi = divmod(j, SLOTS_PER_M)
            qj = (rope(q[tile, LANES * j:LANES * (j + 1)]) * Q_SCALE).astype(BF16)
            for blk in range(ROW_TILE // BLOCK):
                q_ref[b, blk, m, BLOCK * gi:BLOCK * (gi + 1), :] = qj[BLOCK * blk:BLOCK * (blk + 1)]
        for m in range(N_KV_PAIRS):
            sl = slice(LANES * m, LANES * (m + 1))
            kr = rope(k[tile, sl])
            vm = v[tile, sl]
            for blk in range(ROW_TILE // BLOCK):
                rows = slice(BLOCK * blk, BLOCK * (blk + 1))
                kbd_ref[b, blk, m, :BLOCK, :] = jnp.where(low, kr[rows], 0.0).astype(BF16)
                kbd_ref[b, blk, m, BLOCK:, :] = jnp.where(low, 0.0, kr[rows]).astype(BF16)
                vt_ref[b, blk, LANES * m:LANES * (m + 1), :] = vm[rows].T.astype(BF16)


def _proj0(x, ctx, mod, norm_w, cos_tab, sin_tab, w_in):
    n_col = w_in.shape[1]
    blocks_per_tile = ROW_TILE // BLOCK
    kv_shape = jax.ShapeDtypeStruct((BATCH, N_BLOCKS, N_KV_PAIRS, 2 * BLOCK, LANES), BF16)
    kv_spec = pl.BlockSpec((BATCH, blocks_per_tile, N_KV_PAIRS, 2 * BLOCK, LANES),
                           lambda i: (0, i, 0, 0, 0))
    row_spec = pl.BlockSpec((BATCH, ROW_TILE, D_MODEL), lambda i: (0, i, 0))
    once = pl.Buffered(1)
    return pl.pallas_call(
        _proj0_kernel,
        grid=(N_ROW_TILES,),
        in_specs=[
            pl.BlockSpec((BATCH, ROW_TILE, D_MODEL), lambda i: (0, jnp.maximum(i - 1, 0), 0)),
            pl.BlockSpec((BATCH, ROW_TILE, D_MODEL), lambda i: (0, 0, 0), pipeline_mode=once),
            pl.BlockSpec((BATCH, 1, SUBLANES, D_MODEL), lambda i: (0, jnp.minimum(i, 1), 0, 0)),
            pl.BlockSpec((1, D_MODEL), lambda i: (0, 0)),
            pl.BlockSpec((ROW_TILE, LANES), lambda i: (i, 0)),
            pl.BlockSpec((ROW_TILE, LANES), lambda i: (i, 0)),
            pl.BlockSpec((D_MODEL, n_col), lambda i: (0, 0), pipeline_mode=once),
        ],
        out_specs=[
            pl.BlockSpec((BATCH, blocks_per_tile, N_KV_PAIRS, SLOTS_PER_M * BLOCK, LANES),
                         lambda i: (0, i, 0, 0, 0)),
            row_spec, kv_spec,
            pl.BlockSpec((BATCH, blocks_per_tile, KV_WIDTH, BLOCK), lambda i: (0, i, 0, 0))],
        out_shape=[
            jax.ShapeDtypeStruct((BATCH, N_BLOCKS, N_KV_PAIRS, SLOTS_PER_M * BLOCK, LANES), BF16),
            jax.ShapeDtypeStruct((BATCH, TOTAL, ATTN_WIDTH), BF16),
            kv_shape,
            jax.ShapeDtypeStruct((BATCH, N_BLOCKS, KV_WIDTH, BLOCK), BF16),
        ],
        compiler_params=_params(("arbitrary",), 56),
        name="attn_projection",
    )(x, ctx, mod, norm_w.reshape(1, D_MODEL), cos_tab, sin_tab, w_in)


def _attn_kernel(q_ref, z_ref, k0_ref, k1_ref, k2_ref, k3_ref, kx_ref,
                 v0_ref, v1_ref, v2_ref, v3_ref, vx_ref,
                 sink_ref, x_ref, c_ref, mod_ref, wo_ref, eye_ref, tri_ref, o_ref):
    step = pl.program_id(1)
    is_lat = step >= N_CTX_BLOCKS // ATTN_QBLOCKS
    n_first = ATTN_QBLOCKS * step - N_CTX_BLOCKS
    q_rows = UNIT_SLOTS * BLOCK
    blocked = tri_ref[2]
    bias = []
    for qb in range(ATTN_QBLOCKS):
        n = n_first + qb
        bias.append([jnp.where(jnp.logical_and(is_lat, n >= 1), tri_ref[0], blocked),
                     jnp.where(is_lat, tri_ref[3], blocked),
                     jnp.where(jnp.logical_and(is_lat, n <= SEQ // BLOCK - 2), tri_ref[1], blocked),
                     None, None])
    k_win = (k0_ref, k1_ref, k2_ref, k3_ref)
    v_win = (v0_ref, v1_ref, v2_ref, v3_ref)

    def kpiece(qb, p, m):
        if p < 3:
            return k_win[qb + p][0, 0, m], v_win[qb + p][0, 0]
        return kx_ref[0, p - 3, m], vx_ref[0, p - 3]

    n_piece = 3 + N_CTX_BLOCKS
    nt = (((1,), (1,)), ((), ()))

    half_rows = slice(0, BLOCK), slice(BLOCK, 2 * BLOCK)

    def scores(qb, m, h):
        qu = q_ref[0, qb, m, q_rows * h:q_rows * (h + 1), :]
        qu_masked = jnp.concatenate([qu, eye_ref[...]], axis=1)
        s_list = []
        for p in range(n_piece):
            kbd, _ = kpiece(qb, p, m)
            if bias[qb][p] is None:
                s = lax.dot_general(kbd, qu, nt, preferred_element_type=F32)
            else:
                s = lax.dot_general(jnp.concatenate([kbd, bias[qb][p]], axis=1), qu_masked, nt,
                                    preferred_element_type=F32)
            s_list.append(s)
        return s_list

    def finish(qb, m, h, s_list):
        slot0 = SLOTS_PER_M * m + UNIT_SLOTS * h
        halves = []
        for hs in range(2):
            sink = jnp.concatenate(
                [sink_ref[2 * (slot0 + gi) + hs:2 * (slot0 + gi) + hs + 1, :]
                 for gi in range(UNIT_SLOTS)], axis=1)
            mx = sink
            for s in s_list:
                mx = jnp.maximum(mx, jnp.max(s[half_rows[hs]], axis=0, keepdims=True))
            probs = [jnp.exp2(s[half_rows[hs]] - mx) for s in s_list]
            denom = jnp.exp2(sink - mx)
            for e in probs:
                denom = denom + jnp.sum(e, axis=0, keepdims=True)
            probs = [e.astype(BF16) for e in probs]
            kv_head = 2 * m + hs
            acc = None
            for p0 in range(0, n_piece, 2):
                group = list(range(p0, min(p0 + 2, n_piece)))
                vt = jnp.concatenate(
                    [kpiece(qb, p, m)[1][HEAD_DIM * kv_head:HEAD_DIM * (kv_head + 1), :]
                     for p in group], axis=1)
                pt = jnp.concatenate([probs[p] for p in group], axis=0)
                part = jnp.dot(vt, pt, preferred_element_type=F32)
                acc = part if acc is None else acc + part
            halves.append(acc * (1.0 / denom))
        o_t = jnp.concatenate(halves, axis=0)
        outs = []
        for gi in range(UNIT_SLOTS):
            j = slot0 + gi
            o = o_t[:, BLOCK * gi:BLOCK * (gi + 1)].T
            zj = z_ref[0, BLOCK * qb:BLOCK * (qb + 1), LANES * j:LANES * (j + 1)].astype(F32)
            outs.append((o * (zj * jax.nn.sigmoid(zj))).astype(BF16))
        return jnp.concatenate(outs, axis=1)

    units = [(qb, m, h) for m in range(N_KV_PAIRS) for h in range(SLOTS_PER_M // UNIT_SLOTS)
             for qb in range(ATTN_QBLOCKS)]
    y = None
    gated = []
    pending = scores(*units[0])
    for u, (qb, m, h) in enumerate(units):
        nxt = scores(*units[u + 1]) if u + 1 < len(units) else None
        gated.append(finish(qb, m, h, pending))
        pending = nxt
        if qb == ATTN_QBLOCKS - 1:
            slot0 = SLOTS_PER_M * m + UNIT_SLOTS * h
            part = jnp.dot(jnp.concatenate(gated, axis=0),
                           wo_ref[LANES * slot0:LANES * (slot0 + UNIT_SLOTS), :],
                           preferred_element_type=F32)
            y = part if y is None else y + part
            gated = []
    resid = jnp.where(is_lat, x_ref[0], c_ref[0])
    o_ref[0] = resid + mod_ref[0, 0, 2:3, :] * y


def _attention(q, z, kbd, vt, sink_tab, x, ctx, mod, w_out):
    last = N_BLOCKS - 1
    n_m = KV_WIDTH // LANES
    nq = ATTN_QBLOCKS
    ctx_steps = N_CTX_BLOCKS // nq
    row_spec = pl.BlockSpec((1, nq * BLOCK, ATTN_WIDTH), lambda b, i: (b, i, 0))

    def kv_spec(off):
        return pl.BlockSpec((1, 1, n_m, 2 * BLOCK, LANES),
                            lambda b, i: (b, jnp.clip(nq * i + off, 0, last), 0, 0, 0))

    ctx_kv_spec = pl.BlockSpec((1, N_CTX_BLOCKS, n_m, 2 * BLOCK, LANES),
                               lambda b, i: (b, 0, 0, 0, 0))

    def vt_spec(off):
        return pl.BlockSpec((1, 1, KV_WIDTH, BLOCK),
                            lambda b, i: (b, jnp.clip(nq * i + off, 0, last), 0, 0))

    ctx_vt_spec = pl.BlockSpec((1, N_CTX_BLOCKS, KV_WIDTH, BLOCK), lambda b, i: (b, 0, 0, 0))
    off = np.arange(BLOCK)
    eye = np.tile(np.eye(BLOCK, dtype=np.float32), (UNIT_SLOTS, 1))
    key_ge = np.where(off[:, None] >= off[None, :], 0.0, NEG_INF)
    key_le = np.where(off[:, None] <= off[None, :], 0.0, NEG_INF)
    tri = np.stack([np.tile(t, (2, 1)) for t in
                    (key_ge, key_le, np.full((BLOCK, BLOCK), NEG_INF), np.zeros((BLOCK, BLOCK)))])
    consts = [jnp.asarray(a, BF16) for a in (eye, tri)]
    return pl.pallas_call(
        _attn_kernel,
        grid=(BATCH, N_BLOCKS // nq),
        in_specs=[
            pl.BlockSpec((1, nq, n_m, SLOTS_PER_M * BLOCK, LANES), lambda b, i: (b, i, 0, 0, 0)),
            row_spec,
            kv_spec(-1), kv_spec(0), kv_spec(1), kv_spec(2), ctx_kv_spec,
            vt_spec(-1), vt_spec(0), vt_spec(1), vt_spec(2), ctx_vt_spec,
            pl.BlockSpec((2 * N_SLOTS, LANES), lambda b, i: (0, 0)),
            pl.BlockSpec((1, nq * BLOCK, D_MODEL),
                         lambda b, i: (b, jnp.maximum(i - ctx_steps, 0), 0)),
            pl.BlockSpec((1, nq * BLOCK, D_MODEL),
                         lambda b, i: (b, jnp.minimum(i, ctx_steps - 1), 0)),
            pl.BlockSpec((1, 1, SUBLANES, D_MODEL),
                         lambda b, i: (b, jnp.minimum(i // ctx_steps, 1), 0, 0)),
            pl.BlockSpec((ATTN_WIDTH, D_MODEL), lambda b, i: (0, 0)),
        ] + [pl.BlockSpec(a.shape, lambda b, i, nd=a.ndim: (0,) * nd) for a in consts],
        out_specs=pl.BlockSpec((1, nq * BLOCK, D_MODEL), lambda b, i: (b, i, 0)),
        out_shape=jax.ShapeDtypeStruct((BATCH, TOTAL, D_MODEL), F32),
        compiler_params=_params(("arbitrary", "arbitrary"), 48),
        name="window_attention",
    )(q, z, kbd, kbd, kbd, kbd, kbd, vt, vt, vt, vt, vt, sink_tab, x, ctx, mod, w_out,
      *consts)


def _proj1_kernel(x_ref, mod_ref, nw_ref, w_ref, u_ref, z_ref, scr_ref):
    h = jnp.concatenate(
        [_modulated_norm(x_ref[b], nw_ref[...], mod_ref, b).astype(BF16) for b in range(BATCH)],
        axis=0)
    u = jnp.dot(h, w_ref[:, :D_MODEL], preferred_element_type=F32)
    z = jnp.dot(h, w_ref[:, D_MODEL:], preferred_element_type=F32)
    for b in range(BATCH):
        z_ref[b] = z[ROW_TILE * b:ROW_TILE * (b + 1)].astype(BF16)
    n_slab = D_MODEL // LANES
    for b in range(BATCH):
        for cc in range(CHUNKS_PER_TILE):
            r0 = ROW_TILE * b + CHUNK * cc
            q0 = SLOT_PITCH * _chunk_slot(b, cc)
            for k in range(n_slab):
                scr_ref[k, q0:q0 + CHUNK, :] = u[r0:r0 + CHUNK, LANES * k:LANES * (k + 1)]
    groups_per_slab = LANES // SSM_GROUP
    for k in range(n_slab):
        for pp in range(PAIRS_PER_TILE // 2):
            parts = []
            for p in (2 * pp, 2 * pp + 1):
                rows = [scr_ref[k, pl.ds(SLOT_PITCH * SUBLANES * p + s, SUBLANES,
                                         stride=SLOT_PITCH), :] for s in range(CHUNK)]
                parts.append([_lane_block_transpose(rows[SUBLANES * m2:SUBLANES * (m2 + 1)])
                              for m2 in range(CHUNK // SUBLANES)])
            for m2 in range(CHUNK // SUBLANES):
                for gl in range(groups_per_slab):
                    val = jnp.concatenate([parts[0][m2][gl], parts[1][m2][gl]], axis=0)
                    u_ref[groups_per_slab * k + gl, 2 * SUBLANES * pp:2 * SUBLANES * (pp + 1),
                          LANES * m2:LANES * (m2 + 1)] = val.astype(BF16)


def _proj1(xc, mod, norm_w, w_in):
    row_spec = pl.BlockSpec((BATCH, ROW_TILE, D_MODEL), lambda i: (0, i, 0))
    tile_rows = PAIRS_PER_TILE * SUBLANES
    return pl.pallas_call(
        _proj1_kernel,
        grid=(N_ROW_TILES,),
        in_specs=[
            row_spec,
            pl.BlockSpec((BATCH, 1, SUBLANES, D_MODEL), lambda i: (0, jnp.minimum(i, 1), 0, 0)),
            pl.BlockSpec((1, D_MODEL), lambda i: (0, 0)),
            pl.BlockSpec((D_MODEL, 2 * D_MODEL), lambda i: (0, 0)),
        ],
        out_specs=[pl.BlockSpec((SSM_GROUPS, tile_rows, CHUNK_W), lambda i: (0, i, 0)), row_spec],
        out_shape=[jax.ShapeDtypeStruct((SSM_GROUPS, SCAN_ROWS, CHUNK_W), BF16),
                   jax.ShapeDtypeStruct((BATCH, TOTAL, D_MODEL), BF16)],
        scratch_shapes=[pltpu.VMEM((D_MODEL // LANES, BATCH * CHUNKS_PER_TILE * SLOT_PITCH, LANES),
                                   F32)],
        compiler_params=_params(("arbitrary",), 56),
        name="ssm_projection",
    )(xc, mod, norm_w.reshape(1, D_MODEL), w_in)


def _s5_kernel(u_ref, m_ref, ws_ref, wy_ref, cst_ref, y_ref, s4_ref, xp_ref):
    even = lax.broadcasted_iota(jnp.int32, (SUBLANES, STATE_W), 0) % 2 == 0
    down = 1
    up = SUBLANES - 1
    fwd = slice(0, STATE_W)
    bwd = slice(STATE_W, 2 * STATE_W)
    fwd_sw = slice(2 * STATE_W, 3 * STATE_W)
    bwd_sw = slice(3 * STATE_W, 4 * STATE_W)
    for g in range(GROUP_BATCH):
        s = jnp.dot(u_ref[g], ws_ref[g], preferred_element_type=F32)
        s = s.reshape(N_SCAN_BLOCKS, SUBLANES, 4 * STATE_W)
        q1f, q2f, q1b, q2b = cst_ref[g, 2], cst_ref[g, 3], cst_ref[g, 6], cst_ref[g, 7]
        zf, zb, zfs, zbs = s[..., fwd], s[..., bwd], s[..., fwd_sw], s[..., bwd_sw]
        rzf, rzfs = pltpu.roll(zf, down, 1), pltpu.roll(zfs, down, 1)
        rzb, rzbs = pltpu.roll(zb, up, 1), pltpu.roll(zbs, up, 1)
        t = jnp.concatenate([zf + q1f * rzf + q2f * rzfs, zb + q1b * rzb + q2b * rzbs,
                             zfs + q1f * rzfs - q2f * rzf, zbs + q1b * rzbs - q2b * rzb], axis=-1)
        s4_ref[g] = t.reshape(SCAN_ROWS, 4 * STATE_W)
        y_ref[g] = jnp.dot(u_ref[g, CTX_SCAN_ROWS:, :], m_ref[g], preferred_element_type=F32)

    def step(j, carry):
        jb = jnp.where(j < N_CTX_SCAN_BLOCKS, N_CTX_SCAN_BLOCKS - 1 - j,
                       N_SCAN_BLOCKS - 1 + N_CTX_SCAN_BLOCKS - j)
        rf = pl.ds(pl.multiple_of(j * SUBLANES, SUBLANES), SUBLANES)
        rb = pl.ds(pl.multiple_of(jb * SUBLANES, SUBLANES), SUBLANES)
        new = []
        for g in range(GROUP_BATCH):
            cf, cfs, cb, cbs = carry[4 * g:4 * g + 4]
            p1f, p2f, p1b, p2b = cst_ref[g, 0], cst_ref[g, 1], cst_ref[g, 4], cst_ref[g, 5]
            xf = p1f * cf + p2f * cfs + s4_ref[g, rf, fwd]
            xfs = p1f * cfs - p2f * cf + s4_ref[g, rf, fwd_sw]
            xp_ref[g, rf, fwd] = jnp.where(even, cf, pltpu.roll(xf, down, 0))
            new += [jnp.where(even, pltpu.roll(xf, up, 0), xf),
                    jnp.where(even, pltpu.roll(xfs, up, 0), xfs)]
            xb = p1b * cb + p2b * cbs + s4_ref[g, rb, bwd]
            xbs = p1b * cbs - p2b * cb + s4_ref[g, rb, bwd_sw]
            xp_ref[g, rb, bwd] = jnp.where(even, pltpu.roll(xb, up, 0), cb)
            new += [jnp.where(even, xb, pltpu.roll(xb, down, 0)),
                    jnp.where(even, xbs, pltpu.roll(xbs, down, 0))]
        return tuple(new)

    zero = jnp.zeros((SUBLANES, STATE_W), F32)
    lax.fori_loop(0, N_SCAN_BLOCKS, step, (zero,) * (4 * GROUP_BATCH))
    for g in range(GROUP_BATCH):
        y_ref[g] = y_ref[g] + lax.dot_general(
            xp_ref[g, CTX_SCAN_ROWS:, :].astype(BF16), wy_ref[g], (((1,), (1,)), ((), ())),
            preferred_element_type=F32)


def _s5_core(u_g, m_mat, ws_mat, wy_mat, consts):
    lat_rows = SCAN_ROWS - CTX_SCAN_ROWS

    def gspec(*tail):
        return pl.BlockSpec((GROUP_BATCH,) + tail, lambda i: (i,) + (0,) * len(tail))

    return pl.pallas_call(
        _s5_kernel,
        grid=(SSM_GROUPS // GROUP_BATCH,),
        in_specs=[
            gspec(SCAN_ROWS, CHUNK_W),
            gspec(CHUNK_W, CHUNK_W),
            gspec(CHUNK_W, 4 * STATE_W),
            gspec(CHUNK_W, 2 * STATE_W),
            gspec(8, SUBLANES, STATE_W),
        ],
        out_specs=gspec(lat_rows, CHUNK_W),
        out_shape=jax.ShapeDtypeStruct((SSM_GROUPS, lat_rows, CHUNK_W), F32),
        scratch_shapes=[
            pltpu.VMEM((GROUP_BATCH, SCAN_ROWS, 4 * STATE_W), F32),
            pltpu.VMEM((GROUP_BATCH, SCAN_ROWS, 2 * STATE_W), F32),
        ],
        compiler_params=_params(("arbitrary",), 48),
        name="s5_scan",
    )(u_g, m_mat, ws_mat, wy_mat, consts)


def _out1_kernel(y_ref, z_ref, x_ref, mod_ref, wg_ref, wo_ref, fnw_ref, o_ref, scr_ref):
    n_slab = D_MODEL // LANES
    groups_per_slab = LANES // SSM_GROUP
    pairs_per_phase = PAIRS_PER_TILE // OUT_PHASES
    rows_per_phase = ROW_TILE // OUT_PHASES

    def relayout(ph):
        for k in range(n_slab):
            for p in range(pairs_per_phase * ph, pairs_per_phase * (ph + 1)):
                for m2 in range(CHUNK // SUBLANES):
                    vals = [y_ref[groups_per_slab * k + gl, SUBLANES * p:SUBLANES * (p + 1),
                                  LANES * m2:LANES * (m2 + 1)] for gl in range(groups_per_slab)]
                    steps = _lane_block_transpose(vals)
                    for s2 in range(SUBLANES):
                        t_idx = SUBLANES * m2 + s2
                        scr_ref[k, pl.ds(SLOT_PITCH * SUBLANES * p + t_idx, SUBLANES,
                                         stride=SLOT_PITCH), :] = steps[s2]

    def glu_matmul(ph):
        chunks = range(2 * pairs_per_phase * ph, 2 * pairs_per_phase * (ph + 1))
        y = jnp.concatenate(
            [jnp.concatenate(
                [scr_ref[k, SLOT_PITCH * _chunk_slot(b, cc):SLOT_PITCH * _chunk_slot(b, cc) + CHUNK, :]
                 for k in range(n_slab)], axis=1)
             for b in range(BATCH) for cc in chunks], axis=0)
        g = (0.5 * y * (1.0 + lax.erf(y * (2.0 ** -0.5)))).astype(BF16)
        return jnp.dot(g, wg_ref[...], preferred_element_type=F32)

    def finish(ph, t):
        rows = slice(rows_per_phase * ph, rows_per_phase * (ph + 1))
        z = jnp.concatenate([z_ref[b, rows, :] for b in range(BATCH)], axis=0).astype(F32)
        r = (t[:, :D_MODEL] * jax.nn.sigmoid(t[:, D_MODEL:]) * (z * jax.nn.sigmoid(z))).astype(BF16)
        o = jnp.dot(r, wo_ref[...], preferred_element_type=F32)
        for b in range(BATCH):
            x2 = (x_ref[b, rows, :]
                  + mod_ref[b, 0, 2:3, :] * o[rows_per_phase * b:rows_per_phase * (b + 1)])
            ms = jnp.mean(x2 * x2, axis=-1, keepdims=True)
            o_ref[b, rows, :] = x2 * lax.rsqrt(ms + NORM_EPS) * fnw_ref[...]

    relayout(0)
    relayout(1)
    t_next = glu_matmul(0)
    for ph in range(OUT_PHASES):
        t_cur = t_next
        if ph + 2 < OUT_PHASES:
            relayout(ph + 2)
        if ph + 1 < OUT_PHASES:
            t_next = glu_matmul(ph + 1)
        finish(ph, t_cur)


def _out1(y_g, z, xc, mod, w_glu, w_out, final_norm_w):
    ctx_tiles = CTX_LEN // ROW_TILE
    tile_rows = PAIRS_PER_TILE * SUBLANES
    lat_spec = pl.BlockSpec((BATCH, ROW_TILE, D_MODEL), lambda i: (0, i, 0))
    all_spec = pl.BlockSpec((BATCH, ROW_TILE, D_MODEL), lambda i: (0, i + ctx_tiles, 0))
    return pl.pallas_call(
        _out1_kernel,
        grid=(SEQ // ROW_TILE,),
        in_specs=[
            pl.BlockSpec((SSM_GROUPS, tile_rows, CHUNK_W), lambda i: (0, i, 0)),
            all_spec, all_spec,
            pl.BlockSpec((BATCH, 1, SUBLANES, D_MODEL), lambda i: (0, 1, 0, 0)),
            pl.BlockSpec((D_MODEL, 2 * D_MODEL), lambda i: (0, 0)),
            pl.BlockSpec((D_MODEL, D_MODEL), lambda i: (0, 0)),
            pl.BlockSpec((1, D_MODEL), lambda i: (0, 0)),
        ],
        out_specs=lat_spec,
        out_shape=jax.ShapeDtypeStruct((BATCH, SEQ, D_MODEL), F32),
        scratch_shapes=[pltpu.VMEM((D_MODEL // LANES, BATCH * CHUNKS_PER_TILE * SLOT_PITCH, LANES),
                                   F32)],
        compiler_params=_params(("arbitrary",), 56),
        name="ssm_output",
    )(y_g, z, xc, mod, w_glu, w_out, final_norm_w.reshape(1, D_MODEL))


def _slot_order(t, lead):
    n_m = KV_WIDTH // LANES
    gq = N_HEADS // N_KV_HEADS
    shape = t.shape
    t = t.reshape(shape[:lead] + (n_m, 2, gq) + shape[lead + 1:])
    perm = tuple(range(lead)) + (lead, lead + 2, lead + 1) + tuple(range(lead + 3, t.ndim))
    return jnp.transpose(t, perm)


def _rope_order(t):
    shape = t.shape
    t = t.reshape(shape[:-1] + (2, 2, ROPE_FREQS))
    return jnp.swapaxes(t, -3, -2).reshape(shape)


def _attn_weights(w_in, w_out, sink):
    wq = w_in[:, :ATTN_WIDTH].reshape(D_MODEL, N_HEADS, HEAD_DIM)
    wq = _slot_order(_rope_order(wq), 1).reshape(D_MODEL, ATTN_WIDTH)
    wk = w_in[:, ATTN_WIDTH:ATTN_WIDTH + KV_WIDTH].reshape(D_MODEL, N_KV_HEADS, HEAD_DIM)
    wk = _rope_order(wk).reshape(D_MODEL, KV_WIDTH)
    wv = w_in[:, ATTN_WIDTH + KV_WIDTH:ATTN_WIDTH + 2 * KV_WIDTH]
    wz = w_in[:, ATTN_WIDTH + 2 * KV_WIDTH:].reshape(D_MODEL, N_HEADS, HEAD_DIM)
    wz = _slot_order(wz, 1).reshape(D_MODEL, ATTN_WIDTH)
    w_in_p = jnp.concatenate([wq, wk, wv, wz], axis=1).astype(BF16)
    wo = _slot_order(w_out.reshape(N_HEADS, HEAD_DIM, D_MODEL), 0).reshape(ATTN_WIDTH, D_MODEL)
    sink_p = _slot_order(sink.astype(F32).reshape(N_HEADS), 0).reshape(2 * N_SLOTS)
    sink_tab = jnp.broadcast_to((sink_p * LOG2E)[:, None], (2 * N_SLOTS, LANES))
    return w_in_p, wo.astype(BF16), sink_tab


def _rope_tables():
    inv = ROPE_BASE ** (-np.arange(ROPE_FREQS, dtype=np.float64) / ROPE_FREQS)
    pos = np.arange(SEQ)
    row = (pos // GRID_W)[:, None] * inv
    col = (pos % GRID_W)[:, None] * inv
    w = np.arange(LANES) % HEAD_DIM
    half_w = HEAD_DIM // 2
    half, axis, f = w // half_w, (w % half_w) // ROPE_FREQS, w % ROPE_FREQS
    ang = np.where((axis == 0)[None, :], row[:, f], col[:, f])
    sign = np.where(half == 0, -1.0, 1.0)[None, :]
    cos = np.concatenate([np.ones((CTX_LEN, LANES)), np.cos(ang)], axis=0)
    sin = np.concatenate([np.zeros((CTX_LEN, LANES)), np.sin(ang) * sign], axis=0)
    return jnp.asarray(cos, F32), jnp.asarray(sin, F32)


def _s5_operators(lam_re, lam_im, log_dt, b_re, b_im, c_re, c_im, d_skip):
    t_len = CHUNK
    n_pow = 2 * t_len + 1
    lr, li = lam_re.astype(F32), lam_im.astype(F32)
    dt = jnp.exp(log_dt.astype(F32))[..., None]
    mag = jnp.exp(lr * dt)
    sq = [(mag * jnp.cos(li * dt), mag * jnp.sin(li * dt))]
    while 2 ** len(sq) < n_pow:
        r, i = sq[-1]
        sq.append((r * r - i * i, 2.0 * r * i))
    ks = np.arange(n_pow)
    pr = jnp.ones(lr.shape + (n_pow,), F32)
    pi = jnp.zeros(lr.shape + (n_pow,), F32)
    for bit, (r, i) in enumerate(sq):
        on = jnp.asarray((ks >> bit) & 1 == 1)
        fr = jnp.where(on, r[..., None], 1.0)
        fi = jnp.where(on, i[..., None], 0.0)
        pr, pi = pr * fr - pi * fi, pr * fi + pi * fr
    ar1, ai1 = sq[0][0] - 1.0, sq[0][1]
    den = lr * lr + li * li
    gr, gi = (ar1 * lr + ai1 * li) / den, (ai1 * lr - ar1 * li) / den
    br_, bi_ = b_re.astype(F32), b_im.astype(F32)
    bbr = gr[..., None] * br_ - gi[..., None] * bi_
    bbi = gr[..., None] * bi_ + gi[..., None] * br_
    pad_k = POW_ROWS - n_pow
    prt, pit = jnp.swapaxes(pr, 2, 3), jnp.swapaxes(pi, 2, 3)
    pw = jnp.pad(jnp.concatenate([prt, prt, pit, pit], axis=-1),
                 ((0, 0), (0, 0), (0, pad_k), (0, 0)))
    brt, bit = jnp.swapaxes(bbr, 2, 3), jnp.swapaxes(bbi, 2, 3)
    bb = jnp.concatenate([brt, bit, -bit, brt, bit, brt, brt, -bit], axis=-1)
    cr, ci = c_re.astype(F32), c_im.astype(F32)
    cm = jnp.stack([jnp.concatenate([cr, -ci], axis=-1),
                    jnp.concatenate([-ci, -cr], axis=-1)], axis=2)
    dv = jnp.tile(d_skip.astype(F32).reshape(SSM_GROUPS, 1, SSM_GROUP), (1, 1, t_len))
    return _s5_operator_call(pw, bb, cm, dv)


def _operator_constants():
    t_len = CHUNK
    s_of_row = np.arange(CHUNK_W) // SSM_GROUP
    k_ar = np.arange(POW_ROWS)
    oh_f = (k_ar[None, :] == (t_len - 1 - s_of_row)[:, None]).astype(np.float32)
    oh_b = (k_ar[None, :] == s_of_row[:, None]).astype(np.float32)
    t_of_lane = np.arange(CHUNK_W) // SSM_GROUP
    expo = [s_of_row + 1, t_len - s_of_row, s_of_row, t_len - 1 - s_of_row]
    sel = np.stack([(k_ar[None, :] == e[:, None]) for e in expo]).astype(np.float32)
    h_of_lane = np.arange(CHUNK_W) % SSM_GROUP
    dmask = ((s_of_row[:, None] == t_of_lane[None, :])
             & ((np.arange(CHUNK_W) % SSM_GROUP)[:, None] == h_of_lane[None, :])).astype(np.float32)
    return oh_f, oh_b, sel, dmask


def _s5_op_kernel(*refs):
    for gg in range(OP_GROUP_BATCH):
        _s5_op_group(gg, *refs)


def _s5_op_group(gg, pw_ref, bb_ref, cm_ref, dv_ref, ohf_ref, ohb_ref, sel_ref, dmask_ref,
                 m_ref, ws_ref, wyt_ref, cst_ref):
    t_len = CHUNK

    def split(a):
        hi = a.astype(BF16)
        return hi, (a - hi.astype(F32)).astype(BF16)

    def pick_rows(onehot2, table):
        hi, lo = split(table)
        return jnp.dot(onehot2, jnp.concatenate([hi, lo], axis=0), preferred_element_type=F32)

    def tile_rows(a):
        return jnp.concatenate([a] * t_len, axis=0)

    main, swapped = [], []
    for d, oh_ref in ((0, ohf_ref), (1, ohb_ref)):
        pp = pick_rows(oh_ref[...], pw_ref[d, gg])
        p_re, p_im = pp[:, :STATE_W], pp[:, STATE_W:]
        b0, b1, b2, b3 = [tile_rows(bb_ref[d, gg, :, STATE_W * i:STATE_W * (i + 1)])
                          for i in range(4)]
        main.append(p_re * b0 + p_im * b1)
        swapped.append(p_re * b2 + p_im * b3)
    ws_ref[gg] = jnp.concatenate(main + swapped, axis=1).astype(BF16)

    c_tiled = [[tile_rows(cm_ref[d, gg, i]) for i in range(2)] for d in range(2)]

    def block_t(d, pat):
        pp = pick_rows(sel_ref[pat], pw_ref[d, gg])
        return pp[:, :STATE_W] * c_tiled[d][0] + pp[:, STATE_W:] * c_tiled[d][1]

    wyt_ref[gg] = jnp.concatenate([block_t(0, 0), block_t(1, 1)], axis=1).astype(BF16)

    nt = (((1,), (1,)), ((), ()))
    kt_f = lax.dot_general(bb_ref[0, gg, :, :STATE_W], block_t(0, 2), nt,
                           preferred_element_type=F32, precision=lax.Precision.HIGHEST)
    kt_b = lax.dot_general(bb_ref[1, gg, :, :STATE_W], block_t(1, 3), nt,
                           preferred_element_type=F32, precision=lax.Precision.HIGHEST)
    lane = lax.broadcasted_iota(jnp.int32, (SSM_GROUP, CHUNK_W), 1)
    skip = dv_ref[gg]
    for s in range(t_len):
        fwd = kt_f if s == 0 else pltpu.roll(kt_f, SSM_GROUP * s, 1)
        back = t_len - 1 - s
        bwd = kt_b if back == 0 else pltpu.roll(kt_b, CHUNK_W - SSM_GROUP * back, 1)
        rows = (jnp.where(lane >= SSM_GROUP * s, fwd, 0.0)
                + jnp.where(lane < SSM_GROUP * (s + 1), bwd, 0.0)
                + dmask_ref[SSM_GROUP * s:SSM_GROUP * (s + 1), :] * skip)
        m_ref[gg, SSM_GROUP * s:SSM_GROUP * (s + 1), :] = rows.astype(BF16)

    even = lax.broadcasted_iota(jnp.int32, (SUBLANES, STATE_W), 0) % 2 == 0
    sign = jnp.where(lax.broadcasted_iota(jnp.int32, (1, STATE_W), 1) < SSM_STATE, -1.0, 1.0)

    def w12(d, k):
        row = pw_ref[d, gg, k:k + 1, :]
        return row[:, :STATE_W], row[:, STATE_W:] * sign

    zero = (jnp.zeros((1, STATE_W), F32),) * 2
    pairs = [(w12(0, t_len), w12(0, 2 * t_len)), (zero, w12(0, t_len)),
             (w12(1, 2 * t_len), w12(1, t_len)), (w12(1, t_len), zero)]
    idx = 0
    for top, bot in pairs:
        for part in range(2):
            cst_ref[gg, idx] = jnp.where(even, jnp.broadcast_to(top[part], (SUBLANES, STATE_W)),
                                         jnp.broadcast_to(bot[part], (SUBLANES, STATE_W)))
            idx += 1


def _s5_operator_call(pw, bb, cm, dv):
    oh_f, oh_b, sel, dmask = _operator_constants()
    consts = [jnp.asarray(np.concatenate([a, a], axis=-1), BF16) for a in (oh_f, oh_b, sel)]
    consts.append(jnp.asarray(dmask))

    def per_group(*tail):
        n = len(tail)
        return pl.BlockSpec((2, OP_GROUP_BATCH) + tail, lambda g: (0, g) + (0,) * n)

    def whole(a):
        return pl.BlockSpec(a.shape, lambda g: (0,) * a.ndim)

    def out(*tail):
        return pl.BlockSpec((OP_GROUP_BATCH,) + tail, lambda g: (g,) + (0,) * len(tail))

    return pl.pallas_call(
        _s5_op_kernel,
        grid=(SSM_GROUPS // OP_GROUP_BATCH,),
        in_specs=[per_group(POW_ROWS, 2 * STATE_W), per_group(SSM_GROUP, 4 * STATE_W),
                  per_group(2, SSM_GROUP, STATE_W),
                  pl.BlockSpec((OP_GROUP_BATCH, 1, CHUNK_W), lambda g: (g, 0, 0))]
                 + [whole(a) for a in consts],
        out_specs=[out(CHUNK_W, CHUNK_W), out(CHUNK_W, 4 * STATE_W), out(CHUNK_W, 2 * STATE_W),
                   out(8, SUBLANES, STATE_W)],
        out_shape=[jax.ShapeDtypeStruct((SSM_GROUPS, CHUNK_W, CHUNK_W), BF16),
                   jax.ShapeDtypeStruct((SSM_GROUPS, CHUNK_W, 4 * STATE_W), BF16),
                   jax.ShapeDtypeStruct((SSM_GROUPS, CHUNK_W, 2 * STATE_W), BF16),
                   jax.ShapeDtypeStruct((SSM_GROUPS, 8, SUBLANES, STATE_W), F32)],
        compiler_params=_params(("arbitrary",), 32),
        name="s5_operators",
    )(pw, bb, cm, dv, *consts)


def kernel(x, c, ctx, c_ctx, norm_w, w_ada, b_ada, attn_w_in, attn_sink, attn_w_out,
           ssm_w_in, ssm_lam_re, ssm_lam_im, ssm_log_dt, ssm_b_re, ssm_b_im, ssm_c_re, ssm_c_im,
           ssm_d, ssm_w_glu, ssm_w_out, final_norm_w):
    mod0, mod1 = _modulation(c, c_ctx, w_ada, b_ada)

    w_in0, w_out0, sink_tab = _attn_weights(attn_w_in[0], attn_w_out[0], attn_sink[0])
    cos_tab, sin_tab = _rope_tables()
    q, z0, kbd, vt = _proj0(x, ctx, mod0, norm_w[0], cos_tab, sin_tab, w_in0)
    xc1 = _attention(q, z0, kbd, vt, sink_tab, x, ctx, mod0, w_out0)

    u_g, z1 = _proj1(xc1, mod1, norm_w[1], ssm_w_in[0].astype(BF16))
    m_mat, ws_mat, wy_mat, consts = _s5_operators(
        ssm_lam_re[0], ssm_lam_im[0], ssm_log_dt[0], ssm_b_re[0], ssm_b_im[0],
        ssm_c_re[0], ssm_c_im[0], ssm_d[0])
    y_g = _s5_core(u_g, m_mat, ws_mat, wy_mat, consts)
    return _out1(y_g, z1, xc1, mod1, ssm_w_glu[0].astype(BF16), ssm_w_out[0].astype(BF16),
                 final_norm_w)
```
